```python
import math
import jax, jax.numpy as jnp
from jax import lax
import numpy as np

D_MODEL = 2048
BATCH = 8
SEQ = 8192
DEPTH = 2

N_A = DEPTH // 2
N_B = DEPTH - N_A
CONV_K = 31
FFN_CONV_K = 3
D_FF = 5632
GROUPS = ((128, 1), (512, 4), (2048, 16))
N_GROUPS = len(GROUPS)
HEADS_PER_GROUP = 8
HEAD_DIM = 128
Q_WIDTH = N_GROUPS * HEADS_PER_GROUP * HEAD_DIM
O_WIDTH = HEADS_PER_GROUP * HEAD_DIM
ROT_DIM = HEAD_DIM // 4
ROPE_THETA = 500000.0
BLK = 128
EPS = 1e-6
NEG = -1e30

kernel_name = "yoco_conformer_dilated_hybrid"


def rms_norm(x, g):
    xf = x.astype(jnp.float32)
    y = xf * lax.rsqrt(jnp.mean(xf * xf, axis=-1, keepdims=True) + EPS)
    return (y * g.astype(jnp.float32)).astype(x.dtype)


def layer_norm(x, g, b):
    xf = x.astype(jnp.float32)
    mu = jnp.mean(xf, axis=-1, keepdims=True)
    var = jnp.mean(jnp.square(xf - mu), axis=-1, keepdims=True)
    y = (xf - mu) * lax.rsqrt(var + EPS)
    return (y * g.astype(jnp.float32) + b.astype(jnp.float32)).astype(x.dtype)


def modulate(x, g, shift, scale):
    return rms_norm(x, g) * (1 + scale[:, None, :]) + shift[:, None, :]


def causal_dwconv(x, w, b):
    k, ch = w.shape
    y = lax.conv_general_dilated(
        x, w[:, None, :].astype(x.dtype), window_strides=(1,), padding=[(k - 1, 0)],
        dimension_numbers=("NWC", "WIO", "NWC"), feature_group_count=ch)
    return y + b


def rope_partial(x, positions):
    inv_freq = ROPE_THETA ** (-jnp.arange(0, ROT_DIM, 2, dtype=jnp.float32) / ROT_DIM)
    ang = positions.astype(jnp.float32)[..., None] * inv_freq
    ang = ang.reshape(ang.shape[:2] + (1,) * (x.ndim - 3) + ang.shape[-1:])
    cos, sin = jnp.cos(ang), jnp.sin(ang)
    xf = x.astype(jnp.float32)
    x1 = xf[..., : ROT_DIM // 2]
    x2 = xf[..., ROT_DIM // 2: ROT_DIM]
    out = jnp.concatenate([x1 * cos - x2 * sin, x2 * cos + x1 * sin, xf[..., ROT_DIM:]], axis=-1)
    return out.astype(x.dtype)


def conformer_conv(h, pw1_w, pw1_b, dw_w, dw_b, ln_g, ln_b, pw2_w, pw2_b):
    u = h @ pw1_w + pw1_b
    a, gt = jnp.split(u, 2, axis=-1)
    u = a * jax.nn.sigmoid(gt)
    u = causal_dwconv(u, dw_w, dw_b)
    u = jax.nn.silu(layer_norm(u, ln_g, ln_b))
    return u @ pw2_w + pw2_b


def conv_ffn(h, up_w, dw_w, dw_b, down_w):
    u = h @ up_w
    gt, val = jnp.split(u, 2, axis=-1)
    gt = causal_dwconv(gt, dw_w, dw_b)
    return (jax.nn.silu(gt) * val) @ down_w


def shared_kv(x, c, kv_mod_w, kv_mod_b, kv_norm_g, w_kv, k_norm_g, positions):
    b, s, _ = x.shape
    m = jax.nn.silu(c) @ kv_mod_w + kv_mod_b
    shift, scale = jnp.split(m, 2, axis=-1)
    h = modulate(x, kv_norm_g, shift, scale)
    kv = (h @ w_kv).reshape(b, s, 2, N_GROUPS, HEADS_PER_GROUP, HEAD_DIM)
    k = rope_partial(rms_norm(kv[:, :, 0], k_norm_g), positions)
    v = kv[:, :, 1]
    return k, v


def dilated_band_attn(q, k, v, span, r):
    b, s, h, dh = q.shape
    chunk = r * BLK
    s_pad = -(-s // chunk) * chunk
    nb = s_pad // chunk
    padw = ((0, 0), (0, s_pad - s), (0, 0), (0, 0))
    split = lambda t: jnp.pad(t, padw).reshape(b, nb, BLK, r, h, dh)
    qb, kb, vb = split(q), split(k), split(v)
    shift_prev = lambda t: jnp.concatenate([jnp.zeros_like(t[:, :1]), t[:, :-1]], axis=1)
    kcat = jnp.concatenate([shift_prev(kb), kb], axis=2)
    vcat = jnp.concatenate([shift_prev(vb), vb], axis=2)
    sc = jnp.einsum("bnqrhd,bnkrhd->bnrhqk", qb, kcat,
                    preferred_element_type=jnp.float32) * (1.0 / math.sqrt(dh))
    qi = jnp.arange(BLK)[:, None]
    kj = jnp.arange(2 * BLK)[None, :]
    dist = qi + BLK - kj
    band = (dist >= 0) & (dist <= span)
    kpos = jnp.arange(nb)[:, None, None] * BLK + kj[None] - BLK
    mask = band[None] & (kpos >= 0)
    sc = jnp.where(mask[None, :, None, None], sc, NEG)
    lse = jax.nn.logsumexp(sc, axis=-1)
    p = jnp.exp(sc - lse[..., None])
    o = jnp.einsum("bnrhqk,bnkrhd->bnqrhd", p.astype(v.dtype), vcat)
    o = o.reshape(b, s_pad, h, dh)[:, :s]
    lse = lse.transpose(0, 1, 4, 2, 3).reshape(b, s_pad, h)[:, :s]
    return o, lse


def dilated_mixture_attn(h, k, v, w_q, q_norm_g, w_o, positions):
    b, s, _ = h.shape
    q = (h @ w_q).reshape(b, s, N_GROUPS, HEADS_PER_GROUP, HEAD_DIM)
    q = rope_partial(rms_norm(q, q_norm_g), positions)
    outs, lses = [], []
    for g, (window, dil) in enumerate(GROUPS):
        o_g, lse_g = dilated_band_attn(q[:, :, g], k[:, :, g], v[:, :, g], window // dil, dil)
        outs.append(o_g)
        lses.append(lse_g)
    alpha = jax.nn.softmax(jnp.stack(lses, axis=0), axis=0)
    o = jnp.einsum("gbsh,gbshd->bshd", alpha, jnp.stack(outs, 0).astype(jnp.float32))
    return o.astype(h.dtype).reshape(b, s, O_WIDTH) @ w_o


def _fwd_setup_inputs(seed: int = 0) -> dict:
    key = jax.random.key(seed)
    ks = iter(jax.random.split(key, 40))
    D = D_MODEL
    nrm = lambda shape, scale: jax.random.normal(next(ks), shape, jnp.float32) * scale
    gain = lambda shape: 1.0 + nrm(shape, 0.02)
    return {
        "x": nrm((BATCH, SEQ, D), 1.0),
        "c": nrm((BATCH, D), 1.0),
        "positions": jnp.broadcast_to(jnp.arange(SEQ, dtype=jnp.int32)[None], (BATCH, SEQ)),
        "mod_w": nrm((DEPTH, D, 6 * D), 0.5 * D ** -0.5),
        "mod_b": nrm((DEPTH, 6 * D), 0.01),
        "norm_mix_g": gain((DEPTH, D)),
        "norm_ffn_g": gain((DEPTH, D)),
        "conv_pw1_w": nrm((N_A, D, 2 * D), D ** -0.5),
        "conv_pw1_b": nrm((N_A, 2 * D), 0.01),
        "conv_dw_w": nrm((N_A, CONV_K, D), CONV_K ** -0.5),
        "conv_dw_b": nrm((N_A, D), 0.01),
        "conv_ln_g": gain((N_A, D)),
        "conv_ln_b": nrm((N_A, D), 0.01),
        "conv_pw2_w": nrm((N_A, D, D), D ** -0.5),
        "conv_pw2_b": nrm((N_A, D), 0.01),
        "kv_mod_w": nrm((D, 2 * D), 0.5 * D ** -0.5),
        "kv_mod_b": nrm((2 * D,), 0.01),
        "kv_norm_g": gain((D,)),
        "w_kv": nrm((D, 2 * Q_WIDTH), D ** -0.5),
        "k_norm_g": gain((HEAD_DIM,)),
        "w_q": nrm((N_B, D, Q_WIDTH), D ** -0.5),
        "q_norm_g": gain((N_B, HEAD_DIM)),
        "w_o": nrm((N_B, O_WIDTH, D), O_WIDTH ** -0.5),
        "ffn_up_w": nrm((DEPTH, D, 2 * D_FF), D ** -0.5),
        "ffn_dw_w": nrm((DEPTH, FFN_CONV_K, D_FF), FFN_CONV_K ** -0.5),
        "ffn_dw_b": nrm((DEPTH, D_FF), 0.01),
        "ffn_down_w": nrm((DEPTH, D_FF, D), D_FF ** -0.5),
    }


def _fwd_reference(x, c, positions, mod_w, mod_b, norm_mix_g, norm_ffn_g,
              conv_pw1_w, conv_pw1_b, conv_dw_w, conv_dw_b, conv_ln_g, conv_ln_b,
              conv_pw2_w, conv_pw2_b, kv_mod_w, kv_mod_b, kv_norm_g, w_kv, k_norm_g,
              w_q, q_norm_g, w_o, ffn_up_w, ffn_dw_w, ffn_dw_b, ffn_down_w):
    k_sh = v_sh = None
    for l in range(DEPTH):
        m = jax.nn.silu(c) @ mod_w[l] + mod_b[l]
        sh_m, sc_m, g_m, sh_f, sc_f, g_f = jnp.split(m, 6, axis=-1)
        if l < N_A:
            h = modulate(x, norm_mix_g[l], sh_m, sc_m)
            y = conformer_conv(h, conv_pw1_w[l], conv_pw1_b[l], conv_dw_w[l], conv_dw_b[l],
                               conv_ln_g[l], conv_ln_b[l], conv_pw2_w[l], conv_pw2_b[l])
        else:
            if l == N_A:
                k_sh, v_sh = shared_kv(x, c, kv_mod_w, kv_mod_b, kv_norm_g, w_kv,
                                       k_norm_g, positions)
            j = l - N_A
            h = modulate(x, norm_mix_g[l], sh_m, sc_m)
            y = dilated_mixture_attn(h, k_sh, v_sh, w_q[j], q_norm_g[j], w_o[j], positions)
        x = x + g_m[:, None, :] * y
        h = modulate(x, norm_ffn_g[l], sh_f, sc_f)
        x = x + g_f[:, None, :] * conv_ffn(h, ffn_up_w[l], ffn_dw_w[l], ffn_dw_b[l], ffn_down_w[l])
    return x


import jax as _jax
import jax.numpy as _jnp

TWIN_FORMAT = 'train_step'
FWD_PARAMS = ['x', 'c', 'positions', 'mod_w', 'mod_b', 'norm_mix_g', 'norm_ffn_g', 'conv_pw1_w', 'conv_pw1_b', 'conv_dw_w', 'conv_dw_b', 'conv_ln_g', 'conv_ln_b', 'conv_pw2_w', 'conv_pw2_b', 'kv_mod_w', 'kv_mod_b', 'kv_norm_g', 'w_kv', 'k_norm_g', 'w_q', 'q_norm_g', 'w_o', 'ffn_up_w', 'ffn_dw_w', 'ffn_dw_b', 'ffn_down_w']
TWIN_WEIGHTS = ['mod_w', 'mod_b', 'norm_mix_g', 'norm_ffn_g', 'conv_pw1_w', 'conv_pw1_b', 'conv_dw_w', 'conv_dw_b', 'conv_ln_g', 'conv_ln_b', 'conv_pw2_w', 'conv_pw2_b', 'kv_mod_w', 'kv_mod_b', 'kv_norm_g', 'w_kv', 'k_norm_g', 'w_q', 'q_norm_g', 'w_o', 'ffn_up_w', 'ffn_dw_w', 'ffn_dw_b', 'ffn_down_w']
TWIN_DIFF_INPUT = 'x'
TWIN_INPUTS = ['x', 'c', 'positions', 'mod_w', 'mod_b', 'norm_mix_g', 'norm_ffn_g', 'conv_pw1_w', 'conv_pw1_b', 'conv_dw_w', 'conv_dw_b', 'conv_ln_g', 'conv_ln_b', 'conv_pw2_w', 'conv_pw2_b', 'kv_mod_w', 'kv_mod_b', 'kv_norm_g', 'w_kv', 'k_norm_g', 'w_q', 'q_norm_g', 'w_o', 'ffn_up_w', 'ffn_dw_w', 'ffn_dw_b', 'ffn_down_w', 'loss_target', 'm_mod_w', 'm_mod_b', 'm_norm_mix_g', 'm_norm_ffn_g', 'm_conv_pw1_w', 'm_conv_pw1_b', 'm_conv_dw_w', 'm_conv_dw_b', 'm_conv_ln_g', 'm_conv_ln_b', 'm_conv_pw2_w', 'm_conv_pw2_b', 'm_kv_mod_w', 'm_kv_mod_b', 'm_kv_norm_g', 'm_w_kv', 'm_k_norm_g', 'm_w_q', 'm_q_norm_g', 'm_w_o', 'm_ffn_up_w', 'm_ffn_dw_w', 'm_ffn_dw_b', 'm_ffn_down_w', 'v_mod_w', 'v_mod_b', 'v_norm_mix_g', 'v_norm_ffn_g', 'v_conv_pw1_w', 'v_conv_pw1_b', 'v_conv_dw_w', 'v_conv_dw_b', 'v_conv_ln_g', 'v_conv_ln_b', 'v_conv_pw2_w', 'v_conv_pw2_b', 'v_kv_mod_w', 'v_kv_mod_b', 'v_kv_norm_g', 'v_w_kv', 'v_k_norm_g', 'v_w_q', 'v_q_norm_g', 'v_w_o', 'v_ffn_up_w', 'v_ffn_dw_w', 'v_ffn_dw_b', 'v_ffn_down_w']
TWIN_OUTPUTS = ['loss', 'grad_x', 'grad_mod_w', 'grad_mod_b', 'grad_norm_mix_g', 'grad_norm_ffn_g', 'grad_conv_pw1_w', 'grad_conv_pw1_b', 'grad_conv_dw_w', 'grad_conv_dw_b', 'grad_conv_ln_g', 'grad_conv_ln_b', 'grad_conv_pw2_w', 'grad_conv_pw2_b', 'grad_kv_mod_w', 'grad_kv_mod_b', 'grad_kv_norm_g', 'grad_w_kv', 'grad_k_norm_g', 'grad_w_q', 'grad_q_norm_g', 'grad_w_o', 'grad_ffn_up_w', 'grad_ffn_dw_w', 'grad_ffn_dw_b', 'grad_ffn_down_w', 'delta_mod_w', 'delta_mod_b', 'delta_norm_mix_g', 'delta_norm_ffn_g', 'delta_conv_pw1_w', 'delta_conv_pw1_b', 'delta_conv_dw_w', 'delta_conv_dw_b', 'delta_conv_ln_g', 'delta_conv_ln_b', 'delta_conv_pw2_w', 'delta_conv_pw2_b', 'delta_kv_mod_w', 'delta_kv_mod_b', 'delta_kv_norm_g', 'delta_w_kv', 'delta_k_norm_g', 'delta_w_q', 'delta_q_norm_g', 'delta_w_o', 'delta_ffn_up_w', 'delta_ffn_dw_w', 'delta_ffn_dw_b', 'delta_ffn_down_w', 'new_m_mod_w', 'new_m_mod_b', 'new_m_norm_mix_g', 'new_m_norm_ffn_g', 'new_m_conv_pw1_w', 'new_m_conv_pw1_b', 'new_m_conv_dw_w', 'new_m_conv_dw_b', 'new_m_conv_ln_g', 'new_m_conv_ln_b', 'new_m_conv_pw2_w', 'new_m_conv_pw2_b', 'new_m_kv_mod_w', 'new_m_kv_mod_b', 'new_m_kv_norm_g', 'new_m_w_kv', 'new_m_k_norm_g', 'new_m_w_q', 'new_m_q_norm_g', 'new_m_w_o', 'new_m_ffn_up_w', 'new_m_ffn_dw_w', 'new_m_ffn_dw_b', 'new_m_ffn_down_w', 'new_v_mod_w', 'new_v_mod_b', 'new_v_norm_mix_g', 'new_v_norm_ffn_g', 'new_v_conv_pw1_w', 'new_v_conv_pw1_b', 'new_v_conv_dw_w', 'new_v_conv_dw_b', 'new_v_conv_ln_g', 'new_v_conv_ln_b', 'new_v_conv_pw2_w', 'new_v_conv_pw2_b', 'new_v_kv_mod_w', 'new_v_kv_mod_b', 'new_v_kv_norm_g', 'new_v_w_kv', 'new_v_k_norm_g', 'new_v_w_q', 'new_v_q_norm_g', 'new_v_w_o', 'new_v_ffn_up_w', 'new_v_ffn_dw_w', 'new_v_ffn_dw_b', 'new_v_ffn_down_w']
TWIN_LEAF_KINDS = {'loss': 'loss', 'grad_x': 'grad_x', 'grad_mod_w': 'grad_w', 'grad_mod_b': 'grad_w', 'grad_norm_mix_g': 'grad_w', 'grad_norm_ffn_g': 'grad_w', 'grad_conv_pw1_w': 'grad_w', 'grad_conv_pw1_b': 'grad_w', 'grad_conv_dw_w': 'grad_w', 'grad_conv_dw_b': 'grad_w', 'grad_conv_ln_g': 'grad_w', 'grad_conv_ln_b': 'grad_w', 'grad_conv_pw2_w': 'grad_w', 'grad_conv_pw2_b': 'grad_w', 'grad_kv_mod_w': 'grad_w', 'grad_kv_mod_b': 'grad_w', 'grad_kv_norm_g': 'grad_w', 'grad_w_kv': 'grad_w', 'grad_k_norm_g': 'grad_w', 'grad_w_q': 'grad_w', 'grad_q_norm_g': 'grad_w', 'grad_w_o': 'grad_w', 'grad_ffn_up_w': 'grad_w', 'grad_ffn_dw_w': 'grad_w', 'grad_ffn_dw_b': 'grad_w', 'grad_ffn_down_w': 'grad_w', 'delta_mod_w': 'delta_w', 'delta_mod_b': 'delta_w', 'delta_norm_mix_g': 'delta_w', 'delta_norm_ffn_g': 'delta_w', 'delta_conv_pw1_w': 'delta_w', 'delta_conv_pw1_b': 'delta_w', 'delta_conv_dw_w': 'delta_w', 'delta_conv_dw_b': 'delta_w', 'delta_conv_ln_g': 'delta_w', 'delta_conv_ln_b': 'delta_w', 'delta_conv_pw2_w': 'delta_w', 'delta_conv_pw2_b': 'delta_w', 'delta_kv_mod_w': 'delta_w', 'delta_kv_mod_b': 'delta_w', 'delta_kv_norm_g': 'delta_w', 'delta_w_kv': 'delta_w', 'delta_k_norm_g': 'delta_w', 'delta_w_q': 'delta_w', 'delta_q_norm_g': 'delta_w', 'delta_w_o': 'delta_w', 'delta_ffn_up_w': 'delta_w', 'delta_ffn_dw_w': 'delta_w', 'delta_ffn_dw_b': 'delta_w', 'delta_ffn_down_w': 'delta_w', 'new_m_mod_w': 'new_m', 'new_m_mod_b': 'new_m', 'new_m_norm_mix_g': 'new_m', 'new_m_norm_ffn_g': 'new_m', 'new_m_conv_pw1_w': 'new_m', 'new_m_conv_pw1_b': 'new_m', 'new_m_conv_dw_w': 'new_m', 'new_m_conv_dw_b': 'new_m', 'new_m_conv_ln_g': 'new_m', 'new_m_conv_ln_b': 'new_m', 'new_m_conv_pw2_w': 'new_m', 'new_m_conv_pw2_b': 'new_m', 'new_m_kv_mod_w': 'new_m', 'new_m_kv_mod_b': 'new_m', 'new_m_kv_norm_g': 'new_m', 'new_m_w_kv': 'new_m', 'new_m_k_norm_g': 'new_m', 'new_m_w_q': 'new_m', 'new_m_q_norm_g': 'new_m', 'new_m_w_o': 'new_m', 'new_m_ffn_up_w': 'new_m', 'new_m_ffn_dw_w': 'new_m', 'new_m_ffn_dw_b': 'new_m', 'new_m_ffn_down_w': 'new_m', 'new_v_mod_w': 'new_v', 'new_v_mod_b': 'new_v', 'new_v_norm_mix_g': 'new_v', 'new_v_norm_ffn_g': 'new_v', 'new_v_conv_pw1_w': 'new_v', 'new_v_conv_pw1_b': 'new_v', 'new_v_conv_dw_w': 'new_v', 'new_v_conv_dw_b': 'new_v', 'new_v_conv_ln_g': 'new_v', 'new_v_conv_ln_b': 'new_v', 'new_v_conv_pw2_w': 'new_v', 'new_v_conv_pw2_b': 'new_v', 'new_v_kv_mod_w': 'new_v', 'new_v_kv_mod_b': 'new_v', 'new_v_kv_norm_g': 'new_v', 'new_v_w_kv': 'new_v', 'new_v_k_norm_g': 'new_v', 'new_v_w_q': 'new_v', 'new_v_q_norm_g': 'new_v', 'new_v_w_o': 'new_v', 'new_v_ffn_up_w': 'new_v', 'new_v_ffn_dw_w': 'new_v', 'new_v_ffn_dw_b': 'new_v', 'new_v_ffn_down_w': 'new_v'}


def _forward(args):
    return _fwd_reference(*[args[k] for k in FWD_PARAMS])


def _output_shape():
    def fwd():
        inp = _fwd_setup_inputs(0)
        return _fwd_reference(*[inp[k] for k in FWD_PARAMS])
    out = _jax.eval_shape(fwd)
    return out.shape, out.dtype

N_MICROBATCH = 1
ADAM_LR = 0.001
ADAM_B1 = 0.9
ADAM_B2 = 0.999
ADAM_EPS = 1e-08
ADAM_WD = 0.01
ADAM_STEP = 10
PER_EXAMPLE_BATCH_AXIS = {'x': 0, 'c': 0, 'positions': 0, 'loss_target': 0}
SHARED_INPUTS = []
_WEIGHT_DTYPES = {'mod_w': _jnp.float32, 'mod_b': _jnp.float32, 'norm_mix_g': _jnp.float32, 'norm_ffn_g': _jnp.float32, 'conv_pw1_w': _jnp.float32, 'conv_pw1_b': _jnp.float32, 'conv_dw_w': _jnp.float32, 'conv_dw_b': _jnp.float32, 'conv_ln_g': _jnp.float32, 'conv_ln_b': _jnp.float32, 'conv_pw2_w': _jnp.float32, 'conv_pw2_b': _jnp.float32, 'kv_mod_w': _jnp.float32, 'kv_mod_b': _jnp.float32, 'kv_norm_g': _jnp.float32, 'w_kv': _jnp.float32, 'k_norm_g': _jnp.float32, 'w_q': _jnp.float32, 'q_norm_g': _jnp.float32, 'w_o': _jnp.float32, 'ffn_up_w': _jnp.float32, 'ffn_dw_w': _jnp.float32, 'ffn_dw_b': _jnp.float32, 'ffn_down_w': _jnp.float32}
MOMENT_SCALE = {'mod_w': 6.135174e-01, 'mod_b': 1.661471e+00, 'norm_mix_g': 2.950189e-02, 'norm_ffn_g': 3.216289e+00, 'conv_pw1_w': 4.971136e-02, 'conv_pw1_b': 3.022598e-01, 'conv_dw_w': 7.663871e-02, 'conv_dw_b': 6.574742e-01, 'conv_ln_g': 1.349124e+00, 'conv_ln_b': 9.192088e-01, 'conv_pw2_w': 1.530053e-01, 'conv_pw2_b': 7.897011e-01, 'kv_mod_w': 1.120075e-01, 'kv_mod_b': 2.206591e-01, 'kv_norm_g': 3.013242e-02, 'w_kv': 3.868426e-02, 'k_norm_g': 2.911872e-01, 'w_q': 9.263861e-03, 'q_norm_g': 2.904974e-01, 'w_o': 5.699913e-02, 'ffn_up_w': 5.814987e-02, 'ffn_dw_w': 3.773078e-01, 'ffn_dw_b': 4.192446e-01, 'ffn_down_w': 5.480056e-02}


def _to_microbatches(a, axis):
    t = _jnp.moveaxis(a, axis, 0)
    t = t.reshape((N_MICROBATCH, t.shape[0] // N_MICROBATCH) + t.shape[1:])
    return _jnp.moveaxis(t, 1, axis + 1)


def setup_inputs(seed: int = 0) -> dict:
    inp = _fwd_setup_inputs(seed)
    key = _jax.random.fold_in(_jax.random.key(seed), 7919)
    shape, _ = _output_shape()
    out = dict(inp)
    out["loss_target"] = _jax.random.normal(_jax.random.fold_in(key, 0), shape, _jnp.float32)
    for i, name in enumerate(TWIN_WEIGHTS):
        w = inp[name].astype(_jnp.float32)
        if MOMENT_SCALE is None:
            s = _jnp.sqrt(_jnp.mean(_jnp.square(w)) + 1e-30)
        else:
            s = MOMENT_SCALE[name]
        km, kv = _jax.random.split(_jax.random.fold_in(key, i + 1))
        out[name] = w
        out["m_" + name] = s * _jax.random.normal(km, w.shape, _jnp.float32)
        out["v_" + name] = (s * s) * _jax.random.uniform(kv, w.shape, _jnp.float32, 0.5, 1.5)
    if N_MICROBATCH > 1:
        for name, axis in PER_EXAMPLE_BATCH_AXIS.items():
            out[name] = _to_microbatches(out[name], axis)
    return {'x': out['x'], 'c': out['c'], 'positions': out['positions'], 'mod_w': out['mod_w'], 'mod_b': out['mod_b'], 'norm_mix_g': out['norm_mix_g'], 'norm_ffn_g': out['norm_ffn_g'], 'conv_pw1_w': out['conv_pw1_w'], 'conv_pw1_b': out['conv_pw1_b'], 'conv_dw_w': out['conv_dw_w'], 'conv_dw_b': out['conv_dw_b'], 'conv_ln_g': out['conv_ln_g'], 'conv_ln_b': out['conv_ln_b'], 'conv_pw2_w': out['conv_pw2_w'], 'conv_pw2_b': out['conv_pw2_b'], 'kv_mod_w': out['kv_mod_w'], 'kv_mod_b': out['kv_mod_b'], 'kv_norm_g': out['kv_norm_g'], 'w_kv': out['w_kv'], 'k_norm_g': out['k_norm_g'], 'w_q': out['w_q'], 'q_norm_g': out['q_norm_g'], 'w_o': out['w_o'], 'ffn_up_w': out['ffn_up_w'], 'ffn_dw_w': out['ffn_dw_w'], 'ffn_dw_b': out['ffn_dw_b'], 'ffn_down_w': out['ffn_down_w'], 'loss_target': out['loss_target'], 'm_mod_w': out['m_mod_w'], 'm_mod_b': out['m_mod_b'], 'm_norm_mix_g': out['m_norm_mix_g'], 'm_norm_ffn_g': out['m_norm_ffn_g'], 'm_conv_pw1_w': out['m_conv_pw1_w'], 'm_conv_pw1_b': out['m_conv_pw1_b'], 'm_conv_dw_w': out['m_conv_dw_w'], 'm_conv_dw_b': out['m_conv_dw_b'], 'm_conv_ln_g': out['m_conv_ln_g'], 'm_conv_ln_b': out['m_conv_ln_b'], 'm_conv_pw2_w': out['m_conv_pw2_w'], 'm_conv_pw2_b': out['m_conv_pw2_b'], 'm_kv_mod_w': out['m_kv_mod_w'], 'm_kv_mod_b': out['m_kv_mod_b'], 'm_kv_norm_g': out['m_kv_norm_g'], 'm_w_kv': out['m_w_kv'], 'm_k_norm_g': out['m_k_norm_g'], 'm_w_q': out['m_w_q'], 'm_q_norm_g': out['m_q_norm_g'], 'm_w_o': out['m_w_o'], 'm_ffn_up_w': out['m_ffn_up_w'], 'm_ffn_dw_w': out['m_ffn_dw_w'], 'm_ffn_dw_b': out['m_ffn_dw_b'], 'm_ffn_down_w': out['m_ffn_down_w'], 'v_mod_w': out['v_mod_w'], 'v_mod_b': out['v_mod_b'], 'v_norm_mix_g': out['v_norm_mix_g'], 'v_norm_ffn_g': out['v_norm_ffn_g'], 'v_conv_pw1_w': out['v_conv_pw1_w'], 'v_conv_pw1_b': out['v_conv_pw1_b'], 'v_conv_dw_w': out['v_conv_dw_w'], 'v_conv_dw_b': out['v_conv_dw_b'], 'v_conv_ln_g': out['v_conv_ln_g'], 'v_conv_ln_b': out['v_conv_ln_b'], 'v_conv_pw2_w': out['v_conv_pw2_w'], 'v_conv_pw2_b': out['v_conv_pw2_b'], 'v_kv_mod_w': out['v_kv_mod_w'], 'v_kv_mod_b': out['v_kv_mod_b'], 'v_kv_norm_g': out['v_kv_norm_g'], 'v_w_kv': out['v_w_kv'], 'v_k_norm_g': out['v_k_norm_g'], 'v_w_q': out['v_w_q'], 'v_q_norm_g': out['v_q_norm_g'], 'v_w_o': out['v_w_o'], 'v_ffn_up_w': out['v_ffn_up_w'], 'v_ffn_dw_w': out['v_ffn_dw_w'], 'v_ffn_dw_b': out['v_ffn_dw_b'], 'v_ffn_down_w': out['v_ffn_down_w']}


def _loss(weights, diff, rest, loss_target):
    with _jax.named_scope("forward"):
        args = {**rest, TWIN_DIFF_INPUT: diff, **{k: w.astype(_WEIGHT_DTYPES[k]) for k, w in weights.items()}}
        y = _forward(args)
    with _jax.named_scope("loss_head"):
        err = _jnp.square(y.astype(_jnp.float32) - loss_target)
        return 0.5 * _jnp.sum(_jnp.mean(err, axis=-1)) if err.ndim else 0.5 * err


def _adamw(w, g, m, v):
    m = ADAM_B1 * m + (1.0 - ADAM_B1) * g
    v = ADAM_B2 * v + (1.0 - ADAM_B2) * _jnp.square(g)
    m_hat = m / (1.0 - ADAM_B1 ** ADAM_STEP)
    v_hat = v / (1.0 - ADAM_B2 ** ADAM_STEP)
    delta = -ADAM_LR * (m_hat / (_jnp.sqrt(v_hat) + ADAM_EPS) + ADAM_WD * w)
    return delta, m, v


def reference(x, c, positions, mod_w, mod_b, norm_mix_g, norm_ffn_g, conv_pw1_w, conv_pw1_b, conv_dw_w, conv_dw_b, conv_ln_g, conv_ln_b, conv_pw2_w, conv_pw2_b, kv_mod_w, kv_mod_b, kv_norm_g, w_kv, k_norm_g, w_q, q_norm_g, w_o, ffn_up_w, ffn_dw_w, ffn_dw_b, ffn_down_w, loss_target, m_mod_w, m_mod_b, m_norm_mix_g, m_norm_ffn_g, m_conv_pw1_w, m_conv_pw1_b, m_conv_dw_w, m_conv_dw_b, m_conv_ln_g, m_conv_ln_b, m_conv_pw2_w, m_conv_pw2_b, m_kv_mod_w, m_kv_mod_b, m_kv_norm_g, m_w_kv, m_k_norm_g, m_w_q, m_q_norm_g, m_w_o, m_ffn_up_w, m_ffn_dw_w, m_ffn_dw_b, m_ffn_down_w, v_mod_w, v_mod_b, v_norm_mix_g, v_norm_ffn_g, v_conv_pw1_w, v_conv_pw1_b, v_conv_dw_w, v_conv_dw_b, v_conv_ln_g, v_conv_ln_b, v_conv_pw2_w, v_conv_pw2_b, v_kv_mod_w, v_kv_mod_b, v_kv_norm_g, v_w_kv, v_k_norm_g, v_w_q, v_q_norm_g, v_w_o, v_ffn_up_w, v_ffn_dw_w, v_ffn_dw_b, v_ffn_down_w):
    given = dict(x=x, c=c, positions=positions, mod_w=mod_w, mod_b=mod_b, norm_mix_g=norm_mix_g, norm_ffn_g=norm_ffn_g, conv_pw1_w=conv_pw1_w, conv_pw1_b=conv_pw1_b, conv_dw_w=conv_dw_w, conv_dw_b=conv_dw_b, conv_ln_g=conv_ln_g, conv_ln_b=conv_ln_b, conv_pw2_w=conv_pw2_w, conv_pw2_b=conv_pw2_b, kv_mod_w=kv_mod_w, kv_mod_b=kv_mod_b, kv_norm_g=kv_norm_g, w_kv=w_kv, k_norm_g=k_norm_g, w_q=w_q, q_norm_g=q_norm_g, w_o=w_o, ffn_up_w=ffn_up_w, ffn_dw_w=ffn_dw_w, ffn_dw_b=ffn_dw_b, ffn_down_w=ffn_down_w, loss_target=loss_target, m_mod_w=m_mod_w, m_mod_b=m_mod_b, m_norm_mix_g=m_norm_mix_g, m_norm_ffn_g=m_norm_ffn_g, m_conv_pw1_w=m_conv_pw1_w, m_conv_pw1_b=m_conv_pw1_b, m_conv_dw_w=m_conv_dw_w, m_conv_dw_b=m_conv_dw_b, m_conv_ln_g=m_conv_ln_g, m_conv_ln_b=m_conv_ln_b, m_conv_pw2_w=m_conv_pw2_w, m_conv_pw2_b=m_conv_pw2_b, m_kv_mod_w=m_kv_mod_w, m_kv_mod_b=m_kv_mod_b, m_kv_norm_g=m_kv_norm_g, m_w_kv=m_w_kv, m_k_norm_g=m_k_norm_g, m_w_q=m_w_q, m_q_norm_g=m_q_norm_g, m_w_o=m_w_o, m_ffn_up_w=m_ffn_up_w, m_ffn_dw_w=m_ffn_dw_w, m_ffn_dw_b=m_ffn_dw_b, m_ffn_down_w=m_ffn_down_w, v_mod_w=v_mod_w, v_mod_b=v_mod_b, v_norm_mix_g=v_norm_mix_g, v_norm_ffn_g=v_norm_ffn_g, v_conv_pw1_w=v_conv_pw1_w, v_conv_pw1_b=v_conv_pw1_b, v_conv_dw_w=v_conv_dw_w, v_conv_dw_b=v_conv_dw_b, v_conv_ln_g=v_conv_ln_g, v_conv_ln_b=v_conv_ln_b, v_conv_pw2_w=v_conv_pw2_w, v_conv_pw2_b=v_conv_pw2_b, v_kv_mod_w=v_kv_mod_w, v_kv_mod_b=v_kv_mod_b, v_kv_norm_g=v_kv_norm_g, v_w_kv=v_w_kv, v_k_norm_g=v_k_norm_g, v_w_q=v_w_q, v_q_norm_g=v_q_norm_g, v_w_o=v_w_o, v_ffn_up_w=v_ffn_up_w, v_ffn_dw_w=v_ffn_dw_w, v_ffn_dw_b=v_ffn_dw_b, v_ffn_down_w=v_ffn_down_w)
    weights = {n: given[n] for n in TWIN_WEIGHTS}
    shared = {n: given[n] for n in SHARED_INPUTS}
    per_example = {n: given[n] for n in ['x', 'c', 'positions']}
    grad_fn = _jax.value_and_grad(_loss, argnums=(0, 1))

    def one_microbatch(ex, loss_target):
        ex = dict(ex)
        diff = ex.pop(TWIN_DIFF_INPUT)
        return grad_fn(weights, diff, {**shared, **ex}, loss_target)

    if N_MICROBATCH == 1:
        loss, (grad_w, grad_x) = one_microbatch(per_example, given["loss_target"])
    else:
        def body(carry, xs):
            loss_sum, grad_sum = carry
            l_k, (gw_k, gx_k) = one_microbatch(xs[0], xs[1])
            with _jax.named_scope("update"):
                return (loss_sum + l_k, _jax.tree.map(_jnp.add, grad_sum, gw_k)), gx_k

        init = (_jnp.zeros((), _jnp.float32), _jax.tree.map(_jnp.zeros_like, weights))
        (loss, grad_w), grad_x = _jax.lax.scan(body, init, (per_example, given["loss_target"]))
    with _jax.named_scope("update"):
        delta_w, new_m, new_v = {}, {}, {}
        for n in TWIN_WEIGHTS:
            delta_w[n], new_m[n], new_v[n] = _adamw(weights[n], grad_w[n], given["m_" + n], given["v_" + n])
    return (loss, grad_x, *[grad_w[n] for n in TWIN_WEIGHTS], *[delta_w[n] for n in TWIN_WEIGHTS],
            *[new_m[n] for n in TWIN_WEIGHTS], *[new_v[n] for n in TWIN_WEIGHTS])
```

```python
import functools
import math

import jax
import jax.numpy as jnp
from jax import lax
from jax.experimental import pallas as pl
from jax.experimental.pallas import tpu as pltpu

F32 = jnp.float32
BF16 = jnp.bfloat16
EPS = 1e-6
NEG = -1e30
HEAD_DIM = 128
ROT_DIM = 32
ROPE_THETA = 500000.0
BLK = 128
DILATIONS = (1, 4, 16)
N_GROUPS = 3
CONV_K = 31
FFN_K = 3
ADAM_LR, ADAM_B1, ADAM_B2, ADAM_EPS, ADAM_WD, ADAM_STEP = 0.001, 0.9, 0.999, 1e-08, 0.01, 10
N_DEV = 8
MESH = pl.DeviceIdType.MESH
VMEM_LIMIT_MB = 56
ROW_TILE = 256
ATTN_CHUNK = 2048


def _pick(n, pref, mult=128):
    best = None
    d = mult
    while d <= min(n, pref):
        if n % d == 0:
            best = d
        d += mult
    return best if best is not None else n


def _call(body, *, name, grid, in_specs, out_specs, out_shape, scratch=(), nsp=0):
    return pl.pallas_call(
        body,
        name=name,
        grid_spec=pltpu.PrefetchScalarGridSpec(
            num_scalar_prefetch=nsp, grid=grid, in_specs=in_specs, out_specs=out_specs, scratch_shapes=list(scratch)
        ),
        out_shape=out_shape,
        compiler_params=pltpu.CompilerParams(
            dimension_semantics=("arbitrary",) * len(grid), vmem_limit_bytes=VMEM_LIMIT_MB << 20
        ),
        interpret=False,
    )


def _sds(shape, dtype):
    return jax.ShapeDtypeStruct(shape, dtype)


def _acc(ref, val, i):
    @pl.when(i == 0)
    def _():
        ref[...] = val

    @pl.when(i > 0)
    def _():
        ref[...] += val


def _colsum(v):
    return jnp.sum(v, axis=0, keepdims=True)


def _silu(v):
    return v * jax.nn.sigmoid(v)


def _dsilu(v):
    s = jax.nn.sigmoid(v)
    return s * (1.0 + v * (1.0 - s))


_DIMS = {"nn": (((1,), (0,)), ((), ())), "nt": (((1,), (1,)), ((), ())), "tn": (((0,), (0,)), ((), ()))}


def _matmul(a, b, mode, out_dtype, name):
    a_halves = a.shape[0] if a.ndim == 3 else 0
    b_halves = b.shape[0] if b.ndim == 3 else 0
    if mode == "nn":
        (m, c), (_, n) = a.shape, b.shape
    elif mode == "nt":
        m, c = (a.shape[1], a.shape[0] * a.shape[2]) if a_halves else a.shape
        n = b.shape[0]
    else:
        c, m = a.shape
        n = b.shape[0] * b.shape[2] if b_halves else b.shape[1]
    tm = _pick(m, 1024)
    tn = _pick(n // b_halves, 1024) if b_halves else _pick(n, 1024)
    c_cap = 2048 if mode == "tn" else 2816
    tc = _pick(c // a_halves, c_cap) if a_halves else _pick(c, c_cap)
    nk = c // tc
    if a_halves:
        per_a = c // a_halves // tc
        a_spec = pl.BlockSpec((None, tm, tc), lambda i, j, k: (k // per_a, i, k % per_a))
    else:
        a_spec = {"nn": pl.BlockSpec((tm, tc), lambda i, j, k: (i, k)), "nt": pl.BlockSpec((tm, tc), lambda i, j, k: (i, k)),
                  "tn": pl.BlockSpec((tc, tm), lambda i, j, k: (k, i))}[mode]
    if b_halves:
        per_b = n // b_halves // tn
        b_spec = pl.BlockSpec((None, tc, tn), lambda i, j, k: (j // per_b, k, j % per_b))
    else:
        b_spec = {"nn": pl.BlockSpec((tc, tn), lambda i, j, k: (k, j)), "nt": pl.BlockSpec((tn, tc), lambda i, j, k: (j, k)),
                  "tn": pl.BlockSpec((tc, tn), lambda i, j, k: (k, j))}[mode]
    dims = _DIMS[mode]

    def body(a_ref, b_ref, o_ref, acc_ref):
        k = pl.program_id(2)
        p = lax.dot_general(a_ref[...], b_ref[...], dims, preferred_element_type=F32)
        if nk == 1:
            o_ref[...] = p.astype(out_dtype)
        else:
            @pl.when(k == 0)
            def _():
                acc_ref[...] = p

            @pl.when(k > 0)
            def _():
                acc_ref[...] += p

            @pl.when(k == nk - 1)
            def _():
                o_ref[...] = acc_ref[...].astype(out_dtype)

    return _call(
        body, name=name, grid=(m // tm, n // tn, nk), in_specs=[a_spec, b_spec],
        out_specs=pl.BlockSpec((tm, tn), lambda i, j, k: (i, j)), out_shape=_sds((m, n), out_dtype),
        scratch=[pltpu.VMEM((tm, tn), F32)],
    )(a, b)


def _row_spec(tr, w):
    return pl.BlockSpec((tr, w), lambda i: (i, 0))


def _vec_spec(w):
    return pl.BlockSpec((1, w), lambda i: (0, 0))


def _resid_mod(x, prev, mods, target, name):
    s, d = x.shape
    tr = _pick(s, ROW_TILE, 8)
    n_mod = len(mods)

    def body(*refs):
        it = iter(refs)
        x_ref = next(it)
        if prev is not None:
            y_ref, yb_ref, gate_ref = next(it), next(it), next(it)
        mod_refs = [(next(it), next(it), next(it)) for _ in range(n_mod)]
        if target is not None:
            t_ref = next(it)
        if prev is not None:
            xo_ref = next(it)
        h_refs = [next(it) for _ in range(n_mod)]
        i = pl.program_id(0)
        xv = x_ref[...]
        if prev is not None:
            xv = xv + gate_ref[...] * (y_ref[...] + yb_ref[...])
            xo_ref[...] = xv
        if n_mod:
            nrm = xv * lax.rsqrt(jnp.mean(xv * xv, axis=-1, keepdims=True) + EPS)
            for (g_ref, sc_ref, sh_ref), h_ref in zip(mod_refs, h_refs):
                h_ref[...] = (nrm * g_ref[...] * (1.0 + sc_ref[...]) + sh_ref[...]).astype(BF16)
        if target is not None:
            dx_ref, loss_ref = next(it), next(it)
            err = xv - t_ref[...]
            dx_ref[...] = err * (1.0 / d)
            _acc(loss_ref, _colsum(err * err) * (0.5 / d), i)

    ins, in_specs = [x], [_row_spec(tr, d)]
    if prev is not None:
        ins += list(prev)
        in_specs += [_row_spec(tr, d), _vec_spec(d), _vec_spec(d)]
    for g, sc, sh in mods:
        ins += [g, sc, sh]
        in_specs += [_vec_spec(d)] * 3
    if target is not None:
        ins.append(target)
        in_specs.append(_row_spec(tr, d))
    out_shape, out_specs = [], []
    if prev is not None:
        out_shape.append(_sds((s, d), F32))
        out_specs.append(_row_spec(tr, d))
    for _ in mods:
        out_shape.append(_sds((s, d), BF16))
        out_specs.append(_row_spec(tr, d))
    if target is not None:
        out_shape += [_sds((s, d), F32), _sds((1, d), F32)]
        out_specs += [_row_spec(tr, d), _vec_spec(d)]
    return _call(body, name=name, grid=(s // tr,), in_specs=in_specs, out_specs=out_specs, out_shape=out_shape)(*ins)


def _bwd_step(dx_up, x, mods, prev, name):
    s, d = x.shape
    tr = _pick(s, ROW_TILE, 8)
    n_mod = len(mods)

    def body(*refs):
        it = iter(refs)
        dxu_ref, x_ref = next(it), next(it)
        mod_refs = [(next(it), next(it), next(it)) for _ in range(n_mod)]
        if prev is not None:
            y_ref, yb_ref, gate_ref = next(it), next(it), next(it)
        dx_ref = next(it)
        acc_refs = [(next(it), next(it), next(it)) for _ in range(n_mod)]
        i = pl.program_id(0)
        dx = dxu_ref[...]
        if n_mod:
            xv = x_ref[...]
            rstd = lax.rsqrt(jnp.mean(xv * xv, axis=-1, keepdims=True) + EPS)
            nrm = xv * rstd
        for (dh_ref, g_ref, sc_ref), (dsh_ref, dsc_ref, dg_ref) in zip(mod_refs, acc_refs):
            dh = dh_ref[...]
            gv, one_sc = g_ref[...], 1.0 + sc_ref[...]
            _acc(dsh_ref, _colsum(dh), i)
            t = dh * nrm
            _acc(dsc_ref, _colsum(t) * gv, i)
            _acc(dg_ref, _colsum(t) * one_sc, i)
            dn = dh * (gv * one_sc)
            dx = dx + rstd * (dn - nrm * jnp.mean(dn * nrm, axis=-1, keepdims=True))
        dx_ref[...] = dx
        if prev is not None:
            dy_ref, dgate_ref, dyb_ref = next(it), next(it), next(it)
            dy = gate_ref[...] * dx
            dy_ref[...] = dy.astype(BF16)
            _acc(dgate_ref, _colsum(dx * (y_ref[...] + yb_ref[...])), i)
            _acc(dyb_ref, _colsum(dy), i)

    ins, in_specs = [dx_up, x], [_row_spec(tr, d)] * 2
    for dh, g, sc in mods:
        ins += [dh, g, sc]
        in_specs += [_row_spec(tr, d), _vec_spec(d), _vec_spec(d)]
    if prev is not None:
        ins += list(prev)
        in_specs += [_row_spec(tr, d), _vec_spec(d), _vec_spec(d)]
    out_shape, out_specs = [_sds((s, d), F32)], [_row_spec(tr, d)]
    for _ in mods:
        out_shape += [_sds((1, d), F32)] * 3
        out_specs += [_vec_spec(d)] * 3
    if prev is not None:
        out_shape += [_sds((s, d), BF16), _sds((1, d), F32), _sds((1, d), F32)]
        out_specs += [_row_spec(tr, d), _vec_spec(d), _vec_spec(d)]
    return _call(body, name=name, grid=(s // tr,), in_specs=in_specs, out_specs=out_specs, out_shape=out_shape)(*ins)


def _glu_fwd(u, bias, name):
    s, d2 = u.shape
    d = d2 // 2
    tr = _pick(s, ROW_TILE, 8)

    def body(u_ref, b_ref, o_ref):
        uv = u_ref[...] + b_ref[...]
        o_ref[...] = uv[:, :d] * jax.nn.sigmoid(uv[:, d:])

    return _call(body, name=name, grid=(s // tr,), in_specs=[_row_spec(tr, d2), _vec_spec(d2)],
                 out_specs=_row_spec(tr, d), out_shape=_sds((s, d), F32))(u, bias)


def _glu_bwd(dglu, u, bias, name):
    s, d2 = u.shape
    d = d2 // 2
    tr = _pick(s, ROW_TILE, 8)

    def body(dg_ref, u_ref, b_ref, du_ref, db_ref):
        i = pl.program_id(0)
        uv = u_ref[...] + b_ref[...]
        a, sg = uv[:, :d], jax.nn.sigmoid(uv[:, d:])
        dg = dg_ref[...]
        da = dg * sg
        dgt = dg * a * sg * (1.0 - sg)
        du_ref[:, :d] = da.astype(BF16)
        du_ref[:, d:] = dgt.astype(BF16)
        _acc(db_ref, jnp.concatenate([_colsum(da), _colsum(dgt)], axis=1), i)

    return _call(body, name=name, grid=(s // tr,), in_specs=[_row_spec(tr, d), _row_spec(tr, d2), _vec_spec(d2)],
                 out_specs=[_row_spec(tr, d2), _vec_spec(d2)], out_shape=[_sds((s, d2), BF16), _sds((1, d2), F32)])(dglu, u, bias)


def _halo_rows(k):
    return -(-(k - 1) // 8) * 8


def _dwconv_fwd(x, w, b, name):
    s, c = x.shape
    kk = w.shape[0]
    hb = _halo_rows(kk)
    tr, tc = _pick(s, ROW_TILE, hb), _pick(c, 512)
    per = tr // hb

    def body(xp_ref, x_ref, w_ref, b_ref, o_ref, cat_ref):
        i = pl.program_id(1)
        cat_ref[0:hb, :] = jnp.where(i > 0, xp_ref[...], 0.0)
        cat_ref[hb:hb + tr, :] = x_ref[...]
        acc = jnp.zeros((tr, tc), F32) + b_ref[...]
        for k in range(kk):
            acc = acc + w_ref[k:k + 1, :] * cat_ref[pl.ds(hb - (kk - 1) + k, tr), :]
        o_ref[...] = acc

    return _call(
        body, name=name, grid=(c // tc, s // tr),
        in_specs=[pl.BlockSpec((hb, tc), lambda j, i: (jnp.maximum(i * per - 1, 0), j)), pl.BlockSpec((tr, tc), lambda j, i: (i, j)),
                  pl.BlockSpec((kk, tc), lambda j, i: (0, j)), pl.BlockSpec((1, tc), lambda j, i: (0, j))],
        out_specs=pl.BlockSpec((tr, tc), lambda j, i: (i, j)), out_shape=_sds((s, c), F32),
        scratch=[pltpu.VMEM((tr + hb, tc), F32)],
    )(x, x, w, b)


def _dwconv_bwd(dy, x, w, name):
    s, c = x.shape
    kk = w.shape[0]
    hb = _halo_rows(kk)
    tr, tc = _pick(s, ROW_TILE, hb), _pick(c, 512)
    per, nt = tr // hb, s // tr

    def body(xp_ref, x_ref, dy_ref, dyn_ref, w_ref, dx_ref, dw_ref, db_ref, xcat_ref, dcat_ref):
        i = pl.program_id(1)
        xcat_ref[0:hb, :] = jnp.where(i > 0, xp_ref[...], 0.0)
        xcat_ref[hb:hb + tr, :] = x_ref[...]
        dyv = dy_ref[...]
        dcat_ref[0:tr, :] = dyv
        dcat_ref[tr:tr + hb, :] = jnp.where(i < nt - 1, dyn_ref[...], 0.0)
        acc = jnp.zeros((tr, tc), F32)
        for k in range(kk):
            acc = acc + w_ref[k:k + 1, :] * dcat_ref[pl.ds(kk - 1 - k, tr), :]
        dx_ref[...] = acc

        @pl.when(i == 0)
        def _():
            dw_ref[...] = jnp.zeros_like(dw_ref)
            db_ref[...] = jnp.zeros_like(db_ref)

        db_ref[...] += _colsum(dyv)
        for k in range(kk):
            dw_ref[k:k + 1, :] += _colsum(dyv * xcat_ref[pl.ds(hb - (kk - 1) + k, tr), :])

    return _call(
        body, name=name, grid=(c // tc, nt),
        in_specs=[pl.BlockSpec((hb, tc), lambda j, i: (jnp.maximum(i * per - 1, 0), j)), pl.BlockSpec((tr, tc), lambda j, i: (i, j)),
                  pl.BlockSpec((tr, tc), lambda j, i: (i, j)),
                  pl.BlockSpec((hb, tc), lambda j, i: (jnp.minimum((i + 1) * per, s // hb - 1), j)),
                  pl.BlockSpec((kk, tc), lambda j, i: (0, j))],
        out_specs=[pl.BlockSpec((tr, tc), lambda j, i: (i, j)), pl.BlockSpec((kk, tc), lambda j, i: (0, j)),
                   pl.BlockSpec((1, tc), lambda j, i: (0, j))],
        out_shape=[_sds((s, c), F32), _sds((kk, c), F32), _sds((1, c), F32)],
        scratch=[pltpu.VMEM((tr + hb, tc), F32), pltpu.VMEM((tr + hb, tc), F32)],
    )(x, x, dy, dy, w)


def _ln_silu_fwd(x, g, b, name):
    s, d = x.shape
    tr = _pick(s, ROW_TILE, 8)

    def body(x_ref, g_ref, b_ref, o_ref):
        xv = x_ref[...]
        mu = jnp.mean(xv, axis=-1, keepdims=True)
        xc = xv - mu
        ln = xc * lax.rsqrt(jnp.mean(xc * xc, axis=-1, keepdims=True) + EPS) * g_ref[...] + b_ref[...]
        o_ref[...] = _silu(ln).astype(BF16)

    return _call(body, name=name, grid=(s // tr,), in_specs=[_row_spec(tr, d), _vec_spec(d), _vec_spec(d)],
                 out_specs=_row_spec(tr, d), out_shape=_sds((s, d), BF16))(x, g, b)


def _ln_silu_bwd(dact, x, g, b, name):
    s, d = x.shape
    tr = _pick(s, ROW_TILE, 8)

    def body(da_ref, x_ref, g_ref, b_ref, dx_ref, dg_ref, db_ref):
        i = pl.program_id(0)
        xv = x_ref[...]
        mu = jnp.mean(xv, axis=-1, keepdims=True)
        xc = xv - mu
        rstd = lax.rsqrt(jnp.mean(xc * xc, axis=-1, keepdims=True) + EPS)
        xh = xc * rstd
        ln = xh * g_ref[...] + b_ref[...]
        dln = da_ref[...] * _dsilu(ln)
        _acc(dg_ref, _colsum(dln * xh), i)
        _acc(db_ref, _colsum(dln), i)
        dxh = dln * g_ref[...]
        dx_ref[...] = rstd * (dxh - jnp.mean(dxh, axis=-1, keepdims=True) - xh * jnp.mean(dxh * xh, axis=-1, keepdims=True))

    return _call(body, name=name, grid=(s // tr,), in_specs=[_row_spec(tr, d), _row_spec(tr, d), _vec_spec(d), _vec_spec(d)],
                 out_specs=[_row_spec(tr, d), _vec_spec(d), _vec_spec(d)],
                 out_shape=[_sds((s, d), F32), _sds((1, d), F32), _sds((1, d), F32)])(dact, x, g, b)


def _ffn_act_fwd(u, w, b, name):
    s, f2 = u.shape
    f = f2 // 2
    hb = 8
    tr, tc = _pick(s, ROW_TILE, hb), _pick(f, 1408)
    per, nf = tr // hb, f // tc

    def body(gp_ref, g_ref, v_ref, w_ref, b_ref, z_ref, cat_ref):
        i = pl.program_id(1)
        cat_ref[0:hb, :] = jnp.where(i > 0, gp_ref[...], 0.0)
        cat_ref[hb:hb + tr, :] = g_ref[...]
        gc = jnp.zeros((tr, tc), F32) + b_ref[...]
        for k in range(FFN_K):
            gc = gc + w_ref[k:k + 1, :] * cat_ref[pl.ds(hb - (FFN_K - 1) + k, tr), :]
        z_ref[...] = (_silu(gc) * v_ref[...]).astype(BF16)

    return _call(
        body, name=name, grid=(nf, s // tr),
        in_specs=[pl.BlockSpec((hb, tc), lambda j, i: (jnp.maximum(i * per - 1, 0), j)), pl.BlockSpec((tr, tc), lambda j, i: (i, j)),
                  pl.BlockSpec((tr, tc), lambda j, i: (i, nf + j)), pl.BlockSpec((FFN_K, tc), lambda j, i: (0, j)),
                  pl.BlockSpec((1, tc), lambda j, i: (0, j))],
        out_specs=pl.BlockSpec((tr, tc), lambda j, i: (i, j)), out_shape=_sds((s, f), BF16),
        scratch=[pltpu.VMEM((tr + hb, tc), F32)],
    )(u, u, u, w, b)


def _ffn_act_bwd(dz, u, w, b, name):
    s, f2 = u.shape
    f = f2 // 2
    hb = 8
    tr, tc = _pick(s, ROW_TILE, hb), _pick(f, 1408)
    per, nf, nt = tr // hb, f // tc, s // tr
    ext = tr + hb

    def body(gp_ref, g_ref, gn_ref, v_ref, vn_ref, dz_ref, dzn_ref, w_ref, b_ref, du_ref, dw_ref, db_ref, gcat_ref, dgc_ref):
        i = pl.program_id(1)
        last = i == nt - 1
        gcat_ref[0:hb, :] = jnp.where(i > 0, gp_ref[...], 0.0)
        gcat_ref[hb:hb + tr, :] = g_ref[...]
        gcat_ref[hb + tr:hb + tr + hb, :] = gn_ref[...]
        gc = jnp.zeros((ext, tc), F32) + b_ref[...]
        for k in range(FFN_K):
            gc = gc + w_ref[k:k + 1, :] * gcat_ref[pl.ds(hb - (FFN_K - 1) + k, ext), :]
        dz_cur = dz_ref[...]
        du_ref[1] = (dz_cur * _silu(gc[0:tr, :])).astype(BF16)
        dgc_ref[0:tr, :] = dz_cur * v_ref[...] * _dsilu(gc[0:tr, :])
        dgc_ref[tr:ext, :] = jnp.where(last, 0.0, dzn_ref[...] * vn_ref[...] * _dsilu(gc[tr:ext, :]))
        dgt = jnp.zeros((tr, tc), F32)
        for k in range(FFN_K):
            dgt = dgt + w_ref[k:k + 1, :] * dgc_ref[pl.ds(FFN_K - 1 - k, tr), :]
        du_ref[0] = dgt.astype(BF16)

        @pl.when(i == 0)
        def _():
            dw_ref[...] = jnp.zeros_like(dw_ref)
            db_ref[...] = jnp.zeros_like(db_ref)

        dgc_cur = dgc_ref[0:tr, :]
        db_ref[...] += _colsum(dgc_cur)
        for k in range(FFN_K):
            dw_ref[k:k + 1, :] += _colsum(dgc_cur * gcat_ref[pl.ds(hb - (FFN_K - 1) + k, tr), :])

    prev_map = lambda j, i: (jnp.maximum(i * per - 1, 0), j)
    next_map = lambda j, i: (jnp.minimum((i + 1) * per, s // hb - 1), j)
    next_map_v = lambda j, i: (jnp.minimum((i + 1) * per, s // hb - 1), nf + j)
    return _call(
        body, name=name, grid=(nf, nt),
        in_specs=[pl.BlockSpec((hb, tc), prev_map), pl.BlockSpec((tr, tc), lambda j, i: (i, j)), pl.BlockSpec((hb, tc), next_map),
                  pl.BlockSpec((tr, tc), lambda j, i: (i, nf + j)), pl.BlockSpec((hb, tc), next_map_v),
                  pl.BlockSpec((tr, tc), lambda j, i: (i, j)), pl.BlockSpec((hb, tc), next_map),
                  pl.BlockSpec((FFN_K, tc), lambda j, i: (0, j)), pl.BlockSpec((1, tc), lambda j, i: (0, j))],
        out_specs=[pl.BlockSpec((2, tr, tc), lambda j, i: (0, i, j)), pl.BlockSpec((FFN_K, tc), lambda j, i: (0, j)),
                   pl.BlockSpec((1, tc), lambda j, i: (0, j))],
        out_shape=[_sds((2, s, f), BF16), _sds((FFN_K, f), F32), _sds((1, f), F32)],
        scratch=[pltpu.VMEM((tr + 2 * hb, tc), F32), pltpu.VMEM((ext, tc), F32)],
    )(u, u, u, u, u, dz, dz, w, b)


def _rope_tables(pos, freq, sign, name):
    s = pos.shape[0]
    tr = _pick(s, 512, 8)

    def body(p_ref, f_ref, s_ref, c_ref, sn_ref):
        ang = p_ref[...].astype(F32) * f_ref[...]
        c_ref[...] = jnp.cos(ang)
        sn_ref[...] = jnp.sin(ang) * s_ref[...]

    return _call(body, name=name, grid=(s // tr,), in_specs=[pl.BlockSpec((tr, 1), lambda i: (i, 0)), _vec_spec(128), _vec_spec(128)],
                 out_specs=[_row_spec(tr, 128)] * 2, out_shape=[_sds((s, 128), F32)] * 2)(pos, freq, sign)


def _partner(v):
    lane = lax.broadcasted_iota(jnp.int32, v.shape, 1)
    lower = pltpu.roll(v, HEAD_DIM - ROT_DIM // 2, 1)
    upper = jnp.where(lane < ROT_DIM, pltpu.roll(v, ROT_DIM // 2, 1), 0.0)
    return jnp.where(lane < ROT_DIM // 2, lower, upper)


def _qk_norm_rope_fwd(q, kv, gq, gk, ctab, stab, name):
    s, w = q.shape
    tr = _pick(s, 128, 8)
    heads = w // HEAD_DIM

    def body(q_ref, k_ref, gq_ref, gk_ref, c_ref, s_ref, qn_ref, kn_ref):
        cv, sv = c_ref[...], s_ref[...]
        for src, g_ref, dst in ((q_ref, gq_ref, qn_ref), (k_ref, gk_ref, kn_ref)):
            gv = g_ref[...]
            for h in range(heads):
                cols = pl.ds(h * HEAD_DIM, HEAD_DIM)
                xv = src[:, cols]
                nv = xv * lax.rsqrt(jnp.mean(xv * xv, axis=-1, keepdims=True) + EPS) * gv
                dst[:, cols] = nv * cv + _partner(nv) * sv

    return _call(
        body, name=name, grid=(s // tr,),
        in_specs=[_row_spec(tr, w), _row_spec(tr, w), _vec_spec(128), _vec_spec(128), _row_spec(tr, 128), _row_spec(tr, 128)],
        out_specs=[_row_spec(tr, w)] * 2, out_shape=[_sds((s, w), F32)] * 2,
    )(q, kv, gq, gk, ctab, stab)


def _qk_norm_rope_bwd(dqs, dks, dvs, q, kv, gq, gk, ctab, stab, name):
    s, w = q.shape
    gw = w // N_GROUPS
    tr = _pick(s, 128, 8)
    hpg = gw // HEAD_DIM

    def body(*refs):
        dq_refs, dk_refs, dv_refs = refs[0:3], refs[3:6], refs[6:9]
        q_ref, k_ref, gq_ref, gk_ref, c_ref, s_ref, dq_ref, dkv_ref, dgq_ref, dgk_ref = refs[9:]
        i = pl.program_id(0)
        cv, sv = c_ref[...], s_ref[...]
        for d_refs, src, g_ref, dst, dg_ref in ((dq_refs, q_ref, gq_ref, dq_ref, dgq_ref), (dk_refs, k_ref, gk_ref, dkv_ref, dgk_ref)):
            gv = g_ref[...]
            dg = jnp.zeros((1, HEAD_DIM), F32)
            for g in range(N_GROUPS):
                for h in range(hpg):
                    cols = pl.ds(g * gw + h * HEAD_DIM, HEAD_DIM)
                    dout = d_refs[g][:, pl.ds(h * HEAD_DIM, HEAD_DIM)]
                    dn = dout * cv + _partner(dout * sv)
                    xv = src[:, cols]
                    rstd = lax.rsqrt(jnp.mean(xv * xv, axis=-1, keepdims=True) + EPS)
                    xh = xv * rstd
                    dg = dg + _colsum(dn * xh)
                    dxh = dn * gv
                    dst[:, cols] = (rstd * (dxh - xh * jnp.mean(dxh * xh, axis=-1, keepdims=True))).astype(BF16)
            _acc(dg_ref, dg, i)
        for g in range(N_GROUPS):
            dkv_ref[:, pl.ds(w + g * gw, gw)] = dv_refs[g][...].astype(BF16)

    return _call(
        body, name=name, grid=(s // tr,),
        in_specs=[_row_spec(tr, gw)] * 9 + [_row_spec(tr, w), _row_spec(tr, w), _vec_spec(128), _vec_spec(128),
                                            _row_spec(tr, 128), _row_spec(tr, 128)],
        out_specs=[_row_spec(tr, w), _row_spec(tr, 2 * w), _vec_spec(128), _vec_spec(128)],
        out_shape=[_sds((s, w), BF16), _sds((s, 2 * w), BF16), _sds((1, 128), F32), _sds((1, 128), F32)],
    )(*dqs, *dks, *dvs, q, kv, gq, gk, ctab, stab)


def _rows(j, b, r):
    start = j + r * BLK * b
    return pl.ds(start, BLK, stride=r) if r > 1 else pl.ds(start, BLK)


def _dot_nt(a, b):
    return lax.dot_general(a, b, _DIMS["nt"], preferred_element_type=F32)


def _dot_nn(a, b):
    return lax.dot_general(a, b, _DIMS["nn"], preferred_element_type=F32)


def _band_masks():
    qi = lax.broadcasted_iota(jnp.int32, (BLK, BLK), 0)
    kj = lax.broadcasted_iota(jnp.int32, (BLK, BLK), 1)
    return kj <= qi, kj >= qi


def _attn_fwd(qn, kn, kv, g, name):
    s, w = qn.shape
    r = DILATIONS[g]
    gw = w // N_GROUPS
    cr = min(ATTN_CHUNK, s)
    nb = cr // (BLK * r)
    hp = 1
    cw = hp * HEAD_DIM
    gc = gw // cw
    scale = 1.0 / math.sqrt(HEAD_DIM)

    def body(q_ref, kc_ref, kp_ref, vc_ref, vp_ref, o_ref, l_ref):
        n = pl.program_id(1)
        same_m, prev_m = _band_masks()
        prev_first = jnp.logical_and(prev_m, n > 0)
        for h in range(hp):
            cols = pl.ds(h * HEAD_DIM, HEAD_DIM)
            for j in range(r):
                for b in range(nb):
                    rows = _rows(j, b, r)
                    qv = q_ref[rows, cols].astype(BF16)
                    kc, vc = kc_ref[rows, cols].astype(BF16), vc_ref[rows, cols].astype(BF16)
                    if b > 0:
                        rp = _rows(j, b - 1, r)
                        kp, vp, pm = kc_ref[rp, cols].astype(BF16), vc_ref[rp, cols].astype(BF16), prev_m
                    else:
                        rp = _rows(j, nb - 1, r)
                        kp, vp, pm = kp_ref[rp, cols].astype(BF16), vp_ref[rp, cols].astype(BF16), prev_first
                    sd = jnp.where(same_m, _dot_nt(qv, kc) * scale, NEG)
                    so = jnp.where(pm, _dot_nt(qv, kp) * scale, NEG)
                    m = jnp.maximum(jnp.max(sd, axis=-1, keepdims=True), jnp.max(so, axis=-1, keepdims=True))
                    pd, po = jnp.exp(sd - m), jnp.exp(so - m)
                    den = jnp.sum(pd, axis=-1, keepdims=True) + jnp.sum(po, axis=-1, keepdims=True)
                    ov = (_dot_nn(pd.astype(BF16), vc) + _dot_nn(po.astype(BF16), vp)) / den
                    o_ref[rows, cols] = ov
                    l_ref[rows, cols] = jnp.broadcast_to(m + jnp.log(den), (BLK, HEAD_DIM))

    cur = lambda base: (lambda c, n: (n, base + c))
    prv = lambda base: (lambda c, n: (jnp.maximum(n - 1, 0), base + c))
    qb, kb, vb = g * gc, g * gc, (N_GROUPS + g) * gc
    return _call(
        body, name=name, grid=(gc, s // cr),
        in_specs=[pl.BlockSpec((cr, cw), cur(qb)), pl.BlockSpec((cr, cw), cur(kb)), pl.BlockSpec((cr, cw), prv(kb)),
                  pl.BlockSpec((cr, cw), cur(vb)), pl.BlockSpec((cr, cw), prv(vb))],
        out_specs=[pl.BlockSpec((cr, cw), lambda c, n: (n, c))] * 2, out_shape=[_sds((s, gw), F32)] * 2,
    )(qn, kn, kn, kv, kv)


def _attn_combine(os_, lses, name):
    s, gw = os_[0].shape
    tr = _pick(s, ROW_TILE, 8)

    def body(o0, o1, o2, l0, l1, l2, o_ref, l_ref):
        a, b, c = l0[...], l1[...], l2[...]
        m = jnp.maximum(jnp.maximum(a, b), c)
        ea, eb, ec = jnp.exp(a - m), jnp.exp(b - m), jnp.exp(c - m)
        den = ea + eb + ec
        o_ref[...] = ((ea * o0[...] + eb * o1[...] + ec * o2[...]) / den).astype(BF16)
        l_ref[...] = m + jnp.log(den)

    return _call(body, name=name, grid=(s // tr,), in_specs=[_row_spec(tr, gw)] * 6, out_specs=[_row_spec(tr, gw)] * 2,
                 out_shape=[_sds((s, gw), BF16), _sds((s, gw), F32)])(*os_, *lses)


def _attn_delta(do, o, name):
    s, gw = do.shape
    tr = _pick(s, ROW_TILE, 8)

    def body(do_ref, o_ref, d_ref):
        for h in range(gw // HEAD_DIM):
            cols = pl.ds(h * HEAD_DIM, HEAD_DIM)
            t = jnp.sum(do_ref[:, cols] * o_ref[:, cols].astype(F32), axis=-1, keepdims=True)
            d_ref[:, cols] = jnp.broadcast_to(t, (tr, HEAD_DIM))

    return _call(body, name=name, grid=(s // tr,), in_specs=[_row_spec(tr, gw)] * 2, out_specs=_row_spec(tr, gw),
                 out_shape=_sds((s, gw), F32))(do, o)


def _pair_grads(qv, kv_, vv, dov, lse, delta, mask, scale):
    sc = jnp.where(mask, _dot_nt(qv, kv_) * scale, NEG)
    p = jnp.exp(sc - lse)
    ds = p * (_dot_nt(dov, vv) - delta) * scale
    return p, ds


def _attn_bwd_dq(qn, kn, kv, do, lse, delta, g, name):
    s, w = qn.shape
    r = DILATIONS[g]
    gw = w // N_GROUPS
    cr = min(ATTN_CHUNK, s)
    nb = cr // (BLK * r)
    cw = HEAD_DIM
    gc = gw // cw
    scale = 1.0 / math.sqrt(HEAD_DIM)

    def body(q_ref, kc_ref, kp_ref, vc_ref, vp_ref, do_ref, l_ref, d_ref, dq_ref):
        n = pl.program_id(1)
        same_m, prev_m = _band_masks()
        prev_first = jnp.logical_and(prev_m, n > 0)
        for j in range(r):
            for b in range(nb):
                rows = _rows(j, b, r)
                qv, dov = q_ref[rows, :].astype(BF16), do_ref[rows, :].astype(BF16)
                lse, delta = l_ref[rows, :], d_ref[rows, :]
                kc, vc = kc_ref[rows, :].astype(BF16), vc_ref[rows, :].astype(BF16)
                if b > 0:
                    rp = _rows(j, b - 1, r)
                    kp, vp, pm = kc_ref[rp, :].astype(BF16), vc_ref[rp, :].astype(BF16), prev_m
                else:
                    rp = _rows(j, nb - 1, r)
                    kp, vp, pm = kp_ref[rp, :].astype(BF16), vp_ref[rp, :].astype(BF16), prev_first
                _, dsd = _pair_grads(qv, kc, vc, dov, lse, delta, same_m, scale)
                _, dso = _pair_grads(qv, kp, vp, dov, lse, delta, pm, scale)
                dq_ref[rows, :] = _dot_nn(dsd.astype(BF16), kc) + _dot_nn(dso.astype(BF16), kp)

    cur = lambda base: (lambda c, n: (n, base + c))
    prv = lambda base: (lambda c, n: (jnp.maximum(n - 1, 0), base + c))
    qb, vb = g * gc, (N_GROUPS + g) * gc
    own = pl.BlockSpec((cr, cw), lambda c, n: (n, c))
    return _call(
        body, name=name, grid=(gc, s // cr),
        in_specs=[pl.BlockSpec((cr, cw), cur(qb)), pl.BlockSpec((cr, cw), cur(qb)), pl.BlockSpec((cr, cw), prv(qb)),
                  pl.BlockSpec((cr, cw), cur(vb)), pl.BlockSpec((cr, cw), prv(vb)), own, own, own],
        out_specs=own, out_shape=_sds((s, gw), F32),
    )(qn, kn, kn, kv, kv, do, lse, delta)


def _attn_bwd_dkv(qn, kn, kv, do, lse, delta, g, name):
    s, w = qn.shape
    r = DILATIONS[g]
    gw = w // N_GROUPS
    cr = min(ATTN_CHUNK, s)
    nb = cr // (BLK * r)
    nchunk = s // cr
    cw = HEAD_DIM
    gc = gw // cw
    scale = 1.0 / math.sqrt(HEAD_DIM)

    def body(k_ref, v_ref, qc_ref, qx_ref, doc_ref, dox_ref, lc_ref, lx_ref, dc_ref, dx_ref, dk_ref, dv_ref):
        n = pl.program_id(1)
        same_m, prev_m = _band_masks()
        next_last = jnp.logical_and(prev_m, n < nchunk - 1)
        for j in range(r):
            for b in range(nb):
                rows = _rows(j, b, r)
                kv_, vv = k_ref[rows, :].astype(BF16), v_ref[rows, :].astype(BF16)
                qv, dov = qc_ref[rows, :].astype(BF16), doc_ref[rows, :].astype(BF16)
                pd, dsd = _pair_grads(qv, kv_, vv, dov, lc_ref[rows, :], dc_ref[rows, :], same_m, scale)
                if b < nb - 1:
                    rx = _rows(j, b + 1, r)
                    qx, dox, lx, dlx, xm = qc_ref[rx, :], doc_ref[rx, :], lc_ref[rx, :], dc_ref[rx, :], prev_m
                else:
                    rx = _rows(j, 0, r)
                    qx, dox, lx, dlx, xm = qx_ref[rx, :], dox_ref[rx, :], lx_ref[rx, :], dx_ref[rx, :], next_last
                qx, dox = qx.astype(BF16), dox.astype(BF16)
                po, dso = _pair_grads(qx, kv_, vv, dox, lx, dlx, xm, scale)
                dk_ref[rows, :] = _dot_nn(dsd.T.astype(BF16), qv) + _dot_nn(dso.T.astype(BF16), qx)
                dv_ref[rows, :] = _dot_nn(pd.T.astype(BF16), dov) + _dot_nn(po.T.astype(BF16), dox)

    cur = lambda base: (lambda c, n: (n, base + c))
    nxt = lambda base: (lambda c, n: (jnp.minimum(n + 1, nchunk - 1), base + c))
    qb, vb = g * gc, (N_GROUPS + g) * gc
    blk = lambda f: pl.BlockSpec((cr, cw), f)
    return _call(
        body, name=name, grid=(gc, nchunk),
        in_specs=[blk(cur(qb)), blk(cur(vb)), blk(cur(qb)), blk(nxt(qb)), blk(cur(0)), blk(nxt(0)), blk(cur(0)), blk(nxt(0)),
                  blk(cur(0)), blk(nxt(0))],
        out_specs=[blk(cur(0))] * 2, out_shape=[_sds((s, gw), F32)] * 2,
    )(kn, kv, qn, qn, do, do, lse, lse, delta, delta)


def _mod_proj(sc_all, w, name):
    l, d, ns = w.shape
    tn = _pick(ns, 512)

    def body(c_ref, w_ref, o_ref):
        o_ref[...] = jnp.dot(c_ref[...].astype(BF16), w_ref[...].astype(BF16), preferred_element_type=F32)

    return _call(
        body, name=name, grid=(l, ns // tn),
        in_specs=[pl.BlockSpec((N_DEV, d), lambda a, j: (0, 0)), pl.BlockSpec((None, d, tn), lambda a, j: (a, 0, j))],
        out_specs=pl.BlockSpec((None, N_DEV, tn), lambda a, j: (a, 0, j)), out_shape=_sds((l, N_DEV, ns), F32),
    )(sc_all, w)


def _adamw_math(w, g, m, v):
    m = ADAM_B1 * m + (1.0 - ADAM_B1) * g
    v = ADAM_B2 * v + (1.0 - ADAM_B2) * (g * g)
    m_hat = m / (1.0 - ADAM_B1 ** ADAM_STEP)
    v_hat = v / (1.0 - ADAM_B2 ** ADAM_STEP)
    delta = -ADAM_LR * (m_hat / (jnp.sqrt(v_hat) + ADAM_EPS) + ADAM_WD * w)
    return delta, m, v


def _adamw_big(w, g, m, v, name):
    shape = w.shape
    cols = shape[-1]
    rows = math.prod(shape[:-1])
    tr, tc = _pick(rows, 512, 8), _pick(cols, 1024)
    w2, g2, m2, v2 = (t.reshape(rows, cols) for t in (w, g, m, v))

    def body(w_ref, g_ref, m_ref, v_ref, d_ref, mo_ref, vo_ref):
        d_ref[...], mo_ref[...], vo_ref[...] = _adamw_math(w_ref[...], g_ref[...], m_ref[...], v_ref[...])

    spec = pl.BlockSpec((tr, tc), lambda i, j: (i, j))
    outs = _call(body, name=name, grid=(rows // tr, cols // tc), in_specs=[spec] * 4, out_specs=[spec] * 3,
                 out_shape=[_sds((rows, cols), F32)] * 3)(w2, g2, m2, v2)
    return [t.reshape(shape) for t in outs]


def _adamw_mod(w, sct, dm, m, v, name):
    l, d, ns = w.shape
    tr, tc = _pick(d, 512, 8), _pick(ns, 1024)

    def body(w_ref, c_ref, dm_ref, m_ref, v_ref, g_ref, d_ref, mo_ref, vo_ref):
        cv, dv = c_ref[...].astype(BF16).astype(F32), dm_ref[...].astype(BF16).astype(F32)
        g = jnp.zeros((tr, tc), F32)
        for e in range(N_DEV):
            g = g + cv[:, e:e + 1] * dv[e:e + 1, :]
        g_ref[...] = g
        d_ref[...], mo_ref[...], vo_ref[...] = _adamw_math(w_ref[...], g, m_ref[...], v_ref[...])

    spec = pl.BlockSpec((None, tr, tc), lambda a, i, j: (a, i, j))
    return _call(
        body, name=name, grid=(l, d // tr, ns // tc),
        in_specs=[spec, pl.BlockSpec((tr, N_DEV), lambda a, i, j: (i, 0)), pl.BlockSpec((None, N_DEV, tc), lambda a, i, j: (a, 0, j)),
                  spec, spec],
        out_specs=[spec] * 4, out_shape=[_sds((l, d, ns), F32)] * 4,
    )(w, sct, dm, m, v)


def _adamw_small(ws, gs, ms, vs, name):
    n = len(ws)

    def body(*refs):
        w_r, g_r, m_r, v_r = refs[0:n], refs[n:2 * n], refs[2 * n:3 * n], refs[3 * n:4 * n]
        d_o, m_o, v_o = refs[4 * n:5 * n], refs[5 * n:6 * n], refs[6 * n:7 * n]
        for k in range(n):
            d_o[k][...], m_o[k][...], v_o[k][...] = _adamw_math(w_r[k][...], g_r[k][...], m_r[k][...], v_r[k][...])

    vm = pl.BlockSpec(memory_space=pltpu.VMEM)
    shapes = [_sds(w.shape, F32) for w in ws]
    outs = pl.pallas_call(body, name=name, in_specs=[vm] * (4 * n), out_specs=[vm] * (3 * n), out_shape=shapes * 3,
                          interpret=False)(*ws, *gs, *ms, *vs)
    return outs[0:n], outs[n:2 * n], outs[2 * n:3 * n]


def _sum_rows(a, name):
    n, v = a.shape
    tc = _pick(v, 8192)

    def body(a_ref, o_ref):
        acc = a_ref[0:1, :]
        for e in range(1, n):
            acc = acc + a_ref[e:e + 1, :]
        o_ref[...] = acc

    return _call(body, name=name, grid=(v // tc,), in_specs=[pl.BlockSpec((n, tc), lambda j: (0, j))],
                 out_specs=pl.BlockSpec((1, tc), lambda j: (0, j)), out_shape=_sds((1, v), F32))(a)


def _cast_bf16(w2d, name):
    rows, cols = w2d.shape
    tr, tc = _pick(rows, 512, 16), _pick(cols, 1024)

    def body(w_ref, o_ref):
        o_ref[...] = w_ref[...].astype(BF16)

    spec = pl.BlockSpec((tr, tc), lambda i, j: (i, j))
    return _call(body, name=name, grid=(rows // tr, cols // tc), in_specs=[spec], out_specs=spec,
                 out_shape=_sds((rows, cols), BF16))(w2d)


def _place():
    x, y, c = lax.axis_index("x"), lax.axis_index("y"), lax.axis_index("c")
    chips = [(1 - x, y), (x, 1 - y), (1 - x, 1 - y)]
    return x, y, c, chips


def _remote(src, dst, send_sem, recv_sem, to):
    return pltpu.make_async_remote_copy(src_ref=src, dst_ref=dst, send_sem=send_sem, recv_sem=recv_sem, device_id=to,
                                        device_id_type=MESH)


def _allgather8(a, name):
    m_per, n = a.shape

    def body(x_ref, out_ref, send_sems, recv_sems, local_sem):
        x, y, c, chips = _place()
        me, sibling = (x, y, c), (x, y, 1 - c)

        def rows(px, py, pc):
            return out_ref.at[pl.ds((4 * px + 2 * py + pc) * m_per, m_per), :]

        def copy(k, block, to, src=None):
            return _remote(rows(*block) if src is None else src, rows(*block), send_sems.at[k], recv_sems.at[k], to)

        mine = pltpu.make_async_copy(x_ref, rows(*me), local_sem)
        mine.start()
        first = [copy(0, me, sibling, src=x_ref)]
        first += [copy(1 + j, me, (*chip, c), src=x_ref) for j, chip in enumerate(chips)]
        for cp in first:
            cp.start()
        passed = [copy(4 + j, (*chip, c), sibling) for j, chip in enumerate(chips)]
        for j, chip in enumerate(chips):
            copy(1 + j, (*chip, c), me).wait_recv()
            passed[j].start()
        copy(0, sibling, me).wait_recv()
        for j, chip in enumerate(chips):
            copy(4 + j, (*chip, 1 - c), me).wait_recv()
        for cp in first + passed:
            cp.wait_send()
        mine.wait()

    return pl.pallas_call(
        body, name=name, out_shape=_sds((N_DEV * m_per, n), a.dtype),
        in_specs=[pl.BlockSpec(memory_space=pltpu.VMEM)], out_specs=pl.BlockSpec(memory_space=pltpu.VMEM),
        scratch_shapes=[pltpu.SemaphoreType.DMA((7,)), pltpu.SemaphoreType.DMA((7,)), pltpu.SemaphoreType.DMA],
        interpret=False,
    )(a)


def _region(ref, kind, shard_shape, slot, half):
    r, cs = shard_shape
    hr = r // 2
    if kind == "col":
        return ref.at[pl.ds(half * hr, hr), pl.ds(slot * cs, cs)]
    return ref.at[pl.ds(slot * r + half * hr, hr), :]


def _full_shape(kind, shard_shape):
    r, cs = shard_shape
    return (r, 4 * cs) if kind == "col" else (4 * r, cs)


_HBM = pl.BlockSpec(memory_space=pltpu.HBM)


def _gather_weights(shards, kinds, name):
    n = len(shards)
    shapes = [s.shape for s in shards]

    def body(*refs):
        ins, outs = refs[0:n], refs[n:2 * n]
        send_sems, recv_sems, local_sems = refs[2 * n:]
        x, y, c, chips = _place()
        me_slot, sibling = 2 * x + y, (x, y, 1 - c)
        slots = [2 * cx + cy for cx, cy in chips]
        region = lambda a, slot, half: _region(outs[a], kinds[a], shapes[a], slot, half)
        local, first, passed = [], [], []
        for a in range(n):
            hr = shapes[a][0] // 2
            for hf in range(2):
                cp = pltpu.make_async_copy(ins[a].at[pl.ds(hf * hr, hr), :], region(a, me_slot, hf), local_sems.at[a, hf])
                cp.start()
                local.append(cp)
            for j, chip in enumerate(chips):
                cp = _remote(ins[a].at[pl.ds(c * hr, hr), :], region(a, me_slot, c), send_sems.at[a, j], recv_sems.at[a, j], (*chip, c))
                cp.start()
                first.append(cp)
        for a in range(n):
            for j in range(3):
                land = region(a, slots[j], c)
                _remote(land, land, send_sems.at[a, j], recv_sems.at[a, j], sibling).wait_recv()
                cp = _remote(land, land, send_sems.at[a, 3 + j], recv_sems.at[a, 3 + j], sibling)
                cp.start()
                passed.append(cp)
        for a in range(n):
            for j in range(3):
                land = region(a, slots[j], 1 - c)
                _remote(land, land, send_sems.at[a, 3 + j], recv_sems.at[a, 3 + j], sibling).wait_recv()
        for cp in first + passed:
            cp.wait_send()
        for cp in local:
            cp.wait()

    return pl.pallas_call(
        body, name=name, out_shape=[_sds(_full_shape(k, s), BF16) for k, s in zip(kinds, shapes)],
        in_specs=[_HBM] * n, out_specs=[_HBM] * n,
        scratch_shapes=[pltpu.SemaphoreType.DMA((n, 6)), pltpu.SemaphoreType.DMA((n, 6)), pltpu.SemaphoreType.DMA((n, 2))],
        interpret=False,
    )(*shards)


def _rs_pair_exchange(grads, kinds, shapes, name):
    n = len(grads)

    def body(*refs):
        ins, outs = refs[0:n], refs[n:2 * n]
        send_sems, recv_sems = refs[2 * n:]
        x, y, c, _ = _place()
        sibling = (x, y, 1 - c)
        sent = []
        for a in range(n):
            for slot in range(4):
                cp = _remote(_region(ins[a], kinds[a], shapes[a], slot, 1 - c), outs[a].at[slot], send_sems.at[a, slot],
                             recv_sems.at[a, slot], sibling)
                cp.start()
                sent.append(cp)
        for a in range(n):
            for slot in range(4):
                _remote(_region(ins[a], kinds[a], shapes[a], slot, c), outs[a].at[slot], send_sems.at[a, slot],
                        recv_sems.at[a, slot], sibling).wait_recv()
        for cp in sent:
            cp.wait_send()

    return pl.pallas_call(
        body, name=name, out_shape=[_sds((4, s[0] // 2, s[1]), F32) for s in shapes], in_specs=[_HBM] * n, out_specs=[_HBM] * n,
        scratch_shapes=[pltpu.SemaphoreType.DMA((n, 4)), pltpu.SemaphoreType.DMA((n, 4))], interpret=False,
    )(*grads)


def _rs_pair_add(grad, recv, kind, shape, c_arr, name):
    r, cs = shape
    hr = r // 2
    tr, tc = _pick(hr, 512, 16), _pick(cs, 1024)
    nr, nc = hr // tr, cs // tc

    def body(c_ref, g_ref, r_ref, o_ref):
        o_ref[...] = (g_ref[...] + r_ref[...]).astype(BF16)

    if kind == "col":
        g_map = lambda s_, i, j, c_ref: (c_ref[0] * nr + i, s_ * nc + j)
    else:
        g_map = lambda s_, i, j, c_ref: (s_ * 2 * nr + c_ref[0] * nr + i, j)
    own = pl.BlockSpec((None, tr, tc), lambda s_, i, j, c_ref: (s_, i, j))
    return _call(body, name=name, grid=(4, nr, nc), in_specs=[pl.BlockSpec((tr, tc), g_map), own], out_specs=own,
                 out_shape=_sds((4, hr, cs), BF16), nsp=1)(c_arr, grad, recv)


def _rs_chip_exchange(parts, name):
    n = len(parts)

    def body(*refs):
        ins, outs = refs[0:n], refs[n:2 * n]
        send_sems, recv_sems = refs[2 * n:]
        x, y, c, chips = _place()
        slots = [2 * cx + cy for cx, cy in chips]
        sent = []
        for a in range(n):
            for j, chip in enumerate(chips):
                cp = _remote(ins[a].at[slots[j]], outs[a].at[j], send_sems.at[a, j], recv_sems.at[a, j], (*chip, c))
                cp.start()
                sent.append(cp)
        for a in range(n):
            for j, chip in enumerate(chips):
                _remote(ins[a].at[slots[j]], outs[a].at[j], send_sems.at[a, j], recv_sems.at[a, j], (*chip, c)).wait_recv()
        for cp in sent:
            cp.wait_send()

    return pl.pallas_call(
        body, name=name, out_shape=[_sds((3,) + p.shape[1:], BF16) for p in parts], in_specs=[_HBM] * n, out_specs=[_HBM] * n,
        scratch_shapes=[pltpu.SemaphoreType.DMA((n, 3)), pltpu.SemaphoreType.DMA((n, 3))], interpret=False,
    )(*parts)


def _rs_chip_add(part, recv, slot_arr, name):
    _, hr, cs = part.shape
    tr, tc = _pick(hr, 512, 16), _pick(cs, 1024)

    def body(s_ref, p_ref, r0_ref, r1_ref, r2_ref, o_ref):
        o_ref[...] = ((p_ref[...].astype(F32) + r0_ref[...].astype(F32)) + r1_ref[...].astype(F32)) + r2_ref[...].astype(F32)

    rk = lambda k: pl.BlockSpec((None, tr, tc), lambda i, j, s_ref: (k, i, j))
    return _call(
        body, name=name, grid=(hr // tr, cs // tc),
        in_specs=[pl.BlockSpec((None, tr, tc), lambda i, j, s_ref: (s_ref[0], i, j)), rk(0), rk(1), rk(2)],
        out_specs=pl.BlockSpec((tr, tc), lambda i, j, s_ref: (i, j)), out_shape=_sds((hr, cs), F32), nsp=1,
    )(slot_arr, part, recv, recv, recv)


def _rs_pair_share(halves, dests, out_shapes, name):
    n = len(halves)
    n_out = len(out_shapes)

    def body(*refs):
        ins, outs = refs[0:n], refs[n:n + n_out]
        send_sems, recv_sems, local_sems = refs[n + n_out:]
        x, y, c, _ = _place()
        sibling = (x, y, 1 - c)

        def dst(a, half):
            o, layer = dests[a]
            hr = halves[a].shape[0]
            ref = outs[o] if layer is None else outs[o].at[layer]
            return ref.at[pl.ds(half * hr, hr), :]

        started = []
        for a in range(n):
            loc = pltpu.make_async_copy(ins[a], dst(a, c), local_sems.at[a])
            loc.start()
            cp = _remote(ins[a], dst(a, c), send_sems.at[a], recv_sems.at[a], sibling)
            cp.start()
            started.append((loc, cp))
        for a in range(n):
            _remote(ins[a], dst(a, 1 - c), send_sems.at[a], recv_sems.at[a], sibling).wait_recv()
        for loc, cp in started:
            cp.wait_send()
            loc.wait()

    return pl.pallas_call(
        body, name=name, out_shape=[_sds(s, F32) for s in out_shapes], in_specs=[_HBM] * n, out_specs=[_HBM] * n_out,
        scratch_shapes=[pltpu.SemaphoreType.DMA((n,)), pltpu.SemaphoreType.DMA((n,)), pltpu.SemaphoreType.DMA((n,))],
        interpret=False,
    )(*halves)


def _pad_to(v, mult):
    n = v.shape[0]
    return jnp.pad(v, (0, (-n) % mult))


def kernel(x, c, positions, mod_w, mod_b, norm_mix_g, norm_ffn_g, conv_pw1_w, conv_pw1_b, conv_dw_w, conv_dw_b, conv_ln_g, conv_ln_b, conv_pw2_w, conv_pw2_b, kv_mod_w, kv_mod_b, kv_norm_g, w_kv, k_norm_g, w_q, q_norm_g, w_o, ffn_up_w, ffn_dw_w, ffn_dw_b, ffn_down_w, loss_target, m_mod_w, m_mod_b, m_norm_mix_g, m_norm_ffn_g, m_conv_pw1_w, m_conv_pw1_b, m_conv_dw_w, m_conv_dw_b, m_conv_ln_g, m_conv_ln_b, m_conv_pw2_w, m_conv_pw2_b, m_kv_mod_w, m_kv_mod_b, m_kv_norm_g, m_w_kv, m_k_norm_g, m_w_q, m_q_norm_g, m_w_o, m_ffn_up_w, m_ffn_dw_w, m_ffn_dw_b, m_ffn_down_w, v_mod_w, v_mod_b, v_norm_mix_g, v_norm_ffn_g, v_conv_pw1_w, v_conv_pw1_b, v_conv_dw_w, v_conv_dw_b, v_conv_ln_g, v_conv_ln_b, v_conv_pw2_w, v_conv_pw2_b, v_kv_mod_w, v_kv_mod_b, v_kv_norm_g, v_w_kv, v_k_norm_g, v_w_q, v_q_norm_g, v_w_o, v_ffn_up_w, v_ffn_dw_w, v_ffn_dw_b, v_ffn_down_w):
    _, s, d = x.shape
    f = ffn_dw_b.shape[1]
    qw = w_q.shape[2] * 4
    ax, ay, ac = lax.axis_index("x"), lax.axis_index("y"), lax.axis_index("c")
    slot = 2 * ax + ay
    me8 = 4 * ax + 2 * ay + ac
    slot_arr = jnp.reshape(slot, (1,)).astype(jnp.int32)
    c_arr = jnp.reshape(ac, (1,)).astype(jnp.int32)
    x2 = x.reshape(s, d)
    target = loss_target.reshape(s, d)
    row = lambda v: v.reshape(1, -1)

    c_all = _allgather8(c.reshape(8, d // 8), "gather_c").reshape(N_DEV, d)
    sc_all = jax.nn.silu(c_all)
    mod_part = _mod_proj(sc_all, mod_w, "mod_proj")
    kvm_part = _mod_proj(sc_all, kv_mod_w[None], "kvmod_proj")
    nm, nk = mod_part.shape[2], kvm_part.shape[2]
    small_sharded = [conv_pw1_b, conv_dw_w, conv_dw_b, conv_ln_g, conv_ln_b, conv_pw2_b, ffn_dw_w]
    pack = jnp.concatenate([mod_part.reshape(-1), kvm_part.reshape(-1)] + [t.reshape(-1) for t in small_sharded])
    plen = pack.shape[0]
    pack = _pad_to(pack, 1024)
    gathered = _allgather8(pack.reshape(8, -1), "gather_mod").reshape(4, 2, -1)[:, 0, :plen]
    off = 0

    def take(n_el):
        nonlocal off
        out = lax.slice_in_dim(gathered, off, off + n_el, axis=1)
        off += n_el
        return out

    mod_g = take(2 * N_DEV * nm).reshape(4, 2, N_DEV, nm)
    kvm_g = take(N_DEV * nk).reshape(4, N_DEV, nk)
    mine = lambda t, axis: lax.dynamic_index_in_dim(t, me8, axis=axis, keepdims=False)
    mod_vec = jnp.transpose(mine(mod_g, 2), (1, 0, 2)).reshape(2, 4 * nm) + mod_b
    kvm_vec = mine(kvm_g, 1).reshape(4 * nk) + kv_mod_b
    pw1_b_full = take(conv_pw1_b.shape[1]).reshape(1, -1)
    dw_w_full = jnp.transpose(take(CONV_K * (d // 4)).reshape(4, CONV_K, d // 4), (1, 0, 2)).reshape(CONV_K, d)
    dw_b_full, ln_g_full, ln_b_full, pw2_b_full = (take(d // 4).reshape(1, d) for _ in range(4))
    fdw_full = jnp.transpose(take(2 * FFN_K * (f // 4)).reshape(4, 2, FFN_K, f // 4), (1, 2, 0, 3)).reshape(2, FFN_K, f)
    mods = [[row(mod_vec[l, k * d:(k + 1) * d]) for k in range(6)] for l in range(2)]
    kv_sh, kv_sc = row(kvm_vec[:d]), row(kvm_vec[d:])
    zero_d = jnp.zeros((1, d), F32)

    big = [("pw1", conv_pw1_w[0], "col"), ("pw2", conv_pw2_w[0], "row"), ("wkv", w_kv, "col"), ("wq", w_q[0], "col"),
           ("wo", w_o[0], "col"), ("up0", ffn_up_w[0], "col"), ("up1", ffn_up_w[1], "col"), ("dn0", ffn_down_w[0], "row"),
           ("dn1", ffn_down_w[1], "row")]
    names = [b[0] for b in big]
    kinds = [b[2] for b in big]
    shard_shapes = [b[1].shape for b in big]
    full = dict(zip(names, _gather_weights([_cast_bf16(b[1], "cast_" + b[0]) for b in big], kinds, "gather_weights")))

    def ffn_fwd(l, x_in, h, tag):
        u = _matmul(h, full["up%d" % l], "nn", F32, "mm_up" + tag)
        z = _ffn_act_fwd(u, fdw_full[l], row(ffn_dw_b[l]), "ffn_act" + tag)
        y = _matmul(z, full["dn%d" % l], "nn", F32, "mm_down" + tag)
        return u, z, y

    sh_m0, sc_m0, g_m0, sh_f0, sc_f0, g_f0 = mods[0]
    sh_m1, sc_m1, g_m1, sh_f1, sc_f1, g_f1 = mods[1]
    gmix0, gmix1, gffn0, gffn1 = row(norm_mix_g[0]), row(norm_mix_g[1]), row(norm_ffn_g[0]), row(norm_ffn_g[1])
    (h0,) = _resid_mod(x2, None, [(gmix0, sc_m0, sh_m0)], None, "mod_in")
    u0 = _matmul(h0, full["pw1"], "nn", F32, "mm_pw1")
    glu = _glu_fwd(u0, pw1_b_full, "glu")
    dwc = _dwconv_fwd(glu, dw_w_full, dw_b_full, "dwconv")
    act = _ln_silu_fwd(dwc, ln_g_full, ln_b_full, "ln_silu")
    y0 = _matmul(act, full["pw2"], "nn", F32, "mm_pw2")
    x1, hf0 = _resid_mod(x2, (y0, pw2_b_full, g_m0), [(gffn0, sc_f0, sh_f0)], None, "resid_conv")
    uf0, zf0, yf0 = ffn_fwd(0, x1, hf0, "0")
    gkv = row(kv_norm_g)
    xa, hk, hq = _resid_mod(x1, (yf0, zero_d, g_f0), [(gkv, kv_sc, kv_sh), (gmix1, sc_m1, sh_m1)], None, "resid_ffn0")
    kvp = _matmul(hk, full["wkv"], "nn", F32, "mm_kv")
    qp = _matmul(hq, full["wq"], "nn", F32, "mm_q")
    inv_freq = ROPE_THETA ** (-jnp.arange(0, ROT_DIM, 2, dtype=F32) / ROT_DIM)
    half = ROT_DIM // 2
    freq_l = jnp.concatenate([inv_freq, inv_freq, jnp.zeros((HEAD_DIM - ROT_DIM,), F32)]).reshape(1, HEAD_DIM)
    sign_l = jnp.concatenate([-jnp.ones((half,), F32), jnp.ones((half,), F32), jnp.zeros((HEAD_DIM - ROT_DIM,), F32)]).reshape(1, HEAD_DIM)
    ctab, stab = _rope_tables(positions.reshape(s, 1), freq_l, sign_l, "rope_tables")
    gq, gk = row(q_norm_g[0]), row(k_norm_g)
    qn, kn = _qk_norm_rope_fwd(qp, kvp, gq, gk, ctab, stab, "qk_norm_rope")
    og, lg = zip(*[_attn_fwd(qn, kn, kvp, g, "attn_fwd%d" % g) for g in range(N_GROUPS)])
    o_mix, lse = _attn_combine(og, lg, "attn_combine")
    ya = _matmul(o_mix, full["wo"], "nn", F32, "mm_o")
    xb, hf1 = _resid_mod(xa, (ya, zero_d, g_m1), [(gffn1, sc_f1, sh_f1)], None, "resid_attn")
    uf1, zf1, yf1 = ffn_fwd(1, xb, hf1, "1")
    _, dxo, loss_cols = _resid_mod(xb, (yf1, zero_d, g_f1), [], target, "resid_loss")
    loss = lax.psum(jnp.sum(loss_cols), ("x", "y", "c"))

    gbig = {}

    def ffn_bwd(l, dy, u, z, h, tag):
        dz = _matmul(dy, full["dn%d" % l], "nt", F32, "mm_down_dx" + tag)
        gbig["dn%d" % l] = _matmul(z, dy, "tn", F32, "mm_down_dw" + tag)
        du, dfw, dfb = _ffn_act_bwd(dz, u, fdw_full[l], row(ffn_dw_b[l]), "ffn_act_bwd" + tag)
        dh = _matmul(du, full["up%d" % l], "nt", F32, "mm_up_dx" + tag)
        gbig["up%d" % l] = _matmul(h, du, "tn", F32, "mm_up_dw" + tag)
        return dh, dfw, dfb

    dxb0, dyf1, dg_f1, _ = _bwd_step(dxo, xb, [], (yf1, zero_d, g_f1), "bwd_loss")
    dhf1, dfw1, dfb1 = ffn_bwd(1, dyf1, uf1, zf1, hf1, "1")
    dxb, dsh_f1, dsc_f1, dgffn1, dya, dg_m1, _ = _bwd_step(dxb0, xb, [(dhf1, gffn1, sc_f1)], (ya, zero_d, g_m1), "bwd_attn_out")
    do = _matmul(dya, full["wo"], "nt", F32, "mm_o_dx")
    gbig["wo"] = _matmul(o_mix, dya, "tn", F32, "mm_o_dw")
    delta = _attn_delta(do, o_mix, "attn_delta")
    dqs = [_attn_bwd_dq(qn, kn, kvp, do, lse, delta, g, "attn_dq%d" % g) for g in range(N_GROUPS)]
    dks, dvs = zip(*[_attn_bwd_dkv(qn, kn, kvp, do, lse, delta, g, "attn_dkv%d" % g) for g in range(N_GROUPS)])
    dqp, dkvp, dgq, dgk = _qk_norm_rope_bwd(dqs, dks, dvs, qp, kvp, gq, gk, ctab, stab, "qk_norm_rope_bwd")
    dhq = _matmul(dqp, full["wq"], "nt", F32, "mm_q_dx")
    gbig["wq"] = _matmul(hq, dqp, "tn", F32, "mm_q_dw")
    dhk = _matmul(dkvp, full["wkv"], "nt", F32, "mm_kv_dx")
    gbig["wkv"] = _matmul(hk, dkvp, "tn", F32, "mm_kv_dw")
    (dxa, dsh_kv, dsc_kv, dgkv, dsh_m1, dsc_m1, dgmix1, dyf0, dg_f0, _) = _bwd_step(
        dxb, xa, [(dhk, gkv, kv_sc), (dhq, gmix1, sc_m1)], (yf0, zero_d, g_f0), "bwd_kvq")
    dhf0, dfw0, dfb0 = ffn_bwd(0, dyf0, uf0, zf0, hf0, "0")
    dx1, dsh_f0, dsc_f0, dgffn0, dy0, dg_m0, dpw2_b = _bwd_step(dxa, x1, [(dhf0, gffn0, sc_f0)], (y0, pw2_b_full, g_m0), "bwd_conv_out")
    dact = _matmul(dy0, full["pw2"], "nt", F32, "mm_pw2_dx")
    gbig["pw2"] = _matmul(act, dy0, "tn", F32, "mm_pw2_dw")
    ddwc, dln_g, dln_b = _ln_silu_bwd(dact, dwc, ln_g_full, ln_b_full, "ln_silu_bwd")
    dglu, ddw_w, ddw_b = _dwconv_bwd(ddwc, glu, dw_w_full, "dwconv_bwd")
    du0, dpw1_b = _glu_bwd(dglu, u0, pw1_b_full, "glu_bwd")
    dh0 = _matmul(du0, full["pw1"], "nt", F32, "mm_pw1_dx")
    gbig["pw1"] = _matmul(h0, du0, "tn", F32, "mm_pw1_dw")
    grad_x, dsh_m0, dsc_m0, dgmix0 = _bwd_step(dx1, x2, [(dh0, gmix0, sc_m0)], None, "bwd_in")

    dmod = [jnp.concatenate([dsh_m0, dsc_m0, dg_m0, dsh_f0, dsc_f0, dg_f0], axis=1),
            jnp.concatenate([dsh_m1, dsc_m1, dg_m1, dsh_f1, dsc_f1, dg_f1], axis=1)]
    dkvm = jnp.concatenate([dsh_kv, dsc_kv], axis=1)
    per_ex = [dmod[0], dmod[1], dkvm]
    summed = [dgmix0, dgmix1, dgffn0, dgffn1, dpw1_b, ddw_w, ddw_b, dln_g, dln_b, dpw2_b, dgkv, dgk, dgq, dfw0, dfw1, dfb0, dfb1]
    vec = jnp.concatenate([t.reshape(-1) for t in per_ex + summed])
    vlen = vec.shape[0]
    vec = _pad_to(vec, 1024)
    vall = _allgather8(vec.reshape(8, -1), "gather_small").reshape(N_DEV, -1)
    vsum = _sum_rows(vall, "sum_small")[0]
    n_pe = sum(t.size for t in per_ex)
    dm_all = vall[:, :n_pe]
    off2 = n_pe
    sums = []
    for t in summed:
        sums.append(vsum[off2:off2 + t.size].reshape(t.shape))
        off2 += t.size
    (s_gmix0, s_gmix1, s_gffn0, s_gffn1, s_pw1_b, s_dw_w, s_dw_b, s_ln_g, s_ln_b, s_pw2_b, s_gkv, s_gk, s_gq, s_fw0, s_fw1,
     s_fb0, s_fb1) = sums
    shard_cols = lambda t, width: lax.dynamic_slice_in_dim(t, slot * width, width, axis=t.ndim - 1)
    dm_mod = jnp.stack([shard_cols(dm_all[:, l * 6 * d:(l + 1) * 6 * d], nm) for l in range(2)])
    dm_kv = shard_cols(dm_all[:, 12 * d:14 * d], nk)[None]
    sct = jnp.transpose(sc_all)

    glist = [gbig[nme] for nme in names]
    recv1 = _rs_pair_exchange(glist, kinds, shard_shapes, "rs_pair_exchange")
    parts = [_rs_pair_add(g_, r_, k_, s_, c_arr, "rs_pair_add_" + nme) for g_, r_, k_, s_, nme in zip(glist, recv1, kinds, shard_shapes, names)]
    recv2 = _rs_chip_exchange(parts, "rs_chip_exchange")
    halves = [_rs_chip_add(p_, r_, slot_arr, "rs_chip_add_" + nme) for p_, r_, nme in zip(parts, recv2, names)]
    dests = [(0, None), (1, None), (2, None), (3, None), (4, None), (5, 0), (5, 1), (6, 0), (6, 1)]
    out_shapes = [conv_pw1_w.shape[1:], conv_pw2_w.shape[1:], w_kv.shape, w_q.shape[1:], w_o.shape[1:], ffn_up_w.shape, ffn_down_w.shape]
    g_pw1, g_pw2, g_wkv, g_wq, g_wo, g_up, g_dn = _rs_pair_share(halves, dests, out_shapes, "rs_pair_share")

    grads, deltas, new_m, new_v = {}, {}, {}, {}

    def put(nme, g_, res):
        grads[nme] = g_
        deltas[nme], new_m[nme], new_v[nme] = res

    for nme, g_, w_, m_, v_ in (("conv_pw1_w", g_pw1[None], conv_pw1_w, m_conv_pw1_w, v_conv_pw1_w),
                                ("conv_pw2_w", g_pw2[None], conv_pw2_w, m_conv_pw2_w, v_conv_pw2_w),
                                ("w_kv", g_wkv, w_kv, m_w_kv, v_w_kv), ("w_q", g_wq[None], w_q, m_w_q, v_w_q),
                                ("w_o", g_wo[None], w_o, m_w_o, v_w_o), ("ffn_up_w", g_up, ffn_up_w, m_ffn_up_w, v_ffn_up_w),
                                ("ffn_down_w", g_dn, ffn_down_w, m_ffn_down_w, v_ffn_down_w)):
        put(nme, g_, _adamw_big(w_, g_, m_, v_, "adamw_" + nme))
    g_, *res = _adamw_mod(mod_w, sct, dm_mod, m_mod_w, v_mod_w, "adamw_mod_w")
    put("mod_w", g_, res)
    g_, *res = _adamw_mod(kv_mod_w[None], sct, dm_kv, m_kv_mod_w[None], v_kv_mod_w[None], "adamw_kv_mod_w")
    put("kv_mod_w", g_[0], [t[0] for t in res])

    dm_sum = vsum[:n_pe]
    small = [
        ("mod_b", dm_sum[:12 * d].reshape(2, 6 * d), mod_b, m_mod_b, v_mod_b),
        ("norm_mix_g", jnp.concatenate([s_gmix0, s_gmix1], axis=0), norm_mix_g, m_norm_mix_g, v_norm_mix_g),
        ("norm_ffn_g", jnp.concatenate([s_gffn0, s_gffn1], axis=0), norm_ffn_g, m_norm_ffn_g, v_norm_ffn_g),
        ("conv_pw1_b", shard_cols(s_pw1_b, conv_pw1_b.shape[1]), conv_pw1_b, m_conv_pw1_b, v_conv_pw1_b),
        ("conv_dw_w", shard_cols(s_dw_w, d // 4)[None], conv_dw_w, m_conv_dw_w, v_conv_dw_w),
        ("conv_dw_b", shard_cols(s_dw_b, d // 4), conv_dw_b, m_conv_dw_b, v_conv_dw_b),
        ("conv_ln_g", shard_cols(s_ln_g, d // 4), conv_ln_g, m_conv_ln_g, v_conv_ln_g),
        ("conv_ln_b", shard_cols(s_ln_b, d // 4), conv_ln_b, m_conv_ln_b, v_conv_ln_b),
        ("conv_pw2_b", shard_cols(s_pw2_b, d // 4), conv_pw2_b, m_conv_pw2_b, v_conv_pw2_b),
        ("kv_mod_b", dm_sum[12 * d:14 * d], kv_mod_b, m_kv_mod_b, v_kv_mod_b),
        ("kv_norm_g", s_gkv.reshape(-1), kv_norm_g, m_kv_norm_g, v_kv_norm_g),
        ("k_norm_g", s_gk.reshape(-1), k_norm_g, m_k_norm_g, v_k_norm_g),
        ("q_norm_g", s_gq, q_norm_g, m_q_norm_g, v_q_norm_g),
        ("ffn_dw_w", shard_cols(jnp.stack([s_fw0, s_fw1]), f // 4), ffn_dw_w, m_ffn_dw_w, v_ffn_dw_w),
        ("ffn_dw_b", jnp.concatenate([s_fb0, s_fb1], axis=0), ffn_dw_b, m_ffn_dw_b, v_ffn_dw_b),
    ]
    as2d = lambda t: t.reshape(-1, t.shape[-1])
    sd_, sm_, sv_ = _adamw_small([as2d(t[2]) for t in small], [as2d(t[1]) for t in small], [as2d(t[3]) for t in small],
                                 [as2d(t[4]) for t in small], "adamw_small")
    for (nme, g_, w_, _, _), d_, mo_, vo_ in zip(small, sd_, sm_, sv_):
        put(nme, g_.reshape(w_.shape), [d_.reshape(w_.shape), mo_.reshape(w_.shape), vo_.reshape(w_.shape)])

    order = ["mod_w", "mod_b", "norm_mix_g", "norm_ffn_g", "conv_pw1_w", "conv_pw1_b", "conv_dw_w", "conv_dw_b", "conv_ln_g",
             "conv_ln_b", "conv_pw2_w", "conv_pw2_b", "kv_mod_w", "kv_mod_b", "kv_norm_g", "w_kv", "k_norm_g", "w_q", "q_norm_g",
             "w_o", "ffn_up_w", "ffn_dw_w", "ffn_dw_b", "ffn_down_w"]
    return (loss, grad_x.reshape(x.shape), *[grads[k] for k in order], *[deltas[k] for k in order], *[new_m[k] for k in order],
            *[new_v[k] for k in order])
```

```python
import functools
import math

import jax
import jax.numpy as jnp
from jax import lax
from jax.experimental import pallas as pl
from jax.experimental.pallas import tpu as pltpu

F32 = jnp.float32
BF16 = jnp.bfloat16
EPS = 1e-6
NEG = -1e30
HEAD_DIM = 128
ROT_DIM = 32
ROPE_THETA = 500000.0
BLK = 128
DILATIONS = (1, 4, 16)
N_GROUPS = 3
CONV_K = 31
FFN_K = 3
ADAM_LR, ADAM_B1, ADAM_B2, ADAM_EPS, ADAM_WD, ADAM_STEP = 0.001, 0.9, 0.999, 1e-08, 0.01, 10
N_DEV = 8
MESH = pl.DeviceIdType.MESH
VMEM_LIMIT_MB = 56
ROW_TILE = 256
ATTN_CHUNK = 2048


def _pick(n, pref, mult=128):
    best = None
    d = mult
    while d <= min(n, pref):
        if n % d == 0:
            best = d
        d += mult
    return best if best is not None else n


def _call(body, *, name, grid, in_specs, out_specs, out_shape, scratch=(), nsp=0):
    return pl.pallas_call(
        body,
        name=name,
        grid_spec=pltpu.PrefetchScalarGridSpec(
            num_scalar_prefetch=nsp, grid=grid, in_specs=in_specs, out_specs=out_specs, scratch_shapes=list(scratch)
        ),
        out_shape=out_shape,
        compiler_params=pltpu.CompilerParams(
            dimension_semantics=("arbitrary",) * len(grid), vmem_limit_bytes=VMEM_LIMIT_MB << 20
        ),
        interpret=False,
    )


def _sds(shape, dtype):
    return jax.ShapeDtypeStruct(shape, dtype)


def _acc(ref, val, i):
    @pl.when(i == 0)
    def _():
        ref[...] = val

    @pl.when(i > 0)
    def _():
        ref[...] += val


def _colsum(v):
    return jnp.sum(v, axis=0, keepdims=True)


def _silu(v):
    return v * jax.nn.sigmoid(v)


def _dsilu(v):
    s = jax.nn.sigmoid(v)
    return s * (1.0 + v * (1.0 - s))


_DIMS = {"nn": (((1,), (0,)), ((), ())), "nt": (((1,), (1,)), ((), ())), "tn": (((0,), (0,)), ((), ()))}


def _matmul(a, b, mode, out_dtype, name):
    a_halves = a.shape[0] if a.ndim == 3 else 0
    b_halves = b.shape[0] if b.ndim == 3 else 0
    if mode == "nn":
        (m, c), (_, n) = a.shape, b.shape
    elif mode == "nt":
        m, c = (a.shape[1], a.shape[0] * a.shape[2]) if a_halves else a.shape
        n = b.shape[0]
    else:
        c, m = a.shape
        n = b.shape[0] * b.shape[2] if b_halves else b.shape[1]
    tm = _pick(m, 1024)
    tn = _pick(n // b_halves, 1024) if b_halves else _pick(n, 1024)
    c_cap = 2048 if mode == "tn" else 2816
    tc = _pick(c // a_halves, c_cap) if a_halves else _pick(c, c_cap)
    nk = c // tc
    if a_halves:
        per_a = c // a_halves // tc
        a_spec = pl.BlockSpec((None, tm, tc), lambda i, j, k: (k // per_a, i, k % per_a))
    else:
        a_spec = {"nn": pl.BlockSpec((tm, tc), lambda i, j, k: (i, k)), "nt": pl.BlockSpec((tm, tc), lambda i, j, k: (i, k)),
                  "tn": pl.BlockSpec((tc, tm), lambda i, j, k: (k, i))}[mode]
    if b_halves:
        per_b = n // b_halves // tn
        b_spec = pl.BlockSpec((None, tc, tn), lambda i, j, k: (j // per_b, k, j % per_b))
    else:
        b_spec = {"nn": pl.BlockSpec((tc, tn), lambda i, j, k: (k, j)), "nt": pl.BlockSpec((tn, tc), lambda i, j, k: (j, k)),
                  "tn": pl.BlockSpec((tc, tn), lambda i, j, k: (k, j))}[mode]
    dims = _DIMS[mode]

    def body(a_ref, b_ref, o_ref, acc_ref):
        k = pl.program_id(2)
        p = lax.dot_general(a_ref[...], b_ref[...], dims, preferred_element_type=F32)
        if nk == 1:
            o_ref[...] = p.astype(out_dtype)
        else:
            @pl.when(k == 0)
            def _():
                acc_ref[...] = p

            @pl.when(k > 0)
            def _():
                acc_ref[...] += p

            @pl.when(k == nk - 1)
            def _():
                o_ref[...] = acc_ref[...].astype(out_dtype)

    return _call(
        body, name=name, grid=(m // tm, n // tn, nk), in_specs=[a_spec, b_spec],
        out_specs=pl.BlockSpec((tm, tn), lambda i, j, k: (i, j)), out_shape=_sds((m, n), out_dtype),
        scratch=[pltpu.VMEM((tm, tn), F32)],
    )(a, b)


def _row_spec(tr, w):
    return pl.BlockSpec((tr, w), lambda i: (i, 0))


def _vec_spec(w):
    return pl.BlockSpec((1, w), lambda i: (0, 0))


def _resid_mod(x, prev, mods, target, name):
    s, d = x.shape
    tr = _pick(s, ROW_TILE, 8)
    n_mod = len(mods)

    def body(*refs):
        it = iter(refs)
        x_ref = next(it)
        if prev is not None:
            y_ref, yb_ref, gate_ref = next(it), next(it), next(it)
        mod_refs = [(next(it), next(it), next(it)) for _ in range(n_mod)]
        if target is not None:
            t_ref = next(it)
        if prev is not None:
            xo_ref = next(it)
        h_refs = [next(it) for _ in range(n_mod)]
        i = pl.program_id(0)
        xv = x_ref[...]
        if prev is not None:
            xv = xv + gate_ref[...] * (y_ref[...] + yb_ref[...])
            xo_ref[...] = xv
        if n_mod:
            nrm = xv * lax.rsqrt(jnp.mean(xv * xv, axis=-1, keepdims=True) + EPS)
            for (g_ref, sc_ref, sh_ref), h_ref in zip(mod_refs, h_refs):
                h_ref[...] = (nrm * g_ref[...] * (1.0 + sc_ref[...]) + sh_ref[...]).astype(BF16)
        if target is not None:
            dx_ref, loss_ref = next(it), next(it)
            err = xv - t_ref[...]
            dx_ref[...] = err * (1.0 / d)
            _acc(loss_ref, _colsum(err * err) * (0.5 / d), i)

    ins, in_specs = [x], [_row_spec(tr, d)]
    if prev is not None:
        ins += list(prev)
        in_specs += [_row_spec(tr, d), _vec_spec(d), _vec_spec(d)]
    for g, sc, sh in mods:
        ins += [g, sc, sh]
        in_specs += [_vec_spec(d)] * 3
    if target is not None:
        ins.append(target)
        in_specs.append(_row_spec(tr, d))
    out_shape, out_specs = [], []
    if prev is not None:
        out_shape.append(_sds((s, d), F32))
        out_specs.append(_row_spec(tr, d))
    for _ in mods:
        out_shape.append(_sds((s, d), BF16))
        out_specs.append(_row_spec(tr, d))
    if target is not None:
        out_shape += [_sds((s, d), F32), _sds((1, d), F32)]
        out_specs += [_row_spec(tr, d), _vec_spec(d)]
    return _call(body, name=name, grid=(s // tr,), in_specs=in_specs, out_specs=out_specs, out_shape=out_shape)(*ins)


def _bwd_step(dx_up, x, mods, prev, name):
    s, d = x.shape
    tr = _pick(s, ROW_TILE, 8)
    n_mod = len(mods)

    def body(*refs):
        it = iter(refs)
        dxu_ref, x_ref = next(it), next(it)
        mod_refs = [(next(it), next(it), next(it)) for _ in range(n_mod)]
        if prev is not None:
            y_ref, yb_ref, gate_ref = next(it), next(it), next(it)
        dx_ref = next(it)
        acc_refs = [(next(it), next(it), next(it)) for _ in range(n_mod)]
        i = pl.program_id(0)
        dx = dxu_ref[...]
        if n_mod:
            xv = x_ref[...]
            rstd = lax.rsqrt(jnp.mean(xv * xv, axis=-1, keepdims=True) + EPS)
            nrm = xv * rstd
        for (dh_ref, g_ref, sc_ref), (dsh_ref, dsc_ref, dg_ref) in zip(mod_refs, acc_refs):
            dh = dh_ref[...]
            gv, one_sc = g_ref[...], 1.0 + sc_ref[...]
            _acc(dsh_ref, _colsum(dh), i)
            t = dh * nrm
            _acc(dsc_ref, _colsum(t) * gv, i)
            _acc(dg_ref, _colsum(t) * one_sc, i)
            dn = dh * (gv * one_sc)
            dx = dx + rstd * (dn - nrm * jnp.mean(dn * nrm, axis=-1, keepdims=True))
        dx_ref[...] = dx
        if prev is not None:
            dy_ref, dgate_ref, dyb_ref = next(it), next(it), next(it)
            dy = gate_ref[...] * dx
            dy_ref[...] = dy.astype(BF16)
            _acc(dgate_ref, _colsum(dx * (y_ref[...] + yb_ref[...])), i)
            _acc(dyb_ref, _colsum(dy), i)

    ins, in_specs = [dx_up, x], [_row_spec(tr, d)] * 2
    for dh, g, sc in mods:
        ins += [dh, g, sc]
        in_specs += [_row_spec(tr, d), _vec_spec(d), _vec_spec(d)]
    if prev is not None:
        ins += list(prev)
        in_specs += [_row_spec(tr, d), _vec_spec(d), _vec_spec(d)]
    out_shape, out_specs = [_sds((s, d), F32)], [_row_spec(tr, d)]
    for _ in mods:
        out_shape += [_sds((1, d), F32)] * 3
        out_specs += [_vec_spec(d)] * 3
    if prev is not None:
        out_shape += [_sds((s, d), BF16), _sds((1, d), F32), _sds((1, d), F32)]
        out_specs += [_row_spec(tr, d), _vec_spec(d), _vec_spec(d)]
    return _call(body, name=name, grid=(s // tr,), in_specs=in_specs, out_specs=out_specs, out_shape=out_shape)(*ins)


def _glu_fwd(u, bias, name):
    s, d2 = u.shape
    d = d2 // 2
    tr = _pick(s, ROW_TILE, 8)

    def body(u_ref, b_ref, o_ref):
        uv = u_ref[...] + b_ref[...]
        o_ref[...] = uv[:, :d] * jax.nn.sigmoid(uv[:, d:])

    return _call(body, name=name, grid=(s // tr,), in_specs=[_row_spec(tr, d2), _vec_spec(d2)],
                 out_specs=_row_spec(tr, d), out_shape=_sds((s, d), F32))(u, bias)


def _glu_bwd(dglu, u, bias, name):
    s, d2 = u.shape
    d = d2 // 2
    tr = _pick(s, ROW_TILE, 8)

    def body(dg_ref, u_ref, b_ref, du_ref, db_ref):
        i = pl.program_id(0)
        uv = u_ref[...] + b_ref[...]
        a, sg = uv[:, :d], jax.nn.sigmoid(uv[:, d:])
        dg = dg_ref[...]
        da = dg * sg
        dgt = dg * a * sg * (1.0 - sg)
        du_ref[:, :d] = da.astype(BF16)
        du_ref[:, d:] = dgt.astype(BF16)
        _acc(db_ref, jnp.concatenate([_colsum(da), _colsum(dgt)], axis=1), i)

    return _call(body, name=name, grid=(s // tr,), in_specs=[_row_spec(tr, d), _row_spec(tr, d2), _vec_spec(d2)],
                 out_specs=[_row_spec(tr, d2), _vec_spec(d2)], out_shape=[_sds((s, d2), BF16), _sds((1, d2), F32)])(dglu, u, bias)


def _halo_rows(k):
    return -(-(k - 1) // 8) * 8


def _dwconv_fwd(x, w, b, name):
    s, c = x.shape
    kk = w.shape[0]
    hb = _halo_rows(kk)
    tr, tc = _pick(s, ROW_TILE, hb), _pick(c, 512)
    per = tr // hb

    def body(xp_ref, x_ref, w_ref, b_ref, o_ref, cat_ref):
        i = pl.program_id(1)
        cat_ref[0:hb, :] = jnp.where(i > 0, xp_ref[...], 0.0)
        cat_ref[hb:hb + tr, :] = x_ref[...]
        acc = jnp.zeros((tr, tc), F32) + b_ref[...]
        for k in range(kk):
            acc = acc + w_ref[k:k + 1, :] * cat_ref[pl.ds(hb - (kk - 1) + k, tr), :]
        o_ref[...] = acc

    return _call(
        body, name=name, grid=(c // tc, s // tr),
        in_specs=[pl.BlockSpec((hb, tc), lambda j, i: (jnp.maximum(i * per - 1, 0), j)), pl.BlockSpec((tr, tc), lambda j, i: (i, j)),
                  pl.BlockSpec((kk, tc), lambda j, i: (0, j)), pl.BlockSpec((1, tc), lambda j, i: (0, j))],
        out_specs=pl.BlockSpec((tr, tc), lambda j, i: (i, j)), out_shape=_sds((s, c), F32),
        scratch=[pltpu.VMEM((tr + hb, tc), F32)],
    )(x, x, w, b)


def _dwconv_bwd(dy, x, w, name):
    s, c = x.shape
    kk = w.shape[0]
    hb = _halo_rows(kk)
    tr, tc = _pick(s, ROW_TILE, hb), _pick(c, 512)
    per, nt = tr // hb, s // tr

    def body(xp_ref, x_ref, dy_ref, dyn_ref, w_ref, dx_ref, dw_ref, db_ref, xcat_ref, dcat_ref):
        i = pl.program_id(1)
        xcat_ref[0:hb, :] = jnp.where(i > 0, xp_ref[...], 0.0)
        xcat_ref[hb:hb + tr, :] = x_ref[...]
        dyv = dy_ref[...]
        dcat_ref[0:tr, :] = dyv
        dcat_ref[tr:tr + hb, :] = jnp.where(i < nt - 1, dyn_ref[...], 0.0)
        acc = jnp.zeros((tr, tc), F32)
        for k in range(kk):
            acc = acc + w_ref[k:k + 1, :] * dcat_ref[pl.ds(kk - 1 - k, tr), :]
        dx_ref[...] = acc

        @pl.when(i == 0)
        def _():
            dw_ref[...] = jnp.zeros_like(dw_ref)
            db_ref[...] = jnp.zeros_like(db_ref)

        db_ref[...] += _colsum(dyv)
        for k in range(kk):
            dw_ref[k:k + 1, :] += _colsum(dyv * xcat_ref[pl.ds(hb - (kk - 1) + k, tr), :])

    return _call(
        body, name=name, grid=(c // tc, nt),
        in_specs=[pl.BlockSpec((hb, tc), lambda j, i: (jnp.maximum(i * per - 1, 0), j)), pl.BlockSpec((tr, tc), lambda j, i: (i, j)),
                  pl.BlockSpec((tr, tc), lambda j, i: (i, j)),
                  pl.BlockSpec((hb, tc), lambda j, i: (jnp.minimum((i + 1) * per, s // hb - 1), j)),
                  pl.BlockSpec((kk, tc), lambda j, i: (0, j))],
        out_specs=[pl.BlockSpec((tr, tc), lambda j, i: (i, j)), pl.BlockSpec((kk, tc), lambda j, i: (0, j)),
                   pl.BlockSpec((1, tc), lambda j, i: (0, j))],
        out_shape=[_sds((s, c), F32), _sds((kk, c), F32), _sds((1, c), F32)],
        scratch=[pltpu.VMEM((tr + hb, tc), F32), pltpu.VMEM((tr + hb, tc), F32)],
    )(x, x, dy, dy, w)


def _ln_silu_fwd(x, g, b, name):
    s, d = x.shape
    tr = _pick(s, ROW_TILE, 8)

    def body(x_ref, g_ref, b_ref, o_ref):
        xv = x_ref[...]
        mu = jnp.mean(xv, axis=-1, keepdims=True)
        xc = xv - mu
        ln = xc * lax.rsqrt(jnp.mean(xc * xc, axis=-1, keepdims=True) + EPS) * g_ref[...] + b_ref[...]
        o_ref[...] = _silu(ln).astype(BF16)

    return _call(body, name=name, grid=(s // tr,), in_specs=[_row_spec(tr, d), _vec_spec(d), _vec_spec(d)],
                 out_specs=_row_spec(tr, d), out_shape=_sds((s, d), BF16))(x, g, b)


def _ln_silu_bwd(dact, x, g, b, name):
    s, d = x.shape
    tr = _pick(s, ROW_TILE, 8)

    def body(da_ref, x_ref, g_ref, b_ref, dx_ref, dg_ref, db_ref):
        i = pl.program_id(0)
        xv = x_ref[...]
        mu = jnp.mean(xv, axis=-1, keepdims=True)
        xc = xv - mu
        rstd = lax.rsqrt(jnp.mean(xc * xc, axis=-1, keepdims=True) + EPS)
        xh = xc * rstd
        ln = xh * g_ref[...] + b_ref[...]
        dln = da_ref[...] * _dsilu(ln)
        _acc(dg_ref, _colsum(dln * xh), i)
        _acc(db_ref, _colsum(dln), i)
        dxh = dln * g_ref[...]
        dx_ref[...] = rstd * (dxh - jnp.mean(dxh, axis=-1, keepdims=True) - xh * jnp.mean(dxh * xh, axis=-1, keepdims=True))

    return _call(body, name=name, grid=(s // tr,), in_specs=[_row_spec(tr, d), _row_spec(tr, d), _vec_spec(d), _vec_spec(d)],
                 out_specs=[_row_spec(tr, d), _vec_spec(d), _vec_spec(d)],
                 out_shape=[_sds((s, d), F32), _sds((1, d), F32), _sds((1, d), F32)])(dact, x, g, b)


def _ffn_act_fwd(u, w, b, name):
    s, f2 = u.shape
    f = f2 // 2
    hb = 8
    tr, tc = _pick(s, ROW_TILE, hb), _pick(f, 1408)
    per, nf = tr // hb, f // tc

    def body(gp_ref, g_ref, v_ref, w_ref, b_ref, z_ref, cat_ref):
        i = pl.program_id(1)
        cat_ref[0:hb, :] = jnp.where(i > 0, gp_ref[...], 0.0)
        cat_ref[hb:hb + tr, :] = g_ref[...]
        gc = jnp.zeros((tr, tc), F32) + b_ref[...]
        for k in range(FFN_K):
            gc = gc + w_ref[k:k + 1, :] * cat_ref[pl.ds(hb - (FFN_K - 1) + k, tr), :]
        z_ref[...] = (_silu(gc) * v_ref[...]).astype(BF16)

    return _call(
        body, name=name, grid=(nf, s // tr),
        in_specs=[pl.BlockSpec((hb, tc), lambda j, i: (jnp.maximum(i * per - 1, 0), j)), pl.BlockSpec((tr, tc), lambda j, i: (i, j)),
                  pl.BlockSpec((tr, tc), lambda j, i: (i, nf + j)), pl.BlockSpec((FFN_K, tc), lambda j, i: (0, j)),
                  pl.BlockSpec((1, tc), lambda j, i: (0, j))],
        out_specs=pl.BlockSpec((tr, tc), lambda j, i: (i, j)), out_shape=_sds((s, f), BF16),
        scratch=[pltpu.VMEM((tr + hb, tc), F32)],
    )(u, u, u, w, b)


def _ffn_act_bwd(dz, u, w, b, name):
    s, f2 = u.shape
    f = f2 // 2
    hb = 8
    tr, tc = _pick(s, ROW_TILE, hb), _pick(f, 1408)
    per, nf, nt = tr // hb, f // tc, s // tr
    ext = tr + hb

    def body(gp_ref, g_ref, gn_ref, v_ref, vn_ref, dz_ref, dzn_ref, w_ref, b_ref, du_ref, dw_ref, db_ref, gcat_ref, dgc_ref):
        i = pl.program_id(1)
        last = i == nt - 1
        gcat_ref[0:hb, :] = jnp.where(i > 0, gp_ref[...], 0.0)
        gcat_ref[hb:hb + tr, :] = g_ref[...]
        gcat_ref[hb + tr:hb + tr + hb, :] = gn_ref[...]
        gc = jnp.zeros((ext, tc), F32) + b_ref[...]
        for k in range(FFN_K):
            gc = gc + w_ref[k:k + 1, :] * gcat_ref[pl.ds(hb - (FFN_K - 1) + k, ext), :]
        dz_cur = dz_ref[...]
        du_ref[1] = (dz_cur * _silu(gc[0:tr, :])).astype(BF16)
        dgc_ref[0:tr, :] = dz_cur * v_ref[...] * _dsilu(gc[0:tr, :])
        dgc_ref[tr:ext, :] = jnp.where(last, 0.0, dzn_ref[...] * vn_ref[...] * _dsilu(gc[tr:ext, :]))
        dgt = jnp.zeros((tr, tc), F32)
        for k in range(FFN_K):
            dgt = dgt + w_ref[k:k + 1, :] * dgc_ref[pl.ds(FFN_K - 1 - k, tr), :]
        du_ref[0] = dgt.astype(BF16)

        @pl.when(i == 0)
        def _():
            dw_ref[...] = jnp.zeros_like(dw_ref)
            db_ref[...] = jnp.zeros_like(db_ref)

        dgc_cur = dgc_ref[0:tr, :]
        db_ref[...] += _colsum(dgc_cur)
        for k in range(FFN_K):
            dw_ref[k:k + 1, :] += _colsum(dgc_cur * gcat_ref[pl.ds(hb - (FFN_K - 1) + k, tr), :])

    prev_map = lambda j, i: (jnp.maximum(i * per - 1, 0), j)
    next_map = lambda j, i: (jnp.minimum((i + 1) * per, s // hb - 1), j)
    next_map_v = lambda j, i: (jnp.minimum((i + 1) * per, s // hb - 1), nf + j)
    return _call(
        body, name=name, grid=(nf, nt),
        in_specs=[pl.BlockSpec((hb, tc), prev_map), pl.BlockSpec((tr, tc), lambda j, i: (i, j)), pl.BlockSpec((hb, tc), next_map),
                  pl.BlockSpec((tr, tc), lambda j, i: (i, nf + j)), pl.BlockSpec((hb, tc), next_map_v),
                  pl.BlockSpec((tr, tc), lambda j, i: (i, j)), pl.BlockSpec((hb, tc), next_map),
                  pl.BlockSpec((FFN_K, tc), lambda j, i: (0, j)), pl.BlockSpec((1, tc), lambda j, i: (0, j))],
        out_specs=[pl.BlockSpec((2, tr, tc), lambda j, i: (0, i, j)), pl.BlockSpec((FFN_K, tc), lambda j, i: (0, j)),
                   pl.BlockSpec((1, tc), lambda j, i: (0, j))],
        out_shape=[_sds((2, s, f), BF16), _sds((FFN_K, f), F32), _sds((1, f), F32)],
        scratch=[pltpu.VMEM((tr + 2 * hb, tc), F32), pltpu.VMEM((ext, tc), F32)],
    )(u, u, u, u, u, dz, dz, w, b)


def _rope_tables(pos, freq, sign, name):
    s = pos.shape[0]
    tr = _pick(s, 512, 8)

    def body(p_ref, f_ref, s_ref, c_ref, sn_ref):
        ang = p_ref[...].astype(F32) * f_ref[...]
        c_ref[...] = jnp.cos(ang)
        sn_ref[...] = jnp.sin(ang) * s_ref[...]

    return _call(body, name=name, grid=(s // tr,), in_specs=[pl.BlockSpec((tr, 1), lambda i: (i, 0)), _vec_spec(128), _vec_spec(128)],
                 out_specs=[_row_spec(tr, 128)] * 2, out_shape=[_sds((s, 128), F32)] * 2)(pos, freq, sign)


def _partner(v):
    lane = lax.broadcasted_iota(jnp.int32, v.shape, 1)
    lower = pltpu.roll(v, HEAD_DIM - ROT_DIM // 2, 1)
    upper = jnp.where(lane < ROT_DIM, pltpu.roll(v, ROT_DIM // 2, 1), 0.0)
    return jnp.where(lane < ROT_DIM // 2, lower, upper)


def _qk_norm_rope_fwd(q, kv, gq, gk, ctab, stab, name):
    s, w = q.shape
    tr = _pick(s, 128, 8)
    heads = w // HEAD_DIM

    def body(q_ref, k_ref, gq_ref, gk_ref, c_ref, s_ref, qn_ref, kn_ref):
        cv, sv = c_ref[...], s_ref[...]
        for src, g_ref, dst in ((q_ref, gq_ref, qn_ref), (k_ref, gk_ref, kn_ref)):
            gv = g_ref[...]
            for h in range(heads):
                cols = pl.ds(h * HEAD_DIM, HEAD_DIM)
                xv = src[:, cols]
                nv = xv * lax.rsqrt(jnp.mean(xv * xv, axis=-1, keepdims=True) + EPS) * gv
                dst[:, cols] = nv * cv + _partner(nv) * sv

    return _call(
        body, name=name, grid=(s // tr,),
        in_specs=[_row_spec(tr, w), _row_spec(tr, w), _vec_spec(128), _vec_spec(128), _row_spec(tr, 128), _row_spec(tr, 128)],
        out_specs=[_row_spec(tr, w)] * 2, out_shape=[_sds((s, w), F32)] * 2,
    )(q, kv, gq, gk, ctab, stab)


def _qk_norm_rope_bwd(dqs, dks, dvs, q, kv, gq, gk, ctab, stab, name):
    s, w = q.shape
    gw = w // N_GROUPS
    tr = _pick(s, 128, 8)
    hpg = gw // HEAD_DIM

    def body(*refs):
        dq_refs, dk_refs, dv_refs = refs[0:3], refs[3:6], refs[6:9]
        q_ref, k_ref, gq_ref, gk_ref, c_ref, s_ref, dq_ref, dkv_ref, dgq_ref, dgk_ref = refs[9:]
        i = pl.program_id(0)
        cv, sv = c_ref[...], s_ref[...]
        for d_refs, src, g_ref, dst, dg_ref in ((dq_refs, q_ref, gq_ref, dq_ref, dgq_ref), (dk_refs, k_ref, gk_ref, dkv_ref, dgk_ref)):
            gv = g_ref[...]
            dg = jnp.zeros((1, HEAD_DIM), F32)
            for g in range(N_GROUPS):
                for h in range(hpg):
                    cols = pl.ds(g * gw + h * HEAD_DIM, HEAD_DIM)
                    dout = d_refs[g][:, pl.ds(h * HEAD_DIM, HEAD_DIM)]
                    dn = dout * cv + _partner(dout * sv)
                    xv = src[:, cols]
                    rstd = lax.rsqrt(jnp.mean(xv * xv, axis=-1, keepdims=True) + EPS)
                    xh = xv * rstd
                    dg = dg + _colsum(dn * xh)
                    dxh = dn * gv
                    dst[:, cols] = (rstd * (dxh - xh * jnp.mean(dxh * xh, axis=-1, keepdims=True))).astype(BF16)
            _acc(dg_ref, dg, i)
        for g in range(N_GROUPS):
            dkv_ref[:, pl.ds(w + g * gw, gw)] = dv_refs[g][...].astype(BF16)

    return _call(
        body, name=name, grid=(s // tr,),
        in_specs=[_row_spec(tr, gw)] * 9 + [_row_spec(tr, w), _row_spec(tr, w), _vec_spec(128), _vec_spec(128),
                                            _row_spec(tr, 128), _row_spec(tr, 128)],
        out_specs=[_row_spec(tr, w), _row_spec(tr, 2 * w), _vec_spec(128), _vec_spec(128)],
        out_shape=[_sds((s, w), BF16), _sds((s, 2 * w), BF16), _sds((1, 128), F32), _sds((1, 128), F32)],
    )(*dqs, *dks, *dvs, q, kv, gq, gk, ctab, stab)


def _rows(j, b, r):
    start = j + r * BLK * b
    return pl.ds(start, BLK, stride=r) if r > 1 else pl.ds(start, BLK)


def _dot_nt(a, b):
    return lax.dot_general(a, b, _DIMS["nt"], preferred_element_type=F32)


def _dot_nn(a, b):
    return lax.dot_general(a, b, _DIMS["nn"], preferred_element_type=F32)


def _band_masks():
    qi = lax.broadcasted_iota(jnp.int32, (BLK, BLK), 0)
    kj = lax.broadcasted_iota(jnp.int32, (BLK, BLK), 1)
    return kj <= qi, kj >= qi


def _attn_fwd(qn, kn, kv, g, name):
    s, w = qn.shape
    r = DILATIONS[g]
    gw = w // N_GROUPS
    cr = min(ATTN_CHUNK, s)
    nb = cr // (BLK * r)
    hp = 1
    cw = hp * HEAD_DIM
    gc = gw // cw
    scale = 1.0 / math.sqrt(HEAD_DIM)

    def body(q_ref, kc_ref, kp_ref, vc_ref, vp_ref, o_ref, l_ref):
        n = pl.program_id(1)
        same_m, prev_m = _band_masks()
        prev_first = jnp.logical_and(prev_m, n > 0)
        for h in range(hp):
            cols = pl.ds(h * HEAD_DIM, HEAD_DIM)
            for j in range(r):
                for b in range(nb):
                    rows = _rows(j, b, r)
                    qv = q_ref[rows, cols].astype(BF16)
                    kc, vc = kc_ref[rows, cols].astype(BF16), vc_ref[rows, cols].astype(BF16)
                    if b > 0:
                        rp = _rows(j, b - 1, r)
                        kp, vp, pm = kc_ref[rp, cols].astype(BF16), vc_ref[rp, cols].astype(BF16), prev_m
                    else:
                        rp = _rows(j, nb - 1, r)
                        kp, vp, pm = kp_ref[rp, cols].astype(BF16), vp_ref[rp, cols].astype(BF16), prev_first
                    sd = jnp.where(same_m, _dot_nt(qv, kc) * scale, NEG)
                    so = jnp.where(pm, _dot_nt(qv, kp) * scale, NEG)
                    m = jnp.maximum(jnp.max(sd, axis=-1, keepdims=True), jnp.max(so, axis=-1, keepdims=True))
                    pd, po = jnp.exp(sd - m), jnp.exp(so - m)
                    den = jnp.sum(pd, axis=-1, keepdims=True) + jnp.sum(po, axis=-1, keepdims=True)
                    ov = (_dot_nn(pd.astype(BF16), vc) + _dot_nn(po.astype(BF16), vp)) / den
                    o_ref[rows, cols] = ov
                    l_ref[rows, cols] = jnp.broadcast_to(m + jnp.log(den), (BLK, HEAD_DIM))

    cur = lambda base: (lambda c, n: (n, base + c))
    prv = lambda base: (lambda c, n: (jnp.maximum(n - 1, 0), base + c))
    qb, kb, vb = g * gc, g * gc, (N_GROUPS + g) * gc
    return _call(
        body, name=name, grid=(gc, s // cr),
        in_specs=[pl.BlockSpec((cr, cw), cur(qb)), pl.BlockSpec((cr, cw), cur(kb)), pl.BlockSpec((cr, cw), prv(kb)),
                  pl.BlockSpec((cr, cw), cur(vb)), pl.BlockSpec((cr, cw), prv(vb))],
        out_specs=[pl.BlockSpec((cr, cw), lambda c, n: (n, c))] * 2, out_shape=[_sds((s, gw), F32)] * 2,
    )(qn, kn, kn, kv, kv)


def _attn_combine(os_, lses, name):
    s, gw = os_[0].shape
    tr = _pick(s, ROW_TILE, 8)

    def body(o0, o1, o2, l0, l1, l2, o_ref, l_ref):
        a, b, c = l0[...], l1[...], l2[...]
        m = jnp.maximum(jnp.maximum(a, b), c)
        ea, eb, ec = jnp.exp(a - m), jnp.exp(b - m), jnp.exp(c - m)
        den = ea + eb + ec
        o_ref[...] = ((ea * o0[...] + eb * o1[...] + ec * o2[...]) / den).astype(BF16)
        l_ref[...] = m + jnp.log(den)

    return _call(body, name=name, grid=(s // tr,), in_specs=[_row_spec(tr, gw)] * 6, out_specs=[_row_spec(tr, gw)] * 2,
                 out_shape=[_sds((s, gw), BF16), _sds((s, gw), F32)])(*os_, *lses)


def _attn_delta(do, o, name):
    s, gw = do.shape
    tr = _pick(s, ROW_TILE, 8)

    def body(do_ref, o_ref, d_ref):
        for h in range(gw // HEAD_DIM):
            cols = pl.ds(h * HEAD_DIM, HEAD_DIM)
            t = jnp.sum(do_ref[:, cols] * o_ref[:, cols].astype(F32), axis=-1, keepdims=True)
            d_ref[:, cols] = jnp.broadcast_to(t, (tr, HEAD_DIM))

    return _call(body, name=name, grid=(s // tr,), in_specs=[_row_spec(tr, gw)] * 2, out_specs=_row_spec(tr, gw),
                 out_shape=_sds((s, gw), F32))(do, o)


def _pair_grads(qv, kv_, vv, dov, lse, delta, mask, scale):
    sc = jnp.where(mask, _dot_nt(qv, kv_) * scale, NEG)
    p = jnp.exp(sc - lse)
    ds = p * (_dot_nt(dov, vv) - delta) * scale
    return p, ds


def _attn_bwd_dq(qn, kn, kv, do, lse, delta, g, name):
    s, w = qn.shape
    r = DILATIONS[g]
    gw = w // N_GROUPS
    cr = min(ATTN_CHUNK, s)
    nb = cr // (BLK * r)
    cw = HEAD_DIM
    gc = gw // cw
    scale = 1.0 / math.sqrt(HEAD_DIM)

    def body(q_ref, kc_ref, kp_ref, vc_ref, vp_ref, do_ref, l_ref, d_ref, dq_ref):
        n = pl.program_id(1)
        same_m, prev_m = _band_masks()
        prev_first = jnp.logical_and(prev_m, n > 0)
        for j in range(r):
            for b in range(nb):
                rows = _rows(j, b, r)
                qv, dov = q_ref[rows, :].astype(BF16), do_ref[rows, :].astype(BF16)
                lse, delta = l_ref[rows, :], d_ref[rows, :]
                kc, vc = kc_ref[rows, :].astype(BF16), vc_ref[rows, :].astype(BF16)
                if b > 0:
                    rp = _rows(j, b - 1, r)
                    kp, vp, pm = kc_ref[rp, :].astype(BF16), vc_ref[rp, :].astype(BF16), prev_m
                else:
                    rp = _rows(j, nb - 1, r)
                    kp, vp, pm = kp_ref[rp, :].astype(BF16), vp_ref[rp, :].astype(BF16), prev_first
                _, dsd = _pair_grads(qv, kc, vc, dov, lse, delta, same_m, scale)
                _, dso = _pair_grads(qv, kp, vp, dov, lse, delta, pm, scale)
                dq_ref[rows, :] = _dot_nn(dsd.astype(BF16), kc) + _dot_nn(dso.astype(BF16), kp)

    cur = lambda base: (lambda c, n: (n, base + c))
    prv = lambda base: (lambda c, n: (jnp.maximum(n - 1, 0), base + c))
    qb, vb = g * gc, (N_GROUPS + g) * gc
    own = pl.BlockSpec((cr, cw), lambda c, n: (n, c))
    return _call(
        body, name=name, grid=(gc, s // cr),
        in_specs=[pl.BlockSpec((cr, cw), cur(qb)), pl.BlockSpec((cr, cw), cur(qb)), pl.BlockSpec((cr, cw), prv(qb)),
                  pl.BlockSpec((cr, cw), cur(vb)), pl.BlockSpec((cr, cw), prv(vb)), own, own, own],
        out_specs=own, out_shape=_sds((s, gw), F32),
    )(qn, kn, kn, kv, kv, do, lse, delta)


def _attn_bwd_dkv(qn, kn, kv, do, lse, delta, g, name):
    s, w = qn.shape
    r = DILATIONS[g]
    gw = w // N_GROUPS
    cr = min(ATTN_CHUNK, s)
    nb = cr // (BLK * r)
    nchunk = s // cr
    cw = HEAD_DIM
    gc = gw // cw
    scale = 1.0 / math.sqrt(HEAD_DIM)

    def body(k_ref, v_ref, qc_ref, qx_ref, doc_ref, dox_ref, lc_ref, lx_ref, dc_ref, dx_ref, dk_ref, dv_ref):
        n = pl.program_id(1)
        same_m, prev_m = _band_masks()
        next_last = jnp.logical_and(prev_m, n < nchunk - 1)
        for j in range(r):
            for b in range(nb):
                rows = _rows(j, b, r)
                kv_, vv = k_ref[rows, :].astype(BF16), v_ref[rows, :].astype(BF16)
                qv, dov = qc_ref[rows, :].astype(BF16), doc_ref[rows, :].astype(BF16)
                pd, dsd = _pair_grads(qv, kv_, vv, dov, lc_ref[rows, :], dc_ref[rows, :], same_m, scale)
                if b < nb - 1:
                    rx = _rows(j, b + 1, r)
                    qx, dox, lx, dlx, xm = qc_ref[rx, :], doc_ref[rx, :], lc_ref[rx, :], dc_ref[rx, :], prev_m
                else:
                    rx = _rows(j, 0, r)
                    qx, dox, lx, dlx, xm = qx_ref[rx, :], dox_ref[rx, :], lx_ref[rx, :], dx_ref[rx, :], next_last
                qx, dox = qx.astype(BF16), dox.astype(BF16)
                po, dso = _pair_grads(qx, kv_, vv, dox, lx, dlx, xm, scale)
                dk_ref[rows, :] = _dot_nn(dsd.T.astype(BF16), qv) + _dot_nn(dso.T.astype(BF16), qx)
                dv_ref[rows, :] = _dot_nn(pd.T.astype(BF16), dov) + _dot_nn(po.T.astype(BF16), dox)

    cur = lambda base: (lambda c, n: (n, base + c))
    nxt = lambda base: (lambda c, n: (jnp.minimum(n + 1, nchunk - 1), base + c))
    qb, vb = g * gc, (N_GROUPS + g) * gc
    blk = lambda f: pl.BlockSpec((cr, cw), f)
    return _call(
        body, name=name, grid=(gc, nchunk),
        in_specs=[blk(cur(qb)), blk(cur(vb)), blk(cur(qb)), blk(nxt(qb)), blk(cur(0)), blk(nxt(0)), blk(cur(0)), blk(nxt(0)),
                  blk(cur(0)), blk(nxt(0))],
        out_specs=[blk(cur(0))] * 2, out_shape=[_sds((s, gw), F32)] * 2,
    )(kn, kv, qn, qn, do, do, lse, lse, delta, delta)


def _mod_proj(sc_all, w, name):
    l, d, ns = w.shape
    tn = _pick(ns, 512)

    def body(c_ref, w_ref, o_ref):
        o_ref[...] = jnp.dot(c_ref[...].astype(BF16), w_ref[...].astype(BF16), preferred_element_type=F32)

    return _call(
        body, name=name, grid=(l, ns // tn),
        in_specs=[pl.BlockSpec((N_DEV, d), lambda a, j: (0, 0)), pl.BlockSpec((None, d, tn), lambda a, j: (a, 0, j))],
        out_specs=pl.BlockSpec((None, N_DEV, tn), lambda a, j: (a, 0, j)), out_shape=_sds((l, N_DEV, ns), F32),
    )(sc_all, w)


def _adamw_math(w, g, m, v):
    m = ADAM_B1 * m + (1.0 - ADAM_B1) * g
    v = ADAM_B2 * v + (1.0 - ADAM_B2) * (g * g)
    m_hat = m / (1.0 - ADAM_B1 ** ADAM_STEP)
    v_hat = v / (1.0 - ADAM_B2 ** ADAM_STEP)
    delta = -ADAM_LR * (m_hat / (jnp.sqrt(v_hat) + ADAM_EPS) + ADAM_WD * w)
    return delta, m, v


def _adamw_big(w, g, m, v, name):
    shape = w.shape
    cols = shape[-1]
    rows = math.prod(shape[:-1])
    tr, tc = _pick(rows, 512, 8), _pick(cols, 1024)
    w2, g2, m2, v2 = (t.reshape(rows, cols) for t in (w, g, m, v))

    def body(w_ref, g_ref, m_ref, v_ref, d_ref, mo_ref, vo_ref):
        d_ref[...], mo_ref[...], vo_ref[...] = _adamw_math(w_ref[...], g_ref[...], m_ref[...], v_ref[...])

    spec = pl.BlockSpec((tr, tc), lambda i, j: (i, j))
    outs = _call(body, name=name, grid=(rows // tr, cols // tc), in_specs=[spec] * 4, out_specs=[spec] * 3,
                 out_shape=[_sds((rows, cols), F32)] * 3)(w2, g2, m2, v2)
    return [t.reshape(shape) for t in outs]


def _adamw_mod(w, sct, dm, m, v, name):
    l, d, ns = w.shape
    tr, tc = _pick(d, 512, 8), _pick(ns, 1024)

    def body(w_ref, c_ref, dm_ref, m_ref, v_ref, g_ref, d_ref, mo_ref, vo_ref):
        cv, dv = c_ref[...].astype(BF16).astype(F32), dm_ref[...].astype(BF16).astype(F32)
        g = jnp.zeros((tr, tc), F32)
        for e in range(N_DEV):
            g = g + cv[:, e:e + 1] * dv[e:e + 1, :]
        g_ref[...] = g
        d_ref[...], mo_ref[...], vo_ref[...] = _adamw_math(w_ref[...], g, m_ref[...], v_ref[...])

    spec = pl.BlockSpec((None, tr, tc), lambda a, i, j: (a, i, j))
    return _call(
        body, name=name, grid=(l, d // tr, ns // tc),
        in_specs=[spec, pl.BlockSpec((tr, N_DEV), lambda a, i, j: (i, 0)), pl.BlockSpec((None, N_DEV, tc), lambda a, i, j: (a, 0, j)),
                  spec, spec],
        out_specs=[spec] * 4, out_shape=[_sds((l, d, ns), F32)] * 4,
    )(w, sct, dm, m, v)


def _adamw_small(ws, gs, ms, vs, name):
    n = len(ws)

    def body(*refs):
        w_r, g_r, m_r, v_r = refs[0:n], refs[n:2 * n], refs[2 * n:3 * n], refs[3 * n:4 * n]
        d_o, m_o, v_o = refs[4 * n:5 * n], refs[5 * n:6 * n], refs[6 * n:7 * n]
        for k in range(n):
            d_o[k][...], m_o[k][...], v_o[k][...] = _adamw_math(w_r[k][...], g_r[k][...], m_r[k][...], v_r[k][...])

    vm = pl.BlockSpec(memory_space=pltpu.VMEM)
    shapes = [_sds(w.shape, F32) for w in ws]
    outs = pl.pallas_call(body, name=name, in_specs=[vm] * (4 * n), out_specs=[vm] * (3 * n), out_shape=shapes * 3,
                          interpret=False)(*ws, *gs, *ms, *vs)
    return outs[0:n], outs[n:2 * n], outs[2 * n:3 * n]


def _sum_rows(a, name):
    n, v = a.shape
    tc = _pick(v, 8192)

    def body(a_ref, o_ref):
        acc = a_ref[0:1, :]
        for e in range(1, n):
            acc = acc + a_ref[e:e + 1, :]
        o_ref[...] = acc

    return _call(body, name=name, grid=(v // tc,), in_specs=[pl.BlockSpec((n, tc), lambda j: (0, j))],
                 out_specs=pl.BlockSpec((1, tc), lambda j: (0, j)), out_shape=_sds((1, v), F32))(a)


def _cast_into_full(w2d, kind, slot_arr, name):
    rows, cols = w2d.shape
    tr, tc = _pick(rows, 512, 16), _pick(cols, 1024)
    nr, nc = rows // tr, cols // tc

    def body(s_ref, w_ref, o_ref):
        o_ref[...] = w_ref[...].astype(BF16)

    if kind == "col":
        o_map = lambda i, j, s_ref: (i, s_ref[0] * nc + j)
    else:
        o_map = lambda i, j, s_ref: (s_ref[0] * nr + i, j)
    return _call(body, name=name, grid=(nr, nc), in_specs=[pl.BlockSpec((tr, tc), lambda i, j, s_ref: (i, j))],
                 out_specs=pl.BlockSpec((tr, tc), o_map), out_shape=_sds(_full_shape(kind, (rows, cols)), BF16), nsp=1)(slot_arr, w2d)


def _place():
    x, y, c = lax.axis_index("x"), lax.axis_index("y"), lax.axis_index("c")
    chips = [(1 - x, y), (x, 1 - y), (1 - x, 1 - y)]
    return x, y, c, chips


def _remote(src, dst, send_sem, recv_sem, to):
    return pltpu.make_async_remote_copy(src_ref=src, dst_ref=dst, send_sem=send_sem, recv_sem=recv_sem, device_id=to,
                                        device_id_type=MESH)


def _allgather8(a, name):
    m_per, n = a.shape

    def body(x_ref, out_ref, send_sems, recv_sems, local_sem):
        x, y, c, chips = _place()
        me, sibling = (x, y, c), (x, y, 1 - c)

        def rows(px, py, pc):
            return out_ref.at[pl.ds((4 * px + 2 * py + pc) * m_per, m_per), :]

        def copy(k, block, to, src=None):
            return _remote(rows(*block) if src is None else src, rows(*block), send_sems.at[k], recv_sems.at[k], to)

        mine = pltpu.make_async_copy(x_ref, rows(*me), local_sem)
        mine.start()
        first = [copy(0, me, sibling, src=x_ref)]
        first += [copy(1 + j, me, (*chip, c), src=x_ref) for j, chip in enumerate(chips)]
        for cp in first:
            cp.start()
        passed = [copy(4 + j, (*chip, c), sibling) for j, chip in enumerate(chips)]
        for j, chip in enumerate(chips):
            copy(1 + j, (*chip, c), me).wait_recv()
            passed[j].start()
        copy(0, sibling, me).wait_recv()
        for j, chip in enumerate(chips):
            copy(4 + j, (*chip, 1 - c), me).wait_recv()
        for cp in first + passed:
            cp.wait_send()
        mine.wait()

    return pl.pallas_call(
        body, name=name, out_shape=_sds((N_DEV * m_per, n), a.dtype),
        in_specs=[pl.BlockSpec(memory_space=pltpu.VMEM)], out_specs=pl.BlockSpec(memory_space=pltpu.VMEM),
        scratch_shapes=[pltpu.SemaphoreType.DMA((7,)), pltpu.SemaphoreType.DMA((7,)), pltpu.SemaphoreType.DMA],
        interpret=False,
    )(a)


def _region(ref, kind, shard_shape, slot, half):
    r, cs = shard_shape
    hr = r // 2
    if kind == "col":
        return ref.at[pl.ds(half * hr, hr), pl.ds(slot * cs, cs)]
    return ref.at[pl.ds(slot * r + half * hr, hr), :]


def _full_shape(kind, shard_shape):
    r, cs = shard_shape
    return (r, 4 * cs) if kind == "col" else (4 * r, cs)


_HBM = pl.BlockSpec(memory_space=pltpu.HBM)


def _gather_weights(fulls, kinds, shapes, name):
    n = len(fulls)

    def body(*refs):
        ins, outs = refs[0:n], refs[n:2 * n]
        send_sems, recv_sems = refs[2 * n:]
        x, y, c, chips = _place()
        me_slot, sibling = 2 * x + y, (x, y, 1 - c)
        slots = [2 * cx + cy for cx, cy in chips]
        region = lambda a, slot, half: _region(outs[a], kinds[a], shapes[a], slot, half)
        first, passed = [], []
        for a in range(n):
            for j, chip in enumerate(chips):
                cp = _remote(_region(ins[a], kinds[a], shapes[a], me_slot, c), region(a, me_slot, c), send_sems.at[a, j],
                             recv_sems.at[a, j], (*chip, c))
                cp.start()
                first.append(cp)
        for a in range(n):
            for j in range(3):
                land = region(a, slots[j], c)
                _remote(land, land, send_sems.at[a, j], recv_sems.at[a, j], sibling).wait_recv()
                cp = _remote(land, land, send_sems.at[a, 3 + j], recv_sems.at[a, 3 + j], sibling)
                cp.start()
                passed.append(cp)
        for a in range(n):
            for j in range(3):
                land = region(a, slots[j], 1 - c)
                _remote(land, land, send_sems.at[a, 3 + j], recv_sems.at[a, 3 + j], sibling).wait_recv()
        for cp in first + passed:
            cp.wait_send()

    return pl.pallas_call(
        body, name=name, out_shape=[_sds(t.shape, BF16) for t in fulls],
        in_specs=[_HBM] * n, out_specs=[_HBM] * n, input_output_aliases={a: a for a in range(n)},
        scratch_shapes=[pltpu.SemaphoreType.DMA((n, 6)), pltpu.SemaphoreType.DMA((n, 6))],
        interpret=False,
    )(*fulls)


def _rs_pair_exchange(grads, kinds, shapes, name):
    n = len(grads)

    def body(*refs):
        ins, outs = refs[0:n], refs[n:2 * n]
        send_sems, recv_sems = refs[2 * n:]
        x, y, c, _ = _place()
        sibling = (x, y, 1 - c)
        sent = []
        for a in range(n):
            for slot in range(4):
                cp = _remote(_region(ins[a], kinds[a], shapes[a], slot, 1 - c), outs[a].at[slot], send_sems.at[a, slot],
                             recv_sems.at[a, slot], sibling)
                cp.start()
                sent.append(cp)
        for a in range(n):
            for slot in range(4):
                _remote(_region(ins[a], kinds[a], shapes[a], slot, c), outs[a].at[slot], send_sems.at[a, slot],
                        recv_sems.at[a, slot], sibling).wait_recv()
        for cp in sent:
            cp.wait_send()

    return pl.pallas_call(
        body, name=name, out_shape=[_sds((4, s[0] // 2, s[1]), F32) for s in shapes], in_specs=[_HBM] * n, out_specs=[_HBM] * n,
        scratch_shapes=[pltpu.SemaphoreType.DMA((n, 4)), pltpu.SemaphoreType.DMA((n, 4))], interpret=False,
    )(*grads)


def _rs_pair_add(grad, recv, kind, shape, c_arr, name):
    r, cs = shape
    hr = r // 2
    tr, tc = _pick(hr, 512, 16), _pick(cs, 1024)
    nr, nc = hr // tr, cs // tc

    def body(c_ref, g_ref, r_ref, o_ref):
        o_ref[...] = (g_ref[...] + r_ref[...]).astype(BF16)

    if kind == "col":
        g_map = lambda s_, i, j, c_ref: (c_ref[0] * nr + i, s_ * nc + j)
    else:
        g_map = lambda s_, i, j, c_ref: (s_ * 2 * nr + c_ref[0] * nr + i, j)
    own = pl.BlockSpec((None, tr, tc), lambda s_, i, j, c_ref: (s_, i, j))
    return _call(body, name=name, grid=(4, nr, nc), in_specs=[pl.BlockSpec((tr, tc), g_map), own], out_specs=own,
                 out_shape=_sds((4, hr, cs), BF16), nsp=1)(c_arr, grad, recv)


def _rs_chip_exchange(parts, name):
    n = len(parts)

    def body(*refs):
        ins, outs = refs[0:n], refs[n:2 * n]
        send_sems, recv_sems = refs[2 * n:]
        x, y, c, chips = _place()
        slots = [2 * cx + cy for cx, cy in chips]
        sent = []
        for a in range(n):
            for j, chip in enumerate(chips):
                cp = _remote(ins[a].at[slots[j]], outs[a].at[j], send_sems.at[a, j], recv_sems.at[a, j], (*chip, c))
                cp.start()
                sent.append(cp)
        for a in range(n):
            for j, chip in enumerate(chips):
                _remote(ins[a].at[slots[j]], outs[a].at[j], send_sems.at[a, j], recv_sems.at[a, j], (*chip, c)).wait_recv()
        for cp in sent:
            cp.wait_send()

    return pl.pallas_call(
        body, name=name, out_shape=[_sds((3,) + p.shape[1:], BF16) for p in parts], in_specs=[_HBM] * n, out_specs=[_HBM] * n,
        scratch_shapes=[pltpu.SemaphoreType.DMA((n, 3)), pltpu.SemaphoreType.DMA((n, 3))], interpret=False,
    )(*parts)


def _rs_chip_add(part, recv, slot_arr, c_arr, dest, layer, out_shape, name):
    _, hr, cs = part.shape
    tr, tc = _pick(hr, 512, 16), _pick(cs, 1024)
    nr = hr // tr

    def body(s_ref, c_ref, p_ref, r0_ref, r1_ref, r2_ref, *rest):
        o_ref = rest[-1]
        o_ref[...] = ((p_ref[...].astype(F32) + r0_ref[...].astype(F32)) + r1_ref[...].astype(F32)) + r2_ref[...].astype(F32)

    rk = lambda k: pl.BlockSpec((None, tr, tc), lambda i, j, s_ref, c_ref: (k, i, j))
    if layer is None:
        o_spec = pl.BlockSpec((tr, tc), lambda i, j, s_ref, c_ref: (c_ref[0] * nr + i, j))
    else:
        o_spec = pl.BlockSpec((None, tr, tc), lambda i, j, s_ref, c_ref: (layer, c_ref[0] * nr + i, j))
    in_specs = [pl.BlockSpec((None, tr, tc), lambda i, j, s_ref, c_ref: (s_ref[0], i, j)), rk(0), rk(1), rk(2)]
    args = [slot_arr, c_arr, part, recv, recv, recv]
    aliases = {}
    if dest is not None:
        in_specs.append(pl.BlockSpec(memory_space=pl.ANY))
        args.append(dest)
        aliases = {6: 0}
    return pl.pallas_call(
        body, name=name,
        grid_spec=pltpu.PrefetchScalarGridSpec(num_scalar_prefetch=2, grid=(nr, cs // tc), in_specs=in_specs, out_specs=o_spec),
        out_shape=_sds(out_shape, F32), input_output_aliases=aliases,
        compiler_params=pltpu.CompilerParams(dimension_semantics=("arbitrary",) * 2, vmem_limit_bytes=VMEM_LIMIT_MB << 20),
        interpret=False,
    )(*args)


def _rs_pair_share(shards, name):
    n = len(shards)
    views = []
    for a, t in enumerate(shards):
        views += [(a, None)] if t.ndim == 2 else [(a, l) for l in range(t.shape[0])]
    nv = len(views)

    def body(*refs):
        ins, outs = refs[0:n], refs[n:2 * n]
        send_sems, recv_sems = refs[2 * n:]
        x, y, c, _ = _place()
        sibling = (x, y, 1 - c)

        def rows(ref_list, k, half):
            a, layer = views[k]
            ref = ref_list[a] if layer is None else ref_list[a].at[layer]
            hr = ref.shape[0] // 2
            return ref.at[pl.ds(half * hr, hr), :]

        sent = []
        for k in range(nv):
            cp = _remote(rows(ins, k, c), rows(outs, k, c), send_sems.at[k], recv_sems.at[k], sibling)
            cp.start()
            sent.append(cp)
        for k in range(nv):
            _remote(rows(ins, k, 1 - c), rows(outs, k, 1 - c), send_sems.at[k], recv_sems.at[k], sibling).wait_recv()
        for cp in sent:
            cp.wait_send()

    return pl.pallas_call(
        body, name=name, out_shape=[_sds(t.shape, F32) for t in shards], in_specs=[_HBM] * n, out_specs=[_HBM] * n,
        input_output_aliases={a: a for a in range(n)},
        scratch_shapes=[pltpu.SemaphoreType.DMA((nv,)), pltpu.SemaphoreType.DMA((nv,))], interpret=False,
    )(*shards)


def _pad_to(v, mult):
    n = v.shape[0]
    return jnp.pad(v, (0, (-n) % mult))


def kernel(x, c, positions, mod_w, mod_b, norm_mix_g, norm_ffn_g, conv_pw1_w, conv_pw1_b, conv_dw_w, conv_dw_b, conv_ln_g, conv_ln_b, conv_pw2_w, conv_pw2_b, kv_mod_w, kv_mod_b, kv_norm_g, w_kv, k_norm_g, w_q, q_norm_g, w_o, ffn_up_w, ffn_dw_w, ffn_dw_b, ffn_down_w, loss_target, m_mod_w, m_mod_b, m_norm_mix_g, m_norm_ffn_g, m_conv_pw1_w, m_conv_pw1_b, m_conv_dw_w, m_conv_dw_b, m_conv_ln_g, m_conv_ln_b, m_conv_pw2_w, m_conv_pw2_b, m_kv_mod_w, m_kv_mod_b, m_kv_norm_g, m_w_kv, m_k_norm_g, m_w_q, m_q_norm_g, m_w_o, m_ffn_up_w, m_ffn_dw_w, m_ffn_dw_b, m_ffn_down_w, v_mod_w, v_mod_b, v_norm_mix_g, v_norm_ffn_g, v_conv_pw1_w, v_conv_pw1_b, v_conv_dw_w, v_conv_dw_b, v_conv_ln_g, v_conv_ln_b, v_conv_pw2_w, v_conv_pw2_b, v_kv_mod_w, v_kv_mod_b, v_kv_norm_g, v_w_kv, v_k_norm_g, v_w_q, v_q_norm_g, v_w_o, v_ffn_up_w, v_ffn_dw_w, v_ffn_dw_b, v_ffn_down_w):
    _, s, d = x.shape
    f = ffn_dw_b.shape[1]
    qw = w_q.shape[2] * 4
    ax, ay, ac = lax.axis_index("x"), lax.axis_index("y"), lax.axis_index("c")
    slot = 2 * ax + ay
    me8 = 4 * ax + 2 * ay + ac
    slot_arr = jnp.reshape(slot, (1,)).astype(jnp.int32)
    c_arr = jnp.reshape(ac, (1,)).astype(jnp.int32)
    x2 = x.reshape(s, d)
    target = loss_target.reshape(s, d)
    row = lambda v: v.reshape(1, -1)

    c_all = _allgather8(c.reshape(8, d // 8), "gather_c").reshape(N_DEV, d)
    sc_all = jax.nn.silu(c_all)
    mod_part = _mod_proj(sc_all, mod_w, "mod_proj")
    kvm_part = _mod_proj(sc_all, kv_mod_w[None], "kvmod_proj")
    nm, nk = mod_part.shape[2], kvm_part.shape[2]
    small_sharded = [conv_pw1_b, conv_dw_w, conv_dw_b, conv_ln_g, conv_ln_b, conv_pw2_b, ffn_dw_w]
    pack = jnp.concatenate([mod_part.reshape(-1), kvm_part.reshape(-1)] + [t.reshape(-1) for t in small_sharded])
    plen = pack.shape[0]
    pack = _pad_to(pack, 1024)
    gathered = _allgather8(pack.reshape(8, -1), "gather_mod").reshape(4, 2, -1)[:, 0, :plen]
    off = 0

    def take(n_el):
        nonlocal off
        out = lax.slice_in_dim(gathered, off, off + n_el, axis=1)
        off += n_el
        return out

    mod_g = take(2 * N_DEV * nm).reshape(4, 2, N_DEV, nm)
    kvm_g = take(N_DEV * nk).reshape(4, N_DEV, nk)
    mine = lambda t, axis: lax.dynamic_index_in_dim(t, me8, axis=axis, keepdims=False)
    mod_vec = jnp.transpose(mine(mod_g, 2), (1, 0, 2)).reshape(2, 4 * nm) + mod_b
    kvm_vec = mine(kvm_g, 1).reshape(4 * nk) + kv_mod_b
    pw1_b_full = take(conv_pw1_b.shape[1]).reshape(1, -1)
    dw_w_full = jnp.transpose(take(CONV_K * (d // 4)).reshape(4, CONV_K, d // 4), (1, 0, 2)).reshape(CONV_K, d)
    dw_b_full, ln_g_full, ln_b_full, pw2_b_full = (take(d // 4).reshape(1, d) for _ in range(4))
    fdw_full = jnp.transpose(take(2 * FFN_K * (f // 4)).reshape(4, 2, FFN_K, f // 4), (1, 2, 0, 3)).reshape(2, FFN_K, f)
    mods = [[row(mod_vec[l, k * d:(k + 1) * d]) for k in range(6)] for l in range(2)]
    kv_sh, kv_sc = row(kvm_vec[:d]), row(kvm_vec[d:])
    zero_d = jnp.zeros((1, d), F32)

    big = [("pw1", conv_pw1_w[0], "col"), ("pw2", conv_pw2_w[0], "row"), ("wkv", w_kv, "col"), ("wq", w_q[0], "col"),
           ("wo", w_o[0], "col"), ("up0", ffn_up_w[0], "col"), ("up1", ffn_up_w[1], "col"), ("dn0", ffn_down_w[0], "row"),
           ("dn1", ffn_down_w[1], "row")]
    names = [b[0] for b in big]
    kinds = [b[2] for b in big]
    shard_shapes = [b[1].shape for b in big]
    own = [_cast_into_full(b[1], b[2], slot_arr, "cast_" + b[0]) for b in big]
    full = dict(zip(names, _gather_weights(own, kinds, shard_shapes, "gather_weights")))

    def ffn_fwd(l, x_in, h, tag):
        u = _matmul(h, full["up%d" % l], "nn", F32, "mm_up" + tag)
        z = _ffn_act_fwd(u, fdw_full[l], row(ffn_dw_b[l]), "ffn_act" + tag)
        y = _matmul(z, full["dn%d" % l], "nn", F32, "mm_down" + tag)
        return u, z, y

    sh_m0, sc_m0, g_m0, sh_f0, sc_f0, g_f0 = mods[0]
    sh_m1, sc_m1, g_m1, sh_f1, sc_f1, g_f1 = mods[1]
    gmix0, gmix1, gffn0, gffn1 = row(norm_mix_g[0]), row(norm_mix_g[1]), row(norm_ffn_g[0]), row(norm_ffn_g[1])
    (h0,) = _resid_mod(x2, None, [(gmix0, sc_m0, sh_m0)], None, "mod_in")
    u0 = _matmul(h0, full["pw1"], "nn", F32, "mm_pw1")
    glu = _glu_fwd(u0, pw1_b_full, "glu")
    dwc = _dwconv_fwd(glu, dw_w_full, dw_b_full, "dwconv")
    act = _ln_silu_fwd(dwc, ln_g_full, ln_b_full, "ln_silu")
    y0 = _matmul(act, full["pw2"], "nn", F32, "mm_pw2")
    x1, hf0 = _resid_mod(x2, (y0, pw2_b_full, g_m0), [(gffn0, sc_f0, sh_f0)], None, "resid_conv")
    uf0, zf0, yf0 = ffn_fwd(0, x1, hf0, "0")
    gkv = row(kv_norm_g)
    xa, hk, hq = _resid_mod(x1, (yf0, zero_d, g_f0), [(gkv, kv_sc, kv_sh), (gmix1, sc_m1, sh_m1)], None, "resid_ffn0")
    kvp = _matmul(hk, full["wkv"], "nn", F32, "mm_kv")
    qp = _matmul(hq, full["wq"], "nn", F32, "mm_q")
    inv_freq = ROPE_THETA ** (-jnp.arange(0, ROT_DIM, 2, dtype=F32) / ROT_DIM)
    half = ROT_DIM // 2
    freq_l = jnp.concatenate([inv_freq, inv_freq, jnp.zeros((HEAD_DIM - ROT_DIM,), F32)]).reshape(1, HEAD_DIM)
    sign_l = jnp.concatenate([-jnp.ones((half,), F32), jnp.ones((half,), F32), jnp.zeros((HEAD_DIM - ROT_DIM,), F32)]).reshape(1, HEAD_DIM)
    ctab, stab = _rope_tables(positions.reshape(s, 1), freq_l, sign_l, "rope_tables")
    gq, gk = row(q_norm_g[0]), row(k_norm_g)
    qn, kn = _qk_norm_rope_fwd(qp, kvp, gq, gk, ctab, stab, "qk_norm_rope")
    og, lg = zip(*[_attn_fwd(qn, kn, kvp, g, "attn_fwd%d" % g) for g in range(N_GROUPS)])
    o_mix, lse = _attn_combine(og, lg, "attn_combine")
    ya = _matmul(o_mix, full["wo"], "nn", F32, "mm_o")
    xb, hf1 = _resid_mod(xa, (ya, zero_d, g_m1), [(gffn1, sc_f1, sh_f1)], None, "resid_attn")
    uf1, zf1, yf1 = ffn_fwd(1, xb, hf1, "1")
    _, dxo, loss_cols = _resid_mod(xb, (yf1, zero_d, g_f1), [], target, "resid_loss")
    loss = lax.psum(jnp.sum(loss_cols), ("x", "y", "c"))

    gbig = {}

    def ffn_bwd(l, dy, u, z, h, tag):
        dz = _matmul(dy, full["dn%d" % l], "nt", F32, "mm_down_dx" + tag)
        gbig["dn%d" % l] = _matmul(z, dy, "tn", F32, "mm_down_dw" + tag)
        du, dfw, dfb = _ffn_act_bwd(dz, u, fdw_full[l], row(ffn_dw_b[l]), "ffn_act_bwd" + tag)
        dh = _matmul(du, full["up%d" % l], "nt", F32, "mm_up_dx" + tag)
        gbig["up%d" % l] = _matmul(h, du, "tn", F32, "mm_up_dw" + tag)
        return dh, dfw, dfb

    dxb0, dyf1, dg_f1, _ = _bwd_step(dxo, xb, [], (yf1, zero_d, g_f1), "bwd_loss")
    dhf1, dfw1, dfb1 = ffn_bwd(1, dyf1, uf1, zf1, hf1, "1")
    dxb, dsh_f1, dsc_f1, dgffn1, dya, dg_m1, _ = _bwd_step(dxb0, xb, [(dhf1, gffn1, sc_f1)], (ya, zero_d, g_m1), "bwd_attn_out")
    do = _matmul(dya, full["wo"], "nt", F32, "mm_o_dx")
    gbig["wo"] = _matmul(o_mix, dya, "tn", F32, "mm_o_dw")
    delta = _attn_delta(do, o_mix, "attn_delta")
    dqs = [_attn_bwd_dq(qn, kn, kvp, do, lse, delta, g, "attn_dq%d" % g) for g in range(N_GROUPS)]
    dks, dvs = zip(*[_attn_bwd_dkv(qn, kn, kvp, do, lse, delta, g, "attn_dkv%d" % g) for g in range(N_GROUPS)])
    dqp, dkvp, dgq, dgk = _qk_norm_rope_bwd(dqs, dks, dvs, qp, kvp, gq, gk, ctab, stab, "qk_norm_rope_bwd")
    dhq = _matmul(dqp, full["wq"], "nt", F32, "mm_q_dx")
    gbig["wq"] = _matmul(hq, dqp, "tn", F32, "mm_q_dw")
    dhk = _matmul(dkvp, full["wkv"], "nt", F32, "mm_kv_dx")
    gbig["wkv"] = _matmul(hk, dkvp, "tn", F32, "mm_kv_dw")
    (dxa, dsh_kv, dsc_kv, dgkv, dsh_m1, dsc_m1, dgmix1, dyf0, dg_f0, _) = _bwd_step(
        dxb, xa, [(dhk, gkv, kv_sc), (dhq, gmix1, sc_m1)], (yf0, zero_d, g_f0), "bwd_kvq")
    dhf0, dfw0, dfb0 = ffn_bwd(0, dyf0, uf0, zf0, hf0, "0")
    dx1, dsh_f0, dsc_f0, dgffn0, dy0, dg_m0, dpw2_b = _bwd_step(dxa, x1, [(dhf0, gffn0, sc_f0)], (y0, pw2_b_full, g_m0), "bwd_conv_out")
    dact = _matmul(dy0, full["pw2"], "nt", F32, "mm_pw2_dx")
    gbig["pw2"] = _matmul(act, dy0, "tn", F32, "mm_pw2_dw")
    ddwc, dln_g, dln_b = _ln_silu_bwd(dact, dwc, ln_g_full, ln_b_full, "ln_silu_bwd")
    dglu, ddw_w, ddw_b = _dwconv_bwd(ddwc, glu, dw_w_full, "dwconv_bwd")
    du0, dpw1_b = _glu_bwd(dglu, u0, pw1_b_full, "glu_bwd")
    dh0 = _matmul(du0, full["pw1"], "nt", F32, "mm_pw1_dx")
    gbig["pw1"] = _matmul(h0, du0, "tn", F32, "mm_pw1_dw")
    grad_x, dsh_m0, dsc_m0, dgmix0 = _bwd_step(dx1, x2, [(dh0, gmix0, sc_m0)], None, "bwd_in")

    dmod = [jnp.concatenate([dsh_m0, dsc_m0, dg_m0, dsh_f0, dsc_f0, dg_f0], axis=1),
            jnp.concatenate([dsh_m1, dsc_m1, dg_m1, dsh_f1, dsc_f1, dg_f1], axis=1)]
    dkvm = jnp.concatenate([dsh_kv, dsc_kv], axis=1)
    per_ex = [dmod[0], dmod[1], dkvm]
    summed = [dgmix0, dgmix1, dgffn0, dgffn1, dpw1_b, ddw_w, ddw_b, dln_g, dln_b, dpw2_b, dgkv, dgk, dgq, dfw0, dfw1, dfb0, dfb1]
    vec = jnp.concatenate([t.reshape(-1) for t in per_ex + summed])
    vlen = vec.shape[0]
    vec = _pad_to(vec, 1024)
    vall = _allgather8(vec.reshape(8, -1), "gather_small").reshape(N_DEV, -1)
    vsum = _sum_rows(vall, "sum_small")[0]
    n_pe = sum(t.size for t in per_ex)
    dm_all = vall[:, :n_pe]
    off2 = n_pe
    sums = []
    for t in summed:
        sums.append(vsum[off2:off2 + t.size].reshape(t.shape))
        off2 += t.size
    (s_gmix0, s_gmix1, s_gffn0, s_gffn1, s_pw1_b, s_dw_w, s_dw_b, s_ln_g, s_ln_b, s_pw2_b, s_gkv, s_gk, s_gq, s_fw0, s_fw1,
     s_fb0, s_fb1) = sums
    shard_cols = lambda t, width: lax.dynamic_slice_in_dim(t, slot * width, width, axis=t.ndim - 1)
    dm_mod = jnp.stack([shard_cols(dm_all[:, l * 6 * d:(l + 1) * 6 * d], nm) for l in range(2)])
    dm_kv = shard_cols(dm_all[:, 12 * d:14 * d], nk)[None]
    sct = jnp.transpose(sc_all)

    glist = [gbig[nme] for nme in names]
    recv1 = _rs_pair_exchange(glist, kinds, shard_shapes, "rs_pair_exchange")
    parts = [_rs_pair_add(g_, r_, k_, s_, c_arr, "rs_pair_add_" + nme) for g_, r_, k_, s_, nme in zip(glist, recv1, kinds, shard_shapes, names)]
    recv2 = _rs_chip_exchange(parts, "rs_chip_exchange")
    reduced = {}
    for p_, r_, nme, shp in zip(parts, recv2, names, shard_shapes):
        if nme in ("up0", "up1", "dn0", "dn1"):
            key, layer = nme[:-1], int(nme[-1])
            reduced[key] = _rs_chip_add(p_, r_, slot_arr, c_arr, reduced.get(key), layer, (2,) + shp, "rs_chip_add_" + nme)
        else:
            reduced[nme] = _rs_chip_add(p_, r_, slot_arr, c_arr, None, None, shp, "rs_chip_add_" + nme)
    g_pw1, g_pw2, g_wkv, g_wq, g_wo, g_up, g_dn = _rs_pair_share(
        [reduced[k] for k in ("pw1", "pw2", "wkv", "wq", "wo", "up", "dn")], "rs_pair_share")

    grads, deltas, new_m, new_v = {}, {}, {}, {}

    def put(nme, g_, res):
        grads[nme] = g_
        deltas[nme], new_m[nme], new_v[nme] = res

    for nme, g_, w_, m_, v_ in (("conv_pw1_w", g_pw1[None], conv_pw1_w, m_conv_pw1_w, v_conv_pw1_w),
                                ("conv_pw2_w", g_pw2[None], conv_pw2_w, m_conv_pw2_w, v_conv_pw2_w),
                                ("w_kv", g_wkv, w_kv, m_w_kv, v_w_kv), ("w_q", g_wq[None], w_q, m_w_q, v_w_q),
                                ("w_o", g_wo[None], w_o, m_w_o, v_w_o), ("ffn_up_w", g_up, ffn_up_w, m_ffn_up_w, v_ffn_up_w),
                                ("ffn_down_w", g_dn, ffn_down_w, m_ffn_down_w, v_ffn_down_w)):
        put(nme, g_, _adamw_big(w_, g_, m_, v_, "adamw_" + nme))
    g_, *res = _adamw_mod(mod_w, sct, dm_mod, m_mod_w, v_mod_w, "adamw_mod_w")
    put("mod_w", g_, res)
    g_, *res = _adamw_mod(kv_mod_w[None], sct, dm_kv, m_kv_mod_w[None], v_kv_mod_w[None], "adamw_kv_mod_w")
    put("kv_mod_w", g_[0], [t[0] for t in res])

    dm_sum = vsum[:n_pe]
    small = [
        ("mod_b", dm_sum[:12 * d].reshape(2, 6 * d), mod_b, m_mod_b, v_mod_b),
        ("norm_mix_g", jnp.concatenate([s_gmix0, s_gmix1], axis=0), norm_mix_g, m_norm_mix_g, v_norm_mix_g),
        ("norm_ffn_g", jnp.concatenate([s_gffn0, s_gffn1], axis=0), norm_ffn_g, m_norm_ffn_g, v_norm_ffn_g),
        ("conv_pw1_b", shard_cols(s_pw1_b, conv_pw1_b.shape[1]), conv_pw1_b, m_conv_pw1_b, v_conv_pw1_b),
        ("conv_dw_w", shard_cols(s_dw_w, d // 4)[None], conv_dw_w, m_conv_dw_w, v_conv_dw_w),
        ("conv_dw_b", shard_cols(s_dw_b, d // 4), conv_dw_b, m_conv_dw_b, v_conv_dw_b),
        ("conv_ln_g", shard_cols(s_ln_g, d // 4), conv_ln_g, m_conv_ln_g, v_conv_ln_g),
        ("conv_ln_b", shard_cols(s_ln_b, d // 4), conv_ln_b, m_conv_ln_b, v_conv_ln_b),
        ("conv_pw2_b", shard_cols(s_pw2_b, d // 4), conv_pw2_b, m_conv_pw2_b, v_conv_pw2_b),
        ("kv_mod_b", dm_sum[12 * d:14 * d], kv_mod_b, m_kv_mod_b, v_kv_mod_b),
        ("kv_norm_g", s_gkv.reshape(-1), kv_norm_g, m_kv_norm_g, v_kv_norm_g),
        ("k_norm_g", s_gk.reshape(-1), k_norm_g, m_k_norm_g, v_k_norm_g),
        ("q_norm_g", s_gq, q_norm_g, m_q_norm_g, v_q_norm_g),
        ("ffn_dw_w", shard_cols(jnp.stack([s_fw0, s_fw1]), f // 4), ffn_dw_w, m_ffn_dw_w, v_ffn_dw_w),
        ("ffn_dw_b", jnp.concatenate([s_fb0, s_fb1], axis=0), ffn_dw_b, m_ffn_dw_b, v_ffn_dw_b),
    ]
    as2d = lambda t: t.reshape(-1, t.shape[-1])
    sd_, sm_, sv_ = _adamw_small([as2d(t[2]) for t in small], [as2d(t[1]) for t in small], [as2d(t[3]) for t in small],
                                 [as2d(t[4]) for t in small], "adamw_small")
    for (nme, g_, w_, _, _), d_, mo_, vo_ in zip(small, sd_, sm_, sv_):
        put(nme, g_.reshape(w_.shape), [d_.reshape(w_.shape), mo_.reshape(w_.shape), vo_.reshape(w_.shape)])

    order = ["mod_w", "mod_b", "norm_mix_g", "norm_ffn_g", "conv_pw1_w", "conv_pw1_b", "conv_dw_w", "conv_dw_b", "conv_ln_g",
             "conv_ln_b", "conv_pw2_w", "conv_pw2_b", "kv_mod_w", "kv_mod_b", "kv_norm_g", "w_kv", "k_norm_g", "w_q", "q_norm_g",
             "w_o", "ffn_up_w", "ffn_dw_w", "ffn_dw_b", "ffn_down_w"]
    return (loss, grad_x.reshape(x.shape), *[grads[k] for k in order], *[deltas[k] for k in order], *[new_m[k] for k in order],
            *[new_v[k] for k in order])
```

```python
import functools
import math

import jax
import jax.numpy as jnp
from jax import lax
from jax.experimental import pallas as pl
from jax.experimental.pallas import tpu as pltpu

F32 = jnp.float32
BF16 = jnp.bfloat16
EPS = 1e-6
NEG = -1e30
HEAD_DIM = 128
ROT_DIM = 32
ROPE_THETA = 500000.0
BLK = 128
DILATIONS = (1, 4, 16)
N_GROUPS = 3
CONV_K = 31
FFN_K = 3
ADAM_LR, ADAM_B1, ADAM_B2, ADAM_EPS, ADAM_WD, ADAM_STEP = 0.001, 0.9, 0.999, 1e-08, 0.01, 10
N_DEV = 8
MESH = pl.DeviceIdType.MESH
VMEM_LIMIT_MB = 56
ROW_TILE = 256
ATTN_CHUNK = 2048


def _pick(n, pref, mult=128):
    best = None
    d = mult
    while d <= min(n, pref):
        if n % d == 0:
            best = d
        d += mult
    return best if best is not None else n


class _Rider:
    def __init__(self, ins, out_shapes, aliases, n_sem, copies):
        self.ins, self.out_shapes, self.aliases, self.n_sem, self.copies = list(ins), list(out_shapes), dict(aliases), n_sem, copies
        self.results = None


def _call(body, *, name, grid, in_specs, out_specs, out_shape, scratch=(), nsp=0, rider=None):
    params = pltpu.CompilerParams(dimension_semantics=("arbitrary",) * len(grid), vmem_limit_bytes=VMEM_LIMIT_MB << 20)
    if rider is None:
        return pl.pallas_call(
            body, name=name,
            grid_spec=pltpu.PrefetchScalarGridSpec(num_scalar_prefetch=nsp, grid=grid, in_specs=in_specs, out_specs=out_specs,
                                                   scratch_shapes=list(scratch)),
            out_shape=out_shape, compiler_params=params, interpret=False,
        )
    single = not isinstance(out_shape, (list, tuple))
    out_shapes = [out_shape] if single else list(out_shape)
    out_specs_l = [out_specs] if single else list(out_specs)
    n_in, n_out, n_scr = len(in_specs), len(out_shapes), len(scratch)
    r_in, r_out = len(rider.ins), len(rider.out_shapes)
    hbm = pl.BlockSpec(memory_space=pltpu.HBM)
    last = tuple(g - 1 for g in grid)

    def wrapped(*refs):
        pre, ins, rin = refs[:nsp], refs[nsp:nsp + n_in], refs[nsp + n_in:nsp + n_in + r_in]
        o0 = nsp + n_in + r_in
        outs, rout = refs[o0:o0 + n_out], refs[o0 + n_out:o0 + n_out + r_out]
        s0 = o0 + n_out + r_out
        scr, (send_sems, recv_sems) = refs[s0:s0 + n_scr], refs[s0 + n_scr:]
        ids = [pl.program_id(a) for a in range(len(grid))]
        is_first = functools.reduce(jnp.logical_and, [i == 0 for i in ids])
        is_last = functools.reduce(jnp.logical_and, [i == l for i, l in zip(ids, last)])

        @pl.when(is_first)
        def _():
            for cp in rider.copies(rin, rout, send_sems, recv_sems)[0]:
                cp.start()

        body(*pre, *ins, *outs, *scr)

        @pl.when(is_last)
        def _():
            sends, recvs = rider.copies(rin, rout, send_sems, recv_sems)
            for cp in recvs:
                cp.wait_recv()
            for cp in sends:
                cp.wait_send()

    call = pl.pallas_call(
        wrapped, name=name,
        grid_spec=pltpu.PrefetchScalarGridSpec(
            num_scalar_prefetch=nsp, grid=grid, in_specs=list(in_specs) + [hbm] * r_in, out_specs=out_specs_l + [hbm] * r_out,
            scratch_shapes=list(scratch) + [pltpu.SemaphoreType.DMA((rider.n_sem,)), pltpu.SemaphoreType.DMA((rider.n_sem,))]),
        out_shape=out_shapes + rider.out_shapes,
        input_output_aliases={nsp + n_in + i: n_out + o for i, o in rider.aliases.items()},
        compiler_params=params, interpret=False,
    )

    def run(*args):
        res = call(*args, *rider.ins)
        rider.results = list(res[n_out:])
        return res[0] if single else list(res[:n_out])

    return run


def _run_rider(rider, name):
    r_in, r_out = len(rider.ins), len(rider.out_shapes)
    hbm = pl.BlockSpec(memory_space=pltpu.HBM)

    def body(*refs):
        sends, recvs = rider.copies(refs[:r_in], refs[r_in:r_in + r_out], *refs[r_in + r_out:])
        for cp in sends:
            cp.start()
        for cp in recvs:
            cp.wait_recv()
        for cp in sends:
            cp.wait_send()

    return pl.pallas_call(
        body, name=name, out_shape=rider.out_shapes, in_specs=[hbm] * r_in, out_specs=[hbm] * r_out,
        input_output_aliases=rider.aliases,
        scratch_shapes=[pltpu.SemaphoreType.DMA((rider.n_sem,)), pltpu.SemaphoreType.DMA((rider.n_sem,))], interpret=False,
    )(*rider.ins)


def _sds(shape, dtype):
    return jax.ShapeDtypeStruct(shape, dtype)


def _acc(ref, val, i):
    @pl.when(i == 0)
    def _():
        ref[...] = val

    @pl.when(i > 0)
    def _():
        ref[...] += val


def _colsum(v):
    return jnp.sum(v, axis=0, keepdims=True)


def _silu(v):
    return v * jax.nn.sigmoid(v)


def _dsilu(v):
    s = jax.nn.sigmoid(v)
    return s * (1.0 + v * (1.0 - s))


_DIMS = {"nn": (((1,), (0,)), ((), ())), "nt": (((1,), (1,)), ((), ())), "tn": (((0,), (0,)), ((), ()))}


def _matmul(a, b, mode, out_dtype, name, rider=None):
    a_halves = a.shape[0] if a.ndim == 3 else 0
    b_halves = b.shape[0] if b.ndim == 3 else 0
    if mode == "nn":
        (m, c), (_, n) = a.shape, b.shape
    elif mode == "nt":
        m, c = (a.shape[1], a.shape[0] * a.shape[2]) if a_halves else a.shape
        n = b.shape[0]
    else:
        c, m = a.shape
        n = b.shape[0] * b.shape[2] if b_halves else b.shape[1]
    tm = _pick(m, 1024)
    tn = _pick(n // b_halves, 1024) if b_halves else _pick(n, 1024)
    c_cap = 2048 if mode == "tn" else 2816
    tc = _pick(c // a_halves, c_cap) if a_halves else _pick(c, c_cap)
    nk = c // tc
    if a_halves:
        per_a = c // a_halves // tc
        a_spec = pl.BlockSpec((None, tm, tc), lambda i, j, k: (k // per_a, i, k % per_a))
    else:
        a_spec = {"nn": pl.BlockSpec((tm, tc), lambda i, j, k: (i, k)), "nt": pl.BlockSpec((tm, tc), lambda i, j, k: (i, k)),
                  "tn": pl.BlockSpec((tc, tm), lambda i, j, k: (k, i))}[mode]
    if b_halves:
        per_b = n // b_halves // tn
        b_spec = pl.BlockSpec((None, tc, tn), lambda i, j, k: (j // per_b, k, j % per_b))
    else:
        b_spec = {"nn": pl.BlockSpec((tc, tn), lambda i, j, k: (k, j)), "nt": pl.BlockSpec((tn, tc), lambda i, j, k: (j, k)),
                  "tn": pl.BlockSpec((tc, tn), lambda i, j, k: (k, j))}[mode]
    dims = _DIMS[mode]

    def body(a_ref, b_ref, o_ref, acc_ref):
        k = pl.program_id(2)
        p = lax.dot_general(a_ref[...], b_ref[...], dims, preferred_element_type=F32)
        if nk == 1:
            o_ref[...] = p.astype(out_dtype)
        else:
            @pl.when(k == 0)
            def _():
                acc_ref[...] = p

            @pl.when(k > 0)
            def _():
                acc_ref[...] += p

            @pl.when(k == nk - 1)
            def _():
                o_ref[...] = acc_ref[...].astype(out_dtype)

    return _call(
        body, name=name, grid=(m // tm, n // tn, nk), in_specs=[a_spec, b_spec],
        out_specs=pl.BlockSpec((tm, tn), lambda i, j, k: (i, j)), out_shape=_sds((m, n), out_dtype),
        scratch=[pltpu.VMEM((tm, tn), F32)], rider=rider,
    )(a, b)


def _row_spec(tr, w):
    return pl.BlockSpec((tr, w), lambda i: (i, 0))


def _vec_spec(w):
    return pl.BlockSpec((1, w), lambda i: (0, 0))


def _resid_mod(x, prev, mods, target, name):
    s, d = x.shape
    tr = _pick(s, ROW_TILE, 8)
    n_mod = len(mods)

    def body(*refs):
        it = iter(refs)
        x_ref = next(it)
        if prev is not None:
            y_ref, yb_ref, gate_ref = next(it), next(it), next(it)
        mod_refs = [(next(it), next(it), next(it)) for _ in range(n_mod)]
        if target is not None:
            t_ref = next(it)
        if prev is not None:
            xo_ref = next(it)
        h_refs = [next(it) for _ in range(n_mod)]
        i = pl.program_id(0)
        xv = x_ref[...]
        if prev is not None:
            xv = xv + gate_ref[...] * (y_ref[...] + yb_ref[...])
            xo_ref[...] = xv
        if n_mod:
            nrm = xv * lax.rsqrt(jnp.mean(xv * xv, axis=-1, keepdims=True) + EPS)
            for (g_ref, sc_ref, sh_ref), h_ref in zip(mod_refs, h_refs):
                h_ref[...] = (nrm * g_ref[...] * (1.0 + sc_ref[...]) + sh_ref[...]).astype(BF16)
        if target is not None:
            dx_ref, loss_ref = next(it), next(it)
            err = xv - t_ref[...]
            dx_ref[...] = err * (1.0 / d)
            _acc(loss_ref, _colsum(err * err) * (0.5 / d), i)

    ins, in_specs = [x], [_row_spec(tr, d)]
    if prev is not None:
        ins += list(prev)
        in_specs += [_row_spec(tr, d), _vec_spec(d), _vec_spec(d)]
    for g, sc, sh in mods:
        ins += [g, sc, sh]
        in_specs += [_vec_spec(d)] * 3
    if target is not None:
        ins.append(target)
        in_specs.append(_row_spec(tr, d))
    out_shape, out_specs = [], []
    if prev is not None:
        out_shape.append(_sds((s, d), F32))
        out_specs.append(_row_spec(tr, d))
    for _ in mods:
        out_shape.append(_sds((s, d), BF16))
        out_specs.append(_row_spec(tr, d))
    if target is not None:
        out_shape += [_sds((s, d), F32), _sds((1, d), F32)]
        out_specs += [_row_spec(tr, d), _vec_spec(d)]
    return _call(body, name=name, grid=(s // tr,), in_specs=in_specs, out_specs=out_specs, out_shape=out_shape)(*ins)


def _bwd_step(dx_up, x, mods, prev, name):
    s, d = x.shape
    tr = _pick(s, ROW_TILE, 8)
    n_mod = len(mods)

    def body(*refs):
        it = iter(refs)
        dxu_ref, x_ref = next(it), next(it)
        mod_refs = [(next(it), next(it), next(it)) for _ in range(n_mod)]
        if prev is not None:
            y_ref, yb_ref, gate_ref = next(it), next(it), next(it)
        dx_ref = next(it)
        acc_refs = [(next(it), next(it), next(it)) for _ in range(n_mod)]
        i = pl.program_id(0)
        dx = dxu_ref[...]
        if n_mod:
            xv = x_ref[...]
            rstd = lax.rsqrt(jnp.mean(xv * xv, axis=-1, keepdims=True) + EPS)
            nrm = xv * rstd
        for (dh_ref, g_ref, sc_ref), (dsh_ref, dsc_ref, dg_ref) in zip(mod_refs, acc_refs):
            dh = dh_ref[...]
            gv, one_sc = g_ref[...], 1.0 + sc_ref[...]
            _acc(dsh_ref, _colsum(dh), i)
            t = dh * nrm
            _acc(dsc_ref, _colsum(t) * gv, i)
            _acc(dg_ref, _colsum(t) * one_sc, i)
            dn = dh * (gv * one_sc)
            dx = dx + rstd * (dn - nrm * jnp.mean(dn * nrm, axis=-1, keepdims=True))
        dx_ref[...] = dx
        if prev is not None:
            dy_ref, dgate_ref, dyb_ref = next(it), next(it), next(it)
            dy = gate_ref[...] * dx
            dy_ref[...] = dy.astype(BF16)
            _acc(dgate_ref, _colsum(dx * (y_ref[...] + yb_ref[...])), i)
            _acc(dyb_ref, _colsum(dy), i)

    ins, in_specs = [dx_up, x], [_row_spec(tr, d)] * 2
    for dh, g, sc in mods:
        ins += [dh, g, sc]
        in_specs += [_row_spec(tr, d), _vec_spec(d), _vec_spec(d)]
    if prev is not None:
        ins += list(prev)
        in_specs += [_row_spec(tr, d), _vec_spec(d), _vec_spec(d)]
    out_shape, out_specs = [_sds((s, d), F32)], [_row_spec(tr, d)]
    for _ in mods:
        out_shape += [_sds((1, d), F32)] * 3
        out_specs += [_vec_spec(d)] * 3
    if prev is not None:
        out_shape += [_sds((s, d), BF16), _sds((1, d), F32), _sds((1, d), F32)]
        out_specs += [_row_spec(tr, d), _vec_spec(d), _vec_spec(d)]
    return _call(body, name=name, grid=(s // tr,), in_specs=in_specs, out_specs=out_specs, out_shape=out_shape)(*ins)


def _glu_fwd(u, bias, name):
    s, d2 = u.shape
    d = d2 // 2
    tr = _pick(s, ROW_TILE, 8)

    def body(u_ref, b_ref, o_ref):
        uv = u_ref[...] + b_ref[...]
        o_ref[...] = uv[:, :d] * jax.nn.sigmoid(uv[:, d:])

    return _call(body, name=name, grid=(s // tr,), in_specs=[_row_spec(tr, d2), _vec_spec(d2)],
                 out_specs=_row_spec(tr, d), out_shape=_sds((s, d), F32))(u, bias)


def _glu_bwd(dglu, u, bias, name):
    s, d2 = u.shape
    d = d2 // 2
    tr = _pick(s, ROW_TILE, 8)

    def body(dg_ref, u_ref, b_ref, du_ref, db_ref):
        i = pl.program_id(0)
        uv = u_ref[...] + b_ref[...]
        a, sg = uv[:, :d], jax.nn.sigmoid(uv[:, d:])
        dg = dg_ref[...]
        da = dg * sg
        dgt = dg * a * sg * (1.0 - sg)
        du_ref[:, :d] = da.astype(BF16)
        du_ref[:, d:] = dgt.astype(BF16)
        _acc(db_ref, jnp.concatenate([_colsum(da), _colsum(dgt)], axis=1), i)

    return _call(body, name=name, grid=(s // tr,), in_specs=[_row_spec(tr, d), _row_spec(tr, d2), _vec_spec(d2)],
                 out_specs=[_row_spec(tr, d2), _vec_spec(d2)], out_shape=[_sds((s, d2), BF16), _sds((1, d2), F32)])(dglu, u, bias)


def _halo_rows(k):
    return -(-(k - 1) // 8) * 8


def _dwconv_fwd(x, w, b, name, rider=None):
    s, c = x.shape
    kk = w.shape[0]
    hb = _halo_rows(kk)
    tr, tc = _pick(s, ROW_TILE, hb), _pick(c, 512)
    per = tr // hb

    def body(xp_ref, x_ref, w_ref, b_ref, o_ref, cat_ref):
        i = pl.program_id(1)
        cat_ref[0:hb, :] = jnp.where(i > 0, xp_ref[...], 0.0)
        cat_ref[hb:hb + tr, :] = x_ref[...]
        acc = jnp.zeros((tr, tc), F32) + b_ref[...]
        for k in range(kk):
            acc = acc + w_ref[k:k + 1, :] * cat_ref[pl.ds(hb - (kk - 1) + k, tr), :]
        o_ref[...] = acc

    return _call(
        body, name=name, grid=(c // tc, s // tr),
        in_specs=[pl.BlockSpec((hb, tc), lambda j, i: (jnp.maximum(i * per - 1, 0), j)), pl.BlockSpec((tr, tc), lambda j, i: (i, j)),
                  pl.BlockSpec((kk, tc), lambda j, i: (0, j)), pl.BlockSpec((1, tc), lambda j, i: (0, j))],
        out_specs=pl.BlockSpec((tr, tc), lambda j, i: (i, j)), out_shape=_sds((s, c), F32),
        scratch=[pltpu.VMEM((tr + hb, tc), F32)], rider=rider,
    )(x, x, w, b)


def _dwconv_bwd(dy, x, w, name, rider=None):
    s, c = x.shape
    kk = w.shape[0]
    hb = _halo_rows(kk)
    tr, tc = _pick(s, ROW_TILE, hb), _pick(c, 512)
    per, nt = tr // hb, s // tr

    def body(xp_ref, x_ref, dy_ref, dyn_ref, w_ref, dx_ref, dw_ref, db_ref, xcat_ref, dcat_ref):
        i = pl.program_id(1)
        xcat_ref[0:hb, :] = jnp.where(i > 0, xp_ref[...], 0.0)
        xcat_ref[hb:hb + tr, :] = x_ref[...]
        dyv = dy_ref[...]
        dcat_ref[0:tr, :] = dyv
        dcat_ref[tr:tr + hb, :] = jnp.where(i < nt - 1, dyn_ref[...], 0.0)
        acc = jnp.zeros((tr, tc), F32)
        for k in range(kk):
            acc = acc + w_ref[k:k + 1, :] * dcat_ref[pl.ds(kk - 1 - k, tr), :]
        dx_ref[...] = acc

        @pl.when(i == 0)
        def _():
            dw_ref[...] = jnp.zeros_like(dw_ref)
            db_ref[...] = jnp.zeros_like(db_ref)

        db_ref[...] += _colsum(dyv)
        for k in range(kk):
            dw_ref[k:k + 1, :] += _colsum(dyv * xcat_ref[pl.ds(hb - (kk - 1) + k, tr), :])

    return _call(
        body, name=name, grid=(c // tc, nt),
        in_specs=[pl.BlockSpec((hb, tc), lambda j, i: (jnp.maximum(i * per - 1, 0), j)), pl.BlockSpec((tr, tc), lambda j, i: (i, j)),
                  pl.BlockSpec((tr, tc), lambda j, i: (i, j)),
                  pl.BlockSpec((hb, tc), lambda j, i: (jnp.minimum((i + 1) * per, s // hb - 1), j)),
                  pl.BlockSpec((kk, tc), lambda j, i: (0, j))],
        out_specs=[pl.BlockSpec((tr, tc), lambda j, i: (i, j)), pl.BlockSpec((kk, tc), lambda j, i: (0, j)),
                   pl.BlockSpec((1, tc), lambda j, i: (0, j))],
        out_shape=[_sds((s, c), F32), _sds((kk, c), F32), _sds((1, c), F32)],
        scratch=[pltpu.VMEM((tr + hb, tc), F32), pltpu.VMEM((tr + hb, tc), F32)], rider=rider,
    )(x, x, dy, dy, w)


def _ln_silu_fwd(x, g, b, name):
    s, d = x.shape
    tr = _pick(s, ROW_TILE, 8)

    def body(x_ref, g_ref, b_ref, o_ref):
        xv = x_ref[...]
        mu = jnp.mean(xv, axis=-1, keepdims=True)
        xc = xv - mu
        ln = xc * lax.rsqrt(jnp.mean(xc * xc, axis=-1, keepdims=True) + EPS) * g_ref[...] + b_ref[...]
        o_ref[...] = _silu(ln).astype(BF16)

    return _call(body, name=name, grid=(s // tr,), in_specs=[_row_spec(tr, d), _vec_spec(d), _vec_spec(d)],
                 out_specs=_row_spec(tr, d), out_shape=_sds((s, d), BF16))(x, g, b)


def _ln_silu_bwd(dact, x, g, b, name):
    s, d = x.shape
    tr = _pick(s, ROW_TILE, 8)

    def body(da_ref, x_ref, g_ref, b_ref, dx_ref, dg_ref, db_ref):
        i = pl.program_id(0)
        xv = x_ref[...]
        mu = jnp.mean(xv, axis=-1, keepdims=True)
        xc = xv - mu
        rstd = lax.rsqrt(jnp.mean(xc * xc, axis=-1, keepdims=True) + EPS)
        xh = xc * rstd
        ln = xh * g_ref[...] + b_ref[...]
        dln = da_ref[...] * _dsilu(ln)
        _acc(dg_ref, _colsum(dln * xh), i)
        _acc(db_ref, _colsum(dln), i)
        dxh = dln * g_ref[...]
        dx_ref[...] = rstd * (dxh - jnp.mean(dxh, axis=-1, keepdims=True) - xh * jnp.mean(dxh * xh, axis=-1, keepdims=True))

    return _call(body, name=name, grid=(s // tr,), in_specs=[_row_spec(tr, d), _row_spec(tr, d), _vec_spec(d), _vec_spec(d)],
                 out_specs=[_row_spec(tr, d), _vec_spec(d), _vec_spec(d)],
                 out_shape=[_sds((s, d), F32), _sds((1, d), F32), _sds((1, d), F32)])(dact, x, g, b)


def _ffn_act_fwd(u, w, b, name):
    s, f2 = u.shape
    f = f2 // 2
    hb = 8
    tr, tc = _pick(s, ROW_TILE, hb), _pick(f, 1408)
    per, nf = tr // hb, f // tc

    def body(gp_ref, g_ref, v_ref, w_ref, b_ref, z_ref, cat_ref):
        i = pl.program_id(1)
        cat_ref[0:hb, :] = jnp.where(i > 0, gp_ref[...], 0.0)
        cat_ref[hb:hb + tr, :] = g_ref[...]
        gc = jnp.zeros((tr, tc), F32) + b_ref[...]
        for k in range(FFN_K):
            gc = gc + w_ref[k:k + 1, :] * cat_ref[pl.ds(hb - (FFN_K - 1) + k, tr), :]
        z_ref[...] = (_silu(gc) * v_ref[...]).astype(BF16)

    return _call(
        body, name=name, grid=(nf, s // tr),
        in_specs=[pl.BlockSpec((hb, tc), lambda j, i: (jnp.maximum(i * per - 1, 0), j)), pl.BlockSpec((tr, tc), lambda j, i: (i, j)),
                  pl.BlockSpec((tr, tc), lambda j, i: (i, nf + j)), pl.BlockSpec((FFN_K, tc), lambda j, i: (0, j)),
                  pl.BlockSpec((1, tc), lambda j, i: (0, j))],
        out_specs=pl.BlockSpec((tr, tc), lambda j, i: (i, j)), out_shape=_sds((s, f), BF16),
        scratch=[pltpu.VMEM((tr + hb, tc), F32)],
    )(u, u, u, w, b)


def _ffn_act_bwd(dz, u, w, b, name, rider=None):
    s, f2 = u.shape
    f = f2 // 2
    hb = 8
    tr, tc = _pick(s, ROW_TILE, hb), _pick(f, 1408)
    per, nf, nt = tr // hb, f // tc, s // tr
    ext = tr + hb

    def body(gp_ref, g_ref, gn_ref, v_ref, vn_ref, dz_ref, dzn_ref, w_ref, b_ref, du_ref, dw_ref, db_ref, gcat_ref, dgc_ref):
        i = pl.program_id(1)
        last = i == nt - 1
        gcat_ref[0:hb, :] = jnp.where(i > 0, gp_ref[...], 0.0)
        gcat_ref[hb:hb + tr, :] = g_ref[...]
        gcat_ref[hb + tr:hb + tr + hb, :] = gn_ref[...]
        gc = jnp.zeros((ext, tc), F32) + b_ref[...]
        for k in range(FFN_K):
            gc = gc + w_ref[k:k + 1, :] * gcat_ref[pl.ds(hb - (FFN_K - 1) + k, ext), :]
        dz_cur = dz_ref[...]
        du_ref[1] = (dz_cur * _silu(gc[0:tr, :])).astype(BF16)
        dgc_ref[0:tr, :] = dz_cur * v_ref[...] * _dsilu(gc[0:tr, :])
        dgc_ref[tr:ext, :] = jnp.where(last, 0.0, dzn_ref[...] * vn_ref[...] * _dsilu(gc[tr:ext, :]))
        dgt = jnp.zeros((tr, tc), F32)
        for k in range(FFN_K):
            dgt = dgt + w_ref[k:k + 1, :] * dgc_ref[pl.ds(FFN_K - 1 - k, tr), :]
        du_ref[0] = dgt.astype(BF16)

        @pl.when(i == 0)
        def _():
            dw_ref[...] = jnp.zeros_like(dw_ref)
            db_ref[...] = jnp.zeros_like(db_ref)

        dgc_cur = dgc_ref[0:tr, :]
        db_ref[...] += _colsum(dgc_cur)
        for k in range(FFN_K):
            dw_ref[k:k + 1, :] += _colsum(dgc_cur * gcat_ref[pl.ds(hb - (FFN_K - 1) + k, tr), :])

    prev_map = lambda j, i: (jnp.maximum(i * per - 1, 0), j)
    next_map = lambda j, i: (jnp.minimum((i + 1) * per, s // hb - 1), j)
    next_map_v = lambda j, i: (jnp.minimum((i + 1) * per, s // hb - 1), nf + j)
    return _call(
        body, name=name, grid=(nf, nt),
        in_specs=[pl.BlockSpec((hb, tc), prev_map), pl.BlockSpec((tr, tc), lambda j, i: (i, j)), pl.BlockSpec((hb, tc), next_map),
                  pl.BlockSpec((tr, tc), lambda j, i: (i, nf + j)), pl.BlockSpec((hb, tc), next_map_v),
                  pl.BlockSpec((tr, tc), lambda j, i: (i, j)), pl.BlockSpec((hb, tc), next_map),
                  pl.BlockSpec((FFN_K, tc), lambda j, i: (0, j)), pl.BlockSpec((1, tc), lambda j, i: (0, j))],
        out_specs=[pl.BlockSpec((2, tr, tc), lambda j, i: (0, i, j)), pl.BlockSpec((FFN_K, tc), lambda j, i: (0, j)),
                   pl.BlockSpec((1, tc), lambda j, i: (0, j))],
        out_shape=[_sds((2, s, f), BF16), _sds((FFN_K, f), F32), _sds((1, f), F32)],
        scratch=[pltpu.VMEM((tr + 2 * hb, tc), F32), pltpu.VMEM((ext, tc), F32)], rider=rider,
    )(u, u, u, u, u, dz, dz, w, b)


def _rope_tables(pos, freq, sign, name):
    s = pos.shape[0]
    tr = _pick(s, 512, 8)

    def body(p_ref, f_ref, s_ref, c_ref, sn_ref):
        ang = p_ref[...].astype(F32) * f_ref[...]
        c_ref[...] = jnp.cos(ang)
        sn_ref[...] = jnp.sin(ang) * s_ref[...]

    return _call(body, name=name, grid=(s // tr,), in_specs=[pl.BlockSpec((tr, 1), lambda i: (i, 0)), _vec_spec(128), _vec_spec(128)],
                 out_specs=[_row_spec(tr, 128)] * 2, out_shape=[_sds((s, 128), F32)] * 2)(pos, freq, sign)


def _partner(v):
    lane = lax.broadcasted_iota(jnp.int32, v.shape, 1)
    lower = pltpu.roll(v, HEAD_DIM - ROT_DIM // 2, 1)
    upper = jnp.where(lane < ROT_DIM, pltpu.roll(v, ROT_DIM // 2, 1), 0.0)
    return jnp.where(lane < ROT_DIM // 2, lower, upper)


def _qk_norm_rope_fwd(q, kv, gq, gk, ctab, stab, name, rider=None):
    s, w = q.shape
    tr = _pick(s, 128, 8)
    heads = w // HEAD_DIM

    def body(q_ref, k_ref, gq_ref, gk_ref, c_ref, s_ref, qn_ref, kn_ref):
        cv, sv = c_ref[...], s_ref[...]
        for src, g_ref, dst in ((q_ref, gq_ref, qn_ref), (k_ref, gk_ref, kn_ref)):
            gv = g_ref[...]
            for h in range(heads):
                cols = pl.ds(h * HEAD_DIM, HEAD_DIM)
                xv = src[:, cols]
                nv = xv * lax.rsqrt(jnp.mean(xv * xv, axis=-1, keepdims=True) + EPS) * gv
                dst[:, cols] = nv * cv + _partner(nv) * sv

    return _call(
        body, name=name, grid=(s // tr,),
        in_specs=[_row_spec(tr, w), _row_spec(tr, w), _vec_spec(128), _vec_spec(128), _row_spec(tr, 128), _row_spec(tr, 128)],
        out_specs=[_row_spec(tr, w)] * 2, out_shape=[_sds((s, w), F32)] * 2, rider=rider,
    )(q, kv, gq, gk, ctab, stab)


def _qk_norm_rope_bwd(dqs, dks, dvs, q, kv, gq, gk, ctab, stab, name, rider=None):
    s, w = q.shape
    gw = w // N_GROUPS
    tr = _pick(s, 128, 8)
    hpg = gw // HEAD_DIM

    def body(*refs):
        dq_refs, dk_refs, dv_refs = refs[0:3], refs[3:6], refs[6:9]
        q_ref, k_ref, gq_ref, gk_ref, c_ref, s_ref, dq_ref, dkv_ref, dgq_ref, dgk_ref = refs[9:]
        i = pl.program_id(0)
        cv, sv = c_ref[...], s_ref[...]
        for d_refs, src, g_ref, dst, dg_ref in ((dq_refs, q_ref, gq_ref, dq_ref, dgq_ref), (dk_refs, k_ref, gk_ref, dkv_ref, dgk_ref)):
            gv = g_ref[...]
            dg = jnp.zeros((1, HEAD_DIM), F32)
            for g in range(N_GROUPS):
                for h in range(hpg):
                    cols = pl.ds(g * gw + h * HEAD_DIM, HEAD_DIM)
                    dout = d_refs[g][:, pl.ds(h * HEAD_DIM, HEAD_DIM)]
                    dn = dout * cv + _partner(dout * sv)
                    xv = src[:, cols]
                    rstd = lax.rsqrt(jnp.mean(xv * xv, axis=-1, keepdims=True) + EPS)
                    xh = xv * rstd
                    dg = dg + _colsum(dn * xh)
                    dxh = dn * gv
                    dst[:, cols] = (rstd * (dxh - xh * jnp.mean(dxh * xh, axis=-1, keepdims=True))).astype(BF16)
            _acc(dg_ref, dg, i)
        for g in range(N_GROUPS):
            dkv_ref[:, pl.ds(w + g * gw, gw)] = dv_refs[g][...].astype(BF16)

    return _call(
        body, name=name, grid=(s // tr,),
        in_specs=[_row_spec(tr, gw)] * 9 + [_row_spec(tr, w), _row_spec(tr, w), _vec_spec(128), _vec_spec(128),
                                            _row_spec(tr, 128), _row_spec(tr, 128)],
        out_specs=[_row_spec(tr, w), _row_spec(tr, 2 * w), _vec_spec(128), _vec_spec(128)],
        out_shape=[_sds((s, w), BF16), _sds((s, 2 * w), BF16), _sds((1, 128), F32), _sds((1, 128), F32)], rider=rider,
    )(*dqs, *dks, *dvs, q, kv, gq, gk, ctab, stab)


def _rows(j, b, r):
    start = j + r * BLK * b
    return pl.ds(start, BLK, stride=r) if r > 1 else pl.ds(start, BLK)


def _dot_nt(a, b):
    return lax.dot_general(a, b, _DIMS["nt"], preferred_element_type=F32)


def _dot_nn(a, b):
    return lax.dot_general(a, b, _DIMS["nn"], preferred_element_type=F32)


def _band_masks():
    qi = lax.broadcasted_iota(jnp.int32, (BLK, BLK), 0)
    kj = lax.broadcasted_iota(jnp.int32, (BLK, BLK), 1)
    return kj <= qi, kj >= qi


def _attn_fwd(qn, kn, kv, g, name):
    s, w = qn.shape
    r = DILATIONS[g]
    gw = w // N_GROUPS
    cr = min(ATTN_CHUNK, s)
    nb = cr // (BLK * r)
    hp = 1
    cw = hp * HEAD_DIM
    gc = gw // cw
    scale = 1.0 / math.sqrt(HEAD_DIM)

    def body(q_ref, kc_ref, kp_ref, vc_ref, vp_ref, o_ref, l_ref):
        n = pl.program_id(1)
        same_m, prev_m = _band_masks()
        prev_first = jnp.logical_and(prev_m, n > 0)
        for h in range(hp):
            cols = pl.ds(h * HEAD_DIM, HEAD_DIM)
            for j in range(r):
                for b in range(nb):
                    rows = _rows(j, b, r)
                    qv = q_ref[rows, cols].astype(BF16)
                    kc, vc = kc_ref[rows, cols].astype(BF16), vc_ref[rows, cols].astype(BF16)
                    if b > 0:
                        rp = _rows(j, b - 1, r)
                        kp, vp, pm = kc_ref[rp, cols].astype(BF16), vc_ref[rp, cols].astype(BF16), prev_m
                    else:
                        rp = _rows(j, nb - 1, r)
                        kp, vp, pm = kp_ref[rp, cols].astype(BF16), vp_ref[rp, cols].astype(BF16), prev_first
                    sd = jnp.where(same_m, _dot_nt(qv, kc) * scale, NEG)
                    so = jnp.where(pm, _dot_nt(qv, kp) * scale, NEG)
                    m = jnp.maximum(jnp.max(sd, axis=-1, keepdims=True), jnp.max(so, axis=-1, keepdims=True))
                    pd, po = jnp.exp(sd - m), jnp.exp(so - m)
                    den = jnp.sum(pd, axis=-1, keepdims=True) + jnp.sum(po, axis=-1, keepdims=True)
                    ov = (_dot_nn(pd.astype(BF16), vc) + _dot_nn(po.astype(BF16), vp)) / den
                    o_ref[rows, cols] = ov
                    l_ref[rows, cols] = jnp.broadcast_to(m + jnp.log(den), (BLK, HEAD_DIM))

    cur = lambda base: (lambda c, n: (n, base + c))
    prv = lambda base: (lambda c, n: (jnp.maximum(n - 1, 0), base + c))
    qb, kb, vb = g * gc, g * gc, (N_GROUPS + g) * gc
    return _call(
        body, name=name, grid=(gc, s // cr),
        in_specs=[pl.BlockSpec((cr, cw), cur(qb)), pl.BlockSpec((cr, cw), cur(kb)), pl.BlockSpec((cr, cw), prv(kb)),
                  pl.BlockSpec((cr, cw), cur(vb)), pl.BlockSpec((cr, cw), prv(vb))],
        out_specs=[pl.BlockSpec((cr, cw), lambda c, n: (n, c))] * 2, out_shape=[_sds((s, gw), F32)] * 2,
    )(qn, kn, kn, kv, kv)


def _attn_combine(os_, lses, name):
    s, gw = os_[0].shape
    tr = _pick(s, ROW_TILE, 8)

    def body(o0, o1, o2, l0, l1, l2, o_ref, l_ref):
        a, b, c = l0[...], l1[...], l2[...]
        m = jnp.maximum(jnp.maximum(a, b), c)
        ea, eb, ec = jnp.exp(a - m), jnp.exp(b - m), jnp.exp(c - m)
        den = ea + eb + ec
        o_ref[...] = ((ea * o0[...] + eb * o1[...] + ec * o2[...]) / den).astype(BF16)
        l_ref[...] = m + jnp.log(den)

    return _call(body, name=name, grid=(s // tr,), in_specs=[_row_spec(tr, gw)] * 6, out_specs=[_row_spec(tr, gw)] * 2,
                 out_shape=[_sds((s, gw), BF16), _sds((s, gw), F32)])(*os_, *lses)


def _attn_delta(do, o, name):
    s, gw = do.shape
    tr = _pick(s, ROW_TILE, 8)

    def body(do_ref, o_ref, d_ref):
        for h in range(gw // HEAD_DIM):
            cols = pl.ds(h * HEAD_DIM, HEAD_DIM)
            t = jnp.sum(do_ref[:, cols] * o_ref[:, cols].astype(F32), axis=-1, keepdims=True)
            d_ref[:, cols] = jnp.broadcast_to(t, (tr, HEAD_DIM))

    return _call(body, name=name, grid=(s // tr,), in_specs=[_row_spec(tr, gw)] * 2, out_specs=_row_spec(tr, gw),
                 out_shape=_sds((s, gw), F32))(do, o)


def _pair_grads(qv, kv_, vv, dov, lse, delta, mask, scale):
    sc = jnp.where(mask, _dot_nt(qv, kv_) * scale, NEG)
    p = jnp.exp(sc - lse)
    ds = p * (_dot_nt(dov, vv) - delta) * scale
    return p, ds


def _attn_bwd_dq(qn, kn, kv, do, lse, delta, g, name):
    s, w = qn.shape
    r = DILATIONS[g]
    gw = w // N_GROUPS
    cr = min(ATTN_CHUNK, s)
    nb = cr // (BLK * r)
    cw = HEAD_DIM
    gc = gw // cw
    scale = 1.0 / math.sqrt(HEAD_DIM)

    def body(q_ref, kc_ref, kp_ref, vc_ref, vp_ref, do_ref, l_ref, d_ref, dq_ref):
        n = pl.program_id(1)
        same_m, prev_m = _band_masks()
        prev_first = jnp.logical_and(prev_m, n > 0)
        for j in range(r):
            for b in range(nb):
                rows = _rows(j, b, r)
                qv, dov = q_ref[rows, :].astype(BF16), do_ref[rows, :].astype(BF16)
                lse, delta = l_ref[rows, :], d_ref[rows, :]
                kc, vc = kc_ref[rows, :].astype(BF16), vc_ref[rows, :].astype(BF16)
                if b > 0:
                    rp = _rows(j, b - 1, r)
                    kp, vp, pm = kc_ref[rp, :].astype(BF16), vc_ref[rp, :].astype(BF16), prev_m
                else:
                    rp = _rows(j, nb - 1, r)
                    kp, vp, pm = kp_ref[rp, :].astype(BF16), vp_ref[rp, :].astype(BF16), prev_first
                _, dsd = _pair_grads(qv, kc, vc, dov, lse, delta, same_m, scale)
                _, dso = _pair_grads(qv, kp, vp, dov, lse, delta, pm, scale)
                dq_ref[rows, :] = _dot_nn(dsd.astype(BF16), kc) + _dot_nn(dso.astype(BF16), kp)

    cur = lambda base: (lambda c, n: (n, base + c))
    prv = lambda base: (lambda c, n: (jnp.maximum(n - 1, 0), base + c))
    qb, vb = g * gc, (N_GROUPS + g) * gc
    own = pl.BlockSpec((cr, cw), lambda c, n: (n, c))
    return _call(
        body, name=name, grid=(gc, s // cr),
        in_specs=[pl.BlockSpec((cr, cw), cur(qb)), pl.BlockSpec((cr, cw), cur(qb)), pl.BlockSpec((cr, cw), prv(qb)),
                  pl.BlockSpec((cr, cw), cur(vb)), pl.BlockSpec((cr, cw), prv(vb)), own, own, own],
        out_specs=own, out_shape=_sds((s, gw), F32),
    )(qn, kn, kn, kv, kv, do, lse, delta)


def _attn_bwd_dkv(qn, kn, kv, do, lse, delta, g, name):
    s, w = qn.shape
    r = DILATIONS[g]
    gw = w // N_GROUPS
    cr = min(ATTN_CHUNK, s)
    nb = cr // (BLK * r)
    nchunk = s // cr
    cw = HEAD_DIM
    gc = gw // cw
    scale = 1.0 / math.sqrt(HEAD_DIM)

    def body(k_ref, v_ref, qc_ref, qx_ref, doc_ref, dox_ref, lc_ref, lx_ref, dc_ref, dx_ref, dk_ref, dv_ref):
        n = pl.program_id(1)
        same_m, prev_m = _band_masks()
        next_last = jnp.logical_and(prev_m, n < nchunk - 1)
        for j in range(r):
            for b in range(nb):
                rows = _rows(j, b, r)
                kv_, vv = k_ref[rows, :].astype(BF16), v_ref[rows, :].astype(BF16)
                qv, dov = qc_ref[rows, :].astype(BF16), doc_ref[rows, :].astype(BF16)
                pd, dsd = _pair_grads(qv, kv_, vv, dov, lc_ref[rows, :], dc_ref[rows, :], same_m, scale)
                if b < nb - 1:
                    rx = _rows(j, b + 1, r)
                    qx, dox, lx, dlx, xm = qc_ref[rx, :], doc_ref[rx, :], lc_ref[rx, :], dc_ref[rx, :], prev_m
                else:
                    rx = _rows(j, 0, r)
                    qx, dox, lx, dlx, xm = qx_ref[rx, :], dox_ref[rx, :], lx_ref[rx, :], dx_ref[rx, :], next_last
                qx, dox = qx.astype(BF16), dox.astype(BF16)
                po, dso = _pair_grads(qx, kv_, vv, dox, lx, dlx, xm, scale)
                dk_ref[rows, :] = _dot_nn(dsd.T.astype(BF16), qv) + _dot_nn(dso.T.astype(BF16), qx)
                dv_ref[rows, :] = _dot_nn(pd.T.astype(BF16), dov) + _dot_nn(po.T.astype(BF16), dox)

    cur = lambda base: (lambda c, n: (n, base + c))
    nxt = lambda base: (lambda c, n: (jnp.minimum(n + 1, nchunk - 1), base + c))
    qb, vb = g * gc, (N_GROUPS + g) * gc
    blk = lambda f: pl.BlockSpec((cr, cw), f)
    return _call(
        body, name=name, grid=(gc, nchunk),
        in_specs=[blk(cur(qb)), blk(cur(vb)), blk(cur(qb)), blk(nxt(qb)), blk(cur(0)), blk(nxt(0)), blk(cur(0)), blk(nxt(0)),
                  blk(cur(0)), blk(nxt(0))],
        out_specs=[blk(cur(0))] * 2, out_shape=[_sds((s, gw), F32)] * 2,
    )(kn, kv, qn, qn, do, do, lse, lse, delta, delta)


def _mod_proj(sc_all, w, name):
    l, d, ns = w.shape
    tn = _pick(ns, 512)

    def body(c_ref, w_ref, o_ref):
        o_ref[...] = jnp.dot(c_ref[...].astype(BF16), w_ref[...].astype(BF16), preferred_element_type=F32)

    return _call(
        body, name=name, grid=(l, ns // tn),
        in_specs=[pl.BlockSpec((N_DEV, d), lambda a, j: (0, 0)), pl.BlockSpec((None, d, tn), lambda a, j: (a, 0, j))],
        out_specs=pl.BlockSpec((None, N_DEV, tn), lambda a, j: (a, 0, j)), out_shape=_sds((l, N_DEV, ns), F32),
    )(sc_all, w)


def _adamw_math(w, g, m, v):
    m = ADAM_B1 * m + (1.0 - ADAM_B1) * g
    v = ADAM_B2 * v + (1.0 - ADAM_B2) * (g * g)
    m_hat = m / (1.0 - ADAM_B1 ** ADAM_STEP)
    v_hat = v / (1.0 - ADAM_B2 ** ADAM_STEP)
    delta = -ADAM_LR * (m_hat / (jnp.sqrt(v_hat) + ADAM_EPS) + ADAM_WD * w)
    return delta, m, v


def _adamw_big(w, g, m, v, name):
    shape = w.shape
    cols = shape[-1]
    rows = math.prod(shape[:-1])
    tr, tc = _pick(rows, 512, 8), _pick(cols, 1024)
    w2, g2, m2, v2 = (t.reshape(rows, cols) for t in (w, g, m, v))

    def body(w_ref, g_ref, m_ref, v_ref, d_ref, mo_ref, vo_ref):
        d_ref[...], mo_ref[...], vo_ref[...] = _adamw_math(w_ref[...], g_ref[...], m_ref[...], v_ref[...])

    spec = pl.BlockSpec((tr, tc), lambda i, j: (i, j))
    outs = _call(body, name=name, grid=(rows // tr, cols // tc), in_specs=[spec] * 4, out_specs=[spec] * 3,
                 out_shape=[_sds((rows, cols), F32)] * 3)(w2, g2, m2, v2)
    return [t.reshape(shape) for t in outs]


def _adamw_mod(w, sct, dm, m, v, name):
    l, d, ns = w.shape
    tr, tc = _pick(d, 512, 8), _pick(ns, 1024)

    def body(w_ref, c_ref, dm_ref, m_ref, v_ref, g_ref, d_ref, mo_ref, vo_ref):
        cv, dv = c_ref[...].astype(BF16).astype(F32), dm_ref[...].astype(BF16).astype(F32)
        g = jnp.zeros((tr, tc), F32)
        for e in range(N_DEV):
            g = g + cv[:, e:e + 1] * dv[e:e + 1, :]
        g_ref[...] = g
        d_ref[...], mo_ref[...], vo_ref[...] = _adamw_math(w_ref[...], g, m_ref[...], v_ref[...])

    spec = pl.BlockSpec((None, tr, tc), lambda a, i, j: (a, i, j))
    return _call(
        body, name=name, grid=(l, d // tr, ns // tc),
        in_specs=[spec, pl.BlockSpec((tr, N_DEV), lambda a, i, j: (i, 0)), pl.BlockSpec((None, N_DEV, tc), lambda a, i, j: (a, 0, j)),
                  spec, spec],
        out_specs=[spec] * 4, out_shape=[_sds((l, d, ns), F32)] * 4,
    )(w, sct, dm, m, v)


def _adamw_small(ws, gs, ms, vs, name):
    n = len(ws)

    def body(*refs):
        w_r, g_r, m_r, v_r = refs[0:n], refs[n:2 * n], refs[2 * n:3 * n], refs[3 * n:4 * n]
        d_o, m_o, v_o = refs[4 * n:5 * n], refs[5 * n:6 * n], refs[6 * n:7 * n]
        for k in range(n):
            d_o[k][...], m_o[k][...], v_o[k][...] = _adamw_math(w_r[k][...], g_r[k][...], m_r[k][...], v_r[k][...])

    vm = pl.BlockSpec(memory_space=pltpu.VMEM)
    shapes = [_sds(w.shape, F32) for w in ws]
    outs = pl.pallas_call(body, name=name, in_specs=[vm] * (4 * n), out_specs=[vm] * (3 * n), out_shape=shapes * 3,
                          interpret=False)(*ws, *gs, *ms, *vs)
    return outs[0:n], outs[n:2 * n], outs[2 * n:3 * n]


def _sum_rows(a, name):
    n, v = a.shape
    tc = _pick(v, 8192)

    def body(a_ref, o_ref):
        acc = a_ref[0:1, :]
        for e in range(1, n):
            acc = acc + a_ref[e:e + 1, :]
        o_ref[...] = acc

    return _call(body, name=name, grid=(v // tc,), in_specs=[pl.BlockSpec((n, tc), lambda j: (0, j))],
                 out_specs=pl.BlockSpec((1, tc), lambda j: (0, j)), out_shape=_sds((1, v), F32))(a)


def _cast_into_full(w2d, kind, slot_arr, name):
    rows, cols = w2d.shape
    tr, tc = _pick(rows, 512, 16), _pick(cols, 1024)
    nr, nc = rows // tr, cols // tc

    def body(s_ref, w_ref, o_ref):
        o_ref[...] = w_ref[...].astype(BF16)

    if kind == "col":
        o_map = lambda i, j, s_ref: (i, s_ref[0] * nc + j)
    else:
        o_map = lambda i, j, s_ref: (s_ref[0] * nr + i, j)
    return _call(body, name=name, grid=(nr, nc), in_specs=[pl.BlockSpec((tr, tc), lambda i, j, s_ref: (i, j))],
                 out_specs=pl.BlockSpec((tr, tc), o_map), out_shape=_sds(_full_shape(kind, (rows, cols)), BF16), nsp=1)(slot_arr, w2d)


def _place():
    x, y, c = lax.axis_index("x"), lax.axis_index("y"), lax.axis_index("c")
    chips = [(1 - x, y), (x, 1 - y), (1 - x, 1 - y)]
    return x, y, c, chips


def _remote(src, dst, send_sem, recv_sem, to):
    return pltpu.make_async_remote_copy(src_ref=src, dst_ref=dst, send_sem=send_sem, recv_sem=recv_sem, device_id=to,
                                        device_id_type=MESH)


def _allgather8(a, name):
    m_per, n = a.shape

    def body(x_ref, out_ref, send_sems, recv_sems, local_sem):
        x, y, c, chips = _place()
        me, sibling = (x, y, c), (x, y, 1 - c)

        def rows(px, py, pc):
            return out_ref.at[pl.ds((4 * px + 2 * py + pc) * m_per, m_per), :]

        def copy(k, block, to, src=None):
            return _remote(rows(*block) if src is None else src, rows(*block), send_sems.at[k], recv_sems.at[k], to)

        mine = pltpu.make_async_copy(x_ref, rows(*me), local_sem)
        mine.start()
        first = [copy(0, me, sibling, src=x_ref)]
        first += [copy(1 + j, me, (*chip, c), src=x_ref) for j, chip in enumerate(chips)]
        for cp in first:
            cp.start()
        passed = [copy(4 + j, (*chip, c), sibling) for j, chip in enumerate(chips)]
        for j, chip in enumerate(chips):
            copy(1 + j, (*chip, c), me).wait_recv()
            passed[j].start()
        copy(0, sibling, me).wait_recv()
        for j, chip in enumerate(chips):
            copy(4 + j, (*chip, 1 - c), me).wait_recv()
        for cp in first + passed:
            cp.wait_send()
        mine.wait()

    return pl.pallas_call(
        body, name=name, out_shape=_sds((N_DEV * m_per, n), a.dtype),
        in_specs=[pl.BlockSpec(memory_space=pltpu.VMEM)], out_specs=pl.BlockSpec(memory_space=pltpu.VMEM),
        scratch_shapes=[pltpu.SemaphoreType.DMA((7,)), pltpu.SemaphoreType.DMA((7,)), pltpu.SemaphoreType.DMA],
        interpret=False,
    )(a)


def _region(ref, kind, shard_shape, slot, half):
    r, cs = shard_shape
    hr = r // 2
    if kind == "col":
        return ref.at[pl.ds(half * hr, hr), pl.ds(slot * cs, cs)]
    return ref.at[pl.ds(slot * r + half * hr, hr), :]


def _full_shape(kind, shard_shape):
    r, cs = shard_shape
    return (r, 4 * cs) if kind == "col" else (4 * r, cs)


_HBM = pl.BlockSpec(memory_space=pltpu.HBM)


def _in_place(arrays, n_sem, copies):
    return _Rider(arrays, [_sds(t.shape, t.dtype) for t in arrays], {a: a for a in range(len(arrays))}, n_sem, copies)


def _gather_chips(fulls, kinds, shapes):
    n = len(fulls)

    def copies(ins, outs, send_sems, recv_sems):
        x, y, c, chips = _place()
        me_slot = 2 * x + y
        sends, recvs = [], []
        for a in range(n):
            for j, (cx, cy) in enumerate(chips):
                k = 3 * a + j
                sends.append(_remote(_region(ins[a], kinds[a], shapes[a], me_slot, c), _region(outs[a], kinds[a], shapes[a], me_slot, c),
                                     send_sems.at[k], recv_sems.at[k], (cx, cy, c)))
                land = _region(outs[a], kinds[a], shapes[a], 2 * cx + cy, c)
                recvs.append(_remote(land, land, send_sems.at[k], recv_sems.at[k], (cx, cy, c)))
        return sends, recvs

    return _in_place(fulls, 3 * n, copies)


def _gather_pair(fulls, kinds, shapes):
    n = len(fulls)

    def copies(ins, outs, send_sems, recv_sems):
        x, y, c, chips = _place()
        sibling = (x, y, 1 - c)
        sends, recvs = [], []
        for a in range(n):
            for j, (cx, cy) in enumerate(chips):
                k = 3 * a + j
                sends.append(_remote(_region(ins[a], kinds[a], shapes[a], 2 * cx + cy, c), _region(outs[a], kinds[a], shapes[a], 2 * cx + cy, c),
                                     send_sems.at[k], recv_sems.at[k], sibling))
                land = _region(outs[a], kinds[a], shapes[a], 2 * cx + cy, 1 - c)
                recvs.append(_remote(land, land, send_sems.at[k], recv_sems.at[k], sibling))
        return sends, recvs

    return _in_place(fulls, 3 * n, copies)


def _rs_pair_exchange(grads, kinds, shapes, name):
    n = len(grads)

    def body(*refs):
        ins, outs = refs[0:n], refs[n:2 * n]
        send_sems, recv_sems = refs[2 * n:]
        x, y, c, _ = _place()
        sibling = (x, y, 1 - c)
        sent = []
        for a in range(n):
            for slot in range(4):
                cp = _remote(_region(ins[a], kinds[a], shapes[a], slot, 1 - c), outs[a].at[slot], send_sems.at[a, slot],
                             recv_sems.at[a, slot], sibling)
                cp.start()
                sent.append(cp)
        for a in range(n):
            for slot in range(4):
                _remote(_region(ins[a], kinds[a], shapes[a], slot, c), outs[a].at[slot], send_sems.at[a, slot],
                        recv_sems.at[a, slot], sibling).wait_recv()
        for cp in sent:
            cp.wait_send()

    return pl.pallas_call(
        body, name=name, out_shape=[_sds((4, s[0] // 2, s[1]), F32) for s in shapes], in_specs=[_HBM] * n, out_specs=[_HBM] * n,
        scratch_shapes=[pltpu.SemaphoreType.DMA((n, 4)), pltpu.SemaphoreType.DMA((n, 4))], interpret=False,
    )(*grads)


def _rs_pair_add(grad, recv, kind, shape, c_arr, name):
    r, cs = shape
    hr = r // 2
    tr, tc = _pick(hr, 512, 16), _pick(cs, 1024)
    nr, nc = hr // tr, cs // tc

    def body(c_ref, g_ref, r_ref, o_ref):
        o_ref[...] = (g_ref[...] + r_ref[...]).astype(BF16)

    if kind == "col":
        g_map = lambda s_, i, j, c_ref: (c_ref[0] * nr + i, s_ * nc + j)
    else:
        g_map = lambda s_, i, j, c_ref: (s_ * 2 * nr + c_ref[0] * nr + i, j)
    own = pl.BlockSpec((None, tr, tc), lambda s_, i, j, c_ref: (s_, i, j))
    return _call(body, name=name, grid=(4, nr, nc), in_specs=[pl.BlockSpec((tr, tc), g_map), own], out_specs=own,
                 out_shape=_sds((4, hr, cs), BF16), nsp=1)(c_arr, grad, recv)


def _rs_chips(parts):
    n = len(parts)

    def copies(ins, outs, send_sems, recv_sems):
        x, y, c, chips = _place()
        sends, recvs = [], []
        for a in range(n):
            for j, (cx, cy) in enumerate(chips):
                k = 3 * a + j
                cp = _remote(ins[a].at[2 * cx + cy], outs[a].at[j], send_sems.at[k], recv_sems.at[k], (cx, cy, c))
                sends.append(cp)
                recvs.append(cp)
        return sends, recvs

    return _Rider(parts, [_sds((3,) + p.shape[1:], BF16) for p in parts], {}, 3 * n, copies)


def _rs_chip_add(part, recv, slot_arr, c_arr, dest, layer, out_shape, name):
    _, hr, cs = part.shape
    tr, tc = _pick(hr, 512, 16), _pick(cs, 1024)
    nr = hr // tr

    def body(s_ref, c_ref, p_ref, r0_ref, r1_ref, r2_ref, *rest):
        o_ref = rest[-1]
        o_ref[...] = ((p_ref[...].astype(F32) + r0_ref[...].astype(F32)) + r1_ref[...].astype(F32)) + r2_ref[...].astype(F32)

    rk = lambda k: pl.BlockSpec((None, tr, tc), lambda i, j, s_ref, c_ref: (k, i, j))
    if layer is None:
        o_spec = pl.BlockSpec((tr, tc), lambda i, j, s_ref, c_ref: (c_ref[0] * nr + i, j))
    else:
        o_spec = pl.BlockSpec((None, tr, tc), lambda i, j, s_ref, c_ref: (layer, c_ref[0] * nr + i, j))
    in_specs = [pl.BlockSpec((None, tr, tc), lambda i, j, s_ref, c_ref: (s_ref[0], i, j)), rk(0), rk(1), rk(2)]
    args = [slot_arr, c_arr, part, recv, recv, recv]
    aliases = {}
    if dest is not None:
        in_specs.append(pl.BlockSpec(memory_space=pl.ANY))
        args.append(dest)
        aliases = {6: 0}
    return pl.pallas_call(
        body, name=name,
        grid_spec=pltpu.PrefetchScalarGridSpec(num_scalar_prefetch=2, grid=(nr, cs // tc), in_specs=in_specs, out_specs=o_spec),
        out_shape=_sds(out_shape, F32), input_output_aliases=aliases,
        compiler_params=pltpu.CompilerParams(dimension_semantics=("arbitrary",) * 2, vmem_limit_bytes=VMEM_LIMIT_MB << 20),
        interpret=False,
    )(*args)


def _rs_pair_share(shards, name):
    n = len(shards)
    views = []
    for a, t in enumerate(shards):
        views += [(a, None)] if t.ndim == 2 else [(a, l) for l in range(t.shape[0])]
    nv = len(views)

    def body(*refs):
        ins, outs = refs[0:n], refs[n:2 * n]
        send_sems, recv_sems = refs[2 * n:]
        x, y, c, _ = _place()
        sibling = (x, y, 1 - c)

        def rows(ref_list, k, half):
            a, layer = views[k]
            ref = ref_list[a] if layer is None else ref_list[a].at[layer]
            hr = ref.shape[0] // 2
            return ref.at[pl.ds(half * hr, hr), :]

        sent = []
        for k in range(nv):
            cp = _remote(rows(ins, k, c), rows(outs, k, c), send_sems.at[k], recv_sems.at[k], sibling)
            cp.start()
            sent.append(cp)
        for k in range(nv):
            _remote(rows(ins, k, 1 - c), rows(outs, k, 1 - c), send_sems.at[k], recv_sems.at[k], sibling).wait_recv()
        for cp in sent:
            cp.wait_send()

    return pl.pallas_call(
        body, name=name, out_shape=[_sds(t.shape, F32) for t in shards], in_specs=[_HBM] * n, out_specs=[_HBM] * n,
        input_output_aliases={a: a for a in range(n)},
        scratch_shapes=[pltpu.SemaphoreType.DMA((nv,)), pltpu.SemaphoreType.DMA((nv,))], interpret=False,
    )(*shards)


def _pad_to(v, mult):
    n = v.shape[0]
    return jnp.pad(v, (0, (-n) % mult))


def kernel(x, c, positions, mod_w, mod_b, norm_mix_g, norm_ffn_g, conv_pw1_w, conv_pw1_b, conv_dw_w, conv_dw_b, conv_ln_g, conv_ln_b, conv_pw2_w, conv_pw2_b, kv_mod_w, kv_mod_b, kv_norm_g, w_kv, k_norm_g, w_q, q_norm_g, w_o, ffn_up_w, ffn_dw_w, ffn_dw_b, ffn_down_w, loss_target, m_mod_w, m_mod_b, m_norm_mix_g, m_norm_ffn_g, m_conv_pw1_w, m_conv_pw1_b, m_conv_dw_w, m_conv_dw_b, m_conv_ln_g, m_conv_ln_b, m_conv_pw2_w, m_conv_pw2_b, m_kv_mod_w, m_kv_mod_b, m_kv_norm_g, m_w_kv, m_k_norm_g, m_w_q, m_q_norm_g, m_w_o, m_ffn_up_w, m_ffn_dw_w, m_ffn_dw_b, m_ffn_down_w, v_mod_w, v_mod_b, v_norm_mix_g, v_norm_ffn_g, v_conv_pw1_w, v_conv_pw1_b, v_conv_dw_w, v_conv_dw_b, v_conv_ln_g, v_conv_ln_b, v_conv_pw2_w, v_conv_pw2_b, v_kv_mod_w, v_kv_mod_b, v_kv_norm_g, v_w_kv, v_k_norm_g, v_w_q, v_q_norm_g, v_w_o, v_ffn_up_w, v_ffn_dw_w, v_ffn_dw_b, v_ffn_down_w):
    _, s, d = x.shape
    f = ffn_dw_b.shape[1]
    qw = w_q.shape[2] * 4
    ax, ay, ac = lax.axis_index("x"), lax.axis_index("y"), lax.axis_index("c")
    slot = 2 * ax + ay
    me8 = 4 * ax + 2 * ay + ac
    slot_arr = jnp.reshape(slot, (1,)).astype(jnp.int32)
    c_arr = jnp.reshape(ac, (1,)).astype(jnp.int32)
    x2 = x.reshape(s, d)
    target = loss_target.reshape(s, d)
    row = lambda v: v.reshape(1, -1)

    c_all = _allgather8(c.reshape(8, d // 8), "gather_c").reshape(N_DEV, d)
    sc_all = jax.nn.silu(c_all)
    mod_part = _mod_proj(sc_all, mod_w, "mod_proj")
    kvm_part = _mod_proj(sc_all, kv_mod_w[None], "kvmod_proj")
    nm, nk = mod_part.shape[2], kvm_part.shape[2]
    small_sharded = [conv_pw1_b, conv_dw_w, conv_dw_b, conv_ln_g, conv_ln_b, conv_pw2_b, ffn_dw_w]
    pack = jnp.concatenate([mod_part.reshape(-1), kvm_part.reshape(-1)] + [t.reshape(-1) for t in small_sharded])
    plen = pack.shape[0]
    pack = _pad_to(pack, 1024)
    gathered = _allgather8(pack.reshape(8, -1), "gather_mod").reshape(4, 2, -1)[:, 0, :plen]
    off = 0

    def take(n_el):
        nonlocal off
        out = lax.slice_in_dim(gathered, off, off + n_el, axis=1)
        off += n_el
        return out

    mod_g = take(2 * N_DEV * nm).reshape(4, 2, N_DEV, nm)
    kvm_g = take(N_DEV * nk).reshape(4, N_DEV, nk)
    mine = lambda t, axis: lax.dynamic_index_in_dim(t, me8, axis=axis, keepdims=False)
    mod_vec = jnp.transpose(mine(mod_g, 2), (1, 0, 2)).reshape(2, 4 * nm) + mod_b
    kvm_vec = mine(kvm_g, 1).reshape(4 * nk) + kv_mod_b
    pw1_b_full = take(conv_pw1_b.shape[1]).reshape(1, -1)
    dw_w_full = jnp.transpose(take(CONV_K * (d // 4)).reshape(4, CONV_K, d // 4), (1, 0, 2)).reshape(CONV_K, d)
    dw_b_full, ln_g_full, ln_b_full, pw2_b_full = (take(d // 4).reshape(1, d) for _ in range(4))
    fdw_full = jnp.transpose(take(2 * FFN_K * (f // 4)).reshape(4, 2, FFN_K, f // 4), (1, 2, 0, 3)).reshape(2, FFN_K, f)
    mods = [[row(mod_vec[l, k * d:(k + 1) * d]) for k in range(6)] for l in range(2)]
    kv_sh, kv_sc = row(kvm_vec[:d]), row(kvm_vec[d:])
    zero_d = jnp.zeros((1, d), F32)

    big = [("pw1", conv_pw1_w[0], "col"), ("pw2", conv_pw2_w[0], "row"), ("wkv", w_kv, "col"), ("wq", w_q[0], "col"),
           ("wo", w_o[0], "col"), ("up0", ffn_up_w[0], "col"), ("up1", ffn_up_w[1], "col"), ("dn0", ffn_down_w[0], "row"),
           ("dn1", ffn_down_w[1], "row")]
    names = [b[0] for b in big]
    kinds = [b[2] for b in big]
    shard_shapes = [b[1].shape for b in big]
    own = {b[0]: _cast_into_full(b[1], b[2], slot_arr, "cast_" + b[0]) for b in big}
    meta = dict(zip(names, zip(kinds, shard_shapes)))
    full = {}

    def stage(make, group, src):
        return make([src[k] for k in group], [meta[k][0] for k in group], [meta[k][1] for k in group])

    grp_a, grp_b, grp_c, grp_d = ["pw1", "pw2"], ["up0", "dn0"], ["wkv", "wq", "wo"], ["up1", "dn1"]
    landed_a = dict(zip(grp_a, _run_rider(stage(_gather_chips, grp_a, own), "gather_a_chips")))
    full.update(zip(grp_a, _run_rider(stage(_gather_pair, grp_a, landed_a), "gather_a_pair")))
    chips_b, chips_c, chips_d = (stage(_gather_chips, g_, own) for g_ in (grp_b, grp_c, grp_d))

    def pair_stage(group, chips_rider):
        return stage(_gather_pair, group, dict(zip(group, chips_rider.results)))

    def ffn_fwd(l, x_in, h, tag, rider_up=None, rider_down=None):
        u = _matmul(h, full["up%d" % l], "nn", F32, "mm_up" + tag, rider=rider_up)
        z = _ffn_act_fwd(u, fdw_full[l], row(ffn_dw_b[l]), "ffn_act" + tag)
        y = _matmul(z, full["dn%d" % l], "nn", F32, "mm_down" + tag, rider=rider_down)
        return u, z, y

    sh_m0, sc_m0, g_m0, sh_f0, sc_f0, g_f0 = mods[0]
    sh_m1, sc_m1, g_m1, sh_f1, sc_f1, g_f1 = mods[1]
    gmix0, gmix1, gffn0, gffn1 = row(norm_mix_g[0]), row(norm_mix_g[1]), row(norm_ffn_g[0]), row(norm_ffn_g[1])
    (h0,) = _resid_mod(x2, None, [(gmix0, sc_m0, sh_m0)], None, "mod_in")
    u0 = _matmul(h0, full["pw1"], "nn", F32, "mm_pw1")
    glu = _glu_fwd(u0, pw1_b_full, "glu")
    dwc = _dwconv_fwd(glu, dw_w_full, dw_b_full, "dwconv", rider=chips_b)
    act = _ln_silu_fwd(dwc, ln_g_full, ln_b_full, "ln_silu")
    pair_b = pair_stage(grp_b, chips_b)
    y0 = _matmul(act, full["pw2"], "nn", F32, "mm_pw2", rider=pair_b)
    full.update(zip(grp_b, pair_b.results))
    x1, hf0 = _resid_mod(x2, (y0, pw2_b_full, g_m0), [(gffn0, sc_f0, sh_f0)], None, "resid_conv")
    uf0 = _matmul(hf0, full["up0"], "nn", F32, "mm_up0", rider=chips_c)
    zf0 = _ffn_act_fwd(uf0, fdw_full[0], row(ffn_dw_b[0]), "ffn_act0")
    pair_c = pair_stage(grp_c, chips_c)
    yf0 = _matmul(zf0, full["dn0"], "nn", F32, "mm_down0", rider=pair_c)
    full.update(zip(grp_c, pair_c.results))
    gkv = row(kv_norm_g)
    xa, hk, hq = _resid_mod(x1, (yf0, zero_d, g_f0), [(gkv, kv_sc, kv_sh), (gmix1, sc_m1, sh_m1)], None, "resid_ffn0")
    kvp = _matmul(hk, full["wkv"], "nn", F32, "mm_kv")
    qp = _matmul(hq, full["wq"], "nn", F32, "mm_q")
    inv_freq = ROPE_THETA ** (-jnp.arange(0, ROT_DIM, 2, dtype=F32) / ROT_DIM)
    half = ROT_DIM // 2
    freq_l = jnp.concatenate([inv_freq, inv_freq, jnp.zeros((HEAD_DIM - ROT_DIM,), F32)]).reshape(1, HEAD_DIM)
    sign_l = jnp.concatenate([-jnp.ones((half,), F32), jnp.ones((half,), F32), jnp.zeros((HEAD_DIM - ROT_DIM,), F32)]).reshape(1, HEAD_DIM)
    ctab, stab = _rope_tables(positions.reshape(s, 1), freq_l, sign_l, "rope_tables")
    gq, gk = row(q_norm_g[0]), row(k_norm_g)
    qn, kn = _qk_norm_rope_fwd(qp, kvp, gq, gk, ctab, stab, "qk_norm_rope", rider=chips_d)
    og, lg = zip(*[_attn_fwd(qn, kn, kvp, g, "attn_fwd%d" % g) for g in range(N_GROUPS)])
    o_mix, lse = _attn_combine(og, lg, "attn_combine")
    pair_d = pair_stage(grp_d, chips_d)
    ya = _matmul(o_mix, full["wo"], "nn", F32, "mm_o", rider=pair_d)
    full.update(zip(grp_d, pair_d.results))
    xb, hf1 = _resid_mod(xa, (ya, zero_d, g_m1), [(gffn1, sc_f1, sh_f1)], None, "resid_attn")
    uf1, zf1, yf1 = ffn_fwd(1, xb, hf1, "1")
    _, dxo, loss_cols = _resid_mod(xb, (yf1, zero_d, g_f1), [], target, "resid_loss")
    loss = lax.psum(jnp.sum(loss_cols), ("x", "y", "c"))

    gbig = {}

    def rs_begin(group, tag):
        gl, kk, ss = [gbig[k] for k in group], [meta[k][0] for k in group], [meta[k][1] for k in group]
        recv = _rs_pair_exchange(gl, kk, ss, "rs_pair_exchange_" + tag)
        return [_rs_pair_add(g_, r_, k_, s_, c_arr, "rs_pair_add_" + nme) for g_, r_, k_, s_, nme in zip(gl, recv, kk, ss, group)]

    parts, recv2 = {}, {}

    def rs_chips(group, tag):
        parts.update(zip(group, rs_begin(group, tag)))
        return _rs_chips([parts[k] for k in group])

    def ffn_bwd(l, dy, u, z, h, tag, rider=None):
        dz = _matmul(dy, full["dn%d" % l], "nt", F32, "mm_down_dx" + tag)
        gbig["dn%d" % l] = _matmul(z, dy, "tn", F32, "mm_down_dw" + tag)
        du, dfw, dfb = _ffn_act_bwd(dz, u, fdw_full[l], row(ffn_dw_b[l]), "ffn_act_bwd" + tag, rider=rider)
        dh = _matmul(du, full["up%d" % l], "nt", F32, "mm_up_dx" + tag)
        gbig["up%d" % l] = _matmul(h, du, "tn", F32, "mm_up_dw" + tag)
        return dh, dfw, dfb

    dxb0, dyf1, dg_f1, _ = _bwd_step(dxo, xb, [], (yf1, zero_d, g_f1), "bwd_loss")
    dhf1, dfw1, dfb1 = ffn_bwd(1, dyf1, uf1, zf1, hf1, "1")
    dxb, dsh_f1, dsc_f1, dgffn1, dya, dg_m1, _ = _bwd_step(dxb0, xb, [(dhf1, gffn1, sc_f1)], (ya, zero_d, g_m1), "bwd_attn_out")
    do = _matmul(dya, full["wo"], "nt", F32, "mm_o_dx")
    gbig["wo"] = _matmul(o_mix, dya, "tn", F32, "mm_o_dw")
    delta = _attn_delta(do, o_mix, "attn_delta")
    dqs = [_attn_bwd_dq(qn, kn, kvp, do, lse, delta, g, "attn_dq%d" % g) for g in range(N_GROUPS)]
    dks, dvs = zip(*[_attn_bwd_dkv(qn, kn, kvp, do, lse, delta, g, "attn_dkv%d" % g) for g in range(N_GROUPS)])
    rs_a, rs_b, rs_c, rs_d = ["dn1", "up1"], ["wo", "wq", "wkv"], ["dn0", "up0"], ["pw2", "pw1"]
    ride_a = rs_chips(rs_a, "a")
    dqp, dkvp, dgq, dgk = _qk_norm_rope_bwd(dqs, dks, dvs, qp, kvp, gq, gk, ctab, stab, "qk_norm_rope_bwd", rider=ride_a)
    recv2.update(zip(rs_a, ride_a.results))
    dhq = _matmul(dqp, full["wq"], "nt", F32, "mm_q_dx")
    gbig["wq"] = _matmul(hq, dqp, "tn", F32, "mm_q_dw")
    dhk = _matmul(dkvp, full["wkv"], "nt", F32, "mm_kv_dx")
    gbig["wkv"] = _matmul(hk, dkvp, "tn", F32, "mm_kv_dw")
    (dxa, dsh_kv, dsc_kv, dgkv, dsh_m1, dsc_m1, dgmix1, dyf0, dg_f0, _) = _bwd_step(
        dxb, xa, [(dhk, gkv, kv_sc), (dhq, gmix1, sc_m1)], (yf0, zero_d, g_f0), "bwd_kvq")
    ride_b = rs_chips(rs_b, "b")
    dhf0, dfw0, dfb0 = ffn_bwd(0, dyf0, uf0, zf0, hf0, "0", rider=ride_b)
    recv2.update(zip(rs_b, ride_b.results))
    dx1, dsh_f0, dsc_f0, dgffn0, dy0, dg_m0, dpw2_b = _bwd_step(dxa, x1, [(dhf0, gffn0, sc_f0)], (y0, pw2_b_full, g_m0), "bwd_conv_out")
    dact = _matmul(dy0, full["pw2"], "nt", F32, "mm_pw2_dx")
    gbig["pw2"] = _matmul(act, dy0, "tn", F32, "mm_pw2_dw")
    ddwc, dln_g, dln_b = _ln_silu_bwd(dact, dwc, ln_g_full, ln_b_full, "ln_silu_bwd")
    ride_c = rs_chips(rs_c, "c")
    dglu, ddw_w, ddw_b = _dwconv_bwd(ddwc, glu, dw_w_full, "dwconv_bwd", rider=ride_c)
    recv2.update(zip(rs_c, ride_c.results))
    du0, dpw1_b = _glu_bwd(dglu, u0, pw1_b_full, "glu_bwd")
    dh0 = _matmul(du0, full["pw1"], "nt", F32, "mm_pw1_dx")
    gbig["pw1"] = _matmul(h0, du0, "tn", F32, "mm_pw1_dw")
    grad_x, dsh_m0, dsc_m0, dgmix0 = _bwd_step(dx1, x2, [(dh0, gmix0, sc_m0)], None, "bwd_in")

    dmod = [jnp.concatenate([dsh_m0, dsc_m0, dg_m0, dsh_f0, dsc_f0, dg_f0], axis=1),
            jnp.concatenate([dsh_m1, dsc_m1, dg_m1, dsh_f1, dsc_f1, dg_f1], axis=1)]
    dkvm = jnp.concatenate([dsh_kv, dsc_kv], axis=1)
    per_ex = [dmod[0], dmod[1], dkvm]
    summed = [dgmix0, dgmix1, dgffn0, dgffn1, dpw1_b, ddw_w, ddw_b, dln_g, dln_b, dpw2_b, dgkv, dgk, dgq, dfw0, dfw1, dfb0, dfb1]
    vec = jnp.concatenate([t.reshape(-1) for t in per_ex + summed])
    vlen = vec.shape[0]
    vec = _pad_to(vec, 1024)
    vall = _allgather8(vec.reshape(8, -1), "gather_small").reshape(N_DEV, -1)
    vsum = _sum_rows(vall, "sum_small")[0]
    n_pe = sum(t.size for t in per_ex)
    dm_all = vall[:, :n_pe]
    off2 = n_pe
    sums = []
    for t in summed:
        sums.append(vsum[off2:off2 + t.size].reshape(t.shape))
        off2 += t.size
    (s_gmix0, s_gmix1, s_gffn0, s_gffn1, s_pw1_b, s_dw_w, s_dw_b, s_ln_g, s_ln_b, s_pw2_b, s_gkv, s_gk, s_gq, s_fw0, s_fw1,
     s_fb0, s_fb1) = sums
    shard_cols = lambda t, width: lax.dynamic_slice_in_dim(t, slot * width, width, axis=t.ndim - 1)
    dm_mod = jnp.stack([shard_cols(dm_all[:, l * 6 * d:(l + 1) * 6 * d], nm) for l in range(2)])
    dm_kv = shard_cols(dm_all[:, 12 * d:14 * d], nk)[None]
    sct = jnp.transpose(sc_all)

    recv2.update(zip(rs_d, _run_rider(rs_chips(rs_d, "d"), "rs_chips_d")))
    reduced = {}
    for nme, shp in zip(names, shard_shapes):
        p_, r_ = parts[nme], recv2[nme]
        if nme in ("up0", "up1", "dn0", "dn1"):
            key, layer = nme[:-1], int(nme[-1])
            reduced[key] = _rs_chip_add(p_, r_, slot_arr, c_arr, reduced.get(key), layer, (2,) + shp, "rs_chip_add_" + nme)
        else:
            reduced[nme] = _rs_chip_add(p_, r_, slot_arr, c_arr, None, None, shp, "rs_chip_add_" + nme)
    g_pw1, g_pw2, g_wkv, g_wq, g_wo, g_up, g_dn = _rs_pair_share(
        [reduced[k] for k in ("pw1", "pw2", "wkv", "wq", "wo", "up", "dn")], "rs_pair_share")

    grads, deltas, new_m, new_v = {}, {}, {}, {}

    def put(nme, g_, res):
        grads[nme] = g_
        deltas[nme], new_m[nme], new_v[nme] = res

    for nme, g_, w_, m_, v_ in (("conv_pw1_w", g_pw1[None], conv_pw1_w, m_conv_pw1_w, v_conv_pw1_w),
                                ("conv_pw2_w", g_pw2[None], conv_pw2_w, m_conv_pw2_w, v_conv_pw2_w),
                                ("w_kv", g_wkv, w_kv, m_w_kv, v_w_kv), ("w_q", g_wq[None], w_q, m_w_q, v_w_q),
                                ("w_o", g_wo[None], w_o, m_w_o, v_w_o), ("ffn_up_w", g_up, ffn_up_w, m_ffn_up_w, v_ffn_up_w),
                                ("ffn_down_w", g_dn, ffn_down_w, m_ffn_down_w, v_ffn_down_w)):
        put(nme, g_, _adamw_big(w_, g_, m_, v_, "adamw_" + nme))
    g_, *res = _adamw_mod(mod_w, sct, dm_mod, m_mod_w, v_mod_w, "adamw_mod_w")
    put("mod_w", g_, res)
    g_, *res = _adamw_mod(kv_mod_w[None], sct, dm_kv, m_kv_mod_w[None], v_kv_mod_w[None], "adamw_kv_mod_w")
    put("kv_mod_w", g_[0], [t[0] for t in res])

    dm_sum = vsum[:n_pe]
    small = [
        ("mod_b", dm_sum[:12 * d].reshape(2, 6 * d), mod_b, m_mod_b, v_mod_b),
        ("norm_mix_g", jnp.concatenate([s_gmix0, s_gmix1], axis=0), norm_mix_g, m_norm_mix_g, v_norm_mix_g),
        ("norm_ffn_g", jnp.concatenate([s_gffn0, s_gffn1], axis=0), norm_ffn_g, m_norm_ffn_g, v_norm_ffn_g),
        ("conv_pw1_b", shard_cols(s_pw1_b, conv_pw1_b.shape[1]), conv_pw1_b, m_conv_pw1_b, v_conv_pw1_b),
        ("conv_dw_w", shard_cols(s_dw_w, d // 4)[None], conv_dw_w, m_conv_dw_w, v_conv_dw_w),
        ("conv_dw_b", shard_cols(s_dw_b, d // 4), conv_dw_b, m_conv_dw_b, v_conv_dw_b),
        ("conv_ln_g", shard_cols(s_ln_g, d // 4), conv_ln_g, m_conv_ln_g, v_conv_ln_g),
        ("conv_ln_b", shard_cols(s_ln_b, d // 4), conv_ln_b, m_conv_ln_b, v_conv_ln_b),
        ("conv_pw2_b", shard_cols(s_pw2_b, d // 4), conv_pw2_b, m_conv_pw2_b, v_conv_pw2_b),
        ("kv_mod_b", dm_sum[12 * d:14 * d], kv_mod_b, m_kv_mod_b, v_kv_mod_b),
        ("kv_norm_g", s_gkv.reshape(-1), kv_norm_g, m_kv_norm_g, v_kv_norm_g),
        ("k_norm_g", s_gk.reshape(-1), k_norm_g, m_k_norm_g, v_k_norm_g),
        ("q_norm_g", s_gq, q_norm_g, m_q_norm_g, v_q_norm_g),
        ("ffn_dw_w", shard_cols(jnp.stack([s_fw0, s_fw1]), f // 4), ffn_dw_w, m_ffn_dw_w, v_ffn_dw_w),
        ("ffn_dw_b", jnp.concatenate([s_fb0, s_fb1], axis=0), ffn_dw_b, m_ffn_dw_b, v_ffn_dw_b),
    ]
    as2d = lambda t: t.reshape(-1, t.shape[-1])
    sd_, sm_, sv_ = _adamw_small([as2d(t[2]) for t in small], [as2d(t[1]) for t in small], [as2d(t[3]) for t in small],
                                 [as2d(t[4]) for t in small], "adamw_small")
    for (nme, g_, w_, _, _), d_, mo_, vo_ in zip(small, sd_, sm_, sv_):
        put(nme, g_.reshape(w_.shape), [d_.reshape(w_.shape), mo_.reshape(w_.shape), vo_.reshape(w_.shape)])

    order = ["mod_w", "mod_b", "norm_mix_g", "norm_ffn_g", "conv_pw1_w", "conv_pw1_b", "conv_dw_w", "conv_dw_b", "conv_ln_g",
             "conv_ln_b", "conv_pw2_w", "conv_pw2_b", "kv_mod_w", "kv_mod_b", "kv_norm_g", "w_kv", "k_norm_g", "w_q", "q_norm_g",
             "w_o", "ffn_up_w", "ffn_dw_w", "ffn_dw_b", "ffn_down_w"]
    return (loss, grad_x.reshape(x.shape), *[grads[k] for k in order], *[deltas[k] for k in order], *[new_m[k] for k in order],
            *[new_v[k] for k in order])
```

```python
import functools
import math

import jax
import jax.numpy as jnp
from jax import lax
from jax.experimental import pallas as pl
from jax.experimental.pallas import tpu as pltpu

F32 = jnp.float32
BF16 = jnp.bfloat16
EPS = 1e-6
NEG = -1e30
HEAD_DIM = 128
ROT_DIM = 32
ROPE_THETA = 500000.0
BLK = 128
DILATIONS = (1, 4, 16)
N_GROUPS = 3
CONV_K = 31
FFN_K = 3
ADAM_LR, ADAM_B1, ADAM_B2, ADAM_EPS, ADAM_WD, ADAM_STEP = 0.001, 0.9, 0.999, 1e-08, 0.01, 10
N_DEV = 8
MESH = pl.DeviceIdType.MESH
VMEM_LIMIT_MB = 56
ROW_TILE = 256
ATTN_CHUNK = 2048


def _pick(n, pref, mult=128):
    best = None
    d = mult
    while d <= min(n, pref):
        if n % d == 0:
            best = d
        d += mult
    return best if best is not None else n


class _Rider:
    def __init__(self, ins, out_shapes, aliases, n_sem, copies):
        self.ins, self.out_shapes, self.aliases, self.n_sem, self.copies = list(ins), list(out_shapes), dict(aliases), n_sem, copies
        self.results = None


def _call(body, *, name, grid, in_specs, out_specs, out_shape, scratch=(), nsp=0, rider=None):
    params = pltpu.CompilerParams(dimension_semantics=("arbitrary",) * len(grid), vmem_limit_bytes=VMEM_LIMIT_MB << 20)
    if rider is None:
        return pl.pallas_call(
            body, name=name,
            grid_spec=pltpu.PrefetchScalarGridSpec(num_scalar_prefetch=nsp, grid=grid, in_specs=in_specs, out_specs=out_specs,
                                                   scratch_shapes=list(scratch)),
            out_shape=out_shape, compiler_params=params, interpret=False,
        )
    single = not isinstance(out_shape, (list, tuple))
    out_shapes = [out_shape] if single else list(out_shape)
    out_specs_l = [out_specs] if single else list(out_specs)
    n_in, n_out, n_scr = len(in_specs), len(out_shapes), len(scratch)
    r_in, r_out = len(rider.ins), len(rider.out_shapes)
    hbm = pl.BlockSpec(memory_space=pltpu.HBM)
    last = tuple(g - 1 for g in grid)

    def wrapped(*refs):
        pre, ins, rin = refs[:nsp], refs[nsp:nsp + n_in], refs[nsp + n_in:nsp + n_in + r_in]
        o0 = nsp + n_in + r_in
        outs, rout = refs[o0:o0 + n_out], refs[o0 + n_out:o0 + n_out + r_out]
        s0 = o0 + n_out + r_out
        scr, (send_sems, recv_sems) = refs[s0:s0 + n_scr], refs[s0 + n_scr:]
        ids = [pl.program_id(a) for a in range(len(grid))]
        is_first = functools.reduce(jnp.logical_and, [i == 0 for i in ids])
        is_last = functools.reduce(jnp.logical_and, [i == l for i, l in zip(ids, last)])

        @pl.when(is_first)
        def _():
            for cp in rider.copies(rin, rout, send_sems, recv_sems)[0]:
                cp.start()

        body(*pre, *ins, *outs, *scr)

        @pl.when(is_last)
        def _():
            sends, recvs = rider.copies(rin, rout, send_sems, recv_sems)
            for cp in recvs:
                cp.wait_recv()
            for cp in sends:
                cp.wait_send()

    call = pl.pallas_call(
        wrapped, name=name,
        grid_spec=pltpu.PrefetchScalarGridSpec(
            num_scalar_prefetch=nsp, grid=grid, in_specs=list(in_specs) + [hbm] * r_in, out_specs=out_specs_l + [hbm] * r_out,
            scratch_shapes=list(scratch) + [pltpu.SemaphoreType.DMA((rider.n_sem,)), pltpu.SemaphoreType.DMA((rider.n_sem,))]),
        out_shape=out_shapes + rider.out_shapes,
        input_output_aliases={nsp + n_in + i: n_out + o for i, o in rider.aliases.items()},
        compiler_params=params, interpret=False,
    )

    def run(*args):
        res = call(*args, *rider.ins)
        rider.results = list(res[n_out:])
        return res[0] if single else list(res[:n_out])

    return run


def _run_rider(rider, name):
    r_in, r_out = len(rider.ins), len(rider.out_shapes)
    hbm = pl.BlockSpec(memory_space=pltpu.HBM)

    def body(*refs):
        sends, recvs = rider.copies(refs[:r_in], refs[r_in:r_in + r_out], *refs[r_in + r_out:])
        for cp in sends:
            cp.start()
        for cp in recvs:
            cp.wait_recv()
        for cp in sends:
            cp.wait_send()

    return pl.pallas_call(
        body, name=name, out_shape=rider.out_shapes, in_specs=[hbm] * r_in, out_specs=[hbm] * r_out,
        input_output_aliases=rider.aliases,
        scratch_shapes=[pltpu.SemaphoreType.DMA((rider.n_sem,)), pltpu.SemaphoreType.DMA((rider.n_sem,))], interpret=False,
    )(*rider.ins)


def _sds(shape, dtype):
    return jax.ShapeDtypeStruct(shape, dtype)


def _acc(ref, val, i):
    @pl.when(i == 0)
    def _():
        ref[...] = val

    @pl.when(i > 0)
    def _():
        ref[...] += val


def _colsum(v):
    return jnp.sum(v, axis=0, keepdims=True)


def _silu(v):
    return v * jax.nn.sigmoid(v)


def _dsilu(v):
    s = jax.nn.sigmoid(v)
    return s * (1.0 + v * (1.0 - s))


_DIMS = {"nn": (((1,), (0,)), ((), ())), "nt": (((1,), (1,)), ((), ())), "tn": (((0,), (0,)), ((), ()))}


def _matmul(a, b, mode, out_dtype, name, rider=None):
    a_halves = a.shape[0] if a.ndim == 3 else 0
    b_halves = b.shape[0] if b.ndim == 3 else 0
    if mode == "nn":
        (m, c), (_, n) = a.shape, b.shape
    elif mode == "nt":
        m, c = (a.shape[1], a.shape[0] * a.shape[2]) if a_halves else a.shape
        n = b.shape[0]
    else:
        c, m = a.shape
        n = b.shape[0] * b.shape[2] if b_halves else b.shape[1]
    tm = _pick(m, 1024)
    tn = _pick(n // b_halves, 1024) if b_halves else _pick(n, 1024)
    c_cap = 2048 if mode == "tn" else 2816
    tc = _pick(c // a_halves, c_cap) if a_halves else _pick(c, c_cap)
    nk = c // tc
    if a_halves:
        per_a = c // a_halves // tc
        a_spec = pl.BlockSpec((None, tm, tc), lambda i, j, k: (k // per_a, i, k % per_a))
    else:
        a_spec = {"nn": pl.BlockSpec((tm, tc), lambda i, j, k: (i, k)), "nt": pl.BlockSpec((tm, tc), lambda i, j, k: (i, k)),
                  "tn": pl.BlockSpec((tc, tm), lambda i, j, k: (k, i))}[mode]
    if b_halves:
        per_b = n // b_halves // tn
        b_spec = pl.BlockSpec((None, tc, tn), lambda i, j, k: (j // per_b, k, j % per_b))
    else:
        b_spec = {"nn": pl.BlockSpec((tc, tn), lambda i, j, k: (k, j)), "nt": pl.BlockSpec((tn, tc), lambda i, j, k: (j, k)),
                  "tn": pl.BlockSpec((tc, tn), lambda i, j, k: (k, j))}[mode]
    dims = _DIMS[mode]

    def body(a_ref, b_ref, o_ref, acc_ref):
        k = pl.program_id(2)
        p = lax.dot_general(a_ref[...], b_ref[...], dims, preferred_element_type=F32)
        if nk == 1:
            o_ref[...] = p.astype(out_dtype)
        else:
            @pl.when(k == 0)
            def _():
                acc_ref[...] = p

            @pl.when(k > 0)
            def _():
                acc_ref[...] += p

            @pl.when(k == nk - 1)
            def _():
                o_ref[...] = acc_ref[...].astype(out_dtype)

    return _call(
        body, name=name, grid=(m // tm, n // tn, nk), in_specs=[a_spec, b_spec],
        out_specs=pl.BlockSpec((tm, tn), lambda i, j, k: (i, j)), out_shape=_sds((m, n), out_dtype),
        scratch=[pltpu.VMEM((tm, tn), F32)], rider=rider,
    )(a, b)


def _row_spec(tr, w):
    return pl.BlockSpec((tr, w), lambda i: (i, 0))


def _vec_spec(w):
    return pl.BlockSpec((1, w), lambda i: (0, 0))


def _resid_mod(x, prev, mods, target, name):
    s, d = x.shape
    tr = _pick(s, ROW_TILE, 8)
    n_mod = len(mods)

    def body(*refs):
        it = iter(refs)
        x_ref = next(it)
        if prev is not None:
            y_ref, yb_ref, gate_ref = next(it), next(it), next(it)
        mod_refs = [(next(it), next(it), next(it)) for _ in range(n_mod)]
        if target is not None:
            t_ref = next(it)
        if prev is not None:
            xo_ref = next(it)
        h_refs = [next(it) for _ in range(n_mod)]
        i = pl.program_id(0)
        xv = x_ref[...]
        if prev is not None:
            xv = xv + gate_ref[...] * (y_ref[...] + yb_ref[...])
            xo_ref[...] = xv
        if n_mod:
            nrm = xv * lax.rsqrt(jnp.mean(xv * xv, axis=-1, keepdims=True) + EPS)
            for (g_ref, sc_ref, sh_ref), h_ref in zip(mod_refs, h_refs):
                h_ref[...] = (nrm * g_ref[...] * (1.0 + sc_ref[...]) + sh_ref[...]).astype(BF16)
        if target is not None:
            dx_ref, loss_ref = next(it), next(it)
            err = xv - t_ref[...]
            dx_ref[...] = err * (1.0 / d)
            _acc(loss_ref, _colsum(err * err) * (0.5 / d), i)

    ins, in_specs = [x], [_row_spec(tr, d)]
    if prev is not None:
        ins += list(prev)
        in_specs += [_row_spec(tr, d), _vec_spec(d), _vec_spec(d)]
    for g, sc, sh in mods:
        ins += [g, sc, sh]
        in_specs += [_vec_spec(d)] * 3
    if target is not None:
        ins.append(target)
        in_specs.append(_row_spec(tr, d))
    out_shape, out_specs = [], []
    if prev is not None:
        out_shape.append(_sds((s, d), F32))
        out_specs.append(_row_spec(tr, d))
    for _ in mods:
        out_shape.append(_sds((s, d), BF16))
        out_specs.append(_row_spec(tr, d))
    if target is not None:
        out_shape += [_sds((s, d), F32), _sds((1, d), F32)]
        out_specs += [_row_spec(tr, d), _vec_spec(d)]
    return _call(body, name=name, grid=(s // tr,), in_specs=in_specs, out_specs=out_specs, out_shape=out_shape)(*ins)


def _bwd_step(dx_up, x, mods, prev, name, rider=None):
    s, d = x.shape
    tr = _pick(s, ROW_TILE, 8)
    n_mod = len(mods)

    def body(*refs):
        it = iter(refs)
        dxu_ref, x_ref = next(it), next(it)
        mod_refs = [(next(it), next(it), next(it)) for _ in range(n_mod)]
        if prev is not None:
            y_ref, yb_ref, gate_ref = next(it), next(it), next(it)
        dx_ref = next(it)
        acc_refs = [(next(it), next(it), next(it)) for _ in range(n_mod)]
        i = pl.program_id(0)
        dx = dxu_ref[...]
        if n_mod:
            xv = x_ref[...]
            rstd = lax.rsqrt(jnp.mean(xv * xv, axis=-1, keepdims=True) + EPS)
            nrm = xv * rstd
        for (dh_ref, g_ref, sc_ref), (dsh_ref, dsc_ref, dg_ref) in zip(mod_refs, acc_refs):
            dh = dh_ref[...]
            gv, one_sc = g_ref[...], 1.0 + sc_ref[...]
            _acc(dsh_ref, _colsum(dh), i)
            t = dh * nrm
            _acc(dsc_ref, _colsum(t) * gv, i)
            _acc(dg_ref, _colsum(t) * one_sc, i)
            dn = dh * (gv * one_sc)
            dx = dx + rstd * (dn - nrm * jnp.mean(dn * nrm, axis=-1, keepdims=True))
        dx_ref[...] = dx
        if prev is not None:
            dy_ref, dgate_ref, dyb_ref = next(it), next(it), next(it)
            dy = gate_ref[...] * dx
            dy_ref[...] = dy.astype(BF16)
            _acc(dgate_ref, _colsum(dx * (y_ref[...] + yb_ref[...])), i)
            _acc(dyb_ref, _colsum(dy), i)

    ins, in_specs = [dx_up, x], [_row_spec(tr, d)] * 2
    for dh, g, sc in mods:
        ins += [dh, g, sc]
        in_specs += [_row_spec(tr, d), _vec_spec(d), _vec_spec(d)]
    if prev is not None:
        ins += list(prev)
        in_specs += [_row_spec(tr, d), _vec_spec(d), _vec_spec(d)]
    out_shape, out_specs = [_sds((s, d), F32)], [_row_spec(tr, d)]
    for _ in mods:
        out_shape += [_sds((1, d), F32)] * 3
        out_specs += [_vec_spec(d)] * 3
    if prev is not None:
        out_shape += [_sds((s, d), BF16), _sds((1, d), F32), _sds((1, d), F32)]
        out_specs += [_row_spec(tr, d), _vec_spec(d), _vec_spec(d)]
    return _call(body, name=name, grid=(s // tr,), in_specs=in_specs, out_specs=out_specs, out_shape=out_shape, rider=rider)(*ins)


def _glu_fwd(u, bias, name):
    s, d2 = u.shape
    d = d2 // 2
    tr = _pick(s, ROW_TILE, 8)

    def body(u_ref, b_ref, o_ref):
        uv = u_ref[...] + b_ref[...]
        o_ref[...] = uv[:, :d] * jax.nn.sigmoid(uv[:, d:])

    return _call(body, name=name, grid=(s // tr,), in_specs=[_row_spec(tr, d2), _vec_spec(d2)],
                 out_specs=_row_spec(tr, d), out_shape=_sds((s, d), F32))(u, bias)


def _glu_bwd(dglu, u, bias, name):
    s, d2 = u.shape
    d = d2 // 2
    tr = _pick(s, ROW_TILE, 8)

    def body(dg_ref, u_ref, b_ref, du_ref, db_ref):
        i = pl.program_id(0)
        uv = u_ref[...] + b_ref[...]
        a, sg = uv[:, :d], jax.nn.sigmoid(uv[:, d:])
        dg = dg_ref[...]
        da = dg * sg
        dgt = dg * a * sg * (1.0 - sg)
        du_ref[:, :d] = da.astype(BF16)
        du_ref[:, d:] = dgt.astype(BF16)
        _acc(db_ref, jnp.concatenate([_colsum(da), _colsum(dgt)], axis=1), i)

    return _call(body, name=name, grid=(s // tr,), in_specs=[_row_spec(tr, d), _row_spec(tr, d2), _vec_spec(d2)],
                 out_specs=[_row_spec(tr, d2), _vec_spec(d2)], out_shape=[_sds((s, d2), BF16), _sds((1, d2), F32)])(dglu, u, bias)


def _halo_rows(k):
    return -(-(k - 1) // 8) * 8


CONV_CHUNK = 32


def _sublane_shifts(cat_ref, sh_ref, rows):
    for m in range(8):
        sh_ref[m] = cat_ref[pl.ds(m, rows), :]


def _dwconv_fwd(x, w, b, name, rider=None):
    s, c = x.shape
    kk = w.shape[0]
    hb = _halo_rows(kk)
    tr, tc = _pick(s, ROW_TILE, hb), _pick(c, 512)
    per = tr // hb
    ch = CONV_CHUNK

    def body(xp_ref, x_ref, w_ref, b_ref, o_ref, cat_ref, sh_ref):
        i = pl.program_id(1)
        cat_ref[0:hb, :] = jnp.where(i > 0, xp_ref[...], 0.0)
        cat_ref[hb:hb + tr, :] = x_ref[...]
        cat_ref[hb + tr:hb + tr + 8, :] = jnp.zeros((8, tc), F32)
        _sublane_shifts(cat_ref, sh_ref, tr + hb)

        def chunk(ci, carry):
            r0 = pl.multiple_of(ci * ch, ch)
            acc = jnp.zeros((ch, tc), F32) + b_ref[...]
            for k in range(kk):
                a, m = divmod(hb - (kk - 1) + k, 8)
                acc = acc + w_ref[k:k + 1, :] * sh_ref[m, pl.ds(r0 + 8 * a, ch), :]
            o_ref[pl.ds(r0, ch), :] = acc
            return carry

        lax.fori_loop(0, tr // ch, chunk, 0)

    return _call(
        body, name=name, grid=(c // tc, s // tr),
        in_specs=[pl.BlockSpec((hb, tc), lambda j, i: (jnp.maximum(i * per - 1, 0), j)), pl.BlockSpec((tr, tc), lambda j, i: (i, j)),
                  pl.BlockSpec((kk, tc), lambda j, i: (0, j)), pl.BlockSpec((1, tc), lambda j, i: (0, j))],
        out_specs=pl.BlockSpec((tr, tc), lambda j, i: (i, j)), out_shape=_sds((s, c), F32),
        scratch=[pltpu.VMEM((tr + hb + 8, tc), F32), pltpu.VMEM((8, tr + hb, tc), F32)], rider=rider,
    )(x, x, w, b)


def _dwconv_bwd(dy, x, w, name, rider=None):
    s, c = x.shape
    kk = w.shape[0]
    hb = _halo_rows(kk)
    tr, tc = _pick(s, ROW_TILE, hb), _pick(c, 512)
    per, nt = tr // hb, s // tr
    ch = CONV_CHUNK
    groups = ch // 8

    def body(xp_ref, x_ref, dy_ref, dyn_ref, w_ref, dx_ref, dw_ref, db_ref, cat_ref, shx_ref, shd_ref, acc_ref):
        i = pl.program_id(1)
        cat_ref[0:hb, :] = jnp.where(i > 0, xp_ref[...], 0.0)
        cat_ref[hb:hb + tr, :] = x_ref[...]
        cat_ref[hb + tr:hb + tr + 8, :] = jnp.zeros((8, tc), F32)
        _sublane_shifts(cat_ref, shx_ref, tr + hb)
        cat_ref[0:tr, :] = dy_ref[...]
        cat_ref[tr:tr + hb, :] = jnp.where(i < nt - 1, dyn_ref[...], 0.0)
        _sublane_shifts(cat_ref, shd_ref, tr + hb)

        @pl.when(i == 0)
        def _():
            acc_ref[...] = jnp.zeros_like(acc_ref)

        def fold(v):
            out = v[0:8, :]
            for g in range(1, groups):
                out = out + v[8 * g:8 * g + 8, :]
            return out

        def chunk(ci, carry):
            r0 = pl.multiple_of(ci * ch, ch)
            dyv = dy_ref[pl.ds(r0, ch), :]
            acc_ref[kk] += fold(dyv)
            dxv = jnp.zeros((ch, tc), F32)
            for k in range(kk):
                a, m = divmod(kk - 1 - k, 8)
                dxv = dxv + w_ref[k:k + 1, :] * shd_ref[m, pl.ds(r0 + 8 * a, ch), :]
                a, m = divmod(hb - (kk - 1) + k, 8)
                acc_ref[k] += fold(dyv * shx_ref[m, pl.ds(r0 + 8 * a, ch), :])
            dx_ref[pl.ds(r0, ch), :] = dxv
            return carry

        lax.fori_loop(0, tr // ch, chunk, 0)

        @pl.when(i == nt - 1)
        def _():
            for k in range(kk):
                dw_ref[k:k + 1, :] = _colsum(acc_ref[k])
            db_ref[...] = _colsum(acc_ref[kk])

    return _call(
        body, name=name, grid=(c // tc, nt),
        in_specs=[pl.BlockSpec((hb, tc), lambda j, i: (jnp.maximum(i * per - 1, 0), j)), pl.BlockSpec((tr, tc), lambda j, i: (i, j)),
                  pl.BlockSpec((tr, tc), lambda j, i: (i, j)),
                  pl.BlockSpec((hb, tc), lambda j, i: (jnp.minimum((i + 1) * per, s // hb - 1), j)),
                  pl.BlockSpec((kk, tc), lambda j, i: (0, j))],
        out_specs=[pl.BlockSpec((tr, tc), lambda j, i: (i, j)), pl.BlockSpec((kk, tc), lambda j, i: (0, j)),
                   pl.BlockSpec((1, tc), lambda j, i: (0, j))],
        out_shape=[_sds((s, c), F32), _sds((kk, c), F32), _sds((1, c), F32)],
        scratch=[pltpu.VMEM((tr + hb + 8, tc), F32), pltpu.VMEM((8, tr + hb, tc), F32), pltpu.VMEM((8, tr + hb, tc), F32),
                 pltpu.VMEM((kk + 1, 8, tc), F32)], rider=rider,
    )(x, x, dy, dy, w)


def _ln_silu_fwd(x, g, b, name):
    s, d = x.shape
    tr = _pick(s, ROW_TILE, 8)

    def body(x_ref, g_ref, b_ref, o_ref):
        xv = x_ref[...]
        mu = jnp.mean(xv, axis=-1, keepdims=True)
        xc = xv - mu
        ln = xc * lax.rsqrt(jnp.mean(xc * xc, axis=-1, keepdims=True) + EPS) * g_ref[...] + b_ref[...]
        o_ref[...] = _silu(ln).astype(BF16)

    return _call(body, name=name, grid=(s // tr,), in_specs=[_row_spec(tr, d), _vec_spec(d), _vec_spec(d)],
                 out_specs=_row_spec(tr, d), out_shape=_sds((s, d), BF16))(x, g, b)


def _ln_silu_bwd(dact, x, g, b, name):
    s, d = x.shape
    tr = _pick(s, ROW_TILE, 8)

    def body(da_ref, x_ref, g_ref, b_ref, dx_ref, dg_ref, db_ref):
        i = pl.program_id(0)
        xv = x_ref[...]
        mu = jnp.mean(xv, axis=-1, keepdims=True)
        xc = xv - mu
        rstd = lax.rsqrt(jnp.mean(xc * xc, axis=-1, keepdims=True) + EPS)
        xh = xc * rstd
        ln = xh * g_ref[...] + b_ref[...]
        dln = da_ref[...] * _dsilu(ln)
        _acc(dg_ref, _colsum(dln * xh), i)
        _acc(db_ref, _colsum(dln), i)
        dxh = dln * g_ref[...]
        dx_ref[...] = rstd * (dxh - jnp.mean(dxh, axis=-1, keepdims=True) - xh * jnp.mean(dxh * xh, axis=-1, keepdims=True))

    return _call(body, name=name, grid=(s // tr,), in_specs=[_row_spec(tr, d), _row_spec(tr, d), _vec_spec(d), _vec_spec(d)],
                 out_specs=[_row_spec(tr, d), _vec_spec(d), _vec_spec(d)],
                 out_shape=[_sds((s, d), F32), _sds((1, d), F32), _sds((1, d), F32)])(dact, x, g, b)


def _ffn_act_fwd(u, w, b, name):
    s, f2 = u.shape
    f = f2 // 2
    hb = 8
    tr, tc = _pick(s, ROW_TILE, hb), _pick(f, 1408)
    per, nf = tr // hb, f // tc

    def body(gp_ref, g_ref, v_ref, w_ref, b_ref, z_ref, cat_ref):
        i = pl.program_id(1)
        cat_ref[0:hb, :] = jnp.where(i > 0, gp_ref[...], 0.0)
        cat_ref[hb:hb + tr, :] = g_ref[...]
        gc = jnp.zeros((tr, tc), F32) + b_ref[...]
        for k in range(FFN_K):
            gc = gc + w_ref[k:k + 1, :] * cat_ref[pl.ds(hb - (FFN_K - 1) + k, tr), :]
        z_ref[...] = (_silu(gc) * v_ref[...]).astype(BF16)

    return _call(
        body, name=name, grid=(nf, s // tr),
        in_specs=[pl.BlockSpec((hb, tc), lambda j, i: (jnp.maximum(i * per - 1, 0), j)), pl.BlockSpec((tr, tc), lambda j, i: (i, j)),
                  pl.BlockSpec((tr, tc), lambda j, i: (i, nf + j)), pl.BlockSpec((FFN_K, tc), lambda j, i: (0, j)),
                  pl.BlockSpec((1, tc), lambda j, i: (0, j))],
        out_specs=pl.BlockSpec((tr, tc), lambda j, i: (i, j)), out_shape=_sds((s, f), BF16),
        scratch=[pltpu.VMEM((tr + hb, tc), F32)],
    )(u, u, u, w, b)


def _ffn_act_bwd(dz, u, w, b, name, rider=None):
    s, f2 = u.shape
    f = f2 // 2
    hb = 8
    tr, tc = _pick(s, ROW_TILE, hb), _pick(f, 1408)
    per, nf, nt = tr // hb, f // tc, s // tr
    ext = tr + hb

    def body(gp_ref, g_ref, gn_ref, v_ref, vn_ref, dz_ref, dzn_ref, w_ref, b_ref, du_ref, dw_ref, db_ref, gcat_ref, dgc_ref):
        i = pl.program_id(1)
        last = i == nt - 1
        gcat_ref[0:hb, :] = jnp.where(i > 0, gp_ref[...], 0.0)
        gcat_ref[hb:hb + tr, :] = g_ref[...]
        gcat_ref[hb + tr:hb + tr + hb, :] = gn_ref[...]
        gc = jnp.zeros((ext, tc), F32) + b_ref[...]
        for k in range(FFN_K):
            gc = gc + w_ref[k:k + 1, :] * gcat_ref[pl.ds(hb - (FFN_K - 1) + k, ext), :]
        dz_cur = dz_ref[...]
        du_ref[1] = (dz_cur * _silu(gc[0:tr, :])).astype(BF16)
        dgc_ref[0:tr, :] = dz_cur * v_ref[...] * _dsilu(gc[0:tr, :])
        dgc_ref[tr:ext, :] = jnp.where(last, 0.0, dzn_ref[...] * vn_ref[...] * _dsilu(gc[tr:ext, :]))
        dgt = jnp.zeros((tr, tc), F32)
        for k in range(FFN_K):
            dgt = dgt + w_ref[k:k + 1, :] * dgc_ref[pl.ds(FFN_K - 1 - k, tr), :]
        du_ref[0] = dgt.astype(BF16)

        @pl.when(i == 0)
        def _():
            dw_ref[...] = jnp.zeros_like(dw_ref)
            db_ref[...] = jnp.zeros_like(db_ref)

        dgc_cur = dgc_ref[0:tr, :]
        db_ref[...] += _colsum(dgc_cur)
        for k in range(FFN_K):
            dw_ref[k:k + 1, :] += _colsum(dgc_cur * gcat_ref[pl.ds(hb - (FFN_K - 1) + k, tr), :])

    prev_map = lambda j, i: (jnp.maximum(i * per - 1, 0), j)
    next_map = lambda j, i: (jnp.minimum((i + 1) * per, s // hb - 1), j)
    next_map_v = lambda j, i: (jnp.minimum((i + 1) * per, s // hb - 1), nf + j)
    return _call(
        body, name=name, grid=(nf, nt),
        in_specs=[pl.BlockSpec((hb, tc), prev_map), pl.BlockSpec((tr, tc), lambda j, i: (i, j)), pl.BlockSpec((hb, tc), next_map),
                  pl.BlockSpec((tr, tc), lambda j, i: (i, nf + j)), pl.BlockSpec((hb, tc), next_map_v),
                  pl.BlockSpec((tr, tc), lambda j, i: (i, j)), pl.BlockSpec((hb, tc), next_map),
                  pl.BlockSpec((FFN_K, tc), lambda j, i: (0, j)), pl.BlockSpec((1, tc), lambda j, i: (0, j))],
        out_specs=[pl.BlockSpec((2, tr, tc), lambda j, i: (0, i, j)), pl.BlockSpec((FFN_K, tc), lambda j, i: (0, j)),
                   pl.BlockSpec((1, tc), lambda j, i: (0, j))],
        out_shape=[_sds((2, s, f), BF16), _sds((FFN_K, f), F32), _sds((1, f), F32)],
        scratch=[pltpu.VMEM((tr + 2 * hb, tc), F32), pltpu.VMEM((ext, tc), F32)], rider=rider,
    )(u, u, u, u, u, dz, dz, w, b)


def _rope_tables(pos, freq, sign, name):
    s = pos.shape[0]
    tr = _pick(s, 512, 8)

    def body(p_ref, f_ref, s_ref, c_ref, sn_ref):
        ang = p_ref[...].astype(F32) * f_ref[...]
        c_ref[...] = jnp.cos(ang)
        sn_ref[...] = jnp.sin(ang) * s_ref[...]

    return _call(body, name=name, grid=(s // tr,), in_specs=[pl.BlockSpec((tr, 1), lambda i: (i, 0)), _vec_spec(128), _vec_spec(128)],
                 out_specs=[_row_spec(tr, 128)] * 2, out_shape=[_sds((s, 128), F32)] * 2)(pos, freq, sign)


def _partner(v):
    lane = lax.broadcasted_iota(jnp.int32, v.shape, 1)
    lower = pltpu.roll(v, HEAD_DIM - ROT_DIM // 2, 1)
    upper = jnp.where(lane < ROT_DIM, pltpu.roll(v, ROT_DIM // 2, 1), 0.0)
    return jnp.where(lane < ROT_DIM // 2, lower, upper)


def _qk_norm_rope_fwd(q, kv, gq, gk, ctab, stab, name, rider=None):
    s, w = q.shape
    tr = _pick(s, 128, 8)
    heads = w // HEAD_DIM

    def body(q_ref, k_ref, gq_ref, gk_ref, c_ref, s_ref, qn_ref, kn_ref):
        cv, sv = c_ref[...], s_ref[...]
        for src, g_ref, dst in ((q_ref, gq_ref, qn_ref), (k_ref, gk_ref, kn_ref)):
            gv = g_ref[...]
            for h in range(heads):
                cols = pl.ds(h * HEAD_DIM, HEAD_DIM)
                xv = src[:, cols]
                nv = xv * lax.rsqrt(jnp.mean(xv * xv, axis=-1, keepdims=True) + EPS) * gv
                dst[:, cols] = nv * cv + _partner(nv) * sv

    return _call(
        body, name=name, grid=(s // tr,),
        in_specs=[_row_spec(tr, w), _row_spec(tr, w), _vec_spec(128), _vec_spec(128), _row_spec(tr, 128), _row_spec(tr, 128)],
        out_specs=[_row_spec(tr, w)] * 2, out_shape=[_sds((s, w), F32)] * 2, rider=rider,
    )(q, kv, gq, gk, ctab, stab)


def _qk_norm_rope_bwd(dqs, dks, dvs, q, kv, gq, gk, ctab, stab, name, rider=None):
    s, w = q.shape
    gw = w // N_GROUPS
    tr = _pick(s, 128, 8)
    hpg = gw // HEAD_DIM

    def body(*refs):
        dq_refs, dk_refs, dv_refs = refs[0:3], refs[3:6], refs[6:9]
        q_ref, k_ref, gq_ref, gk_ref, c_ref, s_ref, dq_ref, dkv_ref, dgq_ref, dgk_ref = refs[9:]
        i = pl.program_id(0)
        cv, sv = c_ref[...], s_ref[...]
        for d_refs, src, g_ref, dst, dg_ref in ((dq_refs, q_ref, gq_ref, dq_ref, dgq_ref), (dk_refs, k_ref, gk_ref, dkv_ref, dgk_ref)):
            gv = g_ref[...]
            dg = jnp.zeros((1, HEAD_DIM), F32)
            for g in range(N_GROUPS):
                for h in range(hpg):
                    cols = pl.ds(g * gw + h * HEAD_DIM, HEAD_DIM)
                    dout = d_refs[g][:, pl.ds(h * HEAD_DIM, HEAD_DIM)]
                    dn = dout * cv + _partner(dout * sv)
                    xv = src[:, cols]
                    rstd = lax.rsqrt(jnp.mean(xv * xv, axis=-1, keepdims=True) + EPS)
                    xh = xv * rstd
                    dg = dg + _colsum(dn * xh)
                    dxh = dn * gv
                    dst[:, cols] = (rstd * (dxh - xh * jnp.mean(dxh * xh, axis=-1, keepdims=True))).astype(BF16)
            _acc(dg_ref, dg, i)
        for g in range(N_GROUPS):
            dkv_ref[:, pl.ds(w + g * gw, gw)] = dv_refs[g][...].astype(BF16)

    return _call(
        body, name=name, grid=(s // tr,),
        in_specs=[_row_spec(tr, gw)] * 9 + [_row_spec(tr, w), _row_spec(tr, w), _vec_spec(128), _vec_spec(128),
                                            _row_spec(tr, 128), _row_spec(tr, 128)],
        out_specs=[_row_spec(tr, w), _row_spec(tr, 2 * w), _vec_spec(128), _vec_spec(128)],
        out_shape=[_sds((s, w), BF16), _sds((s, 2 * w), BF16), _sds((1, 128), F32), _sds((1, 128), F32)], rider=rider,
    )(*dqs, *dks, *dvs, q, kv, gq, gk, ctab, stab)


def _rows(j, b, r):
    start = j + r * BLK * b
    return pl.ds(start, BLK, stride=r) if r > 1 else pl.ds(start, BLK)


def _dot_nt(a, b):
    return lax.dot_general(a, b, _DIMS["nt"], preferred_element_type=F32)


def _dot_nn(a, b):
    return lax.dot_general(a, b, _DIMS["nn"], preferred_element_type=F32)


def _band_masks():
    qi = lax.broadcasted_iota(jnp.int32, (BLK, BLK), 0)
    kj = lax.broadcasted_iota(jnp.int32, (BLK, BLK), 1)
    return kj <= qi, kj >= qi


def _attn_fwd(qn, kn, kv, g, name):
    s, w = qn.shape
    r = DILATIONS[g]
    gw = w // N_GROUPS
    cr = min(ATTN_CHUNK, s)
    nb = cr // (BLK * r)
    hp = 1
    cw = hp * HEAD_DIM
    gc = gw // cw
    scale = 1.0 / math.sqrt(HEAD_DIM)

    def body(q_ref, kc_ref, kp_ref, vc_ref, vp_ref, o_ref, l_ref):
        n = pl.program_id(1)
        same_m, prev_m = _band_masks()
        prev_first = jnp.logical_and(prev_m, n > 0)
        for h in range(hp):
            cols = pl.ds(h * HEAD_DIM, HEAD_DIM)
            for j in range(r):
                for b in range(nb):
                    rows = _rows(j, b, r)
                    qv = q_ref[rows, cols].astype(BF16)
                    kc, vc = kc_ref[rows, cols].astype(BF16), vc_ref[rows, cols].astype(BF16)
                    if b > 0:
                        rp = _rows(j, b - 1, r)
                        kp, vp, pm = kc_ref[rp, cols].astype(BF16), vc_ref[rp, cols].astype(BF16), prev_m
                    else:
                        rp = _rows(j, nb - 1, r)
                        kp, vp, pm = kp_ref[rp, cols].astype(BF16), vp_ref[rp, cols].astype(BF16), prev_first
                    sd = jnp.where(same_m, _dot_nt(qv, kc) * scale, NEG)
                    so = jnp.where(pm, _dot_nt(qv, kp) * scale, NEG)
                    m = jnp.maximum(jnp.max(sd, axis=-1, keepdims=True), jnp.max(so, axis=-1, keepdims=True))
                    pd, po = jnp.exp(sd - m), jnp.exp(so - m)
                    den = jnp.sum(pd, axis=-1, keepdims=True) + jnp.sum(po, axis=-1, keepdims=True)
                    ov = (_dot_nn(pd.astype(BF16), vc) + _dot_nn(po.astype(BF16), vp)) / den
                    o_ref[rows, cols] = ov
                    l_ref[rows, cols] = jnp.broadcast_to(m + jnp.log(den), (BLK, HEAD_DIM))

    cur = lambda base: (lambda c, n: (n, base + c))
    prv = lambda base: (lambda c, n: (jnp.maximum(n - 1, 0), base + c))
    qb, kb, vb = g * gc, g * gc, (N_GROUPS + g) * gc
    return _call(
        body, name=name, grid=(gc, s // cr),
        in_specs=[pl.BlockSpec((cr, cw), cur(qb)), pl.BlockSpec((cr, cw), cur(kb)), pl.BlockSpec((cr, cw), prv(kb)),
                  pl.BlockSpec((cr, cw), cur(vb)), pl.BlockSpec((cr, cw), prv(vb))],
        out_specs=[pl.BlockSpec((cr, cw), lambda c, n: (n, c))] * 2, out_shape=[_sds((s, gw), F32)] * 2,
    )(qn, kn, kn, kv, kv)


def _attn_combine(os_, lses, name):
    s, gw = os_[0].shape
    tr = _pick(s, ROW_TILE, 8)

    def body(o0, o1, o2, l0, l1, l2, o_ref, l_ref):
        a, b, c = l0[...], l1[...], l2[...]
        m = jnp.maximum(jnp.maximum(a, b), c)
        ea, eb, ec = jnp.exp(a - m), jnp.exp(b - m), jnp.exp(c - m)
        den = ea + eb + ec
        o_ref[...] = ((ea * o0[...] + eb * o1[...] + ec * o2[...]) / den).astype(BF16)
        l_ref[...] = m + jnp.log(den)

    return _call(body, name=name, grid=(s // tr,), in_specs=[_row_spec(tr, gw)] * 6, out_specs=[_row_spec(tr, gw)] * 2,
                 out_shape=[_sds((s, gw), BF16), _sds((s, gw), F32)])(*os_, *lses)


def _attn_delta(do, o, name):
    s, gw = do.shape
    tr = _pick(s, ROW_TILE, 8)

    def body(do_ref, o_ref, d_ref):
        for h in range(gw // HEAD_DIM):
            cols = pl.ds(h * HEAD_DIM, HEAD_DIM)
            t = jnp.sum(do_ref[:, cols] * o_ref[:, cols].astype(F32), axis=-1, keepdims=True)
            d_ref[:, cols] = jnp.broadcast_to(t, (tr, HEAD_DIM))

    return _call(body, name=name, grid=(s // tr,), in_specs=[_row_spec(tr, gw)] * 2, out_specs=_row_spec(tr, gw),
                 out_shape=_sds((s, gw), F32))(do, o)


def _pair_grads(qv, kv_, vv, dov, lse, delta, mask, scale):
    sc = jnp.where(mask, _dot_nt(qv, kv_) * scale, NEG)
    p = jnp.exp(sc - lse)
    ds = p * (_dot_nt(dov, vv) - delta) * scale
    return p, ds


def _attn_bwd_dq(qn, kn, kv, do, lse, delta, g, name):
    s, w = qn.shape
    r = DILATIONS[g]
    gw = w // N_GROUPS
    cr = min(ATTN_CHUNK, s)
    nb = cr // (BLK * r)
    cw = HEAD_DIM
    gc = gw // cw
    scale = 1.0 / math.sqrt(HEAD_DIM)

    def body(q_ref, kc_ref, kp_ref, vc_ref, vp_ref, do_ref, l_ref, d_ref, dq_ref):
        n = pl.program_id(1)
        same_m, prev_m = _band_masks()
        prev_first = jnp.logical_and(prev_m, n > 0)
        for j in range(r):
            for b in range(nb):
                rows = _rows(j, b, r)
                qv, dov = q_ref[rows, :].astype(BF16), do_ref[rows, :].astype(BF16)
                lse, delta = l_ref[rows, :], d_ref[rows, :]
                kc, vc = kc_ref[rows, :].astype(BF16), vc_ref[rows, :].astype(BF16)
                if b > 0:
                    rp = _rows(j, b - 1, r)
                    kp, vp, pm = kc_ref[rp, :].astype(BF16), vc_ref[rp, :].astype(BF16), prev_m
                else:
                    rp = _rows(j, nb - 1, r)
                    kp, vp, pm = kp_ref[rp, :].astype(BF16), vp_ref[rp, :].astype(BF16), prev_first
                _, dsd = _pair_grads(qv, kc, vc, dov, lse, delta, same_m, scale)
                _, dso = _pair_grads(qv, kp, vp, dov, lse, delta, pm, scale)
                dq_ref[rows, :] = _dot_nn(dsd.astype(BF16), kc) + _dot_nn(dso.astype(BF16), kp)

    cur = lambda base: (lambda c, n: (n, base + c))
    prv = lambda base: (lambda c, n: (jnp.maximum(n - 1, 0), base + c))
    qb, vb = g * gc, (N_GROUPS + g) * gc
    own = pl.BlockSpec((cr, cw), lambda c, n: (n, c))
    return _call(
        body, name=name, grid=(gc, s // cr),
        in_specs=[pl.BlockSpec((cr, cw), cur(qb)), pl.BlockSpec((cr, cw), cur(qb)), pl.BlockSpec((cr, cw), prv(qb)),
                  pl.BlockSpec((cr, cw), cur(vb)), pl.BlockSpec((cr, cw), prv(vb)), own, own, own],
        out_specs=own, out_shape=_sds((s, gw), F32),
    )(qn, kn, kn, kv, kv, do, lse, delta)


def _attn_bwd_dkv(qn, kn, kv, do, lse, delta, g, name):
    s, w = qn.shape
    r = DILATIONS[g]
    gw = w // N_GROUPS
    cr = min(ATTN_CHUNK, s)
    nb = cr // (BLK * r)
    nchunk = s // cr
    cw = HEAD_DIM
    gc = gw // cw
    scale = 1.0 / math.sqrt(HEAD_DIM)

    def body(k_ref, v_ref, qc_ref, qx_ref, doc_ref, dox_ref, lc_ref, lx_ref, dc_ref, dx_ref, dk_ref, dv_ref):
        n = pl.program_id(1)
        same_m, prev_m = _band_masks()
        next_last = jnp.logical_and(prev_m, n < nchunk - 1)
        for j in range(r):
            for b in range(nb):
                rows = _rows(j, b, r)
                kv_, vv = k_ref[rows, :].astype(BF16), v_ref[rows, :].astype(BF16)
                qv, dov = qc_ref[rows, :].astype(BF16), doc_ref[rows, :].astype(BF16)
                pd, dsd = _pair_grads(qv, kv_, vv, dov, lc_ref[rows, :], dc_ref[rows, :], same_m, scale)
                if b < nb - 1:
                    rx = _rows(j, b + 1, r)
                    qx, dox, lx, dlx, xm = qc_ref[rx, :], doc_ref[rx, :], lc_ref[rx, :], dc_ref[rx, :], prev_m
                else:
                    rx = _rows(j, 0, r)
                    qx, dox, lx, dlx, xm = qx_ref[rx, :], dox_ref[rx, :], lx_ref[rx, :], dx_ref[rx, :], next_last
                qx, dox = qx.astype(BF16), dox.astype(BF16)
                po, dso = _pair_grads(qx, kv_, vv, dox, lx, dlx, xm, scale)
                dk_ref[rows, :] = _dot_nn(dsd.T.astype(BF16), qv) + _dot_nn(dso.T.astype(BF16), qx)
                dv_ref[rows, :] = _dot_nn(pd.T.astype(BF16), dov) + _dot_nn(po.T.astype(BF16), dox)

    cur = lambda base: (lambda c, n: (n, base + c))
    nxt = lambda base: (lambda c, n: (jnp.minimum(n + 1, nchunk - 1), base + c))
    qb, vb = g * gc, (N_GROUPS + g) * gc
    blk = lambda f: pl.BlockSpec((cr, cw), f)
    return _call(
        body, name=name, grid=(gc, nchunk),
        in_specs=[blk(cur(qb)), blk(cur(vb)), blk(cur(qb)), blk(nxt(qb)), blk(cur(0)), blk(nxt(0)), blk(cur(0)), blk(nxt(0)),
                  blk(cur(0)), blk(nxt(0))],
        out_specs=[blk(cur(0))] * 2, out_shape=[_sds((s, gw), F32)] * 2,
    )(kn, kv, qn, qn, do, do, lse, lse, delta, delta)


def _mod_proj(sc_all, w, name):
    l, d, ns = w.shape
    tn = _pick(ns, 512)

    def body(c_ref, w_ref, o_ref):
        o_ref[...] = jnp.dot(c_ref[...].astype(BF16), w_ref[...].astype(BF16), preferred_element_type=F32)

    return _call(
        body, name=name, grid=(l, ns // tn),
        in_specs=[pl.BlockSpec((N_DEV, d), lambda a, j: (0, 0)), pl.BlockSpec((None, d, tn), lambda a, j: (a, 0, j))],
        out_specs=pl.BlockSpec((None, N_DEV, tn), lambda a, j: (a, 0, j)), out_shape=_sds((l, N_DEV, ns), F32),
    )(sc_all, w)


def _adamw_math(w, g, m, v):
    m = ADAM_B1 * m + (1.0 - ADAM_B1) * g
    v = ADAM_B2 * v + (1.0 - ADAM_B2) * (g * g)
    m_hat = m / (1.0 - ADAM_B1 ** ADAM_STEP)
    v_hat = v / (1.0 - ADAM_B2 ** ADAM_STEP)
    delta = -ADAM_LR * (m_hat / (jnp.sqrt(v_hat) + ADAM_EPS) + ADAM_WD * w)
    return delta, m, v


def _adamw_big(w, g, m, v, name):
    shape = w.shape
    cols = shape[-1]
    rows = math.prod(shape[:-1])
    tr, tc = _pick(rows, 512, 8), _pick(cols, 1024)
    w2, g2, m2, v2 = (t.reshape(rows, cols) for t in (w, g, m, v))

    def body(w_ref, g_ref, m_ref, v_ref, d_ref, mo_ref, vo_ref):
        d_ref[...], mo_ref[...], vo_ref[...] = _adamw_math(w_ref[...], g_ref[...], m_ref[...], v_ref[...])

    spec = pl.BlockSpec((tr, tc), lambda i, j: (i, j))
    outs = _call(body, name=name, grid=(rows // tr, cols // tc), in_specs=[spec] * 4, out_specs=[spec] * 3,
                 out_shape=[_sds((rows, cols), F32)] * 3)(w2, g2, m2, v2)
    return [t.reshape(shape) for t in outs]


def _adamw_mod(w, sct, dm, m, v, name):
    l, d, ns = w.shape
    tr, tc = _pick(d, 512, 8), _pick(ns, 1024)

    def body(w_ref, c_ref, dm_ref, m_ref, v_ref, g_ref, d_ref, mo_ref, vo_ref):
        cv, dv = c_ref[...].astype(BF16).astype(F32), dm_ref[...].astype(BF16).astype(F32)
        g = jnp.zeros((tr, tc), F32)
        for e in range(N_DEV):
            g = g + cv[:, e:e + 1] * dv[e:e + 1, :]
        g_ref[...] = g
        d_ref[...], mo_ref[...], vo_ref[...] = _adamw_math(w_ref[...], g, m_ref[...], v_ref[...])

    spec = pl.BlockSpec((None, tr, tc), lambda a, i, j: (a, i, j))
    return _call(
        body, name=name, grid=(l, d // tr, ns // tc),
        in_specs=[spec, pl.BlockSpec((tr, N_DEV), lambda a, i, j: (i, 0)), pl.BlockSpec((None, N_DEV, tc), lambda a, i, j: (a, 0, j)),
                  spec, spec],
        out_specs=[spec] * 4, out_shape=[_sds((l, d, ns), F32)] * 4,
    )(w, sct, dm, m, v)


def _adamw_small(ws, gs, ms, vs, name):
    n = len(ws)

    def body(*refs):
        w_r, g_r, m_r, v_r = refs[0:n], refs[n:2 * n], refs[2 * n:3 * n], refs[3 * n:4 * n]
        d_o, m_o, v_o = refs[4 * n:5 * n], refs[5 * n:6 * n], refs[6 * n:7 * n]
        for k in range(n):
            d_o[k][...], m_o[k][...], v_o[k][...] = _adamw_math(w_r[k][...], g_r[k][...], m_r[k][...], v_r[k][...])

    vm = pl.BlockSpec(memory_space=pltpu.VMEM)
    shapes = [_sds(w.shape, F32) for w in ws]
    outs = pl.pallas_call(body, name=name, in_specs=[vm] * (4 * n), out_specs=[vm] * (3 * n), out_shape=shapes * 3,
                          interpret=False)(*ws, *gs, *ms, *vs)
    return outs[0:n], outs[n:2 * n], outs[2 * n:3 * n]


def _sum_rows(a, name):
    n, v = a.shape
    tc = _pick(v, 8192)

    def body(a_ref, o_ref):
        acc = a_ref[0:1, :]
        for e in range(1, n):
            acc = acc + a_ref[e:e + 1, :]
        o_ref[...] = acc

    return _call(body, name=name, grid=(v // tc,), in_specs=[pl.BlockSpec((n, tc), lambda j: (0, j))],
                 out_specs=pl.BlockSpec((1, tc), lambda j: (0, j)), out_shape=_sds((1, v), F32))(a)


def _cast_into_full(w2d, kind, slot_arr, name):
    rows, cols = w2d.shape
    tr, tc = _pick(rows, 512, 16), _pick(cols, 1024)
    nr, nc = rows // tr, cols // tc

    def body(s_ref, w_ref, o_ref):
        o_ref[...] = w_ref[...].astype(BF16)

    if kind == "col":
        o_map = lambda i, j, s_ref: (i, s_ref[0] * nc + j)
    else:
        o_map = lambda i, j, s_ref: (s_ref[0] * nr + i, j)
    return _call(body, name=name, grid=(nr, nc), in_specs=[pl.BlockSpec((tr, tc), lambda i, j, s_ref: (i, j))],
                 out_specs=pl.BlockSpec((tr, tc), o_map), out_shape=_sds(_full_shape(kind, (rows, cols)), BF16), nsp=1)(slot_arr, w2d)


def _place():
    x, y, c = lax.axis_index("x"), lax.axis_index("y"), lax.axis_index("c")
    chips = [(1 - x, y), (x, 1 - y), (1 - x, 1 - y)]
    return x, y, c, chips


def _remote(src, dst, send_sem, recv_sem, to):
    return pltpu.make_async_remote_copy(src_ref=src, dst_ref=dst, send_sem=send_sem, recv_sem=recv_sem, device_id=to,
                                        device_id_type=MESH)


def _allgather8(a, name):
    m_per, n = a.shape

    def body(x_ref, out_ref, send_sems, recv_sems, local_sem):
        x, y, c, chips = _place()
        me, sibling = (x, y, c), (x, y, 1 - c)

        def rows(px, py, pc):
            return out_ref.at[pl.ds((4 * px + 2 * py + pc) * m_per, m_per), :]

        def copy(k, block, to, src=None):
            return _remote(rows(*block) if src is None else src, rows(*block), send_sems.at[k], recv_sems.at[k], to)

        mine = pltpu.make_async_copy(x_ref, rows(*me), local_sem)
        mine.start()
        first = [copy(0, me, sibling, src=x_ref)]
        first += [copy(1 + j, me, (*chip, c), src=x_ref) for j, chip in enumerate(chips)]
        for cp in first:
            cp.start()
        passed = [copy(4 + j, (*chip, c), sibling) for j, chip in enumerate(chips)]
        for j, chip in enumerate(chips):
            copy(1 + j, (*chip, c), me).wait_recv()
            passed[j].start()
        copy(0, sibling, me).wait_recv()
        for j, chip in enumerate(chips):
            copy(4 + j, (*chip, 1 - c), me).wait_recv()
        for cp in first + passed:
            cp.wait_send()
        mine.wait()

    return pl.pallas_call(
        body, name=name, out_shape=_sds((N_DEV * m_per, n), a.dtype),
        in_specs=[pl.BlockSpec(memory_space=pltpu.VMEM)], out_specs=pl.BlockSpec(memory_space=pltpu.VMEM),
        scratch_shapes=[pltpu.SemaphoreType.DMA((7,)), pltpu.SemaphoreType.DMA((7,)), pltpu.SemaphoreType.DMA],
        interpret=False,
    )(a)


def _region(ref, kind, shard_shape, slot, half):
    r, cs = shard_shape
    hr = r // 2
    if kind == "col":
        return ref.at[pl.ds(half * hr, hr), pl.ds(slot * cs, cs)]
    return ref.at[pl.ds(slot * r + half * hr, hr), :]


def _full_shape(kind, shard_shape):
    r, cs = shard_shape
    return (r, 4 * cs) if kind == "col" else (4 * r, cs)


_HBM = pl.BlockSpec(memory_space=pltpu.HBM)


def _in_place(arrays, n_sem, copies):
    return _Rider(arrays, [_sds(t.shape, t.dtype) for t in arrays], {a: a for a in range(len(arrays))}, n_sem, copies)


def _gather_chips(fulls, kinds, shapes):
    n = len(fulls)

    def copies(ins, outs, send_sems, recv_sems):
        x, y, c, chips = _place()
        me_slot = 2 * x + y
        sends, recvs = [], []
        for a in range(n):
            for j, (cx, cy) in enumerate(chips):
                k = 3 * a + j
                sends.append(_remote(_region(ins[a], kinds[a], shapes[a], me_slot, c), _region(outs[a], kinds[a], shapes[a], me_slot, c),
                                     send_sems.at[k], recv_sems.at[k], (cx, cy, c)))
                land = _region(outs[a], kinds[a], shapes[a], 2 * cx + cy, c)
                recvs.append(_remote(land, land, send_sems.at[k], recv_sems.at[k], (cx, cy, c)))
        return sends, recvs

    return _in_place(fulls, 3 * n, copies)


def _gather_pair(fulls, kinds, shapes):
    n = len(fulls)

    def copies(ins, outs, send_sems, recv_sems):
        x, y, c, chips = _place()
        sibling = (x, y, 1 - c)
        sends, recvs = [], []
        for a in range(n):
            for j, (cx, cy) in enumerate(chips):
                k = 3 * a + j
                sends.append(_remote(_region(ins[a], kinds[a], shapes[a], 2 * cx + cy, c), _region(outs[a], kinds[a], shapes[a], 2 * cx + cy, c),
                                     send_sems.at[k], recv_sems.at[k], sibling))
                land = _region(outs[a], kinds[a], shapes[a], 2 * cx + cy, 1 - c)
                recvs.append(_remote(land, land, send_sems.at[k], recv_sems.at[k], sibling))
        return sends, recvs

    return _in_place(fulls, 3 * n, copies)


def _rs_pair(grads, kinds, shapes):
    n = len(grads)

    def copies(ins, outs, send_sems, recv_sems):
        x, y, c, _ = _place()
        sibling = (x, y, 1 - c)
        sends, recvs = [], []
        for a in range(n):
            for slot in range(4):
                k = 4 * a + slot
                sends.append(_remote(_region(ins[a], kinds[a], shapes[a], slot, 1 - c), outs[a].at[slot], send_sems.at[k],
                                     recv_sems.at[k], sibling))
                recvs.append(_remote(_region(ins[a], kinds[a], shapes[a], slot, c), outs[a].at[slot], send_sems.at[k],
                                     recv_sems.at[k], sibling))
        return sends, recvs

    return _Rider(grads, [_sds((4, s[0] // 2, s[1]), F32) for s in shapes], {}, 4 * n, copies)


def _rs_pair_add(grad, recv, kind, shape, c_arr, name):
    r, cs = shape
    hr = r // 2
    tr, tc = _pick(hr, 512, 16), _pick(cs, 1024)
    nr, nc = hr // tr, cs // tc

    def body(c_ref, g_ref, r_ref, o_ref):
        o_ref[...] = (g_ref[...] + r_ref[...]).astype(BF16)

    if kind == "col":
        g_map = lambda s_, i, j, c_ref: (c_ref[0] * nr + i, s_ * nc + j)
    else:
        g_map = lambda s_, i, j, c_ref: (s_ * 2 * nr + c_ref[0] * nr + i, j)
    own = pl.BlockSpec((None, tr, tc), lambda s_, i, j, c_ref: (s_, i, j))
    return _call(body, name=name, grid=(4, nr, nc), in_specs=[pl.BlockSpec((tr, tc), g_map), own], out_specs=own,
                 out_shape=_sds((4, hr, cs), BF16), nsp=1)(c_arr, grad, recv)


def _rs_chips(parts):
    n = len(parts)

    def copies(ins, outs, send_sems, recv_sems):
        x, y, c, chips = _place()
        sends, recvs = [], []
        for a in range(n):
            for j, (cx, cy) in enumerate(chips):
                k = 3 * a + j
                cp = _remote(ins[a].at[2 * cx + cy], outs[a].at[j], send_sems.at[k], recv_sems.at[k], (cx, cy, c))
                sends.append(cp)
                recvs.append(cp)
        return sends, recvs

    return _Rider(parts, [_sds((3,) + p.shape[1:], BF16) for p in parts], {}, 3 * n, copies)


def _rs_chip_add(part, recv, slot_arr, c_arr, dest, layer, out_shape, name):
    _, hr, cs = part.shape
    tr, tc = _pick(hr, 512, 16), _pick(cs, 1024)
    nr = hr // tr

    def body(s_ref, c_ref, p_ref, r0_ref, r1_ref, r2_ref, *rest):
        o_ref = rest[-1]
        o_ref[...] = ((p_ref[...].astype(F32) + r0_ref[...].astype(F32)) + r1_ref[...].astype(F32)) + r2_ref[...].astype(F32)

    rk = lambda k: pl.BlockSpec((None, tr, tc), lambda i, j, s_ref, c_ref: (k, i, j))
    if layer is None:
        o_spec = pl.BlockSpec((tr, tc), lambda i, j, s_ref, c_ref: (c_ref[0] * nr + i, j))
    else:
        o_spec = pl.BlockSpec((None, tr, tc), lambda i, j, s_ref, c_ref: (layer, c_ref[0] * nr + i, j))
    in_specs = [pl.BlockSpec((None, tr, tc), lambda i, j, s_ref, c_ref: (s_ref[0], i, j)), rk(0), rk(1), rk(2)]
    args = [slot_arr, c_arr, part, recv, recv, recv]
    aliases = {}
    if dest is not None:
        in_specs.append(pl.BlockSpec(memory_space=pl.ANY))
        args.append(dest)
        aliases = {6: 0}
    return pl.pallas_call(
        body, name=name,
        grid_spec=pltpu.PrefetchScalarGridSpec(num_scalar_prefetch=2, grid=(nr, cs // tc), in_specs=in_specs, out_specs=o_spec),
        out_shape=_sds(out_shape, F32), input_output_aliases=aliases,
        compiler_params=pltpu.CompilerParams(dimension_semantics=("arbitrary",) * 2, vmem_limit_bytes=VMEM_LIMIT_MB << 20),
        interpret=False,
    )(*args)


def _rs_pair_share(shards, name):
    n = len(shards)
    views = []
    for a, t in enumerate(shards):
        views += [(a, None)] if t.ndim == 2 else [(a, l) for l in range(t.shape[0])]
    nv = len(views)

    def body(*refs):
        ins, outs = refs[0:n], refs[n:2 * n]
        send_sems, recv_sems = refs[2 * n:]
        x, y, c, _ = _place()
        sibling = (x, y, 1 - c)

        def rows(ref_list, k, half):
            a, layer = views[k]
            ref = ref_list[a] if layer is None else ref_list[a].at[layer]
            hr = ref.shape[0] // 2
            return ref.at[pl.ds(half * hr, hr), :]

        sent = []
        for k in range(nv):
            cp = _remote(rows(ins, k, c), rows(outs, k, c), send_sems.at[k], recv_sems.at[k], sibling)
            cp.start()
            sent.append(cp)
        for k in range(nv):
            _remote(rows(ins, k, 1 - c), rows(outs, k, 1 - c), send_sems.at[k], recv_sems.at[k], sibling).wait_recv()
        for cp in sent:
            cp.wait_send()

    return pl.pallas_call(
        body, name=name, out_shape=[_sds(t.shape, F32) for t in shards], in_specs=[_HBM] * n, out_specs=[_HBM] * n,
        input_output_aliases={a: a for a in range(n)},
        scratch_shapes=[pltpu.SemaphoreType.DMA((nv,)), pltpu.SemaphoreType.DMA((nv,))], interpret=False,
    )(*shards)


def _pad_to(v, mult):
    n = v.shape[0]
    return jnp.pad(v, (0, (-n) % mult))


def kernel(x, c, positions, mod_w, mod_b, norm_mix_g, norm_ffn_g, conv_pw1_w, conv_pw1_b, conv_dw_w, conv_dw_b, conv_ln_g, conv_ln_b, conv_pw2_w, conv_pw2_b, kv_mod_w, kv_mod_b, kv_norm_g, w_kv, k_norm_g, w_q, q_norm_g, w_o, ffn_up_w, ffn_dw_w, ffn_dw_b, ffn_down_w, loss_target, m_mod_w, m_mod_b, m_norm_mix_g, m_norm_ffn_g, m_conv_pw1_w, m_conv_pw1_b, m_conv_dw_w, m_conv_dw_b, m_conv_ln_g, m_conv_ln_b, m_conv_pw2_w, m_conv_pw2_b, m_kv_mod_w, m_kv_mod_b, m_kv_norm_g, m_w_kv, m_k_norm_g, m_w_q, m_q_norm_g, m_w_o, m_ffn_up_w, m_ffn_dw_w, m_ffn_dw_b, m_ffn_down_w, v_mod_w, v_mod_b, v_norm_mix_g, v_norm_ffn_g, v_conv_pw1_w, v_conv_pw1_b, v_conv_dw_w, v_conv_dw_b, v_conv_ln_g, v_conv_ln_b, v_conv_pw2_w, v_conv_pw2_b, v_kv_mod_w, v_kv_mod_b, v_kv_norm_g, v_w_kv, v_k_norm_g, v_w_q, v_q_norm_g, v_w_o, v_ffn_up_w, v_ffn_dw_w, v_ffn_dw_b, v_ffn_down_w):
    _, s, d = x.shape
    f = ffn_dw_b.shape[1]
    qw = w_q.shape[2] * 4
    ax, ay, ac = lax.axis_index("x"), lax.axis_index("y"), lax.axis_index("c")
    slot = 2 * ax + ay
    me8 = 4 * ax + 2 * ay + ac
    slot_arr = jnp.reshape(slot, (1,)).astype(jnp.int32)
    c_arr = jnp.reshape(ac, (1,)).astype(jnp.int32)
    x2 = x.reshape(s, d)
    target = loss_target.reshape(s, d)
    row = lambda v: v.reshape(1, -1)

    c_all = _allgather8(c.reshape(8, d // 8), "gather_c").reshape(N_DEV, d)
    sc_all = jax.nn.silu(c_all)
    mod_part = _mod_proj(sc_all, mod_w, "mod_proj")
    kvm_part = _mod_proj(sc_all, kv_mod_w[None], "kvmod_proj")
    nm, nk = mod_part.shape[2], kvm_part.shape[2]
    small_sharded = [conv_pw1_b, conv_dw_w, conv_dw_b, conv_ln_g, conv_ln_b, conv_pw2_b, ffn_dw_w]
    pack = jnp.concatenate([mod_part.reshape(-1), kvm_part.reshape(-1)] + [t.reshape(-1) for t in small_sharded])
    plen = pack.shape[0]
    pack = _pad_to(pack, 1024)
    gathered = _allgather8(pack.reshape(8, -1), "gather_mod").reshape(4, 2, -1)[:, 0, :plen]
    off = 0

    def take(n_el):
        nonlocal off
        out = lax.slice_in_dim(gathered, off, off + n_el, axis=1)
        off += n_el
        return out

    mod_g = take(2 * N_DEV * nm).reshape(4, 2, N_DEV, nm)
    kvm_g = take(N_DEV * nk).reshape(4, N_DEV, nk)
    mine = lambda t, axis: lax.dynamic_index_in_dim(t, me8, axis=axis, keepdims=False)
    mod_vec = jnp.transpose(mine(mod_g, 2), (1, 0, 2)).reshape(2, 4 * nm) + mod_b
    kvm_vec = mine(kvm_g, 1).reshape(4 * nk) + kv_mod_b
    pw1_b_full = take(conv_pw1_b.shape[1]).reshape(1, -1)
    dw_w_full = jnp.transpose(take(CONV_K * (d // 4)).reshape(4, CONV_K, d // 4), (1, 0, 2)).reshape(CONV_K, d)
    dw_b_full, ln_g_full, ln_b_full, pw2_b_full = (take(d // 4).reshape(1, d) for _ in range(4))
    fdw_full = jnp.transpose(take(2 * FFN_K * (f // 4)).reshape(4, 2, FFN_K, f // 4), (1, 2, 0, 3)).reshape(2, FFN_K, f)
    mods = [[row(mod_vec[l, k * d:(k + 1) * d]) for k in range(6)] for l in range(2)]
    kv_sh, kv_sc = row(kvm_vec[:d]), row(kvm_vec[d:])
    zero_d = jnp.zeros((1, d), F32)

    big = [("pw1", conv_pw1_w[0], "col"), ("pw2", conv_pw2_w[0], "row"), ("wkv", w_kv, "col"), ("wq", w_q[0], "col"),
           ("wo", w_o[0], "col"), ("up0", ffn_up_w[0], "col"), ("up1", ffn_up_w[1], "col"), ("dn0", ffn_down_w[0], "row"),
           ("dn1", ffn_down_w[1], "row")]
    names = [b[0] for b in big]
    kinds = [b[2] for b in big]
    shard_shapes = [b[1].shape for b in big]
    own = {b[0]: _cast_into_full(b[1], b[2], slot_arr, "cast_" + b[0]) for b in big}
    meta = dict(zip(names, zip(kinds, shard_shapes)))
    full = {}

    def stage(make, group, src):
        return make([src[k] for k in group], [meta[k][0] for k in group], [meta[k][1] for k in group])

    grp_a, grp_b, grp_c, grp_d = ["pw1", "pw2"], ["up0", "dn0"], ["wkv", "wq", "wo"], ["up1", "dn1"]
    landed_a = dict(zip(grp_a, _run_rider(stage(_gather_chips, grp_a, own), "gather_a_chips")))
    full.update(zip(grp_a, _run_rider(stage(_gather_pair, grp_a, landed_a), "gather_a_pair")))
    chips_b, chips_c, chips_d = (stage(_gather_chips, g_, own) for g_ in (grp_b, grp_c, grp_d))

    def pair_stage(group, chips_rider):
        return stage(_gather_pair, group, dict(zip(group, chips_rider.results)))

    def ffn_fwd(l, x_in, h, tag, rider_up=None, rider_down=None):
        u = _matmul(h, full["up%d" % l], "nn", F32, "mm_up" + tag, rider=rider_up)
        z = _ffn_act_fwd(u, fdw_full[l], row(ffn_dw_b[l]), "ffn_act" + tag)
        y = _matmul(z, full["dn%d" % l], "nn", F32, "mm_down" + tag, rider=rider_down)
        return u, z, y

    sh_m0, sc_m0, g_m0, sh_f0, sc_f0, g_f0 = mods[0]
    sh_m1, sc_m1, g_m1, sh_f1, sc_f1, g_f1 = mods[1]
    gmix0, gmix1, gffn0, gffn1 = row(norm_mix_g[0]), row(norm_mix_g[1]), row(norm_ffn_g[0]), row(norm_ffn_g[1])
    (h0,) = _resid_mod(x2, None, [(gmix0, sc_m0, sh_m0)], None, "mod_in")
    u0 = _matmul(h0, full["pw1"], "nn", F32, "mm_pw1")
    glu = _glu_fwd(u0, pw1_b_full, "glu")
    dwc = _dwconv_fwd(glu, dw_w_full, dw_b_full, "dwconv", rider=chips_b)
    act = _ln_silu_fwd(dwc, ln_g_full, ln_b_full, "ln_silu")
    pair_b = pair_stage(grp_b, chips_b)
    y0 = _matmul(act, full["pw2"], "nn", F32, "mm_pw2", rider=pair_b)
    full.update(zip(grp_b, pair_b.results))
    x1, hf0 = _resid_mod(x2, (y0, pw2_b_full, g_m0), [(gffn0, sc_f0, sh_f0)], None, "resid_conv")
    uf0 = _matmul(hf0, full["up0"], "nn", F32, "mm_up0", rider=chips_c)
    zf0 = _ffn_act_fwd(uf0, fdw_full[0], row(ffn_dw_b[0]), "ffn_act0")
    pair_c = pair_stage(grp_c, chips_c)
    yf0 = _matmul(zf0, full["dn0"], "nn", F32, "mm_down0", rider=pair_c)
    full.update(zip(grp_c, pair_c.results))
    gkv = row(kv_norm_g)
    xa, hk, hq = _resid_mod(x1, (yf0, zero_d, g_f0), [(gkv, kv_sc, kv_sh), (gmix1, sc_m1, sh_m1)], None, "resid_ffn0")
    kvp = _matmul(hk, full["wkv"], "nn", F32, "mm_kv")
    qp = _matmul(hq, full["wq"], "nn", F32, "mm_q")
    inv_freq = ROPE_THETA ** (-jnp.arange(0, ROT_DIM, 2, dtype=F32) / ROT_DIM)
    half = ROT_DIM // 2
    freq_l = jnp.concatenate([inv_freq, inv_freq, jnp.zeros((HEAD_DIM - ROT_DIM,), F32)]).reshape(1, HEAD_DIM)
    sign_l = jnp.concatenate([-jnp.ones((half,), F32), jnp.ones((half,), F32), jnp.zeros((HEAD_DIM - ROT_DIM,), F32)]).reshape(1, HEAD_DIM)
    ctab, stab = _rope_tables(positions.reshape(s, 1), freq_l, sign_l, "rope_tables")
    gq, gk = row(q_norm_g[0]), row(k_norm_g)
    qn, kn = _qk_norm_rope_fwd(qp, kvp, gq, gk, ctab, stab, "qk_norm_rope", rider=chips_d)
    og, lg = zip(*[_attn_fwd(qn, kn, kvp, g, "attn_fwd%d" % g) for g in range(N_GROUPS)])
    o_mix, lse = _attn_combine(og, lg, "attn_combine")
    pair_d = pair_stage(grp_d, chips_d)
    ya = _matmul(o_mix, full["wo"], "nn", F32, "mm_o", rider=pair_d)
    full.update(zip(grp_d, pair_d.results))
    xb, hf1 = _resid_mod(xa, (ya, zero_d, g_m1), [(gffn1, sc_f1, sh_f1)], None, "resid_attn")
    uf1, zf1, yf1 = ffn_fwd(1, xb, hf1, "1")
    _, dxo, loss_cols = _resid_mod(xb, (yf1, zero_d, g_f1), [], target, "resid_loss")
    loss = lax.psum(jnp.sum(loss_cols), ("x", "y", "c"))

    gbig = {}

    parts, recv2 = {}, {}

    def rs_pair(group):
        return _rs_pair([gbig[k] for k in group], [meta[k][0] for k in group], [meta[k][1] for k in group])

    def rs_chips(group, pair_results):
        for k, r_ in zip(group, pair_results):
            parts[k] = _rs_pair_add(gbig[k], r_, meta[k][0], meta[k][1], c_arr, "rs_pair_add_" + k)
        return _rs_chips([parts[k] for k in group])

    rs_a, rs_b, rs_c, rs_d = ["dn1", "up1"], ["wo", "wq", "wkv"], ["dn0", "up0"], ["pw2", "pw1"]

    def ffn_bwd(l, dy, u, z, h, tag, rider=None):
        dz = _matmul(dy, full["dn%d" % l], "nt", F32, "mm_down_dx" + tag)
        gbig["dn%d" % l] = _matmul(z, dy, "tn", F32, "mm_down_dw" + tag)
        du, dfw, dfb = _ffn_act_bwd(dz, u, fdw_full[l], row(ffn_dw_b[l]), "ffn_act_bwd" + tag, rider=rider)
        dh = _matmul(du, full["up%d" % l], "nt", F32, "mm_up_dx" + tag)
        gbig["up%d" % l] = _matmul(h, du, "tn", F32, "mm_up_dw" + tag)
        return dh, dfw, dfb

    dxb0, dyf1, dg_f1, _ = _bwd_step(dxo, xb, [], (yf1, zero_d, g_f1), "bwd_loss")
    dhf1, dfw1, dfb1 = ffn_bwd(1, dyf1, uf1, zf1, hf1, "1")
    pair_a = rs_pair(rs_a)
    dxb, dsh_f1, dsc_f1, dgffn1, dya, dg_m1, _ = _bwd_step(dxb0, xb, [(dhf1, gffn1, sc_f1)], (ya, zero_d, g_m1), "bwd_attn_out",
                                                        rider=pair_a)
    do = _matmul(dya, full["wo"], "nt", F32, "mm_o_dx")
    gbig["wo"] = _matmul(o_mix, dya, "tn", F32, "mm_o_dw")
    delta = _attn_delta(do, o_mix, "attn_delta")
    dqs = [_attn_bwd_dq(qn, kn, kvp, do, lse, delta, g, "attn_dq%d" % g) for g in range(N_GROUPS)]
    dks, dvs = zip(*[_attn_bwd_dkv(qn, kn, kvp, do, lse, delta, g, "attn_dkv%d" % g) for g in range(N_GROUPS)])
    ride_a = rs_chips(rs_a, pair_a.results)
    dqp, dkvp, dgq, dgk = _qk_norm_rope_bwd(dqs, dks, dvs, qp, kvp, gq, gk, ctab, stab, "qk_norm_rope_bwd", rider=ride_a)
    recv2.update(zip(rs_a, ride_a.results))
    dhq = _matmul(dqp, full["wq"], "nt", F32, "mm_q_dx")
    gbig["wq"] = _matmul(hq, dqp, "tn", F32, "mm_q_dw")
    dhk = _matmul(dkvp, full["wkv"], "nt", F32, "mm_kv_dx")
    gbig["wkv"] = _matmul(hk, dkvp, "tn", F32, "mm_kv_dw")
    pair_b = rs_pair(rs_b)
    (dxa, dsh_kv, dsc_kv, dgkv, dsh_m1, dsc_m1, dgmix1, dyf0, dg_f0, _) = _bwd_step(
        dxb, xa, [(dhk, gkv, kv_sc), (dhq, gmix1, sc_m1)], (yf0, zero_d, g_f0), "bwd_kvq", rider=pair_b)
    ride_b = rs_chips(rs_b, pair_b.results)
    dhf0, dfw0, dfb0 = ffn_bwd(0, dyf0, uf0, zf0, hf0, "0", rider=ride_b)
    recv2.update(zip(rs_b, ride_b.results))
    pair_c = rs_pair(rs_c)
    dx1, dsh_f0, dsc_f0, dgffn0, dy0, dg_m0, dpw2_b = _bwd_step(dxa, x1, [(dhf0, gffn0, sc_f0)], (y0, pw2_b_full, g_m0), "bwd_conv_out",
                                                             rider=pair_c)
    dact = _matmul(dy0, full["pw2"], "nt", F32, "mm_pw2_dx")
    gbig["pw2"] = _matmul(act, dy0, "tn", F32, "mm_pw2_dw")
    ddwc, dln_g, dln_b = _ln_silu_bwd(dact, dwc, ln_g_full, ln_b_full, "ln_silu_bwd")
    ride_c = rs_chips(rs_c, pair_c.results)
    dglu, ddw_w, ddw_b = _dwconv_bwd(ddwc, glu, dw_w_full, "dwconv_bwd", rider=ride_c)
    recv2.update(zip(rs_c, ride_c.results))
    du0, dpw1_b = _glu_bwd(dglu, u0, pw1_b_full, "glu_bwd")
    dh0 = _matmul(du0, full["pw1"], "nt", F32, "mm_pw1_dx")
    gbig["pw1"] = _matmul(h0, du0, "tn", F32, "mm_pw1_dw")
    grad_x, dsh_m0, dsc_m0, dgmix0 = _bwd_step(dx1, x2, [(dh0, gmix0, sc_m0)], None, "bwd_in")

    dmod = [jnp.concatenate([dsh_m0, dsc_m0, dg_m0, dsh_f0, dsc_f0, dg_f0], axis=1),
            jnp.concatenate([dsh_m1, dsc_m1, dg_m1, dsh_f1, dsc_f1, dg_f1], axis=1)]
    dkvm = jnp.concatenate([dsh_kv, dsc_kv], axis=1)
    per_ex = [dmod[0], dmod[1], dkvm]
    summed = [dgmix0, dgmix1, dgffn0, dgffn1, dpw1_b, ddw_w, ddw_b, dln_g, dln_b, dpw2_b, dgkv, dgk, dgq, dfw0, dfw1, dfb0, dfb1]
    vec = jnp.concatenate([t.reshape(-1) for t in per_ex + summed])
    vlen = vec.shape[0]
    vec = _pad_to(vec, 1024)
    vall = _allgather8(vec.reshape(8, -1), "gather_small").reshape(N_DEV, -1)
    vsum = _sum_rows(vall, "sum_small")[0]
    n_pe = sum(t.size for t in per_ex)
    dm_all = vall[:, :n_pe]
    off2 = n_pe
    sums = []
    for t in summed:
        sums.append(vsum[off2:off2 + t.size].reshape(t.shape))
        off2 += t.size
    (s_gmix0, s_gmix1, s_gffn0, s_gffn1, s_pw1_b, s_dw_w, s_dw_b, s_ln_g, s_ln_b, s_pw2_b, s_gkv, s_gk, s_gq, s_fw0, s_fw1,
     s_fb0, s_fb1) = sums
    shard_cols = lambda t, width: lax.dynamic_slice_in_dim(t, slot * width, width, axis=t.ndim - 1)
    dm_mod = jnp.stack([shard_cols(dm_all[:, l * 6 * d:(l + 1) * 6 * d], nm) for l in range(2)])
    dm_kv = shard_cols(dm_all[:, 12 * d:14 * d], nk)[None]
    sct = jnp.transpose(sc_all)

    recv2.update(zip(rs_d, _run_rider(rs_chips(rs_d, _run_rider(rs_pair(rs_d), "rs_pair_d")), "rs_chips_d")))
    reduced = {}
    for nme, shp in zip(names, shard_shapes):
        p_, r_ = parts[nme], recv2[nme]
        if nme in ("up0", "up1", "dn0", "dn1"):
            key, layer = nme[:-1], int(nme[-1])
            reduced[key] = _rs_chip_add(p_, r_, slot_arr, c_arr, reduced.get(key), layer, (2,) + shp, "rs_chip_add_" + nme)
        else:
            reduced[nme] = _rs_chip_add(p_, r_, slot_arr, c_arr, None, None, shp, "rs_chip_add_" + nme)
    g_pw1, g_pw2, g_wkv, g_wq, g_wo, g_up, g_dn = _rs_pair_share(
        [reduced[k] for k in ("pw1", "pw2", "wkv", "wq", "wo", "up", "dn")], "rs_pair_share")

    grads, deltas, new_m, new_v = {}, {}, {}, {}

    def put(nme, g_, res):
        grads[nme] = g_
        deltas[nme], new_m[nme], new_v[nme] = res

    for nme, g_, w_, m_, v_ in (("conv_pw1_w", g_pw1[None], conv_pw1_w, m_conv_pw1_w, v_conv_pw1_w),
                                ("conv_pw2_w", g_pw2[None], conv_pw2_w, m_conv_pw2_w, v_conv_pw2_w),
                                ("w_kv", g_wkv, w_kv, m_w_kv, v_w_kv), ("w_q", g_wq[None], w_q, m_w_q, v_w_q),
                                ("w_o", g_wo[None], w_o, m_w_o, v_w_o), ("ffn_up_w", g_up, ffn_up_w, m_ffn_up_w, v_ffn_up_w),
                                ("ffn_down_w", g_dn, ffn_down_w, m_ffn_down_w, v_ffn_down_w)):
        put(nme, g_, _adamw_big(w_, g_, m_, v_, "adamw_" + nme))
    g_, *res = _adamw_mod(mod_w, sct, dm_mod, m_mod_w, v_mod_w, "adamw_mod_w")
    put("mod_w", g_, res)
    g_, *res = _adamw_mod(kv_mod_w[None], sct, dm_kv, m_kv_mod_w[None], v_kv_mod_w[None], "adamw_kv_mod_w")
    put("kv_mod_w", g_[0], [t[0] for t in res])

    dm_sum = vsum[:n_pe]
    small = [
        ("mod_b", dm_sum[:12 * d].reshape(2, 6 * d), mod_b, m_mod_b, v_mod_b),
        ("norm_mix_g", jnp.concatenate([s_gmix0, s_gmix1], axis=0), norm_mix_g, m_norm_mix_g, v_norm_mix_g),
        ("norm_ffn_g", jnp.concatenate([s_gffn0, s_gffn1], axis=0), norm_ffn_g, m_norm_ffn_g, v_norm_ffn_g),
        ("conv_pw1_b", shard_cols(s_pw1_b, conv_pw1_b.shape[1]), conv_pw1_b, m_conv_pw1_b, v_conv_pw1_b),
        ("conv_dw_w", shard_cols(s_dw_w, d // 4)[None], conv_dw_w, m_conv_dw_w, v_conv_dw_w),
        ("conv_dw_b", shard_cols(s_dw_b, d // 4), conv_dw_b, m_conv_dw_b, v_conv_dw_b),
        ("conv_ln_g", shard_cols(s_ln_g, d // 4), conv_ln_g, m_conv_ln_g, v_conv_ln_g),
        ("conv_ln_b", shard_cols(s_ln_b, d // 4), conv_ln_b, m_conv_ln_b, v_conv_ln_b),
        ("conv_pw2_b", shard_cols(s_pw2_b, d // 4), conv_pw2_b, m_conv_pw2_b, v_conv_pw2_b),
        ("kv_mod_b", dm_sum[12 * d:14 * d], kv_mod_b, m_kv_mod_b, v_kv_mod_b),
        ("kv_norm_g", s_gkv.reshape(-1), kv_norm_g, m_kv_norm_g, v_kv_norm_g),
        ("k_norm_g", s_gk.reshape(-1), k_norm_g, m_k_norm_g, v_k_norm_g),
        ("q_norm_g", s_gq, q_norm_g, m_q_norm_g, v_q_norm_g),
        ("ffn_dw_w", shard_cols(jnp.stack([s_fw0, s_fw1]), f // 4), ffn_dw_w, m_ffn_dw_w, v_ffn_dw_w),
        ("ffn_dw_b", jnp.concatenate([s_fb0, s_fb1], axis=0), ffn_dw_b, m_ffn_dw_b, v_ffn_dw_b),
    ]
    as2d = lambda t: t.reshape(-1, t.shape[-1])
    sd_, sm_, sv_ = _adamw_small([as2d(t[2]) for t in small], [as2d(t[1]) for t in small], [as2d(t[3]) for t in small],
                                 [as2d(t[4]) for t in small], "adamw_small")
    for (nme, g_, w_, _, _), d_, mo_, vo_ in zip(small, sd_, sm_, sv_):
        put(nme, g_.reshape(w_.shape), [d_.reshape(w_.shape), mo_.reshape(w_.shape), vo_.reshape(w_.shape)])

    order = ["mod_w", "mod_b", "norm_mix_g", "norm_ffn_g", "conv_pw1_w", "conv_pw1_b", "conv_dw_w", "conv_dw_b", "conv_ln_g",
             "conv_ln_b", "conv_pw2_w", "conv_pw2_b", "kv_mod_w", "kv_mod_b", "kv_norm_g", "w_kv", "k_norm_g", "w_q", "q_norm_g",
             "w_o", "ffn_up_w", "ffn_dw_w", "ffn_dw_b", "ffn_down_w"]
    return (loss, grad_x.reshape(x.shape), *[grads[k] for k in order], *[deltas[k] for k in order], *[new_m[k] for k in order],
            *[new_v[k] for k in order])
```

```python
import functools
import math

import jax
import jax.numpy as jnp
from jax import lax
from jax.experimental import pallas as pl
from jax.experimental.pallas import tpu as pltpu

F32 = jnp.float32
BF16 = jnp.bfloat16
EPS = 1e-6
NEG = -1e30
HEAD_DIM = 128
ROT_DIM = 32
ROPE_THETA = 500000.0
BLK = 128
DILATIONS = (1, 4, 16)
N_GROUPS = 3
CONV_K = 31
FFN_K = 3
ADAM_LR, ADAM_B1, ADAM_B2, ADAM_EPS, ADAM_WD, ADAM_STEP = 0.001, 0.9, 0.999, 1e-08, 0.01, 10
N_DEV = 8
MESH = pl.DeviceIdType.MESH
VMEM_LIMIT_MB = 56
ROW_TILE = 256
ATTN_CHUNK = 2048


def _pick(n, pref, mult=128):
    best = None
    d = mult
    while d <= min(n, pref):
        if n % d == 0:
            best = d
        d += mult
    return best if best is not None else n


class _Rider:
    def __init__(self, ins, out_shapes, aliases, n_sem, copies):
        self.ins, self.out_shapes, self.aliases, self.n_sem, self.copies = list(ins), list(out_shapes), dict(aliases), n_sem, copies
        self.results = None


def _call(body, *, name, grid, in_specs, out_specs, out_shape, scratch=(), nsp=0, rider=None):
    params = pltpu.CompilerParams(dimension_semantics=("arbitrary",) * len(grid), vmem_limit_bytes=VMEM_LIMIT_MB << 20)
    if rider is None:
        return pl.pallas_call(
            body, name=name,
            grid_spec=pltpu.PrefetchScalarGridSpec(num_scalar_prefetch=nsp, grid=grid, in_specs=in_specs, out_specs=out_specs,
                                                   scratch_shapes=list(scratch)),
            out_shape=out_shape, compiler_params=params, interpret=False,
        )
    single = not isinstance(out_shape, (list, tuple))
    out_shapes = [out_shape] if single else list(out_shape)
    out_specs_l = [out_specs] if single else list(out_specs)
    n_in, n_out, n_scr = len(in_specs), len(out_shapes), len(scratch)
    r_in, r_out = len(rider.ins), len(rider.out_shapes)
    hbm = pl.BlockSpec(memory_space=pltpu.HBM)
    last = tuple(g - 1 for g in grid)

    def wrapped(*refs):
        pre, ins, rin = refs[:nsp], refs[nsp:nsp + n_in], refs[nsp + n_in:nsp + n_in + r_in]
        o0 = nsp + n_in + r_in
        outs, rout = refs[o0:o0 + n_out], refs[o0 + n_out:o0 + n_out + r_out]
        s0 = o0 + n_out + r_out
        scr, (send_sems, recv_sems) = refs[s0:s0 + n_scr], refs[s0 + n_scr:]
        ids = [pl.program_id(a) for a in range(len(grid))]
        is_first = functools.reduce(jnp.logical_and, [i == 0 for i in ids])
        is_last = functools.reduce(jnp.logical_and, [i == l for i, l in zip(ids, last)])

        @pl.when(is_first)
        def _():
            for cp in rider.copies(rin, rout, send_sems, recv_sems)[0]:
                cp.start()

        body(*pre, *ins, *outs, *scr)

        @pl.when(is_last)
        def _():
            sends, recvs = rider.copies(rin, rout, send_sems, recv_sems)
            for cp in recvs:
                cp.wait_recv()
            for cp in sends:
                cp.wait_send()

    call = pl.pallas_call(
        wrapped, name=name,
        grid_spec=pltpu.PrefetchScalarGridSpec(
            num_scalar_prefetch=nsp, grid=grid, in_specs=list(in_specs) + [hbm] * r_in, out_specs=out_specs_l + [hbm] * r_out,
            scratch_shapes=list(scratch) + [pltpu.SemaphoreType.DMA((rider.n_sem,)), pltpu.SemaphoreType.DMA((rider.n_sem,))]),
        out_shape=out_shapes + rider.out_shapes,
        input_output_aliases={nsp + n_in + i: n_out + o for i, o in rider.aliases.items()},
        compiler_params=params, interpret=False,
    )

    def run(*args):
        res = call(*args, *rider.ins)
        rider.results = list(res[n_out:])
        return res[0] if single else list(res[:n_out])

    return run


def _run_rider(rider, name):
    r_in, r_out = len(rider.ins), len(rider.out_shapes)
    hbm = pl.BlockSpec(memory_space=pltpu.HBM)

    def body(*refs):
        sends, recvs = rider.copies(refs[:r_in], refs[r_in:r_in + r_out], *refs[r_in + r_out:])
        for cp in sends:
            cp.start()
        for cp in recvs:
            cp.wait_recv()
        for cp in sends:
            cp.wait_send()

    return pl.pallas_call(
        body, name=name, out_shape=rider.out_shapes, in_specs=[hbm] * r_in, out_specs=[hbm] * r_out,
        input_output_aliases=rider.aliases,
        scratch_shapes=[pltpu.SemaphoreType.DMA((rider.n_sem,)), pltpu.SemaphoreType.DMA((rider.n_sem,))], interpret=False,
    )(*rider.ins)


def _sds(shape, dtype):
    return jax.ShapeDtypeStruct(shape, dtype)


def _acc(ref, val, i):
    @pl.when(i == 0)
    def _():
        ref[...] = val

    @pl.when(i > 0)
    def _():
        ref[...] += val


def _colsum(v):
    return jnp.sum(v, axis=0, keepdims=True)


def _silu(v):
    return v * jax.nn.sigmoid(v)


def _dsilu(v):
    s = jax.nn.sigmoid(v)
    return s * (1.0 + v * (1.0 - s))


_DIMS = {"nn": (((1,), (0,)), ((), ())), "nt": (((1,), (1,)), ((), ())), "tn": (((0,), (0,)), ((), ()))}


def _matmul(a, b, mode, out_dtype, name, rider=None):
    a_halves = a.shape[0] if a.ndim == 3 else 0
    b_halves = b.shape[0] if b.ndim == 3 else 0
    if mode == "nn":
        (m, c), (_, n) = a.shape, b.shape
    elif mode == "nt":
        m, c = (a.shape[1], a.shape[0] * a.shape[2]) if a_halves else a.shape
        n = b.shape[0]
    else:
        c, m = a.shape
        n = b.shape[0] * b.shape[2] if b_halves else b.shape[1]
    tm = _pick(m, 1024)
    tn = _pick(n // b_halves, 1024) if b_halves else _pick(n, 1024)
    c_cap = 2048 if mode == "tn" else 2816
    tc = _pick(c // a_halves, c_cap) if a_halves else _pick(c, c_cap)
    nk = c // tc
    if a_halves:
        per_a = c // a_halves // tc
        a_spec = pl.BlockSpec((None, tm, tc), lambda i, j, k: (k // per_a, i, k % per_a))
    else:
        a_spec = {"nn": pl.BlockSpec((tm, tc), lambda i, j, k: (i, k)), "nt": pl.BlockSpec((tm, tc), lambda i, j, k: (i, k)),
                  "tn": pl.BlockSpec((tc, tm), lambda i, j, k: (k, i))}[mode]
    if b_halves:
        per_b = n // b_halves // tn
        b_spec = pl.BlockSpec((None, tc, tn), lambda i, j, k: (j // per_b, k, j % per_b))
    else:
        b_spec = {"nn": pl.BlockSpec((tc, tn), lambda i, j, k: (k, j)), "nt": pl.BlockSpec((tn, tc), lambda i, j, k: (j, k)),
                  "tn": pl.BlockSpec((tc, tn), lambda i, j, k: (k, j))}[mode]
    dims = _DIMS[mode]

    def body(a_ref, b_ref, o_ref, acc_ref):
        k = pl.program_id(2)
        p = lax.dot_general(a_ref[...], b_ref[...], dims, preferred_element_type=F32)
        if nk == 1:
            o_ref[...] = p.astype(out_dtype)
        else:
            @pl.when(k == 0)
            def _():
                acc_ref[...] = p

            @pl.when(k > 0)
            def _():
                acc_ref[...] += p

            @pl.when(k == nk - 1)
            def _():
                o_ref[...] = acc_ref[...].astype(out_dtype)

    return _call(
        body, name=name, grid=(m // tm, n // tn, nk), in_specs=[a_spec, b_spec],
        out_specs=pl.BlockSpec((tm, tn), lambda i, j, k: (i, j)), out_shape=_sds((m, n), out_dtype),
        scratch=[pltpu.VMEM((tm, tn), F32)], rider=rider,
    )(a, b)


def _row_spec(tr, w):
    return pl.BlockSpec((tr, w), lambda i: (i, 0))


def _vec_spec(w):
    return pl.BlockSpec((1, w), lambda i: (0, 0))


def _resid_mod(x, prev, mods, target, name):
    s, d = x.shape
    tr = _pick(s, ROW_TILE, 8)
    n_mod = len(mods)

    def body(*refs):
        it = iter(refs)
        x_ref = next(it)
        if prev is not None:
            y_ref, yb_ref, gate_ref = next(it), next(it), next(it)
        mod_refs = [(next(it), next(it), next(it)) for _ in range(n_mod)]
        if target is not None:
            t_ref = next(it)
        if prev is not None:
            xo_ref = next(it)
        h_refs = [next(it) for _ in range(n_mod)]
        i = pl.program_id(0)
        xv = x_ref[...]
        if prev is not None:
            xv = xv + gate_ref[...] * (y_ref[...] + yb_ref[...])
            xo_ref[...] = xv
        if n_mod:
            nrm = xv * lax.rsqrt(jnp.mean(xv * xv, axis=-1, keepdims=True) + EPS)
            for (g_ref, sc_ref, sh_ref), h_ref in zip(mod_refs, h_refs):
                h_ref[...] = (nrm * g_ref[...] * (1.0 + sc_ref[...]) + sh_ref[...]).astype(BF16)
        if target is not None:
            dx_ref, loss_ref = next(it), next(it)
            err = xv - t_ref[...]
            dx_ref[...] = err * (1.0 / d)
            _acc(loss_ref, _colsum(err * err) * (0.5 / d), i)

    ins, in_specs = [x], [_row_spec(tr, d)]
    if prev is not None:
        ins += list(prev)
        in_specs += [_row_spec(tr, d), _vec_spec(d), _vec_spec(d)]
    for g, sc, sh in mods:
        ins += [g, sc, sh]
        in_specs += [_vec_spec(d)] * 3
    if target is not None:
        ins.append(target)
        in_specs.append(_row_spec(tr, d))
    out_shape, out_specs = [], []
    if prev is not None:
        out_shape.append(_sds((s, d), F32))
        out_specs.append(_row_spec(tr, d))
    for _ in mods:
        out_shape.append(_sds((s, d), BF16))
        out_specs.append(_row_spec(tr, d))
    if target is not None:
        out_shape += [_sds((s, d), F32), _sds((1, d), F32)]
        out_specs += [_row_spec(tr, d), _vec_spec(d)]
    return _call(body, name=name, grid=(s // tr,), in_specs=in_specs, out_specs=out_specs, out_shape=out_shape)(*ins)


def _bwd_step(dx_up, x, mods, prev, name, rider=None):
    s, d = x.shape
    tr = _pick(s, ROW_TILE, 8)
    n_mod = len(mods)

    def body(*refs):
        it = iter(refs)
        dxu_ref, x_ref = next(it), next(it)
        mod_refs = [(next(it), next(it), next(it)) for _ in range(n_mod)]
        if prev is not None:
            y_ref, yb_ref, gate_ref = next(it), next(it), next(it)
        dx_ref = next(it)
        acc_refs = [(next(it), next(it), next(it)) for _ in range(n_mod)]
        i = pl.program_id(0)
        dx = dxu_ref[...]
        if n_mod:
            xv = x_ref[...]
            rstd = lax.rsqrt(jnp.mean(xv * xv, axis=-1, keepdims=True) + EPS)
            nrm = xv * rstd
        for (dh_ref, g_ref, sc_ref), (dsh_ref, dsc_ref, dg_ref) in zip(mod_refs, acc_refs):
            dh = dh_ref[...]
            gv, one_sc = g_ref[...], 1.0 + sc_ref[...]
            _acc(dsh_ref, _colsum(dh), i)
            t = dh * nrm
            _acc(dsc_ref, _colsum(t) * gv, i)
            _acc(dg_ref, _colsum(t) * one_sc, i)
            dn = dh * (gv * one_sc)
            dx = dx + rstd * (dn - nrm * jnp.mean(dn * nrm, axis=-1, keepdims=True))
        dx_ref[...] = dx
        if prev is not None:
            dy_ref, dgate_ref, dyb_ref = next(it), next(it), next(it)
            dy = gate_ref[...] * dx
            dy_ref[...] = dy.astype(BF16)
            _acc(dgate_ref, _colsum(dx * (y_ref[...] + yb_ref[...])), i)
            _acc(dyb_ref, _colsum(dy), i)

    ins, in_specs = [dx_up, x], [_row_spec(tr, d)] * 2
    for dh, g, sc in mods:
        ins += [dh, g, sc]
        in_specs += [_row_spec(tr, d), _vec_spec(d), _vec_spec(d)]
    if prev is not None:
        ins += list(prev)
        in_specs += [_row_spec(tr, d), _vec_spec(d), _vec_spec(d)]
    out_shape, out_specs = [_sds((s, d), F32)], [_row_spec(tr, d)]
    for _ in mods:
        out_shape += [_sds((1, d), F32)] * 3
        out_specs += [_vec_spec(d)] * 3
    if prev is not None:
        out_shape += [_sds((s, d), BF16), _sds((1, d), F32), _sds((1, d), F32)]
        out_specs += [_row_spec(tr, d), _vec_spec(d), _vec_spec(d)]
    return _call(body, name=name, grid=(s // tr,), in_specs=in_specs, out_specs=out_specs, out_shape=out_shape, rider=rider)(*ins)


def _glu_fwd(u, bias, name):
    s, d2 = u.shape
    d = d2 // 2
    tr = _pick(s, ROW_TILE, 8)

    def body(u_ref, b_ref, o_ref):
        uv = u_ref[...] + b_ref[...]
        o_ref[...] = uv[:, :d] * jax.nn.sigmoid(uv[:, d:])

    return _call(body, name=name, grid=(s // tr,), in_specs=[_row_spec(tr, d2), _vec_spec(d2)],
                 out_specs=_row_spec(tr, d), out_shape=_sds((s, d), F32))(u, bias)


def _glu_bwd(dglu, u, bias, name):
    s, d2 = u.shape
    d = d2 // 2
    tr = _pick(s, ROW_TILE, 8)

    def body(dg_ref, u_ref, b_ref, du_ref, db_ref):
        i = pl.program_id(0)
        uv = u_ref[...] + b_ref[...]
        a, sg = uv[:, :d], jax.nn.sigmoid(uv[:, d:])
        dg = dg_ref[...]
        da = dg * sg
        dgt = dg * a * sg * (1.0 - sg)
        du_ref[:, :d] = da.astype(BF16)
        du_ref[:, d:] = dgt.astype(BF16)
        _acc(db_ref, jnp.concatenate([_colsum(da), _colsum(dgt)], axis=1), i)

    return _call(body, name=name, grid=(s // tr,), in_specs=[_row_spec(tr, d), _row_spec(tr, d2), _vec_spec(d2)],
                 out_specs=[_row_spec(tr, d2), _vec_spec(d2)], out_shape=[_sds((s, d2), BF16), _sds((1, d2), F32)])(dglu, u, bias)


def _halo_rows(k):
    return -(-(k - 1) // 8) * 8


CONV_CHUNK = 32


def _sublane_shifts(cat_ref, sh_ref, rows):
    for m in range(8):
        sh_ref[m] = cat_ref[pl.ds(m, rows), :]


def _dwconv_fwd(x, w, b, name, rider=None):
    s, c = x.shape
    kk = w.shape[0]
    hb = _halo_rows(kk)
    tr, tc = _pick(s, ROW_TILE, hb), _pick(c, 512)
    per = tr // hb
    ch = CONV_CHUNK

    def body(xp_ref, x_ref, w_ref, b_ref, o_ref, cat_ref, sh_ref):
        i = pl.program_id(1)
        cat_ref[0:hb, :] = jnp.where(i > 0, xp_ref[...], 0.0)
        cat_ref[hb:hb + tr, :] = x_ref[...]
        cat_ref[hb + tr:hb + tr + 8, :] = jnp.zeros((8, tc), F32)
        _sublane_shifts(cat_ref, sh_ref, tr + hb)

        def chunk(ci, carry):
            r0 = pl.multiple_of(ci * ch, ch)
            acc = jnp.zeros((ch, tc), F32) + b_ref[...]
            for k in range(kk):
                a, m = divmod(hb - (kk - 1) + k, 8)
                acc = acc + w_ref[k:k + 1, :] * sh_ref[m, pl.ds(r0 + 8 * a, ch), :]
            o_ref[pl.ds(r0, ch), :] = acc
            return carry

        lax.fori_loop(0, tr // ch, chunk, 0)

    return _call(
        body, name=name, grid=(c // tc, s // tr),
        in_specs=[pl.BlockSpec((hb, tc), lambda j, i: (jnp.maximum(i * per - 1, 0), j)), pl.BlockSpec((tr, tc), lambda j, i: (i, j)),
                  pl.BlockSpec((kk, tc), lambda j, i: (0, j)), pl.BlockSpec((1, tc), lambda j, i: (0, j))],
        out_specs=pl.BlockSpec((tr, tc), lambda j, i: (i, j)), out_shape=_sds((s, c), F32),
        scratch=[pltpu.VMEM((tr + hb + 8, tc), F32), pltpu.VMEM((8, tr + hb, tc), F32)], rider=rider,
    )(x, x, w, b)


def _dwconv_bwd(dy, x, w, name, rider=None):
    s, c = x.shape
    kk = w.shape[0]
    hb = _halo_rows(kk)
    tr, tc = _pick(s, ROW_TILE, hb), _pick(c, 512)
    per, nt = tr // hb, s // tr
    ch = CONV_CHUNK
    groups = ch // 8

    def body(xp_ref, x_ref, dy_ref, dyn_ref, w_ref, dx_ref, dw_ref, db_ref, cat_ref, shx_ref, shd_ref, acc_ref):
        i = pl.program_id(1)
        cat_ref[0:hb, :] = jnp.where(i > 0, xp_ref[...], 0.0)
        cat_ref[hb:hb + tr, :] = x_ref[...]
        cat_ref[hb + tr:hb + tr + 8, :] = jnp.zeros((8, tc), F32)
        _sublane_shifts(cat_ref, shx_ref, tr + hb)
        cat_ref[0:tr, :] = dy_ref[...]
        cat_ref[tr:tr + hb, :] = jnp.where(i < nt - 1, dyn_ref[...], 0.0)
        _sublane_shifts(cat_ref, shd_ref, tr + hb)

        @pl.when(i == 0)
        def _():
            acc_ref[...] = jnp.zeros_like(acc_ref)

        def fold(v):
            out = v[0:8, :]
            for g in range(1, groups):
                out = out + v[8 * g:8 * g + 8, :]
            return out

        def chunk(ci, carry):
            r0 = pl.multiple_of(ci * ch, ch)
            dyv = dy_ref[pl.ds(r0, ch), :]
            acc_ref[kk] += fold(dyv)
            dxv = jnp.zeros((ch, tc), F32)
            for k in range(kk):
                a, m = divmod(kk - 1 - k, 8)
                dxv = dxv + w_ref[k:k + 1, :] * shd_ref[m, pl.ds(r0 + 8 * a, ch), :]
                a, m = divmod(hb - (kk - 1) + k, 8)
                acc_ref[k] += fold(dyv * shx_ref[m, pl.ds(r0 + 8 * a, ch), :])
            dx_ref[pl.ds(r0, ch), :] = dxv
            return carry

        lax.fori_loop(0, tr // ch, chunk, 0)

        @pl.when(i == nt - 1)
        def _():
            for k in range(kk):
                dw_ref[k:k + 1, :] = _colsum(acc_ref[k])
            db_ref[...] = _colsum(acc_ref[kk])

    return _call(
        body, name=name, grid=(c // tc, nt),
        in_specs=[pl.BlockSpec((hb, tc), lambda j, i: (jnp.maximum(i * per - 1, 0), j)), pl.BlockSpec((tr, tc), lambda j, i: (i, j)),
                  pl.BlockSpec((tr, tc), lambda j, i: (i, j)),
                  pl.BlockSpec((hb, tc), lambda j, i: (jnp.minimum((i + 1) * per, s // hb - 1), j)),
                  pl.BlockSpec((kk, tc), lambda j, i: (0, j))],
        out_specs=[pl.BlockSpec((tr, tc), lambda j, i: (i, j)), pl.BlockSpec((kk, tc), lambda j, i: (0, j)),
                   pl.BlockSpec((1, tc), lambda j, i: (0, j))],
        out_shape=[_sds((s, c), F32), _sds((kk, c), F32), _sds((1, c), F32)],
        scratch=[pltpu.VMEM((tr + hb + 8, tc), F32), pltpu.VMEM((8, tr + hb, tc), F32), pltpu.VMEM((8, tr + hb, tc), F32),
                 pltpu.VMEM((kk + 1, 8, tc), F32)], rider=rider,
    )(x, x, dy, dy, w)


def _ln_silu_fwd(x, g, b, name):
    s, d = x.shape
    tr = _pick(s, ROW_TILE, 8)

    def body(x_ref, g_ref, b_ref, o_ref):
        xv = x_ref[...]
        mu = jnp.mean(xv, axis=-1, keepdims=True)
        xc = xv - mu
        ln = xc * lax.rsqrt(jnp.mean(xc * xc, axis=-1, keepdims=True) + EPS) * g_ref[...] + b_ref[...]
        o_ref[...] = _silu(ln).astype(BF16)

    return _call(body, name=name, grid=(s // tr,), in_specs=[_row_spec(tr, d), _vec_spec(d), _vec_spec(d)],
                 out_specs=_row_spec(tr, d), out_shape=_sds((s, d), BF16))(x, g, b)


def _ln_silu_bwd(dact, x, g, b, name):
    s, d = x.shape
    tr = _pick(s, ROW_TILE, 8)

    def body(da_ref, x_ref, g_ref, b_ref, dx_ref, dg_ref, db_ref):
        i = pl.program_id(0)
        xv = x_ref[...]
        mu = jnp.mean(xv, axis=-1, keepdims=True)
        xc = xv - mu
        rstd = lax.rsqrt(jnp.mean(xc * xc, axis=-1, keepdims=True) + EPS)
        xh = xc * rstd
        ln = xh * g_ref[...] + b_ref[...]
        dln = da_ref[...] * _dsilu(ln)
        _acc(dg_ref, _colsum(dln * xh), i)
        _acc(db_ref, _colsum(dln), i)
        dxh = dln * g_ref[...]
        dx_ref[...] = rstd * (dxh - jnp.mean(dxh, axis=-1, keepdims=True) - xh * jnp.mean(dxh * xh, axis=-1, keepdims=True))

    return _call(body, name=name, grid=(s // tr,), in_specs=[_row_spec(tr, d), _row_spec(tr, d), _vec_spec(d), _vec_spec(d)],
                 out_specs=[_row_spec(tr, d), _vec_spec(d), _vec_spec(d)],
                 out_shape=[_sds((s, d), F32), _sds((1, d), F32), _sds((1, d), F32)])(dact, x, g, b)


FFN_CHUNK = 16


def _fold8(v):
    out = v[0:8, :]
    for g in range(1, v.shape[0] // 8):
        out = out + v[8 * g:8 * g + 8, :]
    return out


def _ffn_act_fwd(u, w, b, name, rider=None):
    s, f2 = u.shape
    f = f2 // 2
    hb = 8
    tr, tc = _pick(s, ROW_TILE, 2 * hb), _pick(f, 1408)
    per, nf = tr // hb, f // tc
    ch = FFN_CHUNK

    def body(gp_ref, g_ref, v_ref, w_ref, b_ref, z_ref, cat_ref, sh_ref):
        i = pl.program_id(1)
        cat_ref[0:hb, :] = jnp.where(i > 0, gp_ref[...], 0.0)
        cat_ref[hb:hb + tr, :] = g_ref[...]
        for k in range(FFN_K - 1):
            sh_ref[k] = cat_ref[pl.ds(hb - (FFN_K - 1) + k, tr), :]

        def chunk(ci, carry):
            rows = pl.ds(pl.multiple_of(ci * ch, ch), ch)
            gc = b_ref[...] + w_ref[FFN_K - 1:FFN_K, :] * g_ref[rows, :]
            for k in range(FFN_K - 1):
                gc = gc + w_ref[k:k + 1, :] * sh_ref[k, rows, :]
            z_ref[rows, :] = (_silu(gc) * v_ref[rows, :]).astype(BF16)
            return carry

        lax.fori_loop(0, tr // ch, chunk, 0)

    return _call(
        body, name=name, grid=(nf, s // tr),
        in_specs=[pl.BlockSpec((hb, tc), lambda j, i: (jnp.maximum(i * per - 1, 0), j)), pl.BlockSpec((tr, tc), lambda j, i: (i, j)),
                  pl.BlockSpec((tr, tc), lambda j, i: (i, nf + j)), pl.BlockSpec((FFN_K, tc), lambda j, i: (0, j)),
                  pl.BlockSpec((1, tc), lambda j, i: (0, j))],
        out_specs=pl.BlockSpec((tr, tc), lambda j, i: (i, j)), out_shape=_sds((s, f), BF16),
        scratch=[pltpu.VMEM((tr + hb, tc), F32), pltpu.VMEM((FFN_K - 1, tr, tc), F32)], rider=rider,
    )(u, u, u, w, b)


def _ffn_act_bwd(dz, u, w, b, name, rider=None):
    s, f2 = u.shape
    f = f2 // 2
    hb = 8
    tr, tc = _pick(s, ROW_TILE, 2 * hb), _pick(f, 1408)
    per, nf, nt = tr // hb, f // tc, s // tr
    ext = tr + hb
    ch = FFN_CHUNK
    last_tap = FFN_K - 1

    def body(gp_ref, g_ref, gn_ref, v_ref, vn_ref, dz_ref, dzn_ref, w_ref, b_ref, du_ref, dw_ref, db_ref, cat_ref, sh_ref, dgc_ref,
             sd_ref, acc_ref):
        i = pl.program_id(1)
        last = i == nt - 1
        cat_ref[0:hb, :] = jnp.where(i > 0, gp_ref[...], 0.0)
        cat_ref[hb:hb + tr, :] = g_ref[...]
        cat_ref[hb + tr:hb + tr + hb, :] = gn_ref[...]
        for k in range(last_tap):
            sh_ref[k] = cat_ref[pl.ds(hb - last_tap + k, ext), :]

        def preact(rows, g_rows):
            gc = b_ref[...] + w_ref[last_tap:FFN_K, :] * g_rows
            for k in range(last_tap):
                gc = gc + w_ref[k:k + 1, :] * sh_ref[k, rows, :]
            return gc

        def chunk1(ci, carry):
            rows = pl.ds(pl.multiple_of(ci * ch, ch), ch)
            gc = preact(rows, g_ref[rows, :])
            sg = jax.nn.sigmoid(gc)
            dzv = dz_ref[rows, :]
            du_ref[1, rows, :] = (dzv * (gc * sg)).astype(BF16)
            dgc_ref[rows, :] = dzv * v_ref[rows, :] * (sg * (1.0 + gc * (1.0 - sg)))
            return carry

        lax.fori_loop(0, tr // ch, chunk1, 0)
        gcn = preact(pl.ds(tr, hb), gn_ref[...])
        dgc_ref[tr:ext, :] = jnp.where(last, 0.0, dzn_ref[...] * vn_ref[...] * _dsilu(gcn))
        for j in range(last_tap):
            sd_ref[j] = dgc_ref[pl.ds(j + 1, tr), :]

        @pl.when(i == 0)
        def _():
            acc_ref[...] = jnp.zeros_like(acc_ref)

        def chunk2(ci, carry):
            rows = pl.ds(pl.multiple_of(ci * ch, ch), ch)
            d0 = dgc_ref[rows, :]
            dgt = w_ref[last_tap:FFN_K, :] * d0
            for k in range(last_tap):
                dgt = dgt + w_ref[k:k + 1, :] * sd_ref[last_tap - 1 - k, rows, :]
            du_ref[0, rows, :] = dgt.astype(BF16)
            acc_ref[FFN_K] += _fold8(d0)
            acc_ref[last_tap] += _fold8(d0 * g_ref[rows, :])
            for k in range(last_tap):
                acc_ref[k] += _fold8(d0 * sh_ref[k, rows, :])
            return carry

        lax.fori_loop(0, tr // ch, chunk2, 0)

        @pl.when(last)
        def _():
            for k in range(FFN_K):
                dw_ref[k:k + 1, :] = _colsum(acc_ref[k])
            db_ref[...] = _colsum(acc_ref[FFN_K])

    prev_map = lambda j, i: (jnp.maximum(i * per - 1, 0), j)
    next_map = lambda j, i: (jnp.minimum((i + 1) * per, s // hb - 1), j)
    next_map_v = lambda j, i: (jnp.minimum((i + 1) * per, s // hb - 1), nf + j)
    return _call(
        body, name=name, grid=(nf, nt),
        in_specs=[pl.BlockSpec((hb, tc), prev_map), pl.BlockSpec((tr, tc), lambda j, i: (i, j)), pl.BlockSpec((hb, tc), next_map),
                  pl.BlockSpec((tr, tc), lambda j, i: (i, nf + j)), pl.BlockSpec((hb, tc), next_map_v),
                  pl.BlockSpec((tr, tc), lambda j, i: (i, j)), pl.BlockSpec((hb, tc), next_map),
                  pl.BlockSpec((FFN_K, tc), lambda j, i: (0, j)), pl.BlockSpec((1, tc), lambda j, i: (0, j))],
        out_specs=[pl.BlockSpec((2, tr, tc), lambda j, i: (0, i, j)), pl.BlockSpec((FFN_K, tc), lambda j, i: (0, j)),
                   pl.BlockSpec((1, tc), lambda j, i: (0, j))],
        out_shape=[_sds((2, s, f), BF16), _sds((FFN_K, f), F32), _sds((1, f), F32)],
        scratch=[pltpu.VMEM((tr + 2 * hb, tc), F32), pltpu.VMEM((last_tap, ext, tc), F32), pltpu.VMEM((ext, tc), F32),
                 pltpu.VMEM((last_tap, tr, tc), F32), pltpu.VMEM((FFN_K + 1, 8, tc), F32)], rider=rider,
    )(u, u, u, u, u, dz, dz, w, b)


def _rope_tables(pos, freq, sign, name):
    s = pos.shape[0]
    tr = _pick(s, 512, 8)

    def body(p_ref, f_ref, s_ref, c_ref, sn_ref):
        ang = p_ref[...].astype(F32) * f_ref[...]
        c_ref[...] = jnp.cos(ang)
        sn_ref[...] = jnp.sin(ang) * s_ref[...]

    return _call(body, name=name, grid=(s // tr,), in_specs=[pl.BlockSpec((tr, 1), lambda i: (i, 0)), _vec_spec(128), _vec_spec(128)],
                 out_specs=[_row_spec(tr, 128)] * 2, out_shape=[_sds((s, 128), F32)] * 2)(pos, freq, sign)


def _partner(v):
    lane = lax.broadcasted_iota(jnp.int32, v.shape, 1)
    lower = pltpu.roll(v, HEAD_DIM - ROT_DIM // 2, 1)
    upper = jnp.where(lane < ROT_DIM, pltpu.roll(v, ROT_DIM // 2, 1), 0.0)
    return jnp.where(lane < ROT_DIM // 2, lower, upper)


def _qk_norm_rope_fwd(q, kv, gq, gk, ctab, stab, name, rider=None):
    s, w = q.shape
    tr = _pick(s, 128, 8)
    heads = w // HEAD_DIM

    def body(q_ref, k_ref, gq_ref, gk_ref, c_ref, s_ref, qn_ref, kn_ref):
        cv, sv = c_ref[...], s_ref[...]
        for src, g_ref, dst in ((q_ref, gq_ref, qn_ref), (k_ref, gk_ref, kn_ref)):
            gv = g_ref[...]
            for h in range(heads):
                cols = pl.ds(h * HEAD_DIM, HEAD_DIM)
                xv = src[:, cols]
                nv = xv * lax.rsqrt(jnp.mean(xv * xv, axis=-1, keepdims=True) + EPS) * gv
                dst[:, cols] = nv * cv + _partner(nv) * sv

    return _call(
        body, name=name, grid=(s // tr,),
        in_specs=[_row_spec(tr, w), _row_spec(tr, w), _vec_spec(128), _vec_spec(128), _row_spec(tr, 128), _row_spec(tr, 128)],
        out_specs=[_row_spec(tr, w)] * 2, out_shape=[_sds((s, w), F32)] * 2, rider=rider,
    )(q, kv, gq, gk, ctab, stab)


def _qk_norm_rope_bwd(dqs, dks, dvs, q, kv, gq, gk, ctab, stab, name, rider=None):
    s, w = q.shape
    gw = w // N_GROUPS
    tr = _pick(s, 128, 8)
    hpg = gw // HEAD_DIM

    def body(*refs):
        dq_refs, dk_refs, dv_refs = refs[0:3], refs[3:6], refs[6:9]
        q_ref, k_ref, gq_ref, gk_ref, c_ref, s_ref, dq_ref, dkv_ref, dgq_ref, dgk_ref = refs[9:]
        i = pl.program_id(0)
        cv, sv = c_ref[...], s_ref[...]
        for d_refs, src, g_ref, dst, dg_ref in ((dq_refs, q_ref, gq_ref, dq_ref, dgq_ref), (dk_refs, k_ref, gk_ref, dkv_ref, dgk_ref)):
            gv = g_ref[...]
            dg = jnp.zeros((1, HEAD_DIM), F32)
            for g in range(N_GROUPS):
                for h in range(hpg):
                    cols = pl.ds(g * gw + h * HEAD_DIM, HEAD_DIM)
                    dout = d_refs[g][:, pl.ds(h * HEAD_DIM, HEAD_DIM)]
                    dn = dout * cv + _partner(dout * sv)
                    xv = src[:, cols]
                    rstd = lax.rsqrt(jnp.mean(xv * xv, axis=-1, keepdims=True) + EPS)
                    xh = xv * rstd
                    dg = dg + _colsum(dn * xh)
                    dxh = dn * gv
                    dst[:, cols] = (rstd * (dxh - xh * jnp.mean(dxh * xh, axis=-1, keepdims=True))).astype(BF16)
            _acc(dg_ref, dg, i)
        for g in range(N_GROUPS):
            dkv_ref[:, pl.ds(w + g * gw, gw)] = dv_refs[g][...].astype(BF16)

    return _call(
        body, name=name, grid=(s // tr,),
        in_specs=[_row_spec(tr, gw)] * 9 + [_row_spec(tr, w), _row_spec(tr, w), _vec_spec(128), _vec_spec(128),
                                            _row_spec(tr, 128), _row_spec(tr, 128)],
        out_specs=[_row_spec(tr, w), _row_spec(tr, 2 * w), _vec_spec(128), _vec_spec(128)],
        out_shape=[_sds((s, w), BF16), _sds((s, 2 * w), BF16), _sds((1, 128), F32), _sds((1, 128), F32)], rider=rider,
    )(*dqs, *dks, *dvs, q, kv, gq, gk, ctab, stab)


def _rows(j, b, r):
    start = j + r * BLK * b
    return pl.ds(start, BLK, stride=r) if r > 1 else pl.ds(start, BLK)


def _dot_nt(a, b):
    return lax.dot_general(a, b, _DIMS["nt"], preferred_element_type=F32)


def _dot_nn(a, b):
    return lax.dot_general(a, b, _DIMS["nn"], preferred_element_type=F32)


def _band_masks():
    qi = lax.broadcasted_iota(jnp.int32, (BLK, BLK), 0)
    kj = lax.broadcasted_iota(jnp.int32, (BLK, BLK), 1)
    return kj <= qi, kj >= qi


def _attn_fwd(qn, kn, kv, g, name):
    s, w = qn.shape
    r = DILATIONS[g]
    gw = w // N_GROUPS
    cr = min(ATTN_CHUNK, s)
    nb = cr // (BLK * r)
    hp = 1
    cw = hp * HEAD_DIM
    gc = gw // cw
    scale = 1.0 / math.sqrt(HEAD_DIM)

    def body(q_ref, kc_ref, kp_ref, vc_ref, vp_ref, o_ref, l_ref):
        n = pl.program_id(1)
        same_m, prev_m = _band_masks()
        prev_first = jnp.logical_and(prev_m, n > 0)
        for h in range(hp):
            cols = pl.ds(h * HEAD_DIM, HEAD_DIM)
            for j in range(r):
                for b in range(nb):
                    rows = _rows(j, b, r)
                    qv = q_ref[rows, cols].astype(BF16)
                    kc, vc = kc_ref[rows, cols].astype(BF16), vc_ref[rows, cols].astype(BF16)
                    if b > 0:
                        rp = _rows(j, b - 1, r)
                        kp, vp, pm = kc_ref[rp, cols].astype(BF16), vc_ref[rp, cols].astype(BF16), prev_m
                    else:
                        rp = _rows(j, nb - 1, r)
                        kp, vp, pm = kp_ref[rp, cols].astype(BF16), vp_ref[rp, cols].astype(BF16), prev_first
                    sd = jnp.where(same_m, _dot_nt(qv, kc) * scale, NEG)
                    so = jnp.where(pm, _dot_nt(qv, kp) * scale, NEG)
                    m = jnp.maximum(jnp.max(sd, axis=-1, keepdims=True), jnp.max(so, axis=-1, keepdims=True))
                    pd, po = jnp.exp(sd - m), jnp.exp(so - m)
                    den = jnp.sum(pd, axis=-1, keepdims=True) + jnp.sum(po, axis=-1, keepdims=True)
                    ov = (_dot_nn(pd.astype(BF16), vc) + _dot_nn(po.astype(BF16), vp)) / den
                    o_ref[rows, cols] = ov
                    l_ref[rows, cols] = jnp.broadcast_to(m + jnp.log(den), (BLK, HEAD_DIM))

    cur = lambda base: (lambda c, n: (n, base + c))
    prv = lambda base: (lambda c, n: (jnp.maximum(n - 1, 0), base + c))
    qb, kb, vb = g * gc, g * gc, (N_GROUPS + g) * gc
    return _call(
        body, name=name, grid=(gc, s // cr),
        in_specs=[pl.BlockSpec((cr, cw), cur(qb)), pl.BlockSpec((cr, cw), cur(kb)), pl.BlockSpec((cr, cw), prv(kb)),
                  pl.BlockSpec((cr, cw), cur(vb)), pl.BlockSpec((cr, cw), prv(vb))],
        out_specs=[pl.BlockSpec((cr, cw), lambda c, n: (n, c))] * 2, out_shape=[_sds((s, gw), F32)] * 2,
    )(qn, kn, kn, kv, kv)


def _attn_combine(os_, lses, name):
    s, gw = os_[0].shape
    tr = _pick(s, ROW_TILE, 8)

    def body(o0, o1, o2, l0, l1, l2, o_ref, l_ref):
        a, b, c = l0[...], l1[...], l2[...]
        m = jnp.maximum(jnp.maximum(a, b), c)
        ea, eb, ec = jnp.exp(a - m), jnp.exp(b - m), jnp.exp(c - m)
        den = ea + eb + ec
        o_ref[...] = ((ea * o0[...] + eb * o1[...] + ec * o2[...]) / den).astype(BF16)
        l_ref[...] = m + jnp.log(den)

    return _call(body, name=name, grid=(s // tr,), in_specs=[_row_spec(tr, gw)] * 6, out_specs=[_row_spec(tr, gw)] * 2,
                 out_shape=[_sds((s, gw), BF16), _sds((s, gw), F32)])(*os_, *lses)


def _attn_delta(do, o, name):
    s, gw = do.shape
    tr = _pick(s, ROW_TILE, 8)

    def body(do_ref, o_ref, d_ref):
        for h in range(gw // HEAD_DIM):
            cols = pl.ds(h * HEAD_DIM, HEAD_DIM)
            t = jnp.sum(do_ref[:, cols] * o_ref[:, cols].astype(F32), axis=-1, keepdims=True)
            d_ref[:, cols] = jnp.broadcast_to(t, (tr, HEAD_DIM))

    return _call(body, name=name, grid=(s // tr,), in_specs=[_row_spec(tr, gw)] * 2, out_specs=_row_spec(tr, gw),
                 out_shape=_sds((s, gw), F32))(do, o)


def _pair_grads(qv, kv_, vv, dov, lse, delta, mask, scale):
    sc = jnp.where(mask, _dot_nt(qv, kv_) * scale, NEG)
    p = jnp.exp(sc - lse)
    ds = p * (_dot_nt(dov, vv) - delta) * scale
    return p, ds


def _attn_bwd_dq(qn, kn, kv, do, lse, delta, g, name):
    s, w = qn.shape
    r = DILATIONS[g]
    gw = w // N_GROUPS
    cr = min(ATTN_CHUNK, s)
    nb = cr // (BLK * r)
    cw = HEAD_DIM
    gc = gw // cw
    scale = 1.0 / math.sqrt(HEAD_DIM)

    def body(q_ref, kc_ref, kp_ref, vc_ref, vp_ref, do_ref, l_ref, d_ref, dq_ref):
        n = pl.program_id(1)
        same_m, prev_m = _band_masks()
        prev_first = jnp.logical_and(prev_m, n > 0)
        for j in range(r):
            for b in range(nb):
                rows = _rows(j, b, r)
                qv, dov = q_ref[rows, :].astype(BF16), do_ref[rows, :].astype(BF16)
                lse, delta = l_ref[rows, :], d_ref[rows, :]
                kc, vc = kc_ref[rows, :].astype(BF16), vc_ref[rows, :].astype(BF16)
                if b > 0:
                    rp = _rows(j, b - 1, r)
                    kp, vp, pm = kc_ref[rp, :].astype(BF16), vc_ref[rp, :].astype(BF16), prev_m
                else:
                    rp = _rows(j, nb - 1, r)
                    kp, vp, pm = kp_ref[rp, :].astype(BF16), vp_ref[rp, :].astype(BF16), prev_first
                _, dsd = _pair_grads(qv, kc, vc, dov, lse, delta, same_m, scale)
                _, dso = _pair_grads(qv, kp, vp, dov, lse, delta, pm, scale)
                dq_ref[rows, :] = _dot_nn(dsd.astype(BF16), kc) + _dot_nn(dso.astype(BF16), kp)

    cur = lambda base: (lambda c, n: (n, base + c))
    prv = lambda base: (lambda c, n: (jnp.maximum(n - 1, 0), base + c))
    qb, vb = g * gc, (N_GROUPS + g) * gc
    own = pl.BlockSpec((cr, cw), lambda c, n: (n, c))
    return _call(
        body, name=name, grid=(gc, s // cr),
        in_specs=[pl.BlockSpec((cr, cw), cur(qb)), pl.BlockSpec((cr, cw), cur(qb)), pl.BlockSpec((cr, cw), prv(qb)),
                  pl.BlockSpec((cr, cw), cur(vb)), pl.BlockSpec((cr, cw), prv(vb)), own, own, own],
        out_specs=own, out_shape=_sds((s, gw), F32),
    )(qn, kn, kn, kv, kv, do, lse, delta)


def _attn_bwd_dkv(qn, kn, kv, do, lse, delta, g, name):
    s, w = qn.shape
    r = DILATIONS[g]
    gw = w // N_GROUPS
    cr = min(ATTN_CHUNK, s)
    nb = cr // (BLK * r)
    nchunk = s // cr
    cw = HEAD_DIM
    gc = gw // cw
    scale = 1.0 / math.sqrt(HEAD_DIM)

    def body(k_ref, v_ref, qc_ref, qx_ref, doc_ref, dox_ref, lc_ref, lx_ref, dc_ref, dx_ref, dk_ref, dv_ref):
        n = pl.program_id(1)
        same_m, prev_m = _band_masks()
        next_last = jnp.logical_and(prev_m, n < nchunk - 1)
        for j in range(r):
            for b in range(nb):
                rows = _rows(j, b, r)
                kv_, vv = k_ref[rows, :].astype(BF16), v_ref[rows, :].astype(BF16)
                qv, dov = qc_ref[rows, :].astype(BF16), doc_ref[rows, :].astype(BF16)
                pd, dsd = _pair_grads(qv, kv_, vv, dov, lc_ref[rows, :], dc_ref[rows, :], same_m, scale)
                if b < nb - 1:
                    rx = _rows(j, b + 1, r)
                    qx, dox, lx, dlx, xm = qc_ref[rx, :], doc_ref[rx, :], lc_ref[rx, :], dc_ref[rx, :], prev_m
                else:
                    rx = _rows(j, 0, r)
                    qx, dox, lx, dlx, xm = qx_ref[rx, :], dox_ref[rx, :], lx_ref[rx, :], dx_ref[rx, :], next_last
                qx, dox = qx.astype(BF16), dox.astype(BF16)
                po, dso = _pair_grads(qx, kv_, vv, dox, lx, dlx, xm, scale)
                dk_ref[rows, :] = _dot_nn(dsd.T.astype(BF16), qv) + _dot_nn(dso.T.astype(BF16), qx)
                dv_ref[rows, :] = _dot_nn(pd.T.astype(BF16), dov) + _dot_nn(po.T.astype(BF16), dox)

    cur = lambda base: (lambda c, n: (n, base + c))
    nxt = lambda base: (lambda c, n: (jnp.minimum(n + 1, nchunk - 1), base + c))
    qb, vb = g * gc, (N_GROUPS + g) * gc
    blk = lambda f: pl.BlockSpec((cr, cw), f)
    return _call(
        body, name=name, grid=(gc, nchunk),
        in_specs=[blk(cur(qb)), blk(cur(vb)), blk(cur(qb)), blk(nxt(qb)), blk(cur(0)), blk(nxt(0)), blk(cur(0)), blk(nxt(0)),
                  blk(cur(0)), blk(nxt(0))],
        out_specs=[blk(cur(0))] * 2, out_shape=[_sds((s, gw), F32)] * 2,
    )(kn, kv, qn, qn, do, do, lse, lse, delta, delta)


def _mod_proj(sc_all, w, name):
    l, d, ns = w.shape
    tn = _pick(ns, 512)

    def body(c_ref, w_ref, o_ref):
        o_ref[...] = jnp.dot(c_ref[...].astype(BF16), w_ref[...].astype(BF16), preferred_element_type=F32)

    return _call(
        body, name=name, grid=(l, ns // tn),
        in_specs=[pl.BlockSpec((N_DEV, d), lambda a, j: (0, 0)), pl.BlockSpec((None, d, tn), lambda a, j: (a, 0, j))],
        out_specs=pl.BlockSpec((None, N_DEV, tn), lambda a, j: (a, 0, j)), out_shape=_sds((l, N_DEV, ns), F32),
    )(sc_all, w)


def _adamw_math(w, g, m, v):
    m = ADAM_B1 * m + (1.0 - ADAM_B1) * g
    v = ADAM_B2 * v + (1.0 - ADAM_B2) * (g * g)
    m_hat = m / (1.0 - ADAM_B1 ** ADAM_STEP)
    v_hat = v / (1.0 - ADAM_B2 ** ADAM_STEP)
    delta = -ADAM_LR * (m_hat / (jnp.sqrt(v_hat) + ADAM_EPS) + ADAM_WD * w)
    return delta, m, v


def _adamw_big(w, g, m, v, name):
    shape = w.shape
    cols = shape[-1]
    rows = math.prod(shape[:-1])
    tr, tc = _pick(rows, 512, 8), _pick(cols, 1024)
    w2, g2, m2, v2 = (t.reshape(rows, cols) for t in (w, g, m, v))

    def body(w_ref, g_ref, m_ref, v_ref, d_ref, mo_ref, vo_ref):
        d_ref[...], mo_ref[...], vo_ref[...] = _adamw_math(w_ref[...], g_ref[...], m_ref[...], v_ref[...])

    spec = pl.BlockSpec((tr, tc), lambda i, j: (i, j))
    outs = _call(body, name=name, grid=(rows // tr, cols // tc), in_specs=[spec] * 4, out_specs=[spec] * 3,
                 out_shape=[_sds((rows, cols), F32)] * 3)(w2, g2, m2, v2)
    return [t.reshape(shape) for t in outs]


def _adamw_mod(w, sct, dm, m, v, name):
    l, d, ns = w.shape
    tr, tc = _pick(d, 512, 8), _pick(ns, 1024)

    def body(w_ref, c_ref, dm_ref, m_ref, v_ref, g_ref, d_ref, mo_ref, vo_ref):
        cv, dv = c_ref[...].astype(BF16).astype(F32), dm_ref[...].astype(BF16).astype(F32)
        g = jnp.zeros((tr, tc), F32)
        for e in range(N_DEV):
            g = g + cv[:, e:e + 1] * dv[e:e + 1, :]
        g_ref[...] = g
        d_ref[...], mo_ref[...], vo_ref[...] = _adamw_math(w_ref[...], g, m_ref[...], v_ref[...])

    spec = pl.BlockSpec((None, tr, tc), lambda a, i, j: (a, i, j))
    return _call(
        body, name=name, grid=(l, d // tr, ns // tc),
        in_specs=[spec, pl.BlockSpec((tr, N_DEV), lambda a, i, j: (i, 0)), pl.BlockSpec((None, N_DEV, tc), lambda a, i, j: (a, 0, j)),
                  spec, spec],
        out_specs=[spec] * 4, out_shape=[_sds((l, d, ns), F32)] * 4,
    )(w, sct, dm, m, v)


def _adamw_small(ws, gs, ms, vs, name):
    n = len(ws)

    def body(*refs):
        w_r, g_r, m_r, v_r = refs[0:n], refs[n:2 * n], refs[2 * n:3 * n], refs[3 * n:4 * n]
        d_o, m_o, v_o = refs[4 * n:5 * n], refs[5 * n:6 * n], refs[6 * n:7 * n]
        for k in range(n):
            d_o[k][...], m_o[k][...], v_o[k][...] = _adamw_math(w_r[k][...], g_r[k][...], m_r[k][...], v_r[k][...])

    vm = pl.BlockSpec(memory_space=pltpu.VMEM)
    shapes = [_sds(w.shape, F32) for w in ws]
    outs = pl.pallas_call(body, name=name, in_specs=[vm] * (4 * n), out_specs=[vm] * (3 * n), out_shape=shapes * 3,
                          interpret=False)(*ws, *gs, *ms, *vs)
    return outs[0:n], outs[n:2 * n], outs[2 * n:3 * n]


def _sum_rows(a, name):
    n, v = a.shape
    tc = _pick(v, 8192)

    def body(a_ref, o_ref):
        acc = a_ref[0:1, :]
        for e in range(1, n):
            acc = acc + a_ref[e:e + 1, :]
        o_ref[...] = acc

    return _call(body, name=name, grid=(v // tc,), in_specs=[pl.BlockSpec((n, tc), lambda j: (0, j))],
                 out_specs=pl.BlockSpec((1, tc), lambda j: (0, j)), out_shape=_sds((1, v), F32))(a)


def _cast_into_full(w2d, kind, slot_arr, name):
    rows, cols = w2d.shape
    tr, tc = _pick(rows, 512, 16), _pick(cols, 1024)
    nr, nc = rows // tr, cols // tc

    def body(s_ref, w_ref, o_ref):
        o_ref[...] = w_ref[...].astype(BF16)

    if kind == "col":
        o_map = lambda i, j, s_ref: (i, s_ref[0] * nc + j)
    else:
        o_map = lambda i, j, s_ref: (s_ref[0] * nr + i, j)
    return _call(body, name=name, grid=(nr, nc), in_specs=[pl.BlockSpec((tr, tc), lambda i, j, s_ref: (i, j))],
                 out_specs=pl.BlockSpec((tr, tc), o_map), out_shape=_sds(_full_shape(kind, (rows, cols)), BF16), nsp=1)(slot_arr, w2d)


def _place():
    x, y, c = lax.axis_index("x"), lax.axis_index("y"), lax.axis_index("c")
    chips = [(1 - x, y), (x, 1 - y), (1 - x, 1 - y)]
    return x, y, c, chips


def _remote(src, dst, send_sem, recv_sem, to):
    return pltpu.make_async_remote_copy(src_ref=src, dst_ref=dst, send_sem=send_sem, recv_sem=recv_sem, device_id=to,
                                        device_id_type=MESH)


def _allgather8(a, name):
    m_per, n = a.shape

    def body(x_ref, out_ref, send_sems, recv_sems, local_sem):
        x, y, c, chips = _place()
        me, sibling = (x, y, c), (x, y, 1 - c)

        def rows(px, py, pc):
            return out_ref.at[pl.ds((4 * px + 2 * py + pc) * m_per, m_per), :]

        def copy(k, block, to, src=None):
            return _remote(rows(*block) if src is None else src, rows(*block), send_sems.at[k], recv_sems.at[k], to)

        mine = pltpu.make_async_copy(x_ref, rows(*me), local_sem)
        mine.start()
        first = [copy(0, me, sibling, src=x_ref)]
        first += [copy(1 + j, me, (*chip, c), src=x_ref) for j, chip in enumerate(chips)]
        for cp in first:
            cp.start()
        passed = [copy(4 + j, (*chip, c), sibling) for j, chip in enumerate(chips)]
        for j, chip in enumerate(chips):
            copy(1 + j, (*chip, c), me).wait_recv()
            passed[j].start()
        copy(0, sibling, me).wait_recv()
        for j, chip in enumerate(chips):
            copy(4 + j, (*chip, 1 - c), me).wait_recv()
        for cp in first + passed:
            cp.wait_send()
        mine.wait()

    return pl.pallas_call(
        body, name=name, out_shape=_sds((N_DEV * m_per, n), a.dtype),
        in_specs=[pl.BlockSpec(memory_space=pltpu.VMEM)], out_specs=pl.BlockSpec(memory_space=pltpu.VMEM),
        scratch_shapes=[pltpu.SemaphoreType.DMA((7,)), pltpu.SemaphoreType.DMA((7,)), pltpu.SemaphoreType.DMA],
        interpret=False,
    )(a)


def _region(ref, kind, shard_shape, slot, half, piece=(0, 1)):
    r, cs = shard_shape
    hr = r // 2
    n = hr // piece[1]
    start = half * hr + piece[0] * n
    if kind == "col":
        return ref.at[pl.ds(start, n), pl.ds(slot * cs, cs)]
    return ref.at[pl.ds(slot * r + start, n), :]


def _full_shape(kind, shard_shape):
    r, cs = shard_shape
    return (r, 4 * cs) if kind == "col" else (4 * r, cs)


_HBM = pl.BlockSpec(memory_space=pltpu.HBM)


def _in_place(arrays, n_sem, copies):
    return _Rider(arrays, [_sds(t.shape, t.dtype) for t in arrays], {a: a for a in range(len(arrays))}, n_sem, copies)


def _gather_chips(fulls, kinds, shapes, piece=(0, 1)):
    n = len(fulls)

    def copies(ins, outs, send_sems, recv_sems):
        x, y, c, chips = _place()
        me_slot = 2 * x + y
        sends, recvs = [], []
        for a in range(n):
            for j, (cx, cy) in enumerate(chips):
                k = 3 * a + j
                sends.append(_remote(_region(ins[a], kinds[a], shapes[a], me_slot, c, piece),
                                     _region(outs[a], kinds[a], shapes[a], me_slot, c, piece),
                                     send_sems.at[k], recv_sems.at[k], (cx, cy, c)))
                land = _region(outs[a], kinds[a], shapes[a], 2 * cx + cy, c, piece)
                recvs.append(_remote(land, land, send_sems.at[k], recv_sems.at[k], (cx, cy, c)))
        return sends, recvs

    return _in_place(fulls, 3 * n, copies)


def _gather_pair(fulls, kinds, shapes):
    n = len(fulls)

    def copies(ins, outs, send_sems, recv_sems):
        x, y, c, chips = _place()
        sibling = (x, y, 1 - c)
        sends, recvs = [], []
        for a in range(n):
            for j, (cx, cy) in enumerate(chips):
                k = 3 * a + j
                sends.append(_remote(_region(ins[a], kinds[a], shapes[a], 2 * cx + cy, c), _region(outs[a], kinds[a], shapes[a], 2 * cx + cy, c),
                                     send_sems.at[k], recv_sems.at[k], sibling))
                land = _region(outs[a], kinds[a], shapes[a], 2 * cx + cy, 1 - c)
                recvs.append(_remote(land, land, send_sems.at[k], recv_sems.at[k], sibling))
        return sends, recvs

    return _in_place(fulls, 3 * n, copies)


def _rs_pair(grads, kinds, shapes):
    n = len(grads)

    def copies(ins, outs, send_sems, recv_sems):
        x, y, c, _ = _place()
        sibling = (x, y, 1 - c)
        sends, recvs = [], []
        for a in range(n):
            for slot in range(4):
                k = 4 * a + slot
                sends.append(_remote(_region(ins[a], kinds[a], shapes[a], slot, 1 - c), outs[a].at[slot], send_sems.at[k],
                                     recv_sems.at[k], sibling))
                recvs.append(_remote(_region(ins[a], kinds[a], shapes[a], slot, c), outs[a].at[slot], send_sems.at[k],
                                     recv_sems.at[k], sibling))
        return sends, recvs

    return _Rider(grads, [_sds((4, s[0] // 2, s[1]), F32) for s in shapes], {}, 4 * n, copies)


def _rs_pair_add(grad, recv, kind, shape, c_arr, name):
    r, cs = shape
    hr = r // 2
    tr, tc = _pick(hr, 512, 16), _pick(cs, 1024)
    nr, nc = hr // tr, cs // tc

    def body(c_ref, g_ref, r_ref, o_ref):
        o_ref[...] = (g_ref[...] + r_ref[...]).astype(BF16)

    if kind == "col":
        g_map = lambda s_, i, j, c_ref: (c_ref[0] * nr + i, s_ * nc + j)
    else:
        g_map = lambda s_, i, j, c_ref: (s_ * 2 * nr + c_ref[0] * nr + i, j)
    own = pl.BlockSpec((None, tr, tc), lambda s_, i, j, c_ref: (s_, i, j))
    return _call(body, name=name, grid=(4, nr, nc), in_specs=[pl.BlockSpec((tr, tc), g_map), own], out_specs=own,
                 out_shape=_sds((4, hr, cs), BF16), nsp=1)(c_arr, grad, recv)


def _rs_chips(parts):
    n = len(parts)

    def copies(ins, outs, send_sems, recv_sems):
        x, y, c, chips = _place()
        sends, recvs = [], []
        for a in range(n):
            for j, (cx, cy) in enumerate(chips):
                k = 3 * a + j
                cp = _remote(ins[a].at[2 * cx + cy], outs[a].at[j], send_sems.at[k], recv_sems.at[k], (cx, cy, c))
                sends.append(cp)
                recvs.append(cp)
        return sends, recvs

    return _Rider(parts, [_sds((3,) + p.shape[1:], BF16) for p in parts], {}, 3 * n, copies)


def _rs_chip_add(part, recv, slot_arr, c_arr, dest, layer, out_shape, name):
    _, hr, cs = part.shape
    tr, tc = _pick(hr, 512, 16), _pick(cs, 1024)
    nr = hr // tr

    def body(s_ref, c_ref, p_ref, r0_ref, r1_ref, r2_ref, *rest):
        o_ref = rest[-1]
        o_ref[...] = ((p_ref[...].astype(F32) + r0_ref[...].astype(F32)) + r1_ref[...].astype(F32)) + r2_ref[...].astype(F32)

    rk = lambda k: pl.BlockSpec((None, tr, tc), lambda i, j, s_ref, c_ref: (k, i, j))
    if layer is None:
        o_spec = pl.BlockSpec((tr, tc), lambda i, j, s_ref, c_ref: (c_ref[0] * nr + i, j))
    else:
        o_spec = pl.BlockSpec((None, tr, tc), lambda i, j, s_ref, c_ref: (layer, c_ref[0] * nr + i, j))
    in_specs = [pl.BlockSpec((None, tr, tc), lambda i, j, s_ref, c_ref: (s_ref[0], i, j)), rk(0), rk(1), rk(2)]
    args = [slot_arr, c_arr, part, recv, recv, recv]
    aliases = {}
    if dest is not None:
        in_specs.append(pl.BlockSpec(memory_space=pl.ANY))
        args.append(dest)
        aliases = {6: 0}
    return pl.pallas_call(
        body, name=name,
        grid_spec=pltpu.PrefetchScalarGridSpec(num_scalar_prefetch=2, grid=(nr, cs // tc), in_specs=in_specs, out_specs=o_spec),
        out_shape=_sds(out_shape, F32), input_output_aliases=aliases,
        compiler_params=pltpu.CompilerParams(dimension_semantics=("arbitrary",) * 2, vmem_limit_bytes=VMEM_LIMIT_MB << 20),
        interpret=False,
    )(*args)


def _rs_pair_share(shards, name):
    n = len(shards)
    views = []
    for a, t in enumerate(shards):
        views += [(a, None)] if t.ndim == 2 else [(a, l) for l in range(t.shape[0])]
    nv = len(views)

    def body(*refs):
        ins, outs = refs[0:n], refs[n:2 * n]
        send_sems, recv_sems = refs[2 * n:]
        x, y, c, _ = _place()
        sibling = (x, y, 1 - c)

        def rows(ref_list, k, half):
            a, layer = views[k]
            ref = ref_list[a] if layer is None else ref_list[a].at[layer]
            hr = ref.shape[0] // 2
            return ref.at[pl.ds(half * hr, hr), :]

        sent = []
        for k in range(nv):
            cp = _remote(rows(ins, k, c), rows(outs, k, c), send_sems.at[k], recv_sems.at[k], sibling)
            cp.start()
            sent.append(cp)
        for k in range(nv):
            _remote(rows(ins, k, 1 - c), rows(outs, k, 1 - c), send_sems.at[k], recv_sems.at[k], sibling).wait_recv()
        for cp in sent:
            cp.wait_send()

    return pl.pallas_call(
        body, name=name, out_shape=[_sds(t.shape, F32) for t in shards], in_specs=[_HBM] * n, out_specs=[_HBM] * n,
        input_output_aliases={a: a for a in range(n)},
        scratch_shapes=[pltpu.SemaphoreType.DMA((nv,)), pltpu.SemaphoreType.DMA((nv,))], interpret=False,
    )(*shards)


def _pad_to(v, mult):
    n = v.shape[0]
    return jnp.pad(v, (0, (-n) % mult))


def kernel(x, c, positions, mod_w, mod_b, norm_mix_g, norm_ffn_g, conv_pw1_w, conv_pw1_b, conv_dw_w, conv_dw_b, conv_ln_g, conv_ln_b, conv_pw2_w, conv_pw2_b, kv_mod_w, kv_mod_b, kv_norm_g, w_kv, k_norm_g, w_q, q_norm_g, w_o, ffn_up_w, ffn_dw_w, ffn_dw_b, ffn_down_w, loss_target, m_mod_w, m_mod_b, m_norm_mix_g, m_norm_ffn_g, m_conv_pw1_w, m_conv_pw1_b, m_conv_dw_w, m_conv_dw_b, m_conv_ln_g, m_conv_ln_b, m_conv_pw2_w, m_conv_pw2_b, m_kv_mod_w, m_kv_mod_b, m_kv_norm_g, m_w_kv, m_k_norm_g, m_w_q, m_q_norm_g, m_w_o, m_ffn_up_w, m_ffn_dw_w, m_ffn_dw_b, m_ffn_down_w, v_mod_w, v_mod_b, v_norm_mix_g, v_norm_ffn_g, v_conv_pw1_w, v_conv_pw1_b, v_conv_dw_w, v_conv_dw_b, v_conv_ln_g, v_conv_ln_b, v_conv_pw2_w, v_conv_pw2_b, v_kv_mod_w, v_kv_mod_b, v_kv_norm_g, v_w_kv, v_k_norm_g, v_w_q, v_q_norm_g, v_w_o, v_ffn_up_w, v_ffn_dw_w, v_ffn_dw_b, v_ffn_down_w):
    _, s, d = x.shape
    f = ffn_dw_b.shape[1]
    qw = w_q.shape[2] * 4
    ax, ay, ac = lax.axis_index("x"), lax.axis_index("y"), lax.axis_index("c")
    slot = 2 * ax + ay
    me8 = 4 * ax + 2 * ay + ac
    slot_arr = jnp.reshape(slot, (1,)).astype(jnp.int32)
    c_arr = jnp.reshape(ac, (1,)).astype(jnp.int32)
    x2 = x.reshape(s, d)
    target = loss_target.reshape(s, d)
    row = lambda v: v.reshape(1, -1)

    c_all = _allgather8(c.reshape(8, d // 8), "gather_c").reshape(N_DEV, d)
    sc_all = jax.nn.silu(c_all)
    mod_part = _mod_proj(sc_all, mod_w, "mod_proj")
    kvm_part = _mod_proj(sc_all, kv_mod_w[None], "kvmod_proj")
    nm, nk = mod_part.shape[2], kvm_part.shape[2]
    small_sharded = [conv_pw1_b, conv_dw_w, conv_dw_b, conv_ln_g, conv_ln_b, conv_pw2_b, ffn_dw_w]
    pack = jnp.concatenate([mod_part.reshape(-1), kvm_part.reshape(-1)] + [t.reshape(-1) for t in small_sharded])
    plen = pack.shape[0]
    pack = _pad_to(pack, 1024)
    gathered = _allgather8(pack.reshape(8, -1), "gather_mod").reshape(4, 2, -1)[:, 0, :plen]
    off = 0

    def take(n_el):
        nonlocal off
        out = lax.slice_in_dim(gathered, off, off + n_el, axis=1)
        off += n_el
        return out

    mod_g = take(2 * N_DEV * nm).reshape(4, 2, N_DEV, nm)
    kvm_g = take(N_DEV * nk).reshape(4, N_DEV, nk)
    mine = lambda t, axis: lax.dynamic_index_in_dim(t, me8, axis=axis, keepdims=False)
    mod_vec = jnp.transpose(mine(mod_g, 2), (1, 0, 2)).reshape(2, 4 * nm) + mod_b
    kvm_vec = mine(kvm_g, 1).reshape(4 * nk) + kv_mod_b
    pw1_b_full = take(conv_pw1_b.shape[1]).reshape(1, -1)
    dw_w_full = jnp.transpose(take(CONV_K * (d // 4)).reshape(4, CONV_K, d // 4), (1, 0, 2)).reshape(CONV_K, d)
    dw_b_full, ln_g_full, ln_b_full, pw2_b_full = (take(d // 4).reshape(1, d) for _ in range(4))
    fdw_full = jnp.transpose(take(2 * FFN_K * (f // 4)).reshape(4, 2, FFN_K, f // 4), (1, 2, 0, 3)).reshape(2, FFN_K, f)
    mods = [[row(mod_vec[l, k * d:(k + 1) * d]) for k in range(6)] for l in range(2)]
    kv_sh, kv_sc = row(kvm_vec[:d]), row(kvm_vec[d:])
    zero_d = jnp.zeros((1, d), F32)

    big = [("pw1", conv_pw1_w[0], "col"), ("pw2", conv_pw2_w[0], "row"), ("wkv", w_kv, "col"), ("wq", w_q[0], "col"),
           ("wo", w_o[0], "col"), ("up0", ffn_up_w[0], "col"), ("up1", ffn_up_w[1], "col"), ("dn0", ffn_down_w[0], "row"),
           ("dn1", ffn_down_w[1], "row")]
    names = [b[0] for b in big]
    kinds = [b[2] for b in big]
    shard_shapes = [b[1].shape for b in big]
    own = {b[0]: _cast_into_full(b[1], b[2], slot_arr, "cast_" + b[0]) for b in big}
    meta = dict(zip(names, zip(kinds, shard_shapes)))
    full = {}

    def stage(make, group, src, **kw):
        return make([src[k] for k in group], [meta[k][0] for k in group], [meta[k][1] for k in group], **kw)

    grp_a, grp_b, grp_c, grp_d = ["pw1", "pw2"], ["up0", "dn0"], ["wkv", "wq", "wo"], ["up1", "dn1"]
    landed_a = dict(zip(grp_a, _run_rider(stage(_gather_chips, grp_a, own), "gather_a_chips")))
    full.update(zip(grp_a, _run_rider(stage(_gather_pair, grp_a, landed_a), "gather_a_pair")))
    chips_c, chips_d = (stage(_gather_chips, g_, own) for g_ in (grp_c, grp_d))

    def pair_stage(group, chips_rider):
        return stage(_gather_pair, group, dict(zip(group, chips_rider.results)))

    def ffn_fwd(l, x_in, h, tag, rider_up=None, rider_down=None):
        u = _matmul(h, full["up%d" % l], "nn", F32, "mm_up" + tag, rider=rider_up)
        z = _ffn_act_fwd(u, fdw_full[l], row(ffn_dw_b[l]), "ffn_act" + tag)
        y = _matmul(z, full["dn%d" % l], "nn", F32, "mm_down" + tag, rider=rider_down)
        return u, z, y

    sh_m0, sc_m0, g_m0, sh_f0, sc_f0, g_f0 = mods[0]
    sh_m1, sc_m1, g_m1, sh_f1, sc_f1, g_f1 = mods[1]
    gmix0, gmix1, gffn0, gffn1 = row(norm_mix_g[0]), row(norm_mix_g[1]), row(norm_ffn_g[0]), row(norm_ffn_g[1])
    (h0,) = _resid_mod(x2, None, [(gmix0, sc_m0, sh_m0)], None, "mod_in")
    chips_b0 = stage(_gather_chips, grp_b, own, piece=(0, 2))
    u0 = _matmul(h0, full["pw1"], "nn", F32, "mm_pw1", rider=chips_b0)
    glu = _glu_fwd(u0, pw1_b_full, "glu")
    chips_b = stage(_gather_chips, grp_b, dict(zip(grp_b, chips_b0.results)), piece=(1, 2))
    dwc = _dwconv_fwd(glu, dw_w_full, dw_b_full, "dwconv", rider=chips_b)
    act = _ln_silu_fwd(dwc, ln_g_full, ln_b_full, "ln_silu")
    pair_b = pair_stage(grp_b, chips_b)
    y0 = _matmul(act, full["pw2"], "nn", F32, "mm_pw2", rider=pair_b)
    full.update(zip(grp_b, pair_b.results))
    x1, hf0 = _resid_mod(x2, (y0, pw2_b_full, g_m0), [(gffn0, sc_f0, sh_f0)], None, "resid_conv")
    uf0 = _matmul(hf0, full["up0"], "nn", F32, "mm_up0", rider=chips_c)
    zf0 = _ffn_act_fwd(uf0, fdw_full[0], row(ffn_dw_b[0]), "ffn_act0")
    pair_c = pair_stage(grp_c, chips_c)
    yf0 = _matmul(zf0, full["dn0"], "nn", F32, "mm_down0", rider=pair_c)
    full.update(zip(grp_c, pair_c.results))
    gkv = row(kv_norm_g)
    xa, hk, hq = _resid_mod(x1, (yf0, zero_d, g_f0), [(gkv, kv_sc, kv_sh), (gmix1, sc_m1, sh_m1)], None, "resid_ffn0")
    kvp = _matmul(hk, full["wkv"], "nn", F32, "mm_kv")
    qp = _matmul(hq, full["wq"], "nn", F32, "mm_q")
    inv_freq = ROPE_THETA ** (-jnp.arange(0, ROT_DIM, 2, dtype=F32) / ROT_DIM)
    half = ROT_DIM // 2
    freq_l = jnp.concatenate([inv_freq, inv_freq, jnp.zeros((HEAD_DIM - ROT_DIM,), F32)]).reshape(1, HEAD_DIM)
    sign_l = jnp.concatenate([-jnp.ones((half,), F32), jnp.ones((half,), F32), jnp.zeros((HEAD_DIM - ROT_DIM,), F32)]).reshape(1, HEAD_DIM)
    ctab, stab = _rope_tables(positions.reshape(s, 1), freq_l, sign_l, "rope_tables")
    gq, gk = row(q_norm_g[0]), row(k_norm_g)
    qn, kn = _qk_norm_rope_fwd(qp, kvp, gq, gk, ctab, stab, "qk_norm_rope", rider=chips_d)
    og, lg = zip(*[_attn_fwd(qn, kn, kvp, g, "attn_fwd%d" % g) for g in range(N_GROUPS)])
    o_mix, lse = _attn_combine(og, lg, "attn_combine")
    pair_d = pair_stage(grp_d, chips_d)
    ya = _matmul(o_mix, full["wo"], "nn", F32, "mm_o", rider=pair_d)
    full.update(zip(grp_d, pair_d.results))
    xb, hf1 = _resid_mod(xa, (ya, zero_d, g_m1), [(gffn1, sc_f1, sh_f1)], None, "resid_attn")
    uf1, zf1, yf1 = ffn_fwd(1, xb, hf1, "1")
    _, dxo, loss_cols = _resid_mod(xb, (yf1, zero_d, g_f1), [], target, "resid_loss")
    loss = lax.psum(jnp.sum(loss_cols), ("x", "y", "c"))

    gbig = {}

    parts, recv2 = {}, {}

    def rs_pair(group):
        return _rs_pair([gbig[k] for k in group], [meta[k][0] for k in group], [meta[k][1] for k in group])

    def rs_chips(group, pair_results):
        for k, r_ in zip(group, pair_results):
            parts[k] = _rs_pair_add(gbig[k], r_, meta[k][0], meta[k][1], c_arr, "rs_pair_add_" + k)
        return _rs_chips([parts[k] for k in group])

    rs_a, rs_b, rs_c, rs_d = ["dn1", "up1"], ["wo", "wq", "wkv"], ["dn0", "up0"], ["pw2", "pw1"]

    def ffn_bwd(l, dy, u, z, h, tag, rider=None):
        dz = _matmul(dy, full["dn%d" % l], "nt", F32, "mm_down_dx" + tag)
        gbig["dn%d" % l] = _matmul(z, dy, "tn", F32, "mm_down_dw" + tag)
        du, dfw, dfb = _ffn_act_bwd(dz, u, fdw_full[l], row(ffn_dw_b[l]), "ffn_act_bwd" + tag, rider=rider)
        dh = _matmul(du, full["up%d" % l], "nt", F32, "mm_up_dx" + tag)
        gbig["up%d" % l] = _matmul(h, du, "tn", F32, "mm_up_dw" + tag)
        return dh, dfw, dfb

    dxb0, dyf1, dg_f1, _ = _bwd_step(dxo, xb, [], (yf1, zero_d, g_f1), "bwd_loss")
    dhf1, dfw1, dfb1 = ffn_bwd(1, dyf1, uf1, zf1, hf1, "1")
    pair_a = rs_pair(rs_a)
    dxb, dsh_f1, dsc_f1, dgffn1, dya, dg_m1, _ = _bwd_step(dxb0, xb, [(dhf1, gffn1, sc_f1)], (ya, zero_d, g_m1), "bwd_attn_out",
                                                        rider=pair_a)
    do = _matmul(dya, full["wo"], "nt", F32, "mm_o_dx")
    gbig["wo"] = _matmul(o_mix, dya, "tn", F32, "mm_o_dw")
    delta = _attn_delta(do, o_mix, "attn_delta")
    dqs = [_attn_bwd_dq(qn, kn, kvp, do, lse, delta, g, "attn_dq%d" % g) for g in range(N_GROUPS)]
    dks, dvs = zip(*[_attn_bwd_dkv(qn, kn, kvp, do, lse, delta, g, "attn_dkv%d" % g) for g in range(N_GROUPS)])
    ride_a = rs_chips(rs_a, pair_a.results)
    dqp, dkvp, dgq, dgk = _qk_norm_rope_bwd(dqs, dks, dvs, qp, kvp, gq, gk, ctab, stab, "qk_norm_rope_bwd", rider=ride_a)
    recv2.update(zip(rs_a, ride_a.results))
    dhq = _matmul(dqp, full["wq"], "nt", F32, "mm_q_dx")
    gbig["wq"] = _matmul(hq, dqp, "tn", F32, "mm_q_dw")
    dhk = _matmul(dkvp, full["wkv"], "nt", F32, "mm_kv_dx")
    gbig["wkv"] = _matmul(hk, dkvp, "tn", F32, "mm_kv_dw")
    pair_b = rs_pair(rs_b)
    (dxa, dsh_kv, dsc_kv, dgkv, dsh_m1, dsc_m1, dgmix1, dyf0, dg_f0, _) = _bwd_step(
        dxb, xa, [(dhk, gkv, kv_sc), (dhq, gmix1, sc_m1)], (yf0, zero_d, g_f0), "bwd_kvq", rider=pair_b)
    ride_b = rs_chips(rs_b, pair_b.results)
    dhf0, dfw0, dfb0 = ffn_bwd(0, dyf0, uf0, zf0, hf0, "0", rider=ride_b)
    recv2.update(zip(rs_b, ride_b.results))
    pair_c = rs_pair(rs_c)
    dx1, dsh_f0, dsc_f0, dgffn0, dy0, dg_m0, dpw2_b = _bwd_step(dxa, x1, [(dhf0, gffn0, sc_f0)], (y0, pw2_b_full, g_m0), "bwd_conv_out",
                                                             rider=pair_c)
    dact = _matmul(dy0, full["pw2"], "nt", F32, "mm_pw2_dx")
    gbig["pw2"] = _matmul(act, dy0, "tn", F32, "mm_pw2_dw")
    ddwc, dln_g, dln_b = _ln_silu_bwd(dact, dwc, ln_g_full, ln_b_full, "ln_silu_bwd")
    ride_c = rs_chips(rs_c, pair_c.results)
    dglu, ddw_w, ddw_b = _dwconv_bwd(ddwc, glu, dw_w_full, "dwconv_bwd", rider=ride_c)
    recv2.update(zip(rs_c, ride_c.results))
    du0, dpw1_b = _glu_bwd(dglu, u0, pw1_b_full, "glu_bwd")
    dh0 = _matmul(du0, full["pw1"], "nt", F32, "mm_pw1_dx")
    gbig["pw1"] = _matmul(h0, du0, "tn", F32, "mm_pw1_dw")
    grad_x, dsh_m0, dsc_m0, dgmix0 = _bwd_step(dx1, x2, [(dh0, gmix0, sc_m0)], None, "bwd_in")

    dmod = [jnp.concatenate([dsh_m0, dsc_m0, dg_m0, dsh_f0, dsc_f0, dg_f0], axis=1),
            jnp.concatenate([dsh_m1, dsc_m1, dg_m1, dsh_f1, dsc_f1, dg_f1], axis=1)]
    dkvm = jnp.concatenate([dsh_kv, dsc_kv], axis=1)
    per_ex = [dmod[0], dmod[1], dkvm]
    summed = [dgmix0, dgmix1, dgffn0, dgffn1, dpw1_b, ddw_w, ddw_b, dln_g, dln_b, dpw2_b, dgkv, dgk, dgq, dfw0, dfw1, dfb0, dfb1]
    vec = jnp.concatenate([t.reshape(-1) for t in per_ex + summed])
    vlen = vec.shape[0]
    vec = _pad_to(vec, 1024)
    vall = _allgather8(vec.reshape(8, -1), "gather_small").reshape(N_DEV, -1)
    vsum = _sum_rows(vall, "sum_small")[0]
    n_pe = sum(t.size for t in per_ex)
    dm_all = vall[:, :n_pe]
    off2 = n_pe
    sums = []
    for t in summed:
        sums.append(vsum[off2:off2 + t.size].reshape(t.shape))
        off2 += t.size
    (s_gmix0, s_gmix1, s_gffn0, s_gffn1, s_pw1_b, s_dw_w, s_dw_b, s_ln_g, s_ln_b, s_pw2_b, s_gkv, s_gk, s_gq, s_fw0, s_fw1,
     s_fb0, s_fb1) = sums
    shard_cols = lambda t, width: lax.dynamic_slice_in_dim(t, slot * width, width, axis=t.ndim - 1)
    dm_mod = jnp.stack([shard_cols(dm_all[:, l * 6 * d:(l + 1) * 6 * d], nm) for l in range(2)])
    dm_kv = shard_cols(dm_all[:, 12 * d:14 * d], nk)[None]
    sct = jnp.transpose(sc_all)

    recv2.update(zip(rs_d, _run_rider(rs_chips(rs_d, _run_rider(rs_pair(rs_d), "rs_pair_d")), "rs_chips_d")))
    reduced = {}
    for nme, shp in zip(names, shard_shapes):
        p_, r_ = parts[nme], recv2[nme]
        if nme in ("up0", "up1", "dn0", "dn1"):
            key, layer = nme[:-1], int(nme[-1])
            reduced[key] = _rs_chip_add(p_, r_, slot_arr, c_arr, reduced.get(key), layer, (2,) + shp, "rs_chip_add_" + nme)
        else:
            reduced[nme] = _rs_chip_add(p_, r_, slot_arr, c_arr, None, None, shp, "rs_chip_add_" + nme)
    g_pw1, g_pw2, g_wkv, g_wq, g_wo, g_up, g_dn = _rs_pair_share(
        [reduced[k] for k in ("pw1", "pw2", "wkv", "wq", "wo", "up", "dn")], "rs_pair_share")

    grads, deltas, new_m, new_v = {}, {}, {}, {}

    def put(nme, g_, res):
        grads[nme] = g_
        deltas[nme], new_m[nme], new_v[nme] = res

    for nme, g_, w_, m_, v_ in (("conv_pw1_w", g_pw1[None], conv_pw1_w, m_conv_pw1_w, v_conv_pw1_w),
                                ("conv_pw2_w", g_pw2[None], conv_pw2_w, m_conv_pw2_w, v_conv_pw2_w),
                                ("w_kv", g_wkv, w_kv, m_w_kv, v_w_kv), ("w_q", g_wq[None], w_q, m_w_q, v_w_q),
                                ("w_o", g_wo[None], w_o, m_w_o, v_w_o), ("ffn_up_w", g_up, ffn_up_w, m_ffn_up_w, v_ffn_up_w),
                                ("ffn_down_w", g_dn, ffn_down_w, m_ffn_down_w, v_ffn_down_w)):
        put(nme, g_, _adamw_big(w_, g_, m_, v_, "adamw_" + nme))
    g_, *res = _adamw_mod(mod_w, sct, dm_mod, m_mod_w, v_mod_w, "adamw_mod_w")
    put("mod_w", g_, res)
    g_, *res = _adamw_mod(kv_mod_w[None], sct, dm_kv, m_kv_mod_w[None], v_kv_mod_w[None], "adamw_kv_mod_w")
    put("kv_mod_w", g_[0], [t[0] for t in res])

    dm_sum = vsum[:n_pe]
    small = [
        ("mod_b", dm_sum[:12 * d].reshape(2, 6 * d), mod_b, m_mod_b, v_mod_b),
        ("norm_mix_g", jnp.concatenate([s_gmix0, s_gmix1], axis=0), norm_mix_g, m_norm_mix_g, v_norm_mix_g),
        ("norm_ffn_g", jnp.concatenate([s_gffn0, s_gffn1], axis=0), norm_ffn_g, m_norm_ffn_g, v_norm_ffn_g),
        ("conv_pw1_b", shard_cols(s_pw1_b, conv_pw1_b.shape[1]), conv_pw1_b, m_conv_pw1_b, v_conv_pw1_b),
        ("conv_dw_w", shard_cols(s_dw_w, d // 4)[None], conv_dw_w, m_conv_dw_w, v_conv_dw_w),
        ("conv_dw_b", shard_cols(s_dw_b, d // 4), conv_dw_b, m_conv_dw_b, v_conv_dw_b),
        ("conv_ln_g", shard_cols(s_ln_g, d // 4), conv_ln_g, m_conv_ln_g, v_conv_ln_g),
        ("conv_ln_b", shard_cols(s_ln_b, d // 4), conv_ln_b, m_conv_ln_b, v_conv_ln_b),
        ("conv_pw2_b", shard_cols(s_pw2_b, d // 4), conv_pw2_b, m_conv_pw2_b, v_conv_pw2_b),
        ("kv_mod_b", dm_sum[12 * d:14 * d], kv_mod_b, m_kv_mod_b, v_kv_mod_b),
        ("kv_norm_g", s_gkv.reshape(-1), kv_norm_g, m_kv_norm_g, v_kv_norm_g),
        ("k_norm_g", s_gk.reshape(-1), k_norm_g, m_k_norm_g, v_k_norm_g),
        ("q_norm_g", s_gq, q_norm_g, m_q_norm_g, v_q_norm_g),
        ("ffn_dw_w", shard_cols(jnp.stack([s_fw0, s_fw1]), f // 4), ffn_dw_w, m_ffn_dw_w, v_ffn_dw_w),
        ("ffn_dw_b", jnp.concatenate([s_fb0, s_fb1], axis=0), ffn_dw_b, m_ffn_dw_b, v_ffn_dw_b),
    ]
    as2d = lambda t: t.reshape(-1, t.shape[-1])
    sd_, sm_, sv_ = _adamw_small([as2d(t[2]) for t in small], [as2d(t[1]) for t in small], [as2d(t[3]) for t in small],
                                 [as2d(t[4]) for t in small], "adamw_small")
    for (nme, g_, w_, _, _), d_, mo_, vo_ in zip(small, sd_, sm_, sv_):
        put(nme, g_.reshape(w_.shape), [d_.reshape(w_.shape), mo_.reshape(w_.shape), vo_.reshape(w_.shape)])

    order = ["mod_w", "mod_b", "norm_mix_g", "norm_ffn_g", "conv_pw1_w", "conv_pw1_b", "conv_dw_w", "conv_dw_b", "conv_ln_g",
             "conv_ln_b", "conv_pw2_w", "conv_pw2_b", "kv_mod_w", "kv_mod_b", "kv_norm_g", "w_kv", "k_norm_g", "w_q", "q_norm_g",
             "w_o", "ffn_up_w", "ffn_dw_w", "ffn_dw_b", "ffn_down_w"]
    return (loss, grad_x.reshape(x.shape), *[grads[k] for k in order], *[deltas[k] for k in order], *[new_m[k] for k in order],
            *[new_v[k] for k in order])
```

```python
import functools
import math

import jax
import jax.numpy as jnp
from jax import lax
from jax.experimental import pallas as pl
from jax.experimental.pallas import tpu as pltpu

F32 = jnp.float32
BF16 = jnp.bfloat16
EPS = 1e-6
NEG = -1e30
HEAD_DIM = 128
ROT_DIM = 32
ROPE_THETA = 500000.0
BLK = 128
DILATIONS = (1, 4, 16)
N_GROUPS = 3
CONV_K = 31
FFN_K = 3
ADAM_LR, ADAM_B1, ADAM_B2, ADAM_EPS, ADAM_WD, ADAM_STEP = 0.001, 0.9, 0.999, 1e-08, 0.01, 10
N_DEV = 8
MESH = pl.DeviceIdType.MESH
VMEM_LIMIT_MB = 56
ROW_TILE = 256
ATTN_CHUNK = 2048


def _pick(n, pref, mult=128):
    best = None
    d = mult
    while d <= min(n, pref):
        if n % d == 0:
            best = d
        d += mult
    return best if best is not None else n


def _wide_tile(rows, cols):
    tc = _pick(cols, 4096)
    return _pick(rows, max(16, (1 << 19) // tc), 16), tc


class _Rider:
    def __init__(self, ins, out_shapes, aliases, n_sem, copies):
        self.ins, self.out_shapes, self.aliases, self.n_sem, self.copies = list(ins), list(out_shapes), dict(aliases), n_sem, copies
        self.results = None


def _call(body, *, name, grid, in_specs, out_specs, out_shape, scratch=(), nsp=0, rider=None):
    params = pltpu.CompilerParams(dimension_semantics=("arbitrary",) * len(grid), vmem_limit_bytes=VMEM_LIMIT_MB << 20)
    if rider is None:
        return pl.pallas_call(
            body, name=name,
            grid_spec=pltpu.PrefetchScalarGridSpec(num_scalar_prefetch=nsp, grid=grid, in_specs=in_specs, out_specs=out_specs,
                                                   scratch_shapes=list(scratch)),
            out_shape=out_shape, compiler_params=params, interpret=False,
        )
    single = not isinstance(out_shape, (list, tuple))
    out_shapes = [out_shape] if single else list(out_shape)
    out_specs_l = [out_specs] if single else list(out_specs)
    n_in, n_out, n_scr = len(in_specs), len(out_shapes), len(scratch)
    r_in, r_out = len(rider.ins), len(rider.out_shapes)
    hbm = pl.BlockSpec(memory_space=pltpu.HBM)
    last = tuple(g - 1 for g in grid)

    def wrapped(*refs):
        pre, ins, rin = refs[:nsp], refs[nsp:nsp + n_in], refs[nsp + n_in:nsp + n_in + r_in]
        o0 = nsp + n_in + r_in
        outs, rout = refs[o0:o0 + n_out], refs[o0 + n_out:o0 + n_out + r_out]
        s0 = o0 + n_out + r_out
        scr, (send_sems, recv_sems) = refs[s0:s0 + n_scr], refs[s0 + n_scr:]
        ids = [pl.program_id(a) for a in range(len(grid))]
        is_first = functools.reduce(jnp.logical_and, [i == 0 for i in ids])
        is_last = functools.reduce(jnp.logical_and, [i == l for i, l in zip(ids, last)])

        @pl.when(is_first)
        def _():
            for cp in rider.copies(rin, rout, send_sems, recv_sems)[0]:
                cp.start()

        body(*pre, *ins, *outs, *scr)

        @pl.when(is_last)
        def _():
            sends, recvs = rider.copies(rin, rout, send_sems, recv_sems)
            for cp in recvs:
                cp.wait_recv()
            for cp in sends:
                cp.wait_send()

    call = pl.pallas_call(
        wrapped, name=name,
        grid_spec=pltpu.PrefetchScalarGridSpec(
            num_scalar_prefetch=nsp, grid=grid, in_specs=list(in_specs) + [hbm] * r_in, out_specs=out_specs_l + [hbm] * r_out,
            scratch_shapes=list(scratch) + [pltpu.SemaphoreType.DMA((rider.n_sem,)), pltpu.SemaphoreType.DMA((rider.n_sem,))]),
        out_shape=out_shapes + rider.out_shapes,
        input_output_aliases={nsp + n_in + i: n_out + o for i, o in rider.aliases.items()},
        compiler_params=params, interpret=False,
    )

    def run(*args):
        res = call(*args, *rider.ins)
        rider.results = list(res[n_out:])
        return res[0] if single else list(res[:n_out])

    return run


def _run_rider(rider, name):
    r_in, r_out = len(rider.ins), len(rider.out_shapes)
    hbm = pl.BlockSpec(memory_space=pltpu.HBM)

    def body(*refs):
        sends, recvs = rider.copies(refs[:r_in], refs[r_in:r_in + r_out], *refs[r_in + r_out:])
        for cp in sends:
            cp.start()
        for cp in recvs:
            cp.wait_recv()
        for cp in sends:
            cp.wait_send()

    return pl.pallas_call(
        body, name=name, out_shape=rider.out_shapes, in_specs=[hbm] * r_in, out_specs=[hbm] * r_out,
        input_output_aliases=rider.aliases,
        scratch_shapes=[pltpu.SemaphoreType.DMA((rider.n_sem,)), pltpu.SemaphoreType.DMA((rider.n_sem,))], interpret=False,
    )(*rider.ins)


def _sds(shape, dtype):
    return jax.ShapeDtypeStruct(shape, dtype)


def _acc(ref, val, i):
    @pl.when(i == 0)
    def _():
        ref[...] = val

    @pl.when(i > 0)
    def _():
        ref[...] += val


def _colsum(v):
    return jnp.sum(v, axis=0, keepdims=True)


def _silu(v):
    return v * jax.nn.sigmoid(v)


def _dsilu(v):
    s = jax.nn.sigmoid(v)
    return s * (1.0 + v * (1.0 - s))


_DIMS = {"nn": (((1,), (0,)), ((), ())), "nt": (((1,), (1,)), ((), ())), "tn": (((0,), (0,)), ((), ()))}


def _matmul(a, b, mode, out_dtype, name, rider=None):
    a_halves = a.shape[0] if a.ndim == 3 else 0
    b_halves = b.shape[0] if b.ndim == 3 else 0
    if mode == "nn":
        (m, c), (_, n) = a.shape, b.shape
    elif mode == "nt":
        m, c = (a.shape[1], a.shape[0] * a.shape[2]) if a_halves else a.shape
        n = b.shape[0]
    else:
        c, m = a.shape
        n = b.shape[0] * b.shape[2] if b_halves else b.shape[1]
    tm = _pick(m, 1024)
    tn = _pick(n // b_halves, 1024) if b_halves else _pick(n, 1024)
    c_cap = 2048 if mode == "tn" else 2816
    tc = _pick(c // a_halves, c_cap) if a_halves else _pick(c, c_cap)
    nk = c // tc
    if a_halves:
        per_a = c // a_halves // tc
        a_spec = pl.BlockSpec((None, tm, tc), lambda i, j, k: (k // per_a, i, k % per_a))
    else:
        a_spec = {"nn": pl.BlockSpec((tm, tc), lambda i, j, k: (i, k)), "nt": pl.BlockSpec((tm, tc), lambda i, j, k: (i, k)),
                  "tn": pl.BlockSpec((tc, tm), lambda i, j, k: (k, i))}[mode]
    if b_halves:
        per_b = n // b_halves // tn
        b_spec = pl.BlockSpec((None, tc, tn), lambda i, j, k: (j // per_b, k, j % per_b))
    else:
        b_spec = {"nn": pl.BlockSpec((tc, tn), lambda i, j, k: (k, j)), "nt": pl.BlockSpec((tn, tc), lambda i, j, k: (j, k)),
                  "tn": pl.BlockSpec((tc, tn), lambda i, j, k: (k, j))}[mode]
    dims = _DIMS[mode]

    def body(a_ref, b_ref, o_ref, acc_ref):
        k = pl.program_id(2)
        p = lax.dot_general(a_ref[...], b_ref[...], dims, preferred_element_type=F32)
        if nk == 1:
            o_ref[...] = p.astype(out_dtype)
        else:
            @pl.when(k == 0)
            def _():
                acc_ref[...] = p

            @pl.when(k > 0)
            def _():
                acc_ref[...] += p

            @pl.when(k == nk - 1)
            def _():
                o_ref[...] = acc_ref[...].astype(out_dtype)

    return _call(
        body, name=name, grid=(m // tm, n // tn, nk), in_specs=[a_spec, b_spec],
        out_specs=pl.BlockSpec((tm, tn), lambda i, j, k: (i, j)), out_shape=_sds((m, n), out_dtype),
        scratch=[pltpu.VMEM((tm, tn), F32)], rider=rider,
    )(a, b)


def _row_spec(tr, w):
    return pl.BlockSpec((tr, w), lambda i: (i, 0))


def _vec_spec(w):
    return pl.BlockSpec((1, w), lambda i: (0, 0))


def _resid_mod(x, prev, mods, target, name):
    s, d = x.shape
    tr = _pick(s, ROW_TILE, 8)
    n_mod = len(mods)

    def body(*refs):
        it = iter(refs)
        x_ref = next(it)
        if prev is not None:
            y_ref, yb_ref, gate_ref = next(it), next(it), next(it)
        mod_refs = [(next(it), next(it), next(it)) for _ in range(n_mod)]
        if target is not None:
            t_ref = next(it)
        if prev is not None:
            xo_ref = next(it)
        h_refs = [next(it) for _ in range(n_mod)]
        i = pl.program_id(0)
        xv = x_ref[...]
        if prev is not None:
            xv = xv + gate_ref[...] * (y_ref[...] + yb_ref[...])
            xo_ref[...] = xv
        if n_mod:
            nrm = xv * lax.rsqrt(jnp.mean(xv * xv, axis=-1, keepdims=True) + EPS)
            for (g_ref, sc_ref, sh_ref), h_ref in zip(mod_refs, h_refs):
                h_ref[...] = (nrm * g_ref[...] * (1.0 + sc_ref[...]) + sh_ref[...]).astype(BF16)
        if target is not None:
            dx_ref, loss_ref = next(it), next(it)
            err = xv - t_ref[...]
            dx_ref[...] = err * (1.0 / d)
            _acc(loss_ref, _colsum(err * err) * (0.5 / d), i)

    ins, in_specs = [x], [_row_spec(tr, d)]
    if prev is not None:
        ins += list(prev)
        in_specs += [_row_spec(tr, d), _vec_spec(d), _vec_spec(d)]
    for g, sc, sh in mods:
        ins += [g, sc, sh]
        in_specs += [_vec_spec(d)] * 3
    if target is not None:
        ins.append(target)
        in_specs.append(_row_spec(tr, d))
    out_shape, out_specs = [], []
    if prev is not None:
        out_shape.append(_sds((s, d), F32))
        out_specs.append(_row_spec(tr, d))
    for _ in mods:
        out_shape.append(_sds((s, d), BF16))
        out_specs.append(_row_spec(tr, d))
    if target is not None:
        out_shape += [_sds((s, d), F32), _sds((1, d), F32)]
        out_specs += [_row_spec(tr, d), _vec_spec(d)]
    return _call(body, name=name, grid=(s // tr,), in_specs=in_specs, out_specs=out_specs, out_shape=out_shape)(*ins)


def _bwd_step(dx_up, x, mods, prev, name, rider=None):
    s, d = x.shape
    tr = _pick(s, ROW_TILE, 8)
    n_mod = len(mods)

    def body(*refs):
        it = iter(refs)
        dxu_ref, x_ref = next(it), next(it)
        mod_refs = [(next(it), next(it), next(it)) for _ in range(n_mod)]
        if prev is not None:
            y_ref, yb_ref, gate_ref = next(it), next(it), next(it)
        dx_ref = next(it)
        acc_refs = [(next(it), next(it), next(it)) for _ in range(n_mod)]
        i = pl.program_id(0)
        dx = dxu_ref[...]
        if n_mod:
            xv = x_ref[...]
            rstd = lax.rsqrt(jnp.mean(xv * xv, axis=-1, keepdims=True) + EPS)
            nrm = xv * rstd
        for (dh_ref, g_ref, sc_ref), (dsh_ref, dsc_ref, dg_ref) in zip(mod_refs, acc_refs):
            dh = dh_ref[...]
            gv, one_sc = g_ref[...], 1.0 + sc_ref[...]
            _acc(dsh_ref, _colsum(dh), i)
            t = dh * nrm
            _acc(dsc_ref, _colsum(t) * gv, i)
            _acc(dg_ref, _colsum(t) * one_sc, i)
            dn = dh * (gv * one_sc)
            dx = dx + rstd * (dn - nrm * jnp.mean(dn * nrm, axis=-1, keepdims=True))
        dx_ref[...] = dx
        if prev is not None:
            dy_ref, dgate_ref, dyb_ref = next(it), next(it), next(it)
            dy = gate_ref[...] * dx
            dy_ref[...] = dy.astype(BF16)
            _acc(dgate_ref, _colsum(dx * (y_ref[...] + yb_ref[...])), i)
            _acc(dyb_ref, _colsum(dy), i)

    ins, in_specs = [dx_up, x], [_row_spec(tr, d)] * 2
    for dh, g, sc in mods:
        ins += [dh, g, sc]
        in_specs += [_row_spec(tr, d), _vec_spec(d), _vec_spec(d)]
    if prev is not None:
        ins += list(prev)
        in_specs += [_row_spec(tr, d), _vec_spec(d), _vec_spec(d)]
    out_shape, out_specs = [_sds((s, d), F32)], [_row_spec(tr, d)]
    for _ in mods:
        out_shape += [_sds((1, d), F32)] * 3
        out_specs += [_vec_spec(d)] * 3
    if prev is not None:
        out_shape += [_sds((s, d), BF16), _sds((1, d), F32), _sds((1, d), F32)]
        out_specs += [_row_spec(tr, d), _vec_spec(d), _vec_spec(d)]
    return _call(body, name=name, grid=(s // tr,), in_specs=in_specs, out_specs=out_specs, out_shape=out_shape, rider=rider)(*ins)


def _glu_fwd(u, bias, name):
    s, d2 = u.shape
    d = d2 // 2
    tr = _pick(s, ROW_TILE, 8)

    def body(u_ref, b_ref, o_ref):
        uv = u_ref[...] + b_ref[...]
        o_ref[...] = uv[:, :d] * jax.nn.sigmoid(uv[:, d:])

    return _call(body, name=name, grid=(s // tr,), in_specs=[_row_spec(tr, d2), _vec_spec(d2)],
                 out_specs=_row_spec(tr, d), out_shape=_sds((s, d), F32))(u, bias)


def _glu_bwd(dglu, u, bias, name):
    s, d2 = u.shape
    d = d2 // 2
    tr = _pick(s, ROW_TILE, 8)

    def body(dg_ref, u_ref, b_ref, du_ref, db_ref):
        i = pl.program_id(0)
        uv = u_ref[...] + b_ref[...]
        a, sg = uv[:, :d], jax.nn.sigmoid(uv[:, d:])
        dg = dg_ref[...]
        da = dg * sg
        dgt = dg * a * sg * (1.0 - sg)
        du_ref[:, :d] = da.astype(BF16)
        du_ref[:, d:] = dgt.astype(BF16)
        _acc(db_ref, jnp.concatenate([_colsum(da), _colsum(dgt)], axis=1), i)

    return _call(body, name=name, grid=(s // tr,), in_specs=[_row_spec(tr, d), _row_spec(tr, d2), _vec_spec(d2)],
                 out_specs=[_row_spec(tr, d2), _vec_spec(d2)], out_shape=[_sds((s, d2), BF16), _sds((1, d2), F32)])(dglu, u, bias)


def _halo_rows(k):
    return -(-(k - 1) // 8) * 8


CONV_CHUNK = 32


def _sublane_shifts(cat_ref, sh_ref, rows):
    for m in range(8):
        sh_ref[m] = cat_ref[pl.ds(m, rows), :]


def _dwconv_fwd(x, w, b, name, rider=None):
    s, c = x.shape
    kk = w.shape[0]
    hb = _halo_rows(kk)
    tr, tc = _pick(s, ROW_TILE, hb), _pick(c, 512)
    per = tr // hb
    ch = CONV_CHUNK

    def body(xp_ref, x_ref, w_ref, b_ref, o_ref, cat_ref, sh_ref):
        i = pl.program_id(1)
        cat_ref[0:hb, :] = jnp.where(i > 0, xp_ref[...], 0.0)
        cat_ref[hb:hb + tr, :] = x_ref[...]
        cat_ref[hb + tr:hb + tr + 8, :] = jnp.zeros((8, tc), F32)
        _sublane_shifts(cat_ref, sh_ref, tr + hb)

        def chunk(ci, carry):
            r0 = pl.multiple_of(ci * ch, ch)
            acc = jnp.zeros((ch, tc), F32) + b_ref[...]
            for k in range(kk):
                a, m = divmod(hb - (kk - 1) + k, 8)
                acc = acc + w_ref[k:k + 1, :] * sh_ref[m, pl.ds(r0 + 8 * a, ch), :]
            o_ref[pl.ds(r0, ch), :] = acc
            return carry

        lax.fori_loop(0, tr // ch, chunk, 0)

    return _call(
        body, name=name, grid=(c // tc, s // tr),
        in_specs=[pl.BlockSpec((hb, tc), lambda j, i: (jnp.maximum(i * per - 1, 0), j)), pl.BlockSpec((tr, tc), lambda j, i: (i, j)),
                  pl.BlockSpec((kk, tc), lambda j, i: (0, j)), pl.BlockSpec((1, tc), lambda j, i: (0, j))],
        out_specs=pl.BlockSpec((tr, tc), lambda j, i: (i, j)), out_shape=_sds((s, c), F32),
        scratch=[pltpu.VMEM((tr + hb + 8, tc), F32), pltpu.VMEM((8, tr + hb, tc), F32)], rider=rider,
    )(x, x, w, b)


def _dwconv_bwd(dy, x, w, name, rider=None):
    s, c = x.shape
    kk = w.shape[0]
    hb = _halo_rows(kk)
    tr, tc = _pick(s, ROW_TILE, hb), _pick(c, 512)
    per, nt = tr // hb, s // tr
    ch = CONV_CHUNK
    groups = ch // 8

    def body(xp_ref, x_ref, dy_ref, dyn_ref, w_ref, dx_ref, dw_ref, db_ref, cat_ref, shx_ref, shd_ref, acc_ref):
        i = pl.program_id(1)
        cat_ref[0:hb, :] = jnp.where(i > 0, xp_ref[...], 0.0)
        cat_ref[hb:hb + tr, :] = x_ref[...]
        cat_ref[hb + tr:hb + tr + 8, :] = jnp.zeros((8, tc), F32)
        _sublane_shifts(cat_ref, shx_ref, tr + hb)
        cat_ref[0:tr, :] = dy_ref[...]
        cat_ref[tr:tr + hb, :] = jnp.where(i < nt - 1, dyn_ref[...], 0.0)
        _sublane_shifts(cat_ref, shd_ref, tr + hb)

        @pl.when(i == 0)
        def _():
            acc_ref[...] = jnp.zeros_like(acc_ref)

        def fold(v):
            out = v[0:8, :]
            for g in range(1, groups):
                out = out + v[8 * g:8 * g + 8, :]
            return out

        def chunk(ci, carry):
            r0 = pl.multiple_of(ci * ch, ch)
            dyv = dy_ref[pl.ds(r0, ch), :]
            acc_ref[kk] += fold(dyv)
            dxv = jnp.zeros((ch, tc), F32)
            for k in range(kk):
                a, m = divmod(kk - 1 - k, 8)
                dxv = dxv + w_ref[k:k + 1, :] * shd_ref[m, pl.ds(r0 + 8 * a, ch), :]
                a, m = divmod(hb - (kk - 1) + k, 8)
                acc_ref[k] += fold(dyv * shx_ref[m, pl.ds(r0 + 8 * a, ch), :])
            dx_ref[pl.ds(r0, ch), :] = dxv
            return carry

        lax.fori_loop(0, tr // ch, chunk, 0)

        @pl.when(i == nt - 1)
        def _():
            for k in range(kk):
                dw_ref[k:k + 1, :] = _colsum(acc_ref[k])
            db_ref[...] = _colsum(acc_ref[kk])

    return _call(
        body, name=name, grid=(c // tc, nt),
        in_specs=[pl.BlockSpec((hb, tc), lambda j, i: (jnp.maximum(i * per - 1, 0), j)), pl.BlockSpec((tr, tc), lambda j, i: (i, j)),
                  pl.BlockSpec((tr, tc), lambda j, i: (i, j)),
                  pl.BlockSpec((hb, tc), lambda j, i: (jnp.minimum((i + 1) * per, s // hb - 1), j)),
                  pl.BlockSpec((kk, tc), lambda j, i: (0, j))],
        out_specs=[pl.BlockSpec((tr, tc), lambda j, i: (i, j)), pl.BlockSpec((kk, tc), lambda j, i: (0, j)),
                   pl.BlockSpec((1, tc), lambda j, i: (0, j))],
        out_shape=[_sds((s, c), F32), _sds((kk, c), F32), _sds((1, c), F32)],
        scratch=[pltpu.VMEM((tr + hb + 8, tc), F32), pltpu.VMEM((8, tr + hb, tc), F32), pltpu.VMEM((8, tr + hb, tc), F32),
                 pltpu.VMEM((kk + 1, 8, tc), F32)], rider=rider,
    )(x, x, dy, dy, w)


def _ln_silu_fwd(x, g, b, name):
    s, d = x.shape
    tr = _pick(s, ROW_TILE, 8)

    def body(x_ref, g_ref, b_ref, o_ref):
        xv = x_ref[...]
        mu = jnp.mean(xv, axis=-1, keepdims=True)
        xc = xv - mu
        ln = xc * lax.rsqrt(jnp.mean(xc * xc, axis=-1, keepdims=True) + EPS) * g_ref[...] + b_ref[...]
        o_ref[...] = _silu(ln).astype(BF16)

    return _call(body, name=name, grid=(s // tr,), in_specs=[_row_spec(tr, d), _vec_spec(d), _vec_spec(d)],
                 out_specs=_row_spec(tr, d), out_shape=_sds((s, d), BF16))(x, g, b)


def _ln_silu_bwd(dact, x, g, b, name):
    s, d = x.shape
    tr = _pick(s, ROW_TILE, 8)

    def body(da_ref, x_ref, g_ref, b_ref, dx_ref, dg_ref, db_ref):
        i = pl.program_id(0)
        xv = x_ref[...]
        mu = jnp.mean(xv, axis=-1, keepdims=True)
        xc = xv - mu
        rstd = lax.rsqrt(jnp.mean(xc * xc, axis=-1, keepdims=True) + EPS)
        xh = xc * rstd
        ln = xh * g_ref[...] + b_ref[...]
        dln = da_ref[...] * _dsilu(ln)
        _acc(dg_ref, _colsum(dln * xh), i)
        _acc(db_ref, _colsum(dln), i)
        dxh = dln * g_ref[...]
        dx_ref[...] = rstd * (dxh - jnp.mean(dxh, axis=-1, keepdims=True) - xh * jnp.mean(dxh * xh, axis=-1, keepdims=True))

    return _call(body, name=name, grid=(s // tr,), in_specs=[_row_spec(tr, d), _row_spec(tr, d), _vec_spec(d), _vec_spec(d)],
                 out_specs=[_row_spec(tr, d), _vec_spec(d), _vec_spec(d)],
                 out_shape=[_sds((s, d), F32), _sds((1, d), F32), _sds((1, d), F32)])(dact, x, g, b)


FFN_CHUNK = 16


def _fold8(v):
    out = v[0:8, :]
    for g in range(1, v.shape[0] // 8):
        out = out + v[8 * g:8 * g + 8, :]
    return out


def _ffn_act_fwd(u, w, b, name, rider=None):
    s, f2 = u.shape
    f = f2 // 2
    hb = 8
    tr, tc = _pick(s, ROW_TILE, 2 * hb), _pick(f, 1408)
    per, nf = tr // hb, f // tc
    ch = FFN_CHUNK

    def body(gp_ref, g_ref, v_ref, w_ref, b_ref, z_ref, cat_ref, sh_ref):
        i = pl.program_id(1)
        cat_ref[0:hb, :] = jnp.where(i > 0, gp_ref[...], 0.0)
        cat_ref[hb:hb + tr, :] = g_ref[...]
        for k in range(FFN_K - 1):
            sh_ref[k] = cat_ref[pl.ds(hb - (FFN_K - 1) + k, tr), :]

        def chunk(ci, carry):
            rows = pl.ds(pl.multiple_of(ci * ch, ch), ch)
            gc = b_ref[...] + w_ref[FFN_K - 1:FFN_K, :] * g_ref[rows, :]
            for k in range(FFN_K - 1):
                gc = gc + w_ref[k:k + 1, :] * sh_ref[k, rows, :]
            z_ref[rows, :] = (_silu(gc) * v_ref[rows, :]).astype(BF16)
            return carry

        lax.fori_loop(0, tr // ch, chunk, 0)

    return _call(
        body, name=name, grid=(nf, s // tr),
        in_specs=[pl.BlockSpec((hb, tc), lambda j, i: (jnp.maximum(i * per - 1, 0), j)), pl.BlockSpec((tr, tc), lambda j, i: (i, j)),
                  pl.BlockSpec((tr, tc), lambda j, i: (i, nf + j)), pl.BlockSpec((FFN_K, tc), lambda j, i: (0, j)),
                  pl.BlockSpec((1, tc), lambda j, i: (0, j))],
        out_specs=pl.BlockSpec((tr, tc), lambda j, i: (i, j)), out_shape=_sds((s, f), BF16),
        scratch=[pltpu.VMEM((tr + hb, tc), F32), pltpu.VMEM((FFN_K - 1, tr, tc), F32)], rider=rider,
    )(u, u, u, w, b)


def _ffn_act_bwd(dz, u, w, b, name, rider=None):
    s, f2 = u.shape
    f = f2 // 2
    hb = 8
    tr, tc = _pick(s, ROW_TILE, 2 * hb), _pick(f, 1408)
    per, nf, nt = tr // hb, f // tc, s // tr
    ext = tr + hb
    ch = FFN_CHUNK
    last_tap = FFN_K - 1

    def body(gp_ref, g_ref, gn_ref, v_ref, vn_ref, dz_ref, dzn_ref, w_ref, b_ref, du_ref, dw_ref, db_ref, cat_ref, sh_ref, dgc_ref,
             sd_ref, acc_ref):
        i = pl.program_id(1)
        last = i == nt - 1
        cat_ref[0:hb, :] = jnp.where(i > 0, gp_ref[...], 0.0)
        cat_ref[hb:hb + tr, :] = g_ref[...]
        cat_ref[hb + tr:hb + tr + hb, :] = gn_ref[...]
        for k in range(last_tap):
            sh_ref[k] = cat_ref[pl.ds(hb - last_tap + k, ext), :]

        def preact(rows, g_rows):
            gc = b_ref[...] + w_ref[last_tap:FFN_K, :] * g_rows
            for k in range(last_tap):
                gc = gc + w_ref[k:k + 1, :] * sh_ref[k, rows, :]
            return gc

        def chunk1(ci, carry):
            rows = pl.ds(pl.multiple_of(ci * ch, ch), ch)
            gc = preact(rows, g_ref[rows, :])
            sg = jax.nn.sigmoid(gc)
            dzv = dz_ref[rows, :]
            du_ref[1, rows, :] = (dzv * (gc * sg)).astype(BF16)
            dgc_ref[rows, :] = dzv * v_ref[rows, :] * (sg * (1.0 + gc * (1.0 - sg)))
            return carry

        lax.fori_loop(0, tr // ch, chunk1, 0)
        gcn = preact(pl.ds(tr, hb), gn_ref[...])
        dgc_ref[tr:ext, :] = jnp.where(last, 0.0, dzn_ref[...] * vn_ref[...] * _dsilu(gcn))
        for j in range(last_tap):
            sd_ref[j] = dgc_ref[pl.ds(j + 1, tr), :]

        @pl.when(i == 0)
        def _():
            acc_ref[...] = jnp.zeros_like(acc_ref)

        def chunk2(ci, carry):
            rows = pl.ds(pl.multiple_of(ci * ch, ch), ch)
            d0 = dgc_ref[rows, :]
            dgt = w_ref[last_tap:FFN_K, :] * d0
            for k in range(last_tap):
                dgt = dgt + w_ref[k:k + 1, :] * sd_ref[last_tap - 1 - k, rows, :]
            du_ref[0, rows, :] = dgt.astype(BF16)
            acc_ref[FFN_K] += _fold8(d0)
            acc_ref[last_tap] += _fold8(d0 * g_ref[rows, :])
            for k in range(last_tap):
                acc_ref[k] += _fold8(d0 * sh_ref[k, rows, :])
            return carry

        lax.fori_loop(0, tr // ch, chunk2, 0)

        @pl.when(last)
        def _():
            for k in range(FFN_K):
                dw_ref[k:k + 1, :] = _colsum(acc_ref[k])
            db_ref[...] = _colsum(acc_ref[FFN_K])

    prev_map = lambda j, i: (jnp.maximum(i * per - 1, 0), j)
    next_map = lambda j, i: (jnp.minimum((i + 1) * per, s // hb - 1), j)
    next_map_v = lambda j, i: (jnp.minimum((i + 1) * per, s // hb - 1), nf + j)
    return _call(
        body, name=name, grid=(nf, nt),
        in_specs=[pl.BlockSpec((hb, tc), prev_map), pl.BlockSpec((tr, tc), lambda j, i: (i, j)), pl.BlockSpec((hb, tc), next_map),
                  pl.BlockSpec((tr, tc), lambda j, i: (i, nf + j)), pl.BlockSpec((hb, tc), next_map_v),
                  pl.BlockSpec((tr, tc), lambda j, i: (i, j)), pl.BlockSpec((hb, tc), next_map),
                  pl.BlockSpec((FFN_K, tc), lambda j, i: (0, j)), pl.BlockSpec((1, tc), lambda j, i: (0, j))],
        out_specs=[pl.BlockSpec((2, tr, tc), lambda j, i: (0, i, j)), pl.BlockSpec((FFN_K, tc), lambda j, i: (0, j)),
                   pl.BlockSpec((1, tc), lambda j, i: (0, j))],
        out_shape=[_sds((2, s, f), BF16), _sds((FFN_K, f), F32), _sds((1, f), F32)],
        scratch=[pltpu.VMEM((tr + 2 * hb, tc), F32), pltpu.VMEM((last_tap, ext, tc), F32), pltpu.VMEM((ext, tc), F32),
                 pltpu.VMEM((last_tap, tr, tc), F32), pltpu.VMEM((FFN_K + 1, 8, tc), F32)], rider=rider,
    )(u, u, u, u, u, dz, dz, w, b)


def _rope_tables(pos, freq, sign, name):
    s = pos.shape[0]
    tr = _pick(s, 512, 8)

    def body(p_ref, f_ref, s_ref, c_ref, sn_ref):
        ang = p_ref[...].astype(F32) * f_ref[...]
        c_ref[...] = jnp.cos(ang)
        sn_ref[...] = jnp.sin(ang) * s_ref[...]

    return _call(body, name=name, grid=(s // tr,), in_specs=[pl.BlockSpec((tr, 1), lambda i: (i, 0)), _vec_spec(128), _vec_spec(128)],
                 out_specs=[_row_spec(tr, 128)] * 2, out_shape=[_sds((s, 128), F32)] * 2)(pos, freq, sign)


def _partner(v):
    lane = lax.broadcasted_iota(jnp.int32, v.shape, 1)
    lower = pltpu.roll(v, HEAD_DIM - ROT_DIM // 2, 1)
    upper = jnp.where(lane < ROT_DIM, pltpu.roll(v, ROT_DIM // 2, 1), 0.0)
    return jnp.where(lane < ROT_DIM // 2, lower, upper)


def _head_sum(v):
    hi = v.astype(BF16)
    lo = (v - hi.astype(F32)).astype(BF16)
    ones = jnp.ones((HEAD_DIM, HEAD_DIM), BF16)
    return lax.dot_general(hi, ones, _DIMS["nn"], preferred_element_type=F32) + lax.dot_general(
        lo, ones, _DIMS["nn"], preferred_element_type=F32)


def _qk_norm_rope_fwd(q, kv, gq, gk, ctab, stab, name, rider=None):
    s, w = q.shape
    tr = _pick(s, 128, 8)
    heads = w // HEAD_DIM

    def body(q_ref, k_ref, gq_ref, gk_ref, c_ref, s_ref, qn_ref, kn_ref):
        cv, sv = c_ref[...], s_ref[...]
        for src, g_ref, dst in ((q_ref, gq_ref, qn_ref), (k_ref, gk_ref, kn_ref)):
            gv = g_ref[...]
            for h in range(heads):
                cols = pl.ds(h * HEAD_DIM, HEAD_DIM)
                xv = src[:, cols]
                nv = xv * lax.rsqrt(_head_sum(xv * xv) * (1.0 / HEAD_DIM) + EPS) * gv
                dst[:, cols] = nv * cv + _partner(nv) * sv

    return _call(
        body, name=name, grid=(s // tr,),
        in_specs=[_row_spec(tr, w), _row_spec(tr, w), _vec_spec(128), _vec_spec(128), _row_spec(tr, 128), _row_spec(tr, 128)],
        out_specs=[_row_spec(tr, w)] * 2, out_shape=[_sds((s, w), F32)] * 2, rider=rider,
    )(q, kv, gq, gk, ctab, stab)


def _qk_norm_rope_bwd(dqs, dks, dvs, q, kv, gq, gk, ctab, stab, name, rider=None):
    s, w = q.shape
    gw = w // N_GROUPS
    tr = _pick(s, 128, 8)
    hpg = gw // HEAD_DIM

    def body(*refs):
        dq_refs, dk_refs, dv_refs = refs[0:3], refs[3:6], refs[6:9]
        q_ref, k_ref, gq_ref, gk_ref, c_ref, s_ref, dq_ref, dkv_ref, dgq_ref, dgk_ref = refs[9:]
        i = pl.program_id(0)
        cv, sv = c_ref[...], s_ref[...]
        for d_refs, src, g_ref, dst, dg_ref in ((dq_refs, q_ref, gq_ref, dq_ref, dgq_ref), (dk_refs, k_ref, gk_ref, dkv_ref, dgk_ref)):
            gv = g_ref[...]
            dg = jnp.zeros((1, HEAD_DIM), F32)
            for g in range(N_GROUPS):
                for h in range(hpg):
                    cols = pl.ds(g * gw + h * HEAD_DIM, HEAD_DIM)
                    dout = d_refs[g][:, pl.ds(h * HEAD_DIM, HEAD_DIM)]
                    dn = dout * cv + _partner(dout * sv)
                    xv = src[:, cols]
                    rstd = lax.rsqrt(_head_sum(xv * xv) * (1.0 / HEAD_DIM) + EPS)
                    xh = xv * rstd
                    dg = dg + _colsum(dn * xh)
                    dxh = dn * gv
                    dst[:, cols] = (rstd * (dxh - xh * (_head_sum(dxh * xh) * (1.0 / HEAD_DIM)))).astype(BF16)
            _acc(dg_ref, dg, i)
        for g in range(N_GROUPS):
            dkv_ref[:, pl.ds(w + g * gw, gw)] = dv_refs[g][...].astype(BF16)

    return _call(
        body, name=name, grid=(s // tr,),
        in_specs=[_row_spec(tr, gw)] * 9 + [_row_spec(tr, w), _row_spec(tr, w), _vec_spec(128), _vec_spec(128),
                                            _row_spec(tr, 128), _row_spec(tr, 128)],
        out_specs=[_row_spec(tr, w), _row_spec(tr, 2 * w), _vec_spec(128), _vec_spec(128)],
        out_shape=[_sds((s, w), BF16), _sds((s, 2 * w), BF16), _sds((1, 128), F32), _sds((1, 128), F32)], rider=rider,
    )(*dqs, *dks, *dvs, q, kv, gq, gk, ctab, stab)


def _rows(j, b, r):
    start = j + r * BLK * b
    return pl.ds(start, BLK, stride=r) if r > 1 else pl.ds(start, BLK)


def _dot_nt(a, b):
    return lax.dot_general(a, b, _DIMS["nt"], preferred_element_type=F32)


def _dot_nn(a, b):
    return lax.dot_general(a, b, _DIMS["nn"], preferred_element_type=F32)


def _band_masks():
    qi = lax.broadcasted_iota(jnp.int32, (BLK, BLK), 0)
    kj = lax.broadcasted_iota(jnp.int32, (BLK, BLK), 1)
    return kj <= qi, kj >= qi


def _attn_fwd(qn, kn, kv, g, name):
    s, w = qn.shape
    r = DILATIONS[g]
    gw = w // N_GROUPS
    cr = min(ATTN_CHUNK, s)
    nb = cr // (BLK * r)
    hp = 1
    cw = hp * HEAD_DIM
    gc = gw // cw
    scale = 1.0 / math.sqrt(HEAD_DIM)

    def body(q_ref, kc_ref, kp_ref, vc_ref, vp_ref, o_ref, l_ref):
        n = pl.program_id(1)
        same_m, prev_m = _band_masks()
        prev_first = jnp.logical_and(prev_m, n > 0)
        for h in range(hp):
            cols = pl.ds(h * HEAD_DIM, HEAD_DIM)
            for j in range(r):
                for b in range(nb):
                    rows = _rows(j, b, r)
                    qv = q_ref[rows, cols].astype(BF16)
                    kc, vc = kc_ref[rows, cols].astype(BF16), vc_ref[rows, cols].astype(BF16)
                    if b > 0:
                        rp = _rows(j, b - 1, r)
                        kp, vp, pm = kc_ref[rp, cols].astype(BF16), vc_ref[rp, cols].astype(BF16), prev_m
                    else:
                        rp = _rows(j, nb - 1, r)
                        kp, vp, pm = kp_ref[rp, cols].astype(BF16), vp_ref[rp, cols].astype(BF16), prev_first
                    sd = jnp.where(same_m, _dot_nt(qv, kc) * scale, NEG)
                    so = jnp.where(pm, _dot_nt(qv, kp) * scale, NEG)
                    m = jnp.maximum(jnp.max(sd, axis=-1, keepdims=True), jnp.max(so, axis=-1, keepdims=True))
                    pd, po = jnp.exp(sd - m), jnp.exp(so - m)
                    den = jnp.sum(pd, axis=-1, keepdims=True) + jnp.sum(po, axis=-1, keepdims=True)
                    ov = (_dot_nn(pd.astype(BF16), vc) + _dot_nn(po.astype(BF16), vp)) / den
                    o_ref[rows, cols] = ov
                    l_ref[rows, cols] = jnp.broadcast_to(m + jnp.log(den), (BLK, HEAD_DIM))

    cur = lambda base: (lambda c, n: (n, base + c))
    prv = lambda base: (lambda c, n: (jnp.maximum(n - 1, 0), base + c))
    qb, kb, vb = g * gc, g * gc, (N_GROUPS + g) * gc
    return _call(
        body, name=name, grid=(gc, s // cr),
        in_specs=[pl.BlockSpec((cr, cw), cur(qb)), pl.BlockSpec((cr, cw), cur(kb)), pl.BlockSpec((cr, cw), prv(kb)),
                  pl.BlockSpec((cr, cw), cur(vb)), pl.BlockSpec((cr, cw), prv(vb))],
        out_specs=[pl.BlockSpec((cr, cw), lambda c, n: (n, c))] * 2, out_shape=[_sds((s, gw), F32)] * 2,
    )(qn, kn, kn, kv, kv)


def _attn_combine(os_, lses, name):
    s, gw = os_[0].shape
    tr = _pick(s, ROW_TILE, 8)

    def body(o0, o1, o2, l0, l1, l2, o_ref, l_ref):
        a, b, c = l0[...], l1[...], l2[...]
        m = jnp.maximum(jnp.maximum(a, b), c)
        ea, eb, ec = jnp.exp(a - m), jnp.exp(b - m), jnp.exp(c - m)
        den = ea + eb + ec
        o_ref[...] = ((ea * o0[...] + eb * o1[...] + ec * o2[...]) / den).astype(BF16)
        l_ref[...] = m + jnp.log(den)

    return _call(body, name=name, grid=(s // tr,), in_specs=[_row_spec(tr, gw)] * 6, out_specs=[_row_spec(tr, gw)] * 2,
                 out_shape=[_sds((s, gw), BF16), _sds((s, gw), F32)])(*os_, *lses)


def _attn_delta(do, o, name):
    s, gw = do.shape
    tr = _pick(s, ROW_TILE, 8)

    def body(do_ref, o_ref, d_ref):
        for h in range(gw // HEAD_DIM):
            cols = pl.ds(h * HEAD_DIM, HEAD_DIM)
            t = jnp.sum(do_ref[:, cols] * o_ref[:, cols].astype(F32), axis=-1, keepdims=True)
            d_ref[:, cols] = jnp.broadcast_to(t, (tr, HEAD_DIM))

    return _call(body, name=name, grid=(s // tr,), in_specs=[_row_spec(tr, gw)] * 2, out_specs=_row_spec(tr, gw),
                 out_shape=_sds((s, gw), F32))(do, o)


def _pair_grads(qv, kv_, vv, dov, lse, delta, mask, scale):
    sc = jnp.where(mask, _dot_nt(qv, kv_) * scale, NEG)
    p = jnp.exp(sc - lse)
    ds = p * (_dot_nt(dov, vv) - delta) * scale
    return p, ds


def _attn_bwd_dq(qn, kn, kv, do, lse, delta, g, name):
    s, w = qn.shape
    r = DILATIONS[g]
    gw = w // N_GROUPS
    cr = min(ATTN_CHUNK, s)
    nb = cr // (BLK * r)
    cw = HEAD_DIM
    gc = gw // cw
    scale = 1.0 / math.sqrt(HEAD_DIM)

    def body(q_ref, kc_ref, kp_ref, vc_ref, vp_ref, do_ref, l_ref, d_ref, dq_ref):
        n = pl.program_id(1)
        same_m, prev_m = _band_masks()
        prev_first = jnp.logical_and(prev_m, n > 0)
        for j in range(r):
            for b in range(nb):
                rows = _rows(j, b, r)
                qv, dov = q_ref[rows, :].astype(BF16), do_ref[rows, :].astype(BF16)
                lse, delta = l_ref[rows, :], d_ref[rows, :]
                kc, vc = kc_ref[rows, :].astype(BF16), vc_ref[rows, :].astype(BF16)
                if b > 0:
                    rp = _rows(j, b - 1, r)
                    kp, vp, pm = kc_ref[rp, :].astype(BF16), vc_ref[rp, :].astype(BF16), prev_m
                else:
                    rp = _rows(j, nb - 1, r)
                    kp, vp, pm = kp_ref[rp, :].astype(BF16), vp_ref[rp, :].astype(BF16), prev_first
                _, dsd = _pair_grads(qv, kc, vc, dov, lse, delta, same_m, scale)
                _, dso = _pair_grads(qv, kp, vp, dov, lse, delta, pm, scale)
                dq_ref[rows, :] = _dot_nn(dsd.astype(BF16), kc) + _dot_nn(dso.astype(BF16), kp)

    cur = lambda base: (lambda c, n: (n, base + c))
    prv = lambda base: (lambda c, n: (jnp.maximum(n - 1, 0), base + c))
    qb, vb = g * gc, (N_GROUPS + g) * gc
    own = pl.BlockSpec((cr, cw), lambda c, n: (n, c))
    return _call(
        body, name=name, grid=(gc, s // cr),
        in_specs=[pl.BlockSpec((cr, cw), cur(qb)), pl.BlockSpec((cr, cw), cur(qb)), pl.BlockSpec((cr, cw), prv(qb)),
                  pl.BlockSpec((cr, cw), cur(vb)), pl.BlockSpec((cr, cw), prv(vb)), own, own, own],
        out_specs=own, out_shape=_sds((s, gw), F32),
    )(qn, kn, kn, kv, kv, do, lse, delta)


def _attn_bwd_dkv(qn, kn, kv, do, lse, delta, g, name):
    s, w = qn.shape
    r = DILATIONS[g]
    gw = w // N_GROUPS
    cr = min(ATTN_CHUNK, s)
    nb = cr // (BLK * r)
    nchunk = s // cr
    cw = HEAD_DIM
    gc = gw // cw
    scale = 1.0 / math.sqrt(HEAD_DIM)

    def body(k_ref, v_ref, qc_ref, qx_ref, doc_ref, dox_ref, lc_ref, lx_ref, dc_ref, dx_ref, dk_ref, dv_ref):
        n = pl.program_id(1)
        same_m, prev_m = _band_masks()
        next_last = jnp.logical_and(prev_m, n < nchunk - 1)
        for j in range(r):
            for b in range(nb):
                rows = _rows(j, b, r)
                kv_, vv = k_ref[rows, :].astype(BF16), v_ref[rows, :].astype(BF16)
                qv, dov = qc_ref[rows, :].astype(BF16), doc_ref[rows, :].astype(BF16)
                pd, dsd = _pair_grads(qv, kv_, vv, dov, lc_ref[rows, :], dc_ref[rows, :], same_m, scale)
                if b < nb - 1:
                    rx = _rows(j, b + 1, r)
                    qx, dox, lx, dlx, xm = qc_ref[rx, :], doc_ref[rx, :], lc_ref[rx, :], dc_ref[rx, :], prev_m
                else:
                    rx = _rows(j, 0, r)
                    qx, dox, lx, dlx, xm = qx_ref[rx, :], dox_ref[rx, :], lx_ref[rx, :], dx_ref[rx, :], next_last
                qx, dox = qx.astype(BF16), dox.astype(BF16)
                po, dso = _pair_grads(qx, kv_, vv, dox, lx, dlx, xm, scale)
                dk_ref[rows, :] = _dot_nn(dsd.T.astype(BF16), qv) + _dot_nn(dso.T.astype(BF16), qx)
                dv_ref[rows, :] = _dot_nn(pd.T.astype(BF16), dov) + _dot_nn(po.T.astype(BF16), dox)

    cur = lambda base: (lambda c, n: (n, base + c))
    nxt = lambda base: (lambda c, n: (jnp.minimum(n + 1, nchunk - 1), base + c))
    qb, vb = g * gc, (N_GROUPS + g) * gc
    blk = lambda f: pl.BlockSpec((cr, cw), f)
    return _call(
        body, name=name, grid=(gc, nchunk),
        in_specs=[blk(cur(qb)), blk(cur(vb)), blk(cur(qb)), blk(nxt(qb)), blk(cur(0)), blk(nxt(0)), blk(cur(0)), blk(nxt(0)),
                  blk(cur(0)), blk(nxt(0))],
        out_specs=[blk(cur(0))] * 2, out_shape=[_sds((s, gw), F32)] * 2,
    )(kn, kv, qn, qn, do, do, lse, lse, delta, delta)


def _mod_proj(sc_all, w, name):
    l, d, ns = w.shape
    tn = _pick(ns, 512)

    def body(c_ref, w_ref, o_ref):
        o_ref[...] = jnp.dot(c_ref[...].astype(BF16), w_ref[...].astype(BF16), preferred_element_type=F32)

    return _call(
        body, name=name, grid=(l, ns // tn),
        in_specs=[pl.BlockSpec((N_DEV, d), lambda a, j: (0, 0)), pl.BlockSpec((None, d, tn), lambda a, j: (a, 0, j))],
        out_specs=pl.BlockSpec((None, N_DEV, tn), lambda a, j: (a, 0, j)), out_shape=_sds((l, N_DEV, ns), F32),
    )(sc_all, w)


def _adamw_math(w, g, m, v):
    m = ADAM_B1 * m + (1.0 - ADAM_B1) * g
    v = ADAM_B2 * v + (1.0 - ADAM_B2) * (g * g)
    m_hat = m / (1.0 - ADAM_B1 ** ADAM_STEP)
    v_hat = v / (1.0 - ADAM_B2 ** ADAM_STEP)
    delta = -ADAM_LR * (m_hat / (jnp.sqrt(v_hat) + ADAM_EPS) + ADAM_WD * w)
    return delta, m, v


def _adamw_big(w, g, m, v, name):
    shape = w.shape
    cols = shape[-1]
    rows = math.prod(shape[:-1])
    tr, tc = _wide_tile(rows, cols)
    w2, g2, m2, v2 = (t.reshape(rows, cols) for t in (w, g, m, v))

    def body(w_ref, g_ref, m_ref, v_ref, d_ref, mo_ref, vo_ref):
        d_ref[...], mo_ref[...], vo_ref[...] = _adamw_math(w_ref[...], g_ref[...], m_ref[...], v_ref[...])

    spec = pl.BlockSpec((tr, tc), lambda i, j: (i, j))
    outs = _call(body, name=name, grid=(rows // tr, cols // tc), in_specs=[spec] * 4, out_specs=[spec] * 3,
                 out_shape=[_sds((rows, cols), F32)] * 3)(w2, g2, m2, v2)
    return [t.reshape(shape) for t in outs]


def _adamw_mod(w, sct, dm, m, v, name):
    l, d, ns = w.shape
    tr, tc = _pick(d, 512, 8), _pick(ns, 1024)

    def body(w_ref, c_ref, dm_ref, m_ref, v_ref, g_ref, d_ref, mo_ref, vo_ref):
        cv, dv = c_ref[...].astype(BF16).astype(F32), dm_ref[...].astype(BF16).astype(F32)
        g = jnp.zeros((tr, tc), F32)
        for e in range(N_DEV):
            g = g + cv[:, e:e + 1] * dv[e:e + 1, :]
        g_ref[...] = g
        d_ref[...], mo_ref[...], vo_ref[...] = _adamw_math(w_ref[...], g, m_ref[...], v_ref[...])

    spec = pl.BlockSpec((None, tr, tc), lambda a, i, j: (a, i, j))
    return _call(
        body, name=name, grid=(l, d // tr, ns // tc),
        in_specs=[spec, pl.BlockSpec((tr, N_DEV), lambda a, i, j: (i, 0)), pl.BlockSpec((None, N_DEV, tc), lambda a, i, j: (a, 0, j)),
                  spec, spec],
        out_specs=[spec] * 4, out_shape=[_sds((l, d, ns), F32)] * 4,
    )(w, sct, dm, m, v)


def _adamw_small(ws, gs, ms, vs, name):
    n = len(ws)

    def body(*refs):
        w_r, g_r, m_r, v_r = refs[0:n], refs[n:2 * n], refs[2 * n:3 * n], refs[3 * n:4 * n]
        d_o, m_o, v_o = refs[4 * n:5 * n], refs[5 * n:6 * n], refs[6 * n:7 * n]
        for k in range(n):
            d_o[k][...], m_o[k][...], v_o[k][...] = _adamw_math(w_r[k][...], g_r[k][...], m_r[k][...], v_r[k][...])

    vm = pl.BlockSpec(memory_space=pltpu.VMEM)
    shapes = [_sds(w.shape, F32) for w in ws]
    outs = pl.pallas_call(body, name=name, in_specs=[vm] * (4 * n), out_specs=[vm] * (3 * n), out_shape=shapes * 3,
                          interpret=False)(*ws, *gs, *ms, *vs)
    return outs[0:n], outs[n:2 * n], outs[2 * n:3 * n]


def _sum_rows(a, name):
    n, v = a.shape
    tc = _pick(v, 8192)

    def body(a_ref, o_ref):
        acc = a_ref[0:1, :]
        for e in range(1, n):
            acc = acc + a_ref[e:e + 1, :]
        o_ref[...] = acc

    return _call(body, name=name, grid=(v // tc,), in_specs=[pl.BlockSpec((n, tc), lambda j: (0, j))],
                 out_specs=pl.BlockSpec((1, tc), lambda j: (0, j)), out_shape=_sds((1, v), F32))(a)


def _cast_into_full(w2d, kind, slot_arr, name):
    rows, cols = w2d.shape
    tr, tc = _wide_tile(rows, cols)
    nr, nc = rows // tr, cols // tc

    def body(s_ref, w_ref, o_ref):
        o_ref[...] = w_ref[...].astype(BF16)

    if kind == "col":
        o_map = lambda i, j, s_ref: (i, s_ref[0] * nc + j)
    else:
        o_map = lambda i, j, s_ref: (s_ref[0] * nr + i, j)
    return _call(body, name=name, grid=(nr, nc), in_specs=[pl.BlockSpec((tr, tc), lambda i, j, s_ref: (i, j))],
                 out_specs=pl.BlockSpec((tr, tc), o_map), out_shape=_sds(_full_shape(kind, (rows, cols)), BF16), nsp=1)(slot_arr, w2d)


def _place():
    x, y, c = lax.axis_index("x"), lax.axis_index("y"), lax.axis_index("c")
    chips = [(1 - x, y), (x, 1 - y), (1 - x, 1 - y)]
    return x, y, c, chips


def _remote(src, dst, send_sem, recv_sem, to):
    return pltpu.make_async_remote_copy(src_ref=src, dst_ref=dst, send_sem=send_sem, recv_sem=recv_sem, device_id=to,
                                        device_id_type=MESH)


def _allgather8(a, name):
    m_per, n = a.shape

    def body(x_ref, out_ref, send_sems, recv_sems, local_sem):
        x, y, c, chips = _place()
        me, sibling = (x, y, c), (x, y, 1 - c)

        def rows(px, py, pc):
            return out_ref.at[pl.ds((4 * px + 2 * py + pc) * m_per, m_per), :]

        def copy(k, block, to, src=None):
            return _remote(rows(*block) if src is None else src, rows(*block), send_sems.at[k], recv_sems.at[k], to)

        mine = pltpu.make_async_copy(x_ref, rows(*me), local_sem)
        mine.start()
        first = [copy(0, me, sibling, src=x_ref)]
        first += [copy(1 + j, me, (*chip, c), src=x_ref) for j, chip in enumerate(chips)]
        for cp in first:
            cp.start()
        passed = [copy(4 + j, (*chip, c), sibling) for j, chip in enumerate(chips)]
        for j, chip in enumerate(chips):
            copy(1 + j, (*chip, c), me).wait_recv()
            passed[j].start()
        copy(0, sibling, me).wait_recv()
        for j, chip in enumerate(chips):
            copy(4 + j, (*chip, 1 - c), me).wait_recv()
        for cp in first + passed:
            cp.wait_send()
        mine.wait()

    return pl.pallas_call(
        body, name=name, out_shape=_sds((N_DEV * m_per, n), a.dtype),
        in_specs=[pl.BlockSpec(memory_space=pltpu.VMEM)], out_specs=pl.BlockSpec(memory_space=pltpu.VMEM),
        scratch_shapes=[pltpu.SemaphoreType.DMA((7,)), pltpu.SemaphoreType.DMA((7,)), pltpu.SemaphoreType.DMA],
        interpret=False,
    )(a)


def _region(ref, kind, shard_shape, slot, half, piece=(0, 1)):
    r, cs = shard_shape
    hr = r // 2
    n = hr // piece[1]
    start = half * hr + piece[0] * n
    if kind == "col":
        return ref.at[pl.ds(start, n), pl.ds(slot * cs, cs)]
    return ref.at[pl.ds(slot * r + start, n), :]


def _full_shape(kind, shard_shape):
    r, cs = shard_shape
    return (r, 4 * cs) if kind == "col" else (4 * r, cs)


_HBM = pl.BlockSpec(memory_space=pltpu.HBM)


def _in_place(arrays, n_sem, copies):
    return _Rider(arrays, [_sds(t.shape, t.dtype) for t in arrays], {a: a for a in range(len(arrays))}, n_sem, copies)


def _gather_chips(fulls, kinds, shapes, piece=(0, 1)):
    n = len(fulls)

    def copies(ins, outs, send_sems, recv_sems):
        x, y, c, chips = _place()
        me_slot = 2 * x + y
        sends, recvs = [], []
        for a in range(n):
            for j, (cx, cy) in enumerate(chips):
                k = 3 * a + j
                sends.append(_remote(_region(ins[a], kinds[a], shapes[a], me_slot, c, piece),
                                     _region(outs[a], kinds[a], shapes[a], me_slot, c, piece),
                                     send_sems.at[k], recv_sems.at[k], (cx, cy, c)))
                land = _region(outs[a], kinds[a], shapes[a], 2 * cx + cy, c, piece)
                recvs.append(_remote(land, land, send_sems.at[k], recv_sems.at[k], (cx, cy, c)))
        return sends, recvs

    return _in_place(fulls, 3 * n, copies)


def _gather_pair(fulls, kinds, shapes):
    n = len(fulls)

    def copies(ins, outs, send_sems, recv_sems):
        x, y, c, chips = _place()
        sibling = (x, y, 1 - c)
        sends, recvs = [], []
        for a in range(n):
            for j, (cx, cy) in enumerate(chips):
                k = 3 * a + j
                sends.append(_remote(_region(ins[a], kinds[a], shapes[a], 2 * cx + cy, c), _region(outs[a], kinds[a], shapes[a], 2 * cx + cy, c),
                                     send_sems.at[k], recv_sems.at[k], sibling))
                land = _region(outs[a], kinds[a], shapes[a], 2 * cx + cy, 1 - c)
                recvs.append(_remote(land, land, send_sems.at[k], recv_sems.at[k], sibling))
        return sends, recvs

    return _in_place(fulls, 3 * n, copies)


def _rs_pair(grads, kinds, shapes):
    n = len(grads)

    def copies(ins, outs, send_sems, recv_sems):
        x, y, c, _ = _place()
        sibling = (x, y, 1 - c)
        sends, recvs = [], []
        for a in range(n):
            for slot in range(4):
                k = 4 * a + slot
                sends.append(_remote(_region(ins[a], kinds[a], shapes[a], slot, 1 - c), outs[a].at[slot], send_sems.at[k],
                                     recv_sems.at[k], sibling))
                recvs.append(_remote(_region(ins[a], kinds[a], shapes[a], slot, c), outs[a].at[slot], send_sems.at[k],
                                     recv_sems.at[k], sibling))
        return sends, recvs

    return _Rider(grads, [_sds((4, s[0] // 2, s[1]), F32) for s in shapes], {}, 4 * n, copies)


def _rs_pair_add(grad, recv, kind, shape, c_arr, name):
    r, cs = shape
    hr = r // 2
    tr, tc = _wide_tile(hr, cs)
    nr, nc = hr // tr, cs // tc

    def body(c_ref, g_ref, r_ref, o_ref):
        o_ref[...] = (g_ref[...] + r_ref[...]).astype(BF16)

    if kind == "col":
        g_map = lambda s_, i, j, c_ref: (c_ref[0] * nr + i, s_ * nc + j)
    else:
        g_map = lambda s_, i, j, c_ref: (s_ * 2 * nr + c_ref[0] * nr + i, j)
    own = pl.BlockSpec((None, tr, tc), lambda s_, i, j, c_ref: (s_, i, j))
    return _call(body, name=name, grid=(4, nr, nc), in_specs=[pl.BlockSpec((tr, tc), g_map), own], out_specs=own,
                 out_shape=_sds((4, hr, cs), BF16), nsp=1)(c_arr, grad, recv)


def _rs_chips(parts):
    n = len(parts)

    def copies(ins, outs, send_sems, recv_sems):
        x, y, c, chips = _place()
        sends, recvs = [], []
        for a in range(n):
            for j, (cx, cy) in enumerate(chips):
                k = 3 * a + j
                cp = _remote(ins[a].at[2 * cx + cy], outs[a].at[j], send_sems.at[k], recv_sems.at[k], (cx, cy, c))
                sends.append(cp)
                recvs.append(cp)
        return sends, recvs

    return _Rider(parts, [_sds((3,) + p.shape[1:], BF16) for p in parts], {}, 3 * n, copies)


def _rs_chip_add(part, recv, slot_arr, c_arr, dest, layer, out_shape, name):
    _, hr, cs = part.shape
    tr, tc = _wide_tile(hr, cs)
    nr = hr // tr

    def body(s_ref, c_ref, p_ref, r0_ref, r1_ref, r2_ref, *rest):
        o_ref = rest[-1]
        o_ref[...] = ((p_ref[...].astype(F32) + r0_ref[...].astype(F32)) + r1_ref[...].astype(F32)) + r2_ref[...].astype(F32)

    rk = lambda k: pl.BlockSpec((None, tr, tc), lambda i, j, s_ref, c_ref: (k, i, j))
    if layer is None:
        o_spec = pl.BlockSpec((tr, tc), lambda i, j, s_ref, c_ref: (c_ref[0] * nr + i, j))
    else:
        o_spec = pl.BlockSpec((None, tr, tc), lambda i, j, s_ref, c_ref: (layer, c_ref[0] * nr + i, j))
    in_specs = [pl.BlockSpec((None, tr, tc), lambda i, j, s_ref, c_ref: (s_ref[0], i, j)), rk(0), rk(1), rk(2)]
    args = [slot_arr, c_arr, part, recv, recv, recv]
    aliases = {}
    if dest is not None:
        in_specs.append(pl.BlockSpec(memory_space=pl.ANY))
        args.append(dest)
        aliases = {6: 0}
    return pl.pallas_call(
        body, name=name,
        grid_spec=pltpu.PrefetchScalarGridSpec(num_scalar_prefetch=2, grid=(nr, cs // tc), in_specs=in_specs, out_specs=o_spec),
        out_shape=_sds(out_shape, F32), input_output_aliases=aliases,
        compiler_params=pltpu.CompilerParams(dimension_semantics=("arbitrary",) * 2, vmem_limit_bytes=VMEM_LIMIT_MB << 20),
        interpret=False,
    )(*args)


def _rs_pair_share(shards, name):
    n = len(shards)
    views = []
    for a, t in enumerate(shards):
        views += [(a, None)] if t.ndim == 2 else [(a, l) for l in range(t.shape[0])]
    nv = len(views)

    def body(*refs):
        ins, outs = refs[0:n], refs[n:2 * n]
        send_sems, recv_sems = refs[2 * n:]
        x, y, c, _ = _place()
        sibling = (x, y, 1 - c)

        def rows(ref_list, k, half):
            a, layer = views[k]
            ref = ref_list[a] if layer is None else ref_list[a].at[layer]
            hr = ref.shape[0] // 2
            return ref.at[pl.ds(half * hr, hr), :]

        sent = []
        for k in range(nv):
            cp = _remote(rows(ins, k, c), rows(outs, k, c), send_sems.at[k], recv_sems.at[k], sibling)
            cp.start()
            sent.append(cp)
        for k in range(nv):
            _remote(rows(ins, k, 1 - c), rows(outs, k, 1 - c), send_sems.at[k], recv_sems.at[k], sibling).wait_recv()
        for cp in sent:
            cp.wait_send()

    return pl.pallas_call(
        body, name=name, out_shape=[_sds(t.shape, F32) for t in shards], in_specs=[_HBM] * n, out_specs=[_HBM] * n,
        input_output_aliases={a: a for a in range(n)},
        scratch_shapes=[pltpu.SemaphoreType.DMA((nv,)), pltpu.SemaphoreType.DMA((nv,))], interpret=False,
    )(*shards)


def _pad_to(v, mult):
    n = v.shape[0]
    return jnp.pad(v, (0, (-n) % mult))


def kernel(x, c, positions, mod_w, mod_b, norm_mix_g, norm_ffn_g, conv_pw1_w, conv_pw1_b, conv_dw_w, conv_dw_b, conv_ln_g, conv_ln_b, conv_pw2_w, conv_pw2_b, kv_mod_w, kv_mod_b, kv_norm_g, w_kv, k_norm_g, w_q, q_norm_g, w_o, ffn_up_w, ffn_dw_w, ffn_dw_b, ffn_down_w, loss_target, m_mod_w, m_mod_b, m_norm_mix_g, m_norm_ffn_g, m_conv_pw1_w, m_conv_pw1_b, m_conv_dw_w, m_conv_dw_b, m_conv_ln_g, m_conv_ln_b, m_conv_pw2_w, m_conv_pw2_b, m_kv_mod_w, m_kv_mod_b, m_kv_norm_g, m_w_kv, m_k_norm_g, m_w_q, m_q_norm_g, m_w_o, m_ffn_up_w, m_ffn_dw_w, m_ffn_dw_b, m_ffn_down_w, v_mod_w, v_mod_b, v_norm_mix_g, v_norm_ffn_g, v_conv_pw1_w, v_conv_pw1_b, v_conv_dw_w, v_conv_dw_b, v_conv_ln_g, v_conv_ln_b, v_conv_pw2_w, v_conv_pw2_b, v_kv_mod_w, v_kv_mod_b, v_kv_norm_g, v_w_kv, v_k_norm_g, v_w_q, v_q_norm_g, v_w_o, v_ffn_up_w, v_ffn_dw_w, v_ffn_dw_b, v_ffn_down_w):
    _, s, d = x.shape
    f = ffn_dw_b.shape[1]
    qw = w_q.shape[2] * 4
    ax, ay, ac = lax.axis_index("x"), lax.axis_index("y"), lax.axis_index("c")
    slot = 2 * ax + ay
    me8 = 4 * ax + 2 * ay + ac
    slot_arr = jnp.reshape(slot, (1,)).astype(jnp.int32)
    c_arr = jnp.reshape(ac, (1,)).astype(jnp.int32)
    x2 = x.reshape(s, d)
    target = loss_target.reshape(s, d)
    row = lambda v: v.reshape(1, -1)

    c_all = _allgather8(c.reshape(8, d // 8), "gather_c").reshape(N_DEV, d)
    sc_all = jax.nn.silu(c_all)
    mod_part = _mod_proj(sc_all, mod_w, "mod_proj")
    kvm_part = _mod_proj(sc_all, kv_mod_w[None], "kvmod_proj")
    nm, nk = mod_part.shape[2], kvm_part.shape[2]
    small_sharded = [conv_pw1_b, conv_dw_w, conv_dw_b, conv_ln_g, conv_ln_b, conv_pw2_b, ffn_dw_w]
    pack = jnp.concatenate([mod_part.reshape(-1), kvm_part.reshape(-1)] + [t.reshape(-1) for t in small_sharded])
    plen = pack.shape[0]
    pack = _pad_to(pack, 1024)
    gathered = _allgather8(pack.reshape(8, -1), "gather_mod").reshape(4, 2, -1)[:, 0, :plen]
    off = 0

    def take(n_el):
        nonlocal off
        out = lax.slice_in_dim(gathered, off, off + n_el, axis=1)
        off += n_el
        return out

    mod_g = take(2 * N_DEV * nm).reshape(4, 2, N_DEV, nm)
    kvm_g = take(N_DEV * nk).reshape(4, N_DEV, nk)
    mine = lambda t, axis: lax.dynamic_index_in_dim(t, me8, axis=axis, keepdims=False)
    mod_vec = jnp.transpose(mine(mod_g, 2), (1, 0, 2)).reshape(2, 4 * nm) + mod_b
    kvm_vec = mine(kvm_g, 1).reshape(4 * nk) + kv_mod_b
    pw1_b_full = take(conv_pw1_b.shape[1]).reshape(1, -1)
    dw_w_full = jnp.transpose(take(CONV_K * (d // 4)).reshape(4, CONV_K, d // 4), (1, 0, 2)).reshape(CONV_K, d)
    dw_b_full, ln_g_full, ln_b_full, pw2_b_full = (take(d // 4).reshape(1, d) for _ in range(4))
    fdw_full = jnp.transpose(take(2 * FFN_K * (f // 4)).reshape(4, 2, FFN_K, f // 4), (1, 2, 0, 3)).reshape(2, FFN_K, f)
    mods = [[row(mod_vec[l, k * d:(k + 1) * d]) for k in range(6)] for l in range(2)]
    kv_sh, kv_sc = row(kvm_vec[:d]), row(kvm_vec[d:])
    zero_d = jnp.zeros((1, d), F32)

    big = [("pw1", conv_pw1_w[0], "col"), ("pw2", conv_pw2_w[0], "row"), ("wkv", w_kv, "col"), ("wq", w_q[0], "col"),
           ("wo", w_o[0], "col"), ("up0", ffn_up_w[0], "col"), ("up1", ffn_up_w[1], "col"), ("dn0", ffn_down_w[0], "row"),
           ("dn1", ffn_down_w[1], "row")]
    names = [b[0] for b in big]
    kinds = [b[2] for b in big]
    shard_shapes = [b[1].shape for b in big]
    own = {b[0]: _cast_into_full(b[1], b[2], slot_arr, "cast_" + b[0]) for b in big}
    meta = dict(zip(names, zip(kinds, shard_shapes)))
    full = {}

    def stage(make, group, src, **kw):
        return make([src[k] for k in group], [meta[k][0] for k in group], [meta[k][1] for k in group], **kw)

    grp_a, grp_b, grp_c, grp_d = ["pw1", "pw2"], ["up0", "dn0"], ["wkv", "wq", "wo"], ["up1", "dn1"]
    landed_a = dict(zip(grp_a, _run_rider(stage(_gather_chips, grp_a, own), "gather_a_chips")))
    full.update(zip(grp_a, _run_rider(stage(_gather_pair, grp_a, landed_a), "gather_a_pair")))
    chips_c, chips_d = (stage(_gather_chips, g_, own) for g_ in (grp_c, grp_d))

    def pair_stage(group, chips_rider):
        return stage(_gather_pair, group, dict(zip(group, chips_rider.results)))

    def ffn_fwd(l, x_in, h, tag, rider_up=None, rider_down=None):
        u = _matmul(h, full["up%d" % l], "nn", F32, "mm_up" + tag, rider=rider_up)
        z = _ffn_act_fwd(u, fdw_full[l], row(ffn_dw_b[l]), "ffn_act" + tag)
        y = _matmul(z, full["dn%d" % l], "nn", F32, "mm_down" + tag, rider=rider_down)
        return u, z, y

    sh_m0, sc_m0, g_m0, sh_f0, sc_f0, g_f0 = mods[0]
    sh_m1, sc_m1, g_m1, sh_f1, sc_f1, g_f1 = mods[1]
    gmix0, gmix1, gffn0, gffn1 = row(norm_mix_g[0]), row(norm_mix_g[1]), row(norm_ffn_g[0]), row(norm_ffn_g[1])
    (h0,) = _resid_mod(x2, None, [(gmix0, sc_m0, sh_m0)], None, "mod_in")
    chips_b0 = stage(_gather_chips, grp_b, own, piece=(0, 2))
    u0 = _matmul(h0, full["pw1"], "nn", F32, "mm_pw1", rider=chips_b0)
    glu = _glu_fwd(u0, pw1_b_full, "glu")
    chips_b = stage(_gather_chips, grp_b, dict(zip(grp_b, chips_b0.results)), piece=(1, 2))
    dwc = _dwconv_fwd(glu, dw_w_full, dw_b_full, "dwconv", rider=chips_b)
    act = _ln_silu_fwd(dwc, ln_g_full, ln_b_full, "ln_silu")
    pair_b = pair_stage(grp_b, chips_b)
    y0 = _matmul(act, full["pw2"], "nn", F32, "mm_pw2", rider=pair_b)
    full.update(zip(grp_b, pair_b.results))
    x1, hf0 = _resid_mod(x2, (y0, pw2_b_full, g_m0), [(gffn0, sc_f0, sh_f0)], None, "resid_conv")
    uf0 = _matmul(hf0, full["up0"], "nn", F32, "mm_up0", rider=chips_c)
    zf0 = _ffn_act_fwd(uf0, fdw_full[0], row(ffn_dw_b[0]), "ffn_act0")
    pair_c = pair_stage(grp_c, chips_c)
    yf0 = _matmul(zf0, full["dn0"], "nn", F32, "mm_down0", rider=pair_c)
    full.update(zip(grp_c, pair_c.results))
    gkv = row(kv_norm_g)
    xa, hk, hq = _resid_mod(x1, (yf0, zero_d, g_f0), [(gkv, kv_sc, kv_sh), (gmix1, sc_m1, sh_m1)], None, "resid_ffn0")
    kvp = _matmul(hk, full["wkv"], "nn", F32, "mm_kv")
    qp = _matmul(hq, full["wq"], "nn", F32, "mm_q")
    inv_freq = ROPE_THETA ** (-jnp.arange(0, ROT_DIM, 2, dtype=F32) / ROT_DIM)
    half = ROT_DIM // 2
    freq_l = jnp.concatenate([inv_freq, inv_freq, jnp.zeros((HEAD_DIM - ROT_DIM,), F32)]).reshape(1, HEAD_DIM)
    sign_l = jnp.concatenate([-jnp.ones((half,), F32), jnp.ones((half,), F32), jnp.zeros((HEAD_DIM - ROT_DIM,), F32)]).reshape(1, HEAD_DIM)
    ctab, stab = _rope_tables(positions.reshape(s, 1), freq_l, sign_l, "rope_tables")
    gq, gk = row(q_norm_g[0]), row(k_norm_g)
    qn, kn = _qk_norm_rope_fwd(qp, kvp, gq, gk, ctab, stab, "qk_norm_rope", rider=chips_d)
    og, lg = zip(*[_attn_fwd(qn, kn, kvp, g, "attn_fwd%d" % g) for g in range(N_GROUPS)])
    o_mix, lse = _attn_combine(og, lg, "attn_combine")
    pair_d = pair_stage(grp_d, chips_d)
    ya = _matmul(o_mix, full["wo"], "nn", F32, "mm_o", rider=pair_d)
    full.update(zip(grp_d, pair_d.results))
    xb, hf1 = _resid_mod(xa, (ya, zero_d, g_m1), [(gffn1, sc_f1, sh_f1)], None, "resid_attn")
    uf1, zf1, yf1 = ffn_fwd(1, xb, hf1, "1")
    _, dxo, loss_cols = _resid_mod(xb, (yf1, zero_d, g_f1), [], target, "resid_loss")
    loss = lax.psum(jnp.sum(loss_cols), ("x", "y", "c"))

    gbig = {}

    parts, recv2 = {}, {}

    def rs_pair(group):
        return _rs_pair([gbig[k] for k in group], [meta[k][0] for k in group], [meta[k][1] for k in group])

    def rs_chips(group, pair_results):
        for k, r_ in zip(group, pair_results):
            parts[k] = _rs_pair_add(gbig[k], r_, meta[k][0], meta[k][1], c_arr, "rs_pair_add_" + k)
        return _rs_chips([parts[k] for k in group])

    rs_a, rs_b, rs_c, rs_d = ["dn1", "up1"], ["wo", "wq", "wkv"], ["dn0", "up0"], ["pw2", "pw1"]

    def ffn_bwd(l, dy, u, z, h, tag, rider=None):
        dz = _matmul(dy, full["dn%d" % l], "nt", F32, "mm_down_dx" + tag)
        gbig["dn%d" % l] = _matmul(z, dy, "tn", F32, "mm_down_dw" + tag)
        du, dfw, dfb = _ffn_act_bwd(dz, u, fdw_full[l], row(ffn_dw_b[l]), "ffn_act_bwd" + tag, rider=rider)
        dh = _matmul(du, full["up%d" % l], "nt", F32, "mm_up_dx" + tag)
        gbig["up%d" % l] = _matmul(h, du, "tn", F32, "mm_up_dw" + tag)
        return dh, dfw, dfb

    dxb0, dyf1, dg_f1, _ = _bwd_step(dxo, xb, [], (yf1, zero_d, g_f1), "bwd_loss")
    dhf1, dfw1, dfb1 = ffn_bwd(1, dyf1, uf1, zf1, hf1, "1")
    pair_a = rs_pair(rs_a)
    dxb, dsh_f1, dsc_f1, dgffn1, dya, dg_m1, _ = _bwd_step(dxb0, xb, [(dhf1, gffn1, sc_f1)], (ya, zero_d, g_m1), "bwd_attn_out",
                                                        rider=pair_a)
    do = _matmul(dya, full["wo"], "nt", F32, "mm_o_dx")
    gbig["wo"] = _matmul(o_mix, dya, "tn", F32, "mm_o_dw")
    delta = _attn_delta(do, o_mix, "attn_delta")
    dqs = [_attn_bwd_dq(qn, kn, kvp, do, lse, delta, g, "attn_dq%d" % g) for g in range(N_GROUPS)]
    dks, dvs = zip(*[_attn_bwd_dkv(qn, kn, kvp, do, lse, delta, g, "attn_dkv%d" % g) for g in range(N_GROUPS)])
    ride_a = rs_chips(rs_a, pair_a.results)
    dqp, dkvp, dgq, dgk = _qk_norm_rope_bwd(dqs, dks, dvs, qp, kvp, gq, gk, ctab, stab, "qk_norm_rope_bwd", rider=ride_a)
    recv2.update(zip(rs_a, ride_a.results))
    dhq = _matmul(dqp, full["wq"], "nt", F32, "mm_q_dx")
    gbig["wq"] = _matmul(hq, dqp, "tn", F32, "mm_q_dw")
    dhk = _matmul(dkvp, full["wkv"], "nt", F32, "mm_kv_dx")
    gbig["wkv"] = _matmul(hk, dkvp, "tn", F32, "mm_kv_dw")
    pair_b = rs_pair(rs_b)
    (dxa, dsh_kv, dsc_kv, dgkv, dsh_m1, dsc_m1, dgmix1, dyf0, dg_f0, _) = _bwd_step(
        dxb, xa, [(dhk, gkv, kv_sc), (dhq, gmix1, sc_m1)], (yf0, zero_d, g_f0), "bwd_kvq", rider=pair_b)
    ride_b = rs_chips(rs_b, pair_b.results)
    dhf0, dfw0, dfb0 = ffn_bwd(0, dyf0, uf0, zf0, hf0, "0", rider=ride_b)
    recv2.update(zip(rs_b, ride_b.results))
    pair_c = rs_pair(rs_c)
    dx1, dsh_f0, dsc_f0, dgffn0, dy0, dg_m0, dpw2_b = _bwd_step(dxa, x1, [(dhf0, gffn0, sc_f0)], (y0, pw2_b_full, g_m0), "bwd_conv_out",
                                                             rider=pair_c)
    dact = _matmul(dy0, full["pw2"], "nt", F32, "mm_pw2_dx")
    gbig["pw2"] = _matmul(act, dy0, "tn", F32, "mm_pw2_dw")
    ddwc, dln_g, dln_b = _ln_silu_bwd(dact, dwc, ln_g_full, ln_b_full, "ln_silu_bwd")
    ride_c = rs_chips(rs_c, pair_c.results)
    dglu, ddw_w, ddw_b = _dwconv_bwd(ddwc, glu, dw_w_full, "dwconv_bwd", rider=ride_c)
    recv2.update(zip(rs_c, ride_c.results))
    du0, dpw1_b = _glu_bwd(dglu, u0, pw1_b_full, "glu_bwd")
    dh0 = _matmul(du0, full["pw1"], "nt", F32, "mm_pw1_dx")
    gbig["pw1"] = _matmul(h0, du0, "tn", F32, "mm_pw1_dw")
    grad_x, dsh_m0, dsc_m0, dgmix0 = _bwd_step(dx1, x2, [(dh0, gmix0, sc_m0)], None, "bwd_in")

    dmod = [jnp.concatenate([dsh_m0, dsc_m0, dg_m0, dsh_f0, dsc_f0, dg_f0], axis=1),
            jnp.concatenate([dsh_m1, dsc_m1, dg_m1, dsh_f1, dsc_f1, dg_f1], axis=1)]
    dkvm = jnp.concatenate([dsh_kv, dsc_kv], axis=1)
    per_ex = [dmod[0], dmod[1], dkvm]
    summed = [dgmix0, dgmix1, dgffn0, dgffn1, dpw1_b, ddw_w, ddw_b, dln_g, dln_b, dpw2_b, dgkv, dgk, dgq, dfw0, dfw1, dfb0, dfb1]
    vec = jnp.concatenate([t.reshape(-1) for t in per_ex + summed])
    vlen = vec.shape[0]
    vec = _pad_to(vec, 1024)
    vall = _allgather8(vec.reshape(8, -1), "gather_small").reshape(N_DEV, -1)
    vsum = _sum_rows(vall, "sum_small")[0]
    n_pe = sum(t.size for t in per_ex)
    dm_all = vall[:, :n_pe]
    off2 = n_pe
    sums = []
    for t in summed:
        sums.append(vsum[off2:off2 + t.size].reshape(t.shape))
        off2 += t.size
    (s_gmix0, s_gmix1, s_gffn0, s_gffn1, s_pw1_b, s_dw_w, s_dw_b, s_ln_g, s_ln_b, s_pw2_b, s_gkv, s_gk, s_gq, s_fw0, s_fw1,
     s_fb0, s_fb1) = sums
    shard_cols = lambda t, width: lax.dynamic_slice_in_dim(t, slot * width, width, axis=t.ndim - 1)
    dm_mod = jnp.stack([shard_cols(dm_all[:, l * 6 * d:(l + 1) * 6 * d], nm) for l in range(2)])
    dm_kv = shard_cols(dm_all[:, 12 * d:14 * d], nk)[None]
    sct = jnp.transpose(sc_all)

    recv2.update(zip(rs_d, _run_rider(rs_chips(rs_d, _run_rider(rs_pair(rs_d), "rs_pair_d")), "rs_chips_d")))
    reduced = {}
    for nme, shp in zip(names, shard_shapes):
        p_, r_ = parts[nme], recv2[nme]
        if nme in ("up0", "up1", "dn0", "dn1"):
            key, layer = nme[:-1], int(nme[-1])
            reduced[key] = _rs_chip_add(p_, r_, slot_arr, c_arr, reduced.get(key), layer, (2,) + shp, "rs_chip_add_" + nme)
        else:
            reduced[nme] = _rs_chip_add(p_, r_, slot_arr, c_arr, None, None, shp, "rs_chip_add_" + nme)
    g_pw1, g_pw2, g_wkv, g_wq, g_wo, g_up, g_dn = _rs_pair_share(
        [reduced[k] for k in ("pw1", "pw2", "wkv", "wq", "wo", "up", "dn")], "rs_pair_share")

    grads, deltas, new_m, new_v = {}, {}, {}, {}

    def put(nme, g_, res):
        grads[nme] = g_
        deltas[nme], new_m[nme], new_v[nme] = res

    for nme, g_, w_, m_, v_ in (("conv_pw1_w", g_pw1[None], conv_pw1_w, m_conv_pw1_w, v_conv_pw1_w),
                                ("conv_pw2_w", g_pw2[None], conv_pw2_w, m_conv_pw2_w, v_conv_pw2_w),
                                ("w_kv", g_wkv, w_kv, m_w_kv, v_w_kv), ("w_q", g_wq[None], w_q, m_w_q, v_w_q),
                                ("w_o", g_wo[None], w_o, m_w_o, v_w_o), ("ffn_up_w", g_up, ffn_up_w, m_ffn_up_w, v_ffn_up_w),
                                ("ffn_down_w", g_dn, ffn_down_w, m_ffn_down_w, v_ffn_down_w)):
        put(nme, g_, _adamw_big(w_, g_, m_, v_, "adamw_" + nme))
    g_, *res = _adamw_mod(mod_w, sct, dm_mod, m_mod_w, v_mod_w, "adamw_mod_w")
    put("mod_w", g_, res)
    g_, *res = _adamw_mod(kv_mod_w[None], sct, dm_kv, m_kv_mod_w[None], v_kv_mod_w[None], "adamw_kv_mod_w")
    put("kv_mod_w", g_[0], [t[0] for t in res])

    dm_sum = vsum[:n_pe]
    small = [
        ("mod_b", dm_sum[:12 * d].reshape(2, 6 * d), mod_b, m_mod_b, v_mod_b),
        ("norm_mix_g", jnp.concatenate([s_gmix0, s_gmix1], axis=0), norm_mix_g, m_norm_mix_g, v_norm_mix_g),
        ("norm_ffn_g", jnp.concatenate([s_gffn0, s_gffn1], axis=0), norm_ffn_g, m_norm_ffn_g, v_norm_ffn_g),
        ("conv_pw1_b", shard_cols(s_pw1_b, conv_pw1_b.shape[1]), conv_pw1_b, m_conv_pw1_b, v_conv_pw1_b),
        ("conv_dw_w", shard_cols(s_dw_w, d // 4)[None], conv_dw_w, m_conv_dw_w, v_conv_dw_w),
        ("conv_dw_b", shard_cols(s_dw_b, d // 4), conv_dw_b, m_conv_dw_b, v_conv_dw_b),
        ("conv_ln_g", shard_cols(s_ln_g, d // 4), conv_ln_g, m_conv_ln_g, v_conv_ln_g),
        ("conv_ln_b", shard_cols(s_ln_b, d // 4), conv_ln_b, m_conv_ln_b, v_conv_ln_b),
        ("conv_pw2_b", shard_cols(s_pw2_b, d // 4), conv_pw2_b, m_conv_pw2_b, v_conv_pw2_b),
        ("kv_mod_b", dm_sum[12 * d:14 * d], kv_mod_b, m_kv_mod_b, v_kv_mod_b),
        ("kv_norm_g", s_gkv.reshape(-1), kv_norm_g, m_kv_norm_g, v_kv_norm_g),
        ("k_norm_g", s_gk.reshape(-1), k_norm_g, m_k_norm_g, v_k_norm_g),
        ("q_norm_g", s_gq, q_norm_g, m_q_norm_g, v_q_norm_g),
        ("ffn_dw_w", shard_cols(jnp.stack([s_fw0, s_fw1]), f // 4), ffn_dw_w, m_ffn_dw_w, v_ffn_dw_w),
        ("ffn_dw_b", jnp.concatenate([s_fb0, s_fb1], axis=0), ffn_dw_b, m_ffn_dw_b, v_ffn_dw_b),
    ]
    as2d = lambda t: t.reshape(-1, t.shape[-1])
    sd_, sm_, sv_ = _adamw_small([as2d(t[2]) for t in small], [as2d(t[1]) for t in small], [as2d(t[3]) for t in small],
                                 [as2d(t[4]) for t in small], "adamw_small")
    for (nme, g_, w_, _, _), d_, mo_, vo_ in zip(small, sd_, sm_, sv_):
        put(nme, g_.reshape(w_.shape), [d_.reshape(w_.shape), mo_.reshape(w_.shape), vo_.reshape(w_.shape)])

    order = ["mod_w", "mod_b", "norm_mix_g", "norm_ffn_g", "conv_pw1_w", "conv_pw1_b", "conv_dw_w", "conv_dw_b", "conv_ln_g",
             "conv_ln_b", "conv_pw2_w", "conv_pw2_b", "kv_mod_w", "kv_mod_b", "kv_norm_g", "w_kv", "k_norm_g", "w_q", "q_norm_g",
             "w_o", "ffn_up_w", "ffn_dw_w", "ffn_dw_b", "ffn_down_w"]
    return (loss, grad_x.reshape(x.shape), *[grads[k] for k in order], *[deltas[k] for k in order], *[new_m[k] for k in order],
            *[new_v[k] for k in order])
```

```python
import functools
import math

import jax
import jax.numpy as jnp
from jax import lax
from jax.experimental import pallas as pl
from jax.experimental.pallas import tpu as pltpu

F32 = jnp.float32
BF16 = jnp.bfloat16
EPS = 1e-6
NEG = -1e30
HEAD_DIM = 128
ROT_DIM = 32
ROPE_THETA = 500000.0
BLK = 128
DILATIONS = (1, 4, 16)
N_GROUPS = 3
CONV_K = 31
FFN_K = 3
ADAM_LR, ADAM_B1, ADAM_B2, ADAM_EPS, ADAM_WD, ADAM_STEP = 0.001, 0.9, 0.999, 1e-08, 0.01, 10
N_DEV = 8
MESH = pl.DeviceIdType.MESH
VMEM_LIMIT_MB = 56
ROW_TILE = 256
ATTN_CHUNK = 2048


def _pick(n, pref, mult=128):
    best = None
    d = mult
    while d <= min(n, pref):
        if n % d == 0:
            best = d
        d += mult
    return best if best is not None else n


def _wide_tile(rows, cols):
    tc = _pick(cols, 4096)
    return _pick(rows, max(16, (1 << 19) // tc), 16), tc


class _Rider:
    def __init__(self, ins, out_shapes, aliases, n_sem, copies):
        self.ins, self.out_shapes, self.aliases, self.n_sem, self.copies = list(ins), list(out_shapes), dict(aliases), n_sem, copies
        self.results = None


def _call(body, *, name, grid, in_specs, out_specs, out_shape, scratch=(), nsp=0, rider=None):
    params = pltpu.CompilerParams(dimension_semantics=("arbitrary",) * len(grid), vmem_limit_bytes=VMEM_LIMIT_MB << 20)
    if rider is None:
        return pl.pallas_call(
            body, name=name,
            grid_spec=pltpu.PrefetchScalarGridSpec(num_scalar_prefetch=nsp, grid=grid, in_specs=in_specs, out_specs=out_specs,
                                                   scratch_shapes=list(scratch)),
            out_shape=out_shape, compiler_params=params, interpret=False,
        )
    single = not isinstance(out_shape, (list, tuple))
    out_shapes = [out_shape] if single else list(out_shape)
    out_specs_l = [out_specs] if single else list(out_specs)
    n_in, n_out, n_scr = len(in_specs), len(out_shapes), len(scratch)
    r_in, r_out = len(rider.ins), len(rider.out_shapes)
    hbm = pl.BlockSpec(memory_space=pltpu.HBM)
    last = tuple(g - 1 for g in grid)

    def wrapped(*refs):
        pre, ins, rin = refs[:nsp], refs[nsp:nsp + n_in], refs[nsp + n_in:nsp + n_in + r_in]
        o0 = nsp + n_in + r_in
        outs, rout = refs[o0:o0 + n_out], refs[o0 + n_out:o0 + n_out + r_out]
        s0 = o0 + n_out + r_out
        scr, (send_sems, recv_sems) = refs[s0:s0 + n_scr], refs[s0 + n_scr:]
        ids = [pl.program_id(a) for a in range(len(grid))]
        is_first = functools.reduce(jnp.logical_and, [i == 0 for i in ids])
        is_last = functools.reduce(jnp.logical_and, [i == l for i, l in zip(ids, last)])

        @pl.when(is_first)
        def _():
            for cp in rider.copies(rin, rout, send_sems, recv_sems)[0]:
                cp.start()

        body(*pre, *ins, *outs, *scr)

        @pl.when(is_last)
        def _():
            sends, recvs = rider.copies(rin, rout, send_sems, recv_sems)
            for cp in recvs:
                cp.wait_recv()
            for cp in sends:
                cp.wait_send()

    call = pl.pallas_call(
        wrapped, name=name,
        grid_spec=pltpu.PrefetchScalarGridSpec(
            num_scalar_prefetch=nsp, grid=grid, in_specs=list(in_specs) + [hbm] * r_in, out_specs=out_specs_l + [hbm] * r_out,
            scratch_shapes=list(scratch) + [pltpu.SemaphoreType.DMA((rider.n_sem,)), pltpu.SemaphoreType.DMA((rider.n_sem,))]),
        out_shape=out_shapes + rider.out_shapes,
        input_output_aliases={nsp + n_in + i: n_out + o for i, o in rider.aliases.items()},
        compiler_params=params, interpret=False,
    )

    def run(*args):
        res = call(*args, *rider.ins)
        rider.results = list(res[n_out:])
        return res[0] if single else list(res[:n_out])

    return run


def _run_rider(rider, name):
    r_in, r_out = len(rider.ins), len(rider.out_shapes)
    hbm = pl.BlockSpec(memory_space=pltpu.HBM)

    def body(*refs):
        sends, recvs = rider.copies(refs[:r_in], refs[r_in:r_in + r_out], *refs[r_in + r_out:])
        for cp in sends:
            cp.start()
        for cp in recvs:
            cp.wait_recv()
        for cp in sends:
            cp.wait_send()

    return pl.pallas_call(
        body, name=name, out_shape=rider.out_shapes, in_specs=[hbm] * r_in, out_specs=[hbm] * r_out,
        input_output_aliases=rider.aliases,
        scratch_shapes=[pltpu.SemaphoreType.DMA((rider.n_sem,)), pltpu.SemaphoreType.DMA((rider.n_sem,))], interpret=False,
    )(*rider.ins)


def _sds(shape, dtype):
    return jax.ShapeDtypeStruct(shape, dtype)


def _acc(ref, val, i):
    @pl.when(i == 0)
    def _():
        ref[...] = val

    @pl.when(i > 0)
    def _():
        ref[...] += val


def _colsum(v):
    return jnp.sum(v, axis=0, keepdims=True)


def _silu(v):
    return v * jax.nn.sigmoid(v)


def _dsilu(v):
    s = jax.nn.sigmoid(v)
    return s * (1.0 + v * (1.0 - s))


_DIMS = {"nn": (((1,), (0,)), ((), ())), "nt": (((1,), (1,)), ((), ())), "tn": (((0,), (0,)), ((), ()))}


def _matmul(a, b, mode, out_dtype, name, rider=None):
    a_halves = a.shape[0] if a.ndim == 3 else 0
    b_halves = b.shape[0] if b.ndim == 3 else 0
    if mode == "nn":
        (m, c), (_, n) = a.shape, b.shape
    elif mode == "nt":
        m, c = (a.shape[1], a.shape[0] * a.shape[2]) if a_halves else a.shape
        n = b.shape[0]
    else:
        c, m = a.shape
        n = b.shape[0] * b.shape[2] if b_halves else b.shape[1]
    tm = _pick(m, 1024)
    tn = _pick(n // b_halves, 1024) if b_halves else _pick(n, 1024)
    c_cap = 2048 if mode == "tn" else 2816
    tc = _pick(c // a_halves, c_cap) if a_halves else _pick(c, c_cap)
    nk = c // tc
    if a_halves:
        per_a = c // a_halves // tc
        a_spec = pl.BlockSpec((None, tm, tc), lambda i, j, k: (k // per_a, i, k % per_a))
    else:
        a_spec = {"nn": pl.BlockSpec((tm, tc), lambda i, j, k: (i, k)), "nt": pl.BlockSpec((tm, tc), lambda i, j, k: (i, k)),
                  "tn": pl.BlockSpec((tc, tm), lambda i, j, k: (k, i))}[mode]
    if b_halves:
        per_b = n // b_halves // tn
        b_spec = pl.BlockSpec((None, tc, tn), lambda i, j, k: (j // per_b, k, j % per_b))
    else:
        b_spec = {"nn": pl.BlockSpec((tc, tn), lambda i, j, k: (k, j)), "nt": pl.BlockSpec((tn, tc), lambda i, j, k: (j, k)),
                  "tn": pl.BlockSpec((tc, tn), lambda i, j, k: (k, j))}[mode]
    dims = _DIMS[mode]

    def body(a_ref, b_ref, o_ref, acc_ref):
        k = pl.program_id(2)
        p = lax.dot_general(a_ref[...], b_ref[...], dims, preferred_element_type=F32)
        if nk == 1:
            o_ref[...] = p.astype(out_dtype)
        else:
            @pl.when(k == 0)
            def _():
                acc_ref[...] = p

            @pl.when(k > 0)
            def _():
                acc_ref[...] += p

            @pl.when(k == nk - 1)
            def _():
                o_ref[...] = acc_ref[...].astype(out_dtype)

    return _call(
        body, name=name, grid=(m // tm, n // tn, nk), in_specs=[a_spec, b_spec],
        out_specs=pl.BlockSpec((tm, tn), lambda i, j, k: (i, j)), out_shape=_sds((m, n), out_dtype),
        scratch=[pltpu.VMEM((tm, tn), F32)], rider=rider,
    )(a, b)


def _row_spec(tr, w):
    return pl.BlockSpec((tr, w), lambda i: (i, 0))


def _vec_spec(w):
    return pl.BlockSpec((1, w), lambda i: (0, 0))


def _resid_mod(x, prev, mods, target, name):
    s, d = x.shape
    tr = _pick(s, ROW_TILE, 8)
    n_mod = len(mods)

    def body(*refs):
        it = iter(refs)
        x_ref = next(it)
        if prev is not None:
            y_ref, yb_ref, gate_ref = next(it), next(it), next(it)
        mod_refs = [(next(it), next(it), next(it)) for _ in range(n_mod)]
        if target is not None:
            t_ref = next(it)
        if prev is not None:
            xo_ref = next(it)
        h_refs = [next(it) for _ in range(n_mod)]
        i = pl.program_id(0)
        xv = x_ref[...]
        if prev is not None:
            xv = xv + gate_ref[...] * (y_ref[...] + yb_ref[...])
            xo_ref[...] = xv
        if n_mod:
            nrm = xv * lax.rsqrt(jnp.mean(xv * xv, axis=-1, keepdims=True) + EPS)
            for (g_ref, sc_ref, sh_ref), h_ref in zip(mod_refs, h_refs):
                h_ref[...] = (nrm * g_ref[...] * (1.0 + sc_ref[...]) + sh_ref[...]).astype(BF16)
        if target is not None:
            dx_ref, loss_ref = next(it), next(it)
            err = xv - t_ref[...]
            dx_ref[...] = err * (1.0 / d)
            _acc(loss_ref, _colsum(err * err) * (0.5 / d), i)

    ins, in_specs = [x], [_row_spec(tr, d)]
    if prev is not None:
        ins += list(prev)
        in_specs += [_row_spec(tr, d), _vec_spec(d), _vec_spec(d)]
    for g, sc, sh in mods:
        ins += [g, sc, sh]
        in_specs += [_vec_spec(d)] * 3
    if target is not None:
        ins.append(target)
        in_specs.append(_row_spec(tr, d))
    out_shape, out_specs = [], []
    if prev is not None:
        out_shape.append(_sds((s, d), F32))
        out_specs.append(_row_spec(tr, d))
    for _ in mods:
        out_shape.append(_sds((s, d), BF16))
        out_specs.append(_row_spec(tr, d))
    if target is not None:
        out_shape += [_sds((s, d), F32), _sds((1, d), F32)]
        out_specs += [_row_spec(tr, d), _vec_spec(d)]
    return _call(body, name=name, grid=(s // tr,), in_specs=in_specs, out_specs=out_specs, out_shape=out_shape)(*ins)


def _bwd_step(dx_up, x, mods, prev, name, rider=None):
    s, d = x.shape
    tr = _pick(s, ROW_TILE, 8)
    n_mod = len(mods)

    def body(*refs):
        it = iter(refs)
        dxu_ref, x_ref = next(it), next(it)
        mod_refs = [(next(it), next(it), next(it)) for _ in range(n_mod)]
        if prev is not None:
            y_ref, yb_ref, gate_ref = next(it), next(it), next(it)
        dx_ref = next(it)
        acc_refs = [(next(it), next(it), next(it)) for _ in range(n_mod)]
        i = pl.program_id(0)
        dx = dxu_ref[...]
        if n_mod:
            xv = x_ref[...]
            rstd = lax.rsqrt(jnp.mean(xv * xv, axis=-1, keepdims=True) + EPS)
            nrm = xv * rstd
        for (dh_ref, g_ref, sc_ref), (dsh_ref, dsc_ref, dg_ref) in zip(mod_refs, acc_refs):
            dh = dh_ref[...]
            gv, one_sc = g_ref[...], 1.0 + sc_ref[...]
            _acc(dsh_ref, _colsum(dh), i)
            t = dh * nrm
            _acc(dsc_ref, _colsum(t) * gv, i)
            _acc(dg_ref, _colsum(t) * one_sc, i)
            dn = dh * (gv * one_sc)
            dx = dx + rstd * (dn - nrm * jnp.mean(dn * nrm, axis=-1, keepdims=True))
        dx_ref[...] = dx
        if prev is not None:
            dy_ref, dgate_ref, dyb_ref = next(it), next(it), next(it)
            dy = gate_ref[...] * dx
            dy_ref[...] = dy.astype(BF16)
            _acc(dgate_ref, _colsum(dx * (y_ref[...] + yb_ref[...])), i)
            _acc(dyb_ref, _colsum(dy), i)

    ins, in_specs = [dx_up, x], [_row_spec(tr, d)] * 2
    for dh, g, sc in mods:
        ins += [dh, g, sc]
        in_specs += [_row_spec(tr, d), _vec_spec(d), _vec_spec(d)]
    if prev is not None:
        ins += list(prev)
        in_specs += [_row_spec(tr, d), _vec_spec(d), _vec_spec(d)]
    out_shape, out_specs = [_sds((s, d), F32)], [_row_spec(tr, d)]
    for _ in mods:
        out_shape += [_sds((1, d), F32)] * 3
        out_specs += [_vec_spec(d)] * 3
    if prev is not None:
        out_shape += [_sds((s, d), BF16), _sds((1, d), F32), _sds((1, d), F32)]
        out_specs += [_row_spec(tr, d), _vec_spec(d), _vec_spec(d)]
    return _call(body, name=name, grid=(s // tr,), in_specs=in_specs, out_specs=out_specs, out_shape=out_shape, rider=rider)(*ins)


def _glu_fwd(u, bias, name):
    s, d2 = u.shape
    d = d2 // 2
    tr = _pick(s, ROW_TILE, 8)

    def body(u_ref, b_ref, o_ref):
        uv = u_ref[...] + b_ref[...]
        o_ref[...] = uv[:, :d] * jax.nn.sigmoid(uv[:, d:])

    return _call(body, name=name, grid=(s // tr,), in_specs=[_row_spec(tr, d2), _vec_spec(d2)],
                 out_specs=_row_spec(tr, d), out_shape=_sds((s, d), F32))(u, bias)


def _glu_bwd(dglu, u, bias, name):
    s, d2 = u.shape
    d = d2 // 2
    tr = _pick(s, ROW_TILE, 8)

    def body(dg_ref, u_ref, b_ref, du_ref, db_ref):
        i = pl.program_id(0)
        uv = u_ref[...] + b_ref[...]
        a, sg = uv[:, :d], jax.nn.sigmoid(uv[:, d:])
        dg = dg_ref[...]
        da = dg * sg
        dgt = dg * a * sg * (1.0 - sg)
        du_ref[:, :d] = da.astype(BF16)
        du_ref[:, d:] = dgt.astype(BF16)
        _acc(db_ref, jnp.concatenate([_colsum(da), _colsum(dgt)], axis=1), i)

    return _call(body, name=name, grid=(s // tr,), in_specs=[_row_spec(tr, d), _row_spec(tr, d2), _vec_spec(d2)],
                 out_specs=[_row_spec(tr, d2), _vec_spec(d2)], out_shape=[_sds((s, d2), BF16), _sds((1, d2), F32)])(dglu, u, bias)


def _halo_rows(k):
    return -(-(k - 1) // 8) * 8


CONV_CHUNK = 32


def _sublane_shifts(cat_ref, sh_ref, rows):
    for m in range(8):
        sh_ref[m] = cat_ref[pl.ds(m, rows), :]


def _dwconv_fwd(x, w, b, name, rider=None):
    s, c = x.shape
    kk = w.shape[0]
    hb = _halo_rows(kk)
    tr, tc = _pick(s, ROW_TILE, hb), _pick(c, 512)
    per = tr // hb
    ch = CONV_CHUNK

    def body(xp_ref, x_ref, w_ref, b_ref, o_ref, cat_ref, sh_ref):
        i = pl.program_id(1)
        cat_ref[0:hb, :] = jnp.where(i > 0, xp_ref[...], 0.0)
        cat_ref[hb:hb + tr, :] = x_ref[...]
        cat_ref[hb + tr:hb + tr + 8, :] = jnp.zeros((8, tc), F32)
        _sublane_shifts(cat_ref, sh_ref, tr + hb)

        def chunk(ci, carry):
            r0 = pl.multiple_of(ci * ch, ch)
            acc = jnp.zeros((ch, tc), F32) + b_ref[...]
            for k in range(kk):
                a, m = divmod(hb - (kk - 1) + k, 8)
                acc = acc + w_ref[k:k + 1, :] * sh_ref[m, pl.ds(r0 + 8 * a, ch), :]
            o_ref[pl.ds(r0, ch), :] = acc
            return carry

        lax.fori_loop(0, tr // ch, chunk, 0)

    return _call(
        body, name=name, grid=(c // tc, s // tr),
        in_specs=[pl.BlockSpec((hb, tc), lambda j, i: (jnp.maximum(i * per - 1, 0), j)), pl.BlockSpec((tr, tc), lambda j, i: (i, j)),
                  pl.BlockSpec((kk, tc), lambda j, i: (0, j)), pl.BlockSpec((1, tc), lambda j, i: (0, j))],
        out_specs=pl.BlockSpec((tr, tc), lambda j, i: (i, j)), out_shape=_sds((s, c), F32),
        scratch=[pltpu.VMEM((tr + hb + 8, tc), F32), pltpu.VMEM((8, tr + hb, tc), F32)], rider=rider,
    )(x, x, w, b)


def _dwconv_bwd(dy, x, w, name, rider=None):
    s, c = x.shape
    kk = w.shape[0]
    hb = _halo_rows(kk)
    tr, tc = _pick(s, ROW_TILE, hb), _pick(c, 512)
    per, nt = tr // hb, s // tr
    ch = CONV_CHUNK
    groups = ch // 8

    def body(xp_ref, x_ref, dy_ref, dyn_ref, w_ref, dx_ref, dw_ref, db_ref, cat_ref, shx_ref, shd_ref, acc_ref):
        i = pl.program_id(1)
        cat_ref[0:hb, :] = jnp.where(i > 0, xp_ref[...], 0.0)
        cat_ref[hb:hb + tr, :] = x_ref[...]
        cat_ref[hb + tr:hb + tr + 8, :] = jnp.zeros((8, tc), F32)
        _sublane_shifts(cat_ref, shx_ref, tr + hb)
        cat_ref[0:tr, :] = dy_ref[...]
        cat_ref[tr:tr + hb, :] = jnp.where(i < nt - 1, dyn_ref[...], 0.0)
        _sublane_shifts(cat_ref, shd_ref, tr + hb)

        @pl.when(i == 0)
        def _():
            acc_ref[...] = jnp.zeros_like(acc_ref)

        def fold(v):
            out = v[0:8, :]
            for g in range(1, groups):
                out = out + v[8 * g:8 * g + 8, :]
            return out

        def chunk(ci, carry):
            r0 = pl.multiple_of(ci * ch, ch)
            dyv = dy_ref[pl.ds(r0, ch), :]
            acc_ref[kk] += fold(dyv)
            dxv = jnp.zeros((ch, tc), F32)
            for k in range(kk):
                a, m = divmod(kk - 1 - k, 8)
                dxv = dxv + w_ref[k:k + 1, :] * shd_ref[m, pl.ds(r0 + 8 * a, ch), :]
                a, m = divmod(hb - (kk - 1) + k, 8)
                acc_ref[k] += fold(dyv * shx_ref[m, pl.ds(r0 + 8 * a, ch), :])
            dx_ref[pl.ds(r0, ch), :] = dxv
            return carry

        lax.fori_loop(0, tr // ch, chunk, 0)

        @pl.when(i == nt - 1)
        def _():
            for k in range(kk):
                dw_ref[k:k + 1, :] = _colsum(acc_ref[k])
            db_ref[...] = _colsum(acc_ref[kk])

    return _call(
        body, name=name, grid=(c // tc, nt),
        in_specs=[pl.BlockSpec((hb, tc), lambda j, i: (jnp.maximum(i * per - 1, 0), j)), pl.BlockSpec((tr, tc), lambda j, i: (i, j)),
                  pl.BlockSpec((tr, tc), lambda j, i: (i, j)),
                  pl.BlockSpec((hb, tc), lambda j, i: (jnp.minimum((i + 1) * per, s // hb - 1), j)),
                  pl.BlockSpec((kk, tc), lambda j, i: (0, j))],
        out_specs=[pl.BlockSpec((tr, tc), lambda j, i: (i, j)), pl.BlockSpec((kk, tc), lambda j, i: (0, j)),
                   pl.BlockSpec((1, tc), lambda j, i: (0, j))],
        out_shape=[_sds((s, c), F32), _sds((kk, c), F32), _sds((1, c), F32)],
        scratch=[pltpu.VMEM((tr + hb + 8, tc), F32), pltpu.VMEM((8, tr + hb, tc), F32), pltpu.VMEM((8, tr + hb, tc), F32),
                 pltpu.VMEM((kk + 1, 8, tc), F32)], rider=rider,
    )(x, x, dy, dy, w)


def _ln_silu_fwd(x, g, b, name):
    s, d = x.shape
    tr = _pick(s, ROW_TILE, 8)

    def body(x_ref, g_ref, b_ref, o_ref):
        xv = x_ref[...]
        mu = jnp.mean(xv, axis=-1, keepdims=True)
        xc = xv - mu
        ln = xc * lax.rsqrt(jnp.mean(xc * xc, axis=-1, keepdims=True) + EPS) * g_ref[...] + b_ref[...]
        o_ref[...] = _silu(ln).astype(BF16)

    return _call(body, name=name, grid=(s // tr,), in_specs=[_row_spec(tr, d), _vec_spec(d), _vec_spec(d)],
                 out_specs=_row_spec(tr, d), out_shape=_sds((s, d), BF16))(x, g, b)


def _ln_silu_bwd(dact, x, g, b, name):
    s, d = x.shape
    tr = _pick(s, ROW_TILE, 8)

    def body(da_ref, x_ref, g_ref, b_ref, dx_ref, dg_ref, db_ref):
        i = pl.program_id(0)
        xv = x_ref[...]
        mu = jnp.mean(xv, axis=-1, keepdims=True)
        xc = xv - mu
        rstd = lax.rsqrt(jnp.mean(xc * xc, axis=-1, keepdims=True) + EPS)
        xh = xc * rstd
        ln = xh * g_ref[...] + b_ref[...]
        dln = da_ref[...] * _dsilu(ln)
        _acc(dg_ref, _colsum(dln * xh), i)
        _acc(db_ref, _colsum(dln), i)
        dxh = dln * g_ref[...]
        dx_ref[...] = rstd * (dxh - jnp.mean(dxh, axis=-1, keepdims=True) - xh * jnp.mean(dxh * xh, axis=-1, keepdims=True))

    return _call(body, name=name, grid=(s // tr,), in_specs=[_row_spec(tr, d), _row_spec(tr, d), _vec_spec(d), _vec_spec(d)],
                 out_specs=[_row_spec(tr, d), _vec_spec(d), _vec_spec(d)],
                 out_shape=[_sds((s, d), F32), _sds((1, d), F32), _sds((1, d), F32)])(dact, x, g, b)


FFN_CHUNK = 16


def _fold8(v):
    out = v[0:8, :]
    for g in range(1, v.shape[0] // 8):
        out = out + v[8 * g:8 * g + 8, :]
    return out


def _ffn_act_fwd(u, w, b, name, rider=None):
    s, f2 = u.shape
    f = f2 // 2
    hb = 8
    tr, tc = _pick(s, ROW_TILE, 2 * hb), _pick(f, 1408)
    per, nf = tr // hb, f // tc
    ch = FFN_CHUNK

    def body(gp_ref, g_ref, v_ref, w_ref, b_ref, z_ref, cat_ref, sh_ref):
        i = pl.program_id(1)
        cat_ref[0:hb, :] = jnp.where(i > 0, gp_ref[...], 0.0)
        cat_ref[hb:hb + tr, :] = g_ref[...]
        for k in range(FFN_K - 1):
            sh_ref[k] = cat_ref[pl.ds(hb - (FFN_K - 1) + k, tr), :]

        def chunk(ci, carry):
            rows = pl.ds(pl.multiple_of(ci * ch, ch), ch)
            gc = b_ref[...] + w_ref[FFN_K - 1:FFN_K, :] * g_ref[rows, :]
            for k in range(FFN_K - 1):
                gc = gc + w_ref[k:k + 1, :] * sh_ref[k, rows, :]
            z_ref[rows, :] = (_silu(gc) * v_ref[rows, :]).astype(BF16)
            return carry

        lax.fori_loop(0, tr // ch, chunk, 0)

    return _call(
        body, name=name, grid=(nf, s // tr),
        in_specs=[pl.BlockSpec((hb, tc), lambda j, i: (jnp.maximum(i * per - 1, 0), j)), pl.BlockSpec((tr, tc), lambda j, i: (i, j)),
                  pl.BlockSpec((tr, tc), lambda j, i: (i, nf + j)), pl.BlockSpec((FFN_K, tc), lambda j, i: (0, j)),
                  pl.BlockSpec((1, tc), lambda j, i: (0, j))],
        out_specs=pl.BlockSpec((tr, tc), lambda j, i: (i, j)), out_shape=_sds((s, f), BF16),
        scratch=[pltpu.VMEM((tr + hb, tc), F32), pltpu.VMEM((FFN_K - 1, tr, tc), F32)], rider=rider,
    )(u, u, u, w, b)


def _ffn_act_bwd(dz, u, w, b, name, rider=None):
    s, f2 = u.shape
    f = f2 // 2
    hb = 8
    tr, tc = _pick(s, ROW_TILE, 2 * hb), _pick(f, 1408)
    per, nf, nt = tr // hb, f // tc, s // tr
    ext = tr + hb
    ch = FFN_CHUNK
    last_tap = FFN_K - 1

    def body(gp_ref, g_ref, gn_ref, v_ref, vn_ref, dz_ref, dzn_ref, w_ref, b_ref, du_ref, dw_ref, db_ref, cat_ref, sh_ref, dgc_ref,
             sd_ref, acc_ref):
        i = pl.program_id(1)
        last = i == nt - 1
        cat_ref[0:hb, :] = jnp.where(i > 0, gp_ref[...], 0.0)
        cat_ref[hb:hb + tr, :] = g_ref[...]
        cat_ref[hb + tr:hb + tr + hb, :] = gn_ref[...]
        for k in range(last_tap):
            sh_ref[k] = cat_ref[pl.ds(hb - last_tap + k, ext), :]

        def preact(rows, g_rows):
            gc = b_ref[...] + w_ref[last_tap:FFN_K, :] * g_rows
            for k in range(last_tap):
                gc = gc + w_ref[k:k + 1, :] * sh_ref[k, rows, :]
            return gc

        def chunk1(ci, carry):
            rows = pl.ds(pl.multiple_of(ci * ch, ch), ch)
            gc = preact(rows, g_ref[rows, :])
            sg = jax.nn.sigmoid(gc)
            dzv = dz_ref[rows, :]
            du_ref[1, rows, :] = (dzv * (gc * sg)).astype(BF16)
            dgc_ref[rows, :] = dzv * v_ref[rows, :] * (sg * (1.0 + gc * (1.0 - sg)))
            return carry

        lax.fori_loop(0, tr // ch, chunk1, 0)
        gcn = preact(pl.ds(tr, hb), gn_ref[...])
        dgc_ref[tr:ext, :] = jnp.where(last, 0.0, dzn_ref[...] * vn_ref[...] * _dsilu(gcn))
        for j in range(last_tap):
            sd_ref[j] = dgc_ref[pl.ds(j + 1, tr), :]

        @pl.when(i == 0)
        def _():
            acc_ref[...] = jnp.zeros_like(acc_ref)

        def chunk2(ci, carry):
            rows = pl.ds(pl.multiple_of(ci * ch, ch), ch)
            d0 = dgc_ref[rows, :]
            dgt = w_ref[last_tap:FFN_K, :] * d0
            for k in range(last_tap):
                dgt = dgt + w_ref[k:k + 1, :] * sd_ref[last_tap - 1 - k, rows, :]
            du_ref[0, rows, :] = dgt.astype(BF16)
            acc_ref[FFN_K] += _fold8(d0)
            acc_ref[last_tap] += _fold8(d0 * g_ref[rows, :])
            for k in range(last_tap):
                acc_ref[k] += _fold8(d0 * sh_ref[k, rows, :])
            return carry

        lax.fori_loop(0, tr // ch, chunk2, 0)

        @pl.when(last)
        def _():
            for k in range(FFN_K):
                dw_ref[k:k + 1, :] = _colsum(acc_ref[k])
            db_ref[...] = _colsum(acc_ref[FFN_K])

    prev_map = lambda j, i: (jnp.maximum(i * per - 1, 0), j)
    next_map = lambda j, i: (jnp.minimum((i + 1) * per, s // hb - 1), j)
    next_map_v = lambda j, i: (jnp.minimum((i + 1) * per, s // hb - 1), nf + j)
    return _call(
        body, name=name, grid=(nf, nt),
        in_specs=[pl.BlockSpec((hb, tc), prev_map), pl.BlockSpec((tr, tc), lambda j, i: (i, j)), pl.BlockSpec((hb, tc), next_map),
                  pl.BlockSpec((tr, tc), lambda j, i: (i, nf + j)), pl.BlockSpec((hb, tc), next_map_v),
                  pl.BlockSpec((tr, tc), lambda j, i: (i, j)), pl.BlockSpec((hb, tc), next_map),
                  pl.BlockSpec((FFN_K, tc), lambda j, i: (0, j)), pl.BlockSpec((1, tc), lambda j, i: (0, j))],
        out_specs=[pl.BlockSpec((2, tr, tc), lambda j, i: (0, i, j)), pl.BlockSpec((FFN_K, tc), lambda j, i: (0, j)),
                   pl.BlockSpec((1, tc), lambda j, i: (0, j))],
        out_shape=[_sds((2, s, f), BF16), _sds((FFN_K, f), F32), _sds((1, f), F32)],
        scratch=[pltpu.VMEM((tr + 2 * hb, tc), F32), pltpu.VMEM((last_tap, ext, tc), F32), pltpu.VMEM((ext, tc), F32),
                 pltpu.VMEM((last_tap, tr, tc), F32), pltpu.VMEM((FFN_K + 1, 8, tc), F32)], rider=rider,
    )(u, u, u, u, u, dz, dz, w, b)


def _rope_tables(pos, freq, sign, name):
    s = pos.shape[0]
    tr = _pick(s, 512, 8)

    def body(p_ref, f_ref, s_ref, c_ref, sn_ref):
        ang = p_ref[...].astype(F32) * f_ref[...]
        c_ref[...] = jnp.cos(ang)
        sn_ref[...] = jnp.sin(ang) * s_ref[...]

    return _call(body, name=name, grid=(s // tr,), in_specs=[pl.BlockSpec((tr, 1), lambda i: (i, 0)), _vec_spec(128), _vec_spec(128)],
                 out_specs=[_row_spec(tr, 128)] * 2, out_shape=[_sds((s, 128), F32)] * 2)(pos, freq, sign)


def _partner(v):
    lane = lax.broadcasted_iota(jnp.int32, v.shape, 1)
    lower = pltpu.roll(v, HEAD_DIM - ROT_DIM // 2, 1)
    upper = jnp.where(lane < ROT_DIM, pltpu.roll(v, ROT_DIM // 2, 1), 0.0)
    return jnp.where(lane < ROT_DIM // 2, lower, upper)


def _head_sum(v):
    hi = v.astype(BF16)
    lo = (v - hi.astype(F32)).astype(BF16)
    ones = jnp.ones((HEAD_DIM, HEAD_DIM), BF16)
    return lax.dot_general(hi, ones, _DIMS["nn"], preferred_element_type=F32) + lax.dot_general(
        lo, ones, _DIMS["nn"], preferred_element_type=F32)


def _qk_norm_rope_fwd(q, kv, gq, gk, ctab, stab, name, rider=None):
    s, w = q.shape
    tr = _pick(s, 128, 8)
    heads = w // HEAD_DIM

    def body(q_ref, k_ref, gq_ref, gk_ref, c_ref, s_ref, qn_ref, kn_ref):
        cv, sv = c_ref[...], s_ref[...]
        for src, g_ref, dst in ((q_ref, gq_ref, qn_ref), (k_ref, gk_ref, kn_ref)):
            gv = g_ref[...]
            for h in range(heads):
                cols = pl.ds(h * HEAD_DIM, HEAD_DIM)
                xv = src[:, cols]
                nv = xv * lax.rsqrt(_head_sum(xv * xv) * (1.0 / HEAD_DIM) + EPS) * gv
                dst[:, cols] = nv * cv + _partner(nv) * sv

    return _call(
        body, name=name, grid=(s // tr,),
        in_specs=[_row_spec(tr, w), _row_spec(tr, w), _vec_spec(128), _vec_spec(128), _row_spec(tr, 128), _row_spec(tr, 128)],
        out_specs=[_row_spec(tr, w)] * 2, out_shape=[_sds((s, w), F32)] * 2, rider=rider,
    )(q, kv, gq, gk, ctab, stab)


def _qk_norm_rope_bwd(dqs, dks, dvs, q, kv, gq, gk, ctab, stab, name, rider=None):
    s, w = q.shape
    gw = w // N_GROUPS
    tr = _pick(s, 128, 8)
    hpg = gw // HEAD_DIM

    def body(*refs):
        dq_refs, dk_refs, dv_refs = refs[0:3], refs[3:6], refs[6:9]
        q_ref, k_ref, gq_ref, gk_ref, c_ref, s_ref, dq_ref, dkv_ref, dgq_ref, dgk_ref = refs[9:]
        i = pl.program_id(0)
        cv, sv = c_ref[...], s_ref[...]
        for d_refs, src, g_ref, dst, dg_ref in ((dq_refs, q_ref, gq_ref, dq_ref, dgq_ref), (dk_refs, k_ref, gk_ref, dkv_ref, dgk_ref)):
            gv = g_ref[...]
            dg = jnp.zeros((1, HEAD_DIM), F32)
            for g in range(N_GROUPS):
                for h in range(hpg):
                    cols = pl.ds(g * gw + h * HEAD_DIM, HEAD_DIM)
                    dout = d_refs[g][:, pl.ds(h * HEAD_DIM, HEAD_DIM)]
                    dn = dout * cv + _partner(dout * sv)
                    xv = src[:, cols]
                    rstd = lax.rsqrt(_head_sum(xv * xv) * (1.0 / HEAD_DIM) + EPS)
                    xh = xv * rstd
                    dg = dg + _colsum(dn * xh)
                    dxh = dn * gv
                    dst[:, cols] = (rstd * (dxh - xh * (_head_sum(dxh * xh) * (1.0 / HEAD_DIM)))).astype(BF16)
            _acc(dg_ref, dg, i)
        for g in range(N_GROUPS):
            dkv_ref[:, pl.ds(w + g * gw, gw)] = dv_refs[g][...].astype(BF16)

    return _call(
        body, name=name, grid=(s // tr,),
        in_specs=[_row_spec(tr, gw)] * 9 + [_row_spec(tr, w), _row_spec(tr, w), _vec_spec(128), _vec_spec(128),
                                            _row_spec(tr, 128), _row_spec(tr, 128)],
        out_specs=[_row_spec(tr, w), _row_spec(tr, 2 * w), _vec_spec(128), _vec_spec(128)],
        out_shape=[_sds((s, w), BF16), _sds((s, 2 * w), BF16), _sds((1, 128), F32), _sds((1, 128), F32)], rider=rider,
    )(*dqs, *dks, *dvs, q, kv, gq, gk, ctab, stab)


def _rows(j, b, r):
    start = j + r * BLK * b
    return pl.ds(start, BLK, stride=r) if r > 1 else pl.ds(start, BLK)


def _dot_nt(a, b):
    return lax.dot_general(a, b, _DIMS["nt"], preferred_element_type=F32)


def _dot_nn(a, b):
    return lax.dot_general(a, b, _DIMS["nn"], preferred_element_type=F32)


def _band_masks():
    qi = lax.broadcasted_iota(jnp.int32, (BLK, BLK), 0)
    kj = lax.broadcasted_iota(jnp.int32, (BLK, BLK), 1)
    return kj <= qi, kj >= qi


def _attn_fwd(qn, kn, kv, g, name):
    s, w = qn.shape
    r = DILATIONS[g]
    gw = w // N_GROUPS
    cr = min(ATTN_CHUNK, s)
    nb = cr // (BLK * r)
    hp = 1
    cw = hp * HEAD_DIM
    gc = gw // cw
    scale = 1.0 / math.sqrt(HEAD_DIM)

    def body(q_ref, kc_ref, kp_ref, vc_ref, vp_ref, o_ref, l_ref):
        n = pl.program_id(1)
        same_m, prev_m = _band_masks()
        prev_first = jnp.logical_and(prev_m, n > 0)
        for h in range(hp):
            cols = pl.ds(h * HEAD_DIM, HEAD_DIM)
            for j in range(r):
                for b in range(nb):
                    rows = _rows(j, b, r)
                    qv = q_ref[rows, cols].astype(BF16)
                    kc, vc = kc_ref[rows, cols].astype(BF16), vc_ref[rows, cols].astype(BF16)
                    if b > 0:
                        rp = _rows(j, b - 1, r)
                        kp, vp, pm = kc_ref[rp, cols].astype(BF16), vc_ref[rp, cols].astype(BF16), prev_m
                    else:
                        rp = _rows(j, nb - 1, r)
                        kp, vp, pm = kp_ref[rp, cols].astype(BF16), vp_ref[rp, cols].astype(BF16), prev_first
                    sd = jnp.where(same_m, _dot_nt(qv, kc) * scale, NEG)
                    so = jnp.where(pm, _dot_nt(qv, kp) * scale, NEG)
                    m = jnp.maximum(jnp.max(sd, axis=-1, keepdims=True), jnp.max(so, axis=-1, keepdims=True))
                    pd, po = jnp.exp(sd - m), jnp.exp(so - m)
                    den = jnp.sum(pd, axis=-1, keepdims=True) + jnp.sum(po, axis=-1, keepdims=True)
                    ov = (_dot_nn(pd.astype(BF16), vc) + _dot_nn(po.astype(BF16), vp)) / den
                    o_ref[rows, cols] = ov
                    l_ref[rows, cols] = jnp.broadcast_to(m + jnp.log(den), (BLK, HEAD_DIM))

    cur = lambda base: (lambda c, n: (n, base + c))
    prv = lambda base: (lambda c, n: (jnp.maximum(n - 1, 0), base + c))
    qb, kb, vb = g * gc, g * gc, (N_GROUPS + g) * gc
    return _call(
        body, name=name, grid=(gc, s // cr),
        in_specs=[pl.BlockSpec((cr, cw), cur(qb)), pl.BlockSpec((cr, cw), cur(kb)), pl.BlockSpec((cr, cw), prv(kb)),
                  pl.BlockSpec((cr, cw), cur(vb)), pl.BlockSpec((cr, cw), prv(vb))],
        out_specs=[pl.BlockSpec((cr, cw), lambda c, n: (n, c))] * 2, out_shape=[_sds((s, gw), F32)] * 2,
    )(qn, kn, kn, kv, kv)


def _attn_combine(os_, lses, name):
    s, gw = os_[0].shape
    tr = _pick(s, ROW_TILE, 8)

    def body(o0, o1, o2, l0, l1, l2, o_ref, l_ref):
        a, b, c = l0[...], l1[...], l2[...]
        m = jnp.maximum(jnp.maximum(a, b), c)
        ea, eb, ec = jnp.exp(a - m), jnp.exp(b - m), jnp.exp(c - m)
        den = ea + eb + ec
        o_ref[...] = ((ea * o0[...] + eb * o1[...] + ec * o2[...]) / den).astype(BF16)
        l_ref[...] = m + jnp.log(den)

    return _call(body, name=name, grid=(s // tr,), in_specs=[_row_spec(tr, gw)] * 6, out_specs=[_row_spec(tr, gw)] * 2,
                 out_shape=[_sds((s, gw), BF16), _sds((s, gw), F32)])(*os_, *lses)


def _attn_delta(do, o, name):
    s, gw = do.shape
    tr = _pick(s, ROW_TILE, 8)

    def body(do_ref, o_ref, d_ref):
        for h in range(gw // HEAD_DIM):
            cols = pl.ds(h * HEAD_DIM, HEAD_DIM)
            t = jnp.sum(do_ref[:, cols] * o_ref[:, cols].astype(F32), axis=-1, keepdims=True)
            d_ref[:, cols] = jnp.broadcast_to(t, (tr, HEAD_DIM))

    return _call(body, name=name, grid=(s // tr,), in_specs=[_row_spec(tr, gw)] * 2, out_specs=_row_spec(tr, gw),
                 out_shape=_sds((s, gw), F32))(do, o)


def _pair_grads(qv, kv_, vv, dov, lse, delta, mask, scale):
    sc = jnp.where(mask, _dot_nt(qv, kv_) * scale, NEG)
    p = jnp.exp(sc - lse)
    ds = p * (_dot_nt(dov, vv) - delta) * scale
    return p, ds


def _attn_bwd_dq(qn, kn, kv, do, lse, delta, g, name, rider=None):
    s, w = qn.shape
    r = DILATIONS[g]
    gw = w // N_GROUPS
    cr = min(ATTN_CHUNK, s)
    nb = cr // (BLK * r)
    cw = HEAD_DIM
    gc = gw // cw
    scale = 1.0 / math.sqrt(HEAD_DIM)

    def body(q_ref, kc_ref, kp_ref, vc_ref, vp_ref, do_ref, l_ref, d_ref, dq_ref):
        n = pl.program_id(1)
        same_m, prev_m = _band_masks()
        prev_first = jnp.logical_and(prev_m, n > 0)
        for j in range(r):
            for b in range(nb):
                rows = _rows(j, b, r)
                qv, dov = q_ref[rows, :].astype(BF16), do_ref[rows, :].astype(BF16)
                lse, delta = l_ref[rows, :], d_ref[rows, :]
                kc, vc = kc_ref[rows, :].astype(BF16), vc_ref[rows, :].astype(BF16)
                if b > 0:
                    rp = _rows(j, b - 1, r)
                    kp, vp, pm = kc_ref[rp, :].astype(BF16), vc_ref[rp, :].astype(BF16), prev_m
                else:
                    rp = _rows(j, nb - 1, r)
                    kp, vp, pm = kp_ref[rp, :].astype(BF16), vp_ref[rp, :].astype(BF16), prev_first
                _, dsd = _pair_grads(qv, kc, vc, dov, lse, delta, same_m, scale)
                _, dso = _pair_grads(qv, kp, vp, dov, lse, delta, pm, scale)
                dq_ref[rows, :] = _dot_nn(dsd.astype(BF16), kc) + _dot_nn(dso.astype(BF16), kp)

    cur = lambda base: (lambda c, n: (n, base + c))
    prv = lambda base: (lambda c, n: (jnp.maximum(n - 1, 0), base + c))
    qb, vb = g * gc, (N_GROUPS + g) * gc
    own = pl.BlockSpec((cr, cw), lambda c, n: (n, c))
    return _call(
        body, name=name, grid=(gc, s // cr),
        in_specs=[pl.BlockSpec((cr, cw), cur(qb)), pl.BlockSpec((cr, cw), cur(qb)), pl.BlockSpec((cr, cw), prv(qb)),
                  pl.BlockSpec((cr, cw), cur(vb)), pl.BlockSpec((cr, cw), prv(vb)), own, own, own],
        out_specs=own, out_shape=_sds((s, gw), F32), rider=rider,
    )(qn, kn, kn, kv, kv, do, lse, delta)


def _attn_bwd_dkv(qn, kn, kv, do, lse, delta, g, name):
    s, w = qn.shape
    r = DILATIONS[g]
    gw = w // N_GROUPS
    cr = min(ATTN_CHUNK, s)
    nb = cr // (BLK * r)
    nchunk = s // cr
    cw = HEAD_DIM
    gc = gw // cw
    scale = 1.0 / math.sqrt(HEAD_DIM)

    def body(k_ref, v_ref, qc_ref, qx_ref, doc_ref, dox_ref, lc_ref, lx_ref, dc_ref, dx_ref, dk_ref, dv_ref):
        n = pl.program_id(1)
        same_m, prev_m = _band_masks()
        next_last = jnp.logical_and(prev_m, n < nchunk - 1)
        for j in range(r):
            for b in range(nb):
                rows = _rows(j, b, r)
                kv_, vv = k_ref[rows, :].astype(BF16), v_ref[rows, :].astype(BF16)
                qv, dov = qc_ref[rows, :].astype(BF16), doc_ref[rows, :].astype(BF16)
                pd, dsd = _pair_grads(qv, kv_, vv, dov, lc_ref[rows, :], dc_ref[rows, :], same_m, scale)
                if b < nb - 1:
                    rx = _rows(j, b + 1, r)
                    qx, dox, lx, dlx, xm = qc_ref[rx, :], doc_ref[rx, :], lc_ref[rx, :], dc_ref[rx, :], prev_m
                else:
                    rx = _rows(j, 0, r)
                    qx, dox, lx, dlx, xm = qx_ref[rx, :], dox_ref[rx, :], lx_ref[rx, :], dx_ref[rx, :], next_last
                qx, dox = qx.astype(BF16), dox.astype(BF16)
                po, dso = _pair_grads(qx, kv_, vv, dox, lx, dlx, xm, scale)
                dk_ref[rows, :] = _dot_nn(dsd.T.astype(BF16), qv) + _dot_nn(dso.T.astype(BF16), qx)
                dv_ref[rows, :] = _dot_nn(pd.T.astype(BF16), dov) + _dot_nn(po.T.astype(BF16), dox)

    cur = lambda base: (lambda c, n: (n, base + c))
    nxt = lambda base: (lambda c, n: (jnp.minimum(n + 1, nchunk - 1), base + c))
    qb, vb = g * gc, (N_GROUPS + g) * gc
    blk = lambda f: pl.BlockSpec((cr, cw), f)
    return _call(
        body, name=name, grid=(gc, nchunk),
        in_specs=[blk(cur(qb)), blk(cur(vb)), blk(cur(qb)), blk(nxt(qb)), blk(cur(0)), blk(nxt(0)), blk(cur(0)), blk(nxt(0)),
                  blk(cur(0)), blk(nxt(0))],
        out_specs=[blk(cur(0))] * 2, out_shape=[_sds((s, gw), F32)] * 2,
    )(kn, kv, qn, qn, do, do, lse, lse, delta, delta)


def _mod_proj(sc_all, w, name):
    l, d, ns = w.shape
    tn = _pick(ns, 512)

    def body(c_ref, w_ref, o_ref):
        o_ref[...] = jnp.dot(c_ref[...].astype(BF16), w_ref[...].astype(BF16), preferred_element_type=F32)

    return _call(
        body, name=name, grid=(l, ns // tn),
        in_specs=[pl.BlockSpec((N_DEV, d), lambda a, j: (0, 0)), pl.BlockSpec((None, d, tn), lambda a, j: (a, 0, j))],
        out_specs=pl.BlockSpec((None, N_DEV, tn), lambda a, j: (a, 0, j)), out_shape=_sds((l, N_DEV, ns), F32),
    )(sc_all, w)


def _adamw_math(w, g, m, v):
    m = ADAM_B1 * m + (1.0 - ADAM_B1) * g
    v = ADAM_B2 * v + (1.0 - ADAM_B2) * (g * g)
    m_hat = m / (1.0 - ADAM_B1 ** ADAM_STEP)
    v_hat = v / (1.0 - ADAM_B2 ** ADAM_STEP)
    delta = -ADAM_LR * (m_hat / (jnp.sqrt(v_hat) + ADAM_EPS) + ADAM_WD * w)
    return delta, m, v


def _adamw_big(w, g, m, v, name):
    shape = w.shape
    cols = shape[-1]
    rows = math.prod(shape[:-1])
    tr, tc = _wide_tile(rows, cols)
    w2, g2, m2, v2 = (t.reshape(rows, cols) for t in (w, g, m, v))

    def body(w_ref, g_ref, m_ref, v_ref, d_ref, mo_ref, vo_ref):
        d_ref[...], mo_ref[...], vo_ref[...] = _adamw_math(w_ref[...], g_ref[...], m_ref[...], v_ref[...])

    spec = pl.BlockSpec((tr, tc), lambda i, j: (i, j))
    outs = _call(body, name=name, grid=(rows // tr, cols // tc), in_specs=[spec] * 4, out_specs=[spec] * 3,
                 out_shape=[_sds((rows, cols), F32)] * 3)(w2, g2, m2, v2)
    return [t.reshape(shape) for t in outs]


def _adamw_mod(w, sct, dm, m, v, name):
    l, d, ns = w.shape
    tr, tc = _pick(d, 512, 8), _pick(ns, 1024)

    def body(w_ref, c_ref, dm_ref, m_ref, v_ref, g_ref, d_ref, mo_ref, vo_ref):
        cv, dv = c_ref[...].astype(BF16).astype(F32), dm_ref[...].astype(BF16).astype(F32)
        g = jnp.zeros((tr, tc), F32)
        for e in range(N_DEV):
            g = g + cv[:, e:e + 1] * dv[e:e + 1, :]
        g_ref[...] = g
        d_ref[...], mo_ref[...], vo_ref[...] = _adamw_math(w_ref[...], g, m_ref[...], v_ref[...])

    spec = pl.BlockSpec((None, tr, tc), lambda a, i, j: (a, i, j))
    return _call(
        body, name=name, grid=(l, d // tr, ns // tc),
        in_specs=[spec, pl.BlockSpec((tr, N_DEV), lambda a, i, j: (i, 0)), pl.BlockSpec((None, N_DEV, tc), lambda a, i, j: (a, 0, j)),
                  spec, spec],
        out_specs=[spec] * 4, out_shape=[_sds((l, d, ns), F32)] * 4,
    )(w, sct, dm, m, v)


def _adamw_small(ws, gs, ms, vs, name):
    n = len(ws)

    def body(*refs):
        w_r, g_r, m_r, v_r = refs[0:n], refs[n:2 * n], refs[2 * n:3 * n], refs[3 * n:4 * n]
        d_o, m_o, v_o = refs[4 * n:5 * n], refs[5 * n:6 * n], refs[6 * n:7 * n]
        for k in range(n):
            d_o[k][...], m_o[k][...], v_o[k][...] = _adamw_math(w_r[k][...], g_r[k][...], m_r[k][...], v_r[k][...])

    vm = pl.BlockSpec(memory_space=pltpu.VMEM)
    shapes = [_sds(w.shape, F32) for w in ws]
    outs = pl.pallas_call(body, name=name, in_specs=[vm] * (4 * n), out_specs=[vm] * (3 * n), out_shape=shapes * 3,
                          interpret=False)(*ws, *gs, *ms, *vs)
    return outs[0:n], outs[n:2 * n], outs[2 * n:3 * n]


def _sum_rows(a, name):
    n, v = a.shape
    tc = _pick(v, 8192)

    def body(a_ref, o_ref):
        o_ref[...] = jnp.sum(a_ref[...], axis=0, keepdims=True)

    return _call(body, name=name, grid=(v // tc,), in_specs=[pl.BlockSpec((n, tc), lambda j: (0, j))],
                 out_specs=pl.BlockSpec((1, tc), lambda j: (0, j)), out_shape=_sds((1, v), F32))(a)


def _cast_into_full(w2d, kind, slot_arr, name):
    rows, cols = w2d.shape
    tr, tc = _wide_tile(rows, cols)
    nr, nc = rows // tr, cols // tc

    def body(s_ref, w_ref, o_ref):
        o_ref[...] = w_ref[...].astype(BF16)

    if kind == "col":
        o_map = lambda i, j, s_ref: (i, s_ref[0] * nc + j)
    else:
        o_map = lambda i, j, s_ref: (s_ref[0] * nr + i, j)
    return _call(body, name=name, grid=(nr, nc), in_specs=[pl.BlockSpec((tr, tc), lambda i, j, s_ref: (i, j))],
                 out_specs=pl.BlockSpec((tr, tc), o_map), out_shape=_sds(_full_shape(kind, (rows, cols)), BF16), nsp=1)(slot_arr, w2d)


def _place():
    x, y, c = lax.axis_index("x"), lax.axis_index("y"), lax.axis_index("c")
    chips = [(1 - x, y), (x, 1 - y), (1 - x, 1 - y)]
    return x, y, c, chips


def _remote(src, dst, send_sem, recv_sem, to):
    return pltpu.make_async_remote_copy(src_ref=src, dst_ref=dst, send_sem=send_sem, recv_sem=recv_sem, device_id=to,
                                        device_id_type=MESH)


def _allgather8(a, name):
    m_per, n = a.shape

    def body(x_ref, out_ref, send_sems, recv_sems, local_sem):
        x, y, c, chips = _place()
        me, sibling = (x, y, c), (x, y, 1 - c)

        def rows(px, py, pc):
            return out_ref.at[pl.ds((4 * px + 2 * py + pc) * m_per, m_per), :]

        def copy(k, block, to, src=None):
            return _remote(rows(*block) if src is None else src, rows(*block), send_sems.at[k], recv_sems.at[k], to)

        mine = pltpu.make_async_copy(x_ref, rows(*me), local_sem)
        mine.start()
        first = [copy(0, me, sibling, src=x_ref)]
        first += [copy(1 + j, me, (*chip, c), src=x_ref) for j, chip in enumerate(chips)]
        for cp in first:
            cp.start()
        passed = [copy(4 + j, (*chip, c), sibling) for j, chip in enumerate(chips)]
        for j, chip in enumerate(chips):
            copy(1 + j, (*chip, c), me).wait_recv()
            passed[j].start()
        copy(0, sibling, me).wait_recv()
        for j, chip in enumerate(chips):
            copy(4 + j, (*chip, 1 - c), me).wait_recv()
        for cp in first + passed:
            cp.wait_send()
        mine.wait()

    return pl.pallas_call(
        body, name=name, out_shape=_sds((N_DEV * m_per, n), a.dtype),
        in_specs=[pl.BlockSpec(memory_space=pltpu.VMEM)], out_specs=pl.BlockSpec(memory_space=pltpu.VMEM),
        scratch_shapes=[pltpu.SemaphoreType.DMA((7,)), pltpu.SemaphoreType.DMA((7,)), pltpu.SemaphoreType.DMA],
        interpret=False,
    )(a)


def _region(ref, kind, shard_shape, slot, half, piece=(0, 1)):
    r, cs = shard_shape
    hr = r // 2
    n = hr // piece[1]
    start = half * hr + piece[0] * n
    if kind == "col":
        return ref.at[pl.ds(start, n), pl.ds(slot * cs, cs)]
    return ref.at[pl.ds(slot * r + start, n), :]


def _full_shape(kind, shard_shape):
    r, cs = shard_shape
    return (r, 4 * cs) if kind == "col" else (4 * r, cs)


_HBM = pl.BlockSpec(memory_space=pltpu.HBM)


def _in_place(arrays, n_sem, copies):
    return _Rider(arrays, [_sds(t.shape, t.dtype) for t in arrays], {a: a for a in range(len(arrays))}, n_sem, copies)


def _gather_chips(fulls, kinds, shapes, piece=(0, 1)):
    n = len(fulls)

    def copies(ins, outs, send_sems, recv_sems):
        x, y, c, chips = _place()
        me_slot = 2 * x + y
        sends, recvs = [], []
        for a in range(n):
            for j, (cx, cy) in enumerate(chips):
                k = 3 * a + j
                sends.append(_remote(_region(ins[a], kinds[a], shapes[a], me_slot, c, piece),
                                     _region(outs[a], kinds[a], shapes[a], me_slot, c, piece),
                                     send_sems.at[k], recv_sems.at[k], (cx, cy, c)))
                land = _region(outs[a], kinds[a], shapes[a], 2 * cx + cy, c, piece)
                recvs.append(_remote(land, land, send_sems.at[k], recv_sems.at[k], (cx, cy, c)))
        return sends, recvs

    return _in_place(fulls, 3 * n, copies)


def _gather_pair(fulls, kinds, shapes):
    n = len(fulls)

    def copies(ins, outs, send_sems, recv_sems):
        x, y, c, chips = _place()
        sibling = (x, y, 1 - c)
        sends, recvs = [], []
        for a in range(n):
            for j, (cx, cy) in enumerate(chips):
                k = 3 * a + j
                sends.append(_remote(_region(ins[a], kinds[a], shapes[a], 2 * cx + cy, c), _region(outs[a], kinds[a], shapes[a], 2 * cx + cy, c),
                                     send_sems.at[k], recv_sems.at[k], sibling))
                land = _region(outs[a], kinds[a], shapes[a], 2 * cx + cy, 1 - c)
                recvs.append(_remote(land, land, send_sems.at[k], recv_sems.at[k], sibling))
        return sends, recvs

    return _in_place(fulls, 3 * n, copies)


def _rs_pair(grads, kinds, shapes):
    n = len(grads)

    def copies(ins, outs, send_sems, recv_sems):
        x, y, c, _ = _place()
        sibling = (x, y, 1 - c)
        sends, recvs = [], []
        for a in range(n):
            for slot in range(4):
                k = 4 * a + slot
                sends.append(_remote(_region(ins[a], kinds[a], shapes[a], slot, 1 - c), outs[a].at[slot], send_sems.at[k],
                                     recv_sems.at[k], sibling))
                recvs.append(_remote(_region(ins[a], kinds[a], shapes[a], slot, c), outs[a].at[slot], send_sems.at[k],
                                     recv_sems.at[k], sibling))
        return sends, recvs

    return _Rider(grads, [_sds((4, s[0] // 2, s[1]), F32) for s in shapes], {}, 4 * n, copies)


def _rs_pair_add(grad, recv, kind, shape, c_arr, name):
    r, cs = shape
    hr = r // 2
    tr, tc = _wide_tile(hr, cs)
    nr, nc = hr // tr, cs // tc

    def body(c_ref, g_ref, r_ref, o_ref):
        o_ref[...] = (g_ref[...] + r_ref[...]).astype(BF16)

    if kind == "col":
        g_map = lambda s_, i, j, c_ref: (c_ref[0] * nr + i, s_ * nc + j)
    else:
        g_map = lambda s_, i, j, c_ref: (s_ * 2 * nr + c_ref[0] * nr + i, j)
    own = pl.BlockSpec((None, tr, tc), lambda s_, i, j, c_ref: (s_, i, j))
    return _call(body, name=name, grid=(4, nr, nc), in_specs=[pl.BlockSpec((tr, tc), g_map), own], out_specs=own,
                 out_shape=_sds((4, hr, cs), BF16), nsp=1)(c_arr, grad, recv)


def _rs_chips(parts, piece=(0, 1), dest=None):
    n = len(parts)

    def copies(ins, outs, send_sems, recv_sems):
        x, y, c, chips = _place()
        sends, recvs = [], []
        for a in range(n):
            rows = parts[a].shape[1] // piece[1]
            sel = pl.ds(piece[0] * rows, rows)
            for j, (cx, cy) in enumerate(chips):
                k = 3 * a + j
                cp = _remote(ins[a].at[2 * cx + cy, sel], outs[a].at[j, sel], send_sems.at[k], recv_sems.at[k], (cx, cy, c))
                sends.append(cp)
                recvs.append(cp)
        return sends, recvs

    out_shapes = [_sds((3,) + p.shape[1:], BF16) for p in parts]
    if dest is None:
        return _Rider(parts, out_shapes, {}, 3 * n, copies)
    return _Rider(list(parts) + list(dest), out_shapes, {n + a: a for a in range(n)}, 3 * n, copies)


def _rs_chip_add(part, recv, slot_arr, c_arr, dest, layer, out_shape, name):
    _, hr, cs = part.shape
    tr, tc = _wide_tile(hr, cs)
    nr = hr // tr

    def body(s_ref, c_ref, p_ref, r0_ref, r1_ref, r2_ref, *rest):
        o_ref = rest[-1]
        o_ref[...] = ((p_ref[...].astype(F32) + r0_ref[...].astype(F32)) + r1_ref[...].astype(F32)) + r2_ref[...].astype(F32)

    rk = lambda k: pl.BlockSpec((None, tr, tc), lambda i, j, s_ref, c_ref: (k, i, j))
    if layer is None:
        o_spec = pl.BlockSpec((tr, tc), lambda i, j, s_ref, c_ref: (c_ref[0] * nr + i, j))
    else:
        o_spec = pl.BlockSpec((None, tr, tc), lambda i, j, s_ref, c_ref: (layer, c_ref[0] * nr + i, j))
    in_specs = [pl.BlockSpec((None, tr, tc), lambda i, j, s_ref, c_ref: (s_ref[0], i, j)), rk(0), rk(1), rk(2)]
    args = [slot_arr, c_arr, part, recv, recv, recv]
    aliases = {}
    if dest is not None:
        in_specs.append(pl.BlockSpec(memory_space=pl.ANY))
        args.append(dest)
        aliases = {6: 0}
    return pl.pallas_call(
        body, name=name,
        grid_spec=pltpu.PrefetchScalarGridSpec(num_scalar_prefetch=2, grid=(nr, cs // tc), in_specs=in_specs, out_specs=o_spec),
        out_shape=_sds(out_shape, F32), input_output_aliases=aliases,
        compiler_params=pltpu.CompilerParams(dimension_semantics=("arbitrary",) * 2, vmem_limit_bytes=VMEM_LIMIT_MB << 20),
        interpret=False,
    )(*args)


def _rs_pair_share(shards, name):
    n = len(shards)
    views = []
    for a, t in enumerate(shards):
        views += [(a, None)] if t.ndim == 2 else [(a, l) for l in range(t.shape[0])]
    nv = len(views)

    def body(*refs):
        ins, outs = refs[0:n], refs[n:2 * n]
        send_sems, recv_sems = refs[2 * n:]
        x, y, c, _ = _place()
        sibling = (x, y, 1 - c)

        def rows(ref_list, k, half):
            a, layer = views[k]
            ref = ref_list[a] if layer is None else ref_list[a].at[layer]
            hr = ref.shape[0] // 2
            return ref.at[pl.ds(half * hr, hr), :]

        sent = []
        for k in range(nv):
            cp = _remote(rows(ins, k, c), rows(outs, k, c), send_sems.at[k], recv_sems.at[k], sibling)
            cp.start()
            sent.append(cp)
        for k in range(nv):
            _remote(rows(ins, k, 1 - c), rows(outs, k, 1 - c), send_sems.at[k], recv_sems.at[k], sibling).wait_recv()
        for cp in sent:
            cp.wait_send()

    return pl.pallas_call(
        body, name=name, out_shape=[_sds(t.shape, F32) for t in shards], in_specs=[_HBM] * n, out_specs=[_HBM] * n,
        input_output_aliases={a: a for a in range(n)},
        scratch_shapes=[pltpu.SemaphoreType.DMA((nv,)), pltpu.SemaphoreType.DMA((nv,))], interpret=False,
    )(*shards)


def _pad_to(v, mult):
    n = v.shape[0]
    return jnp.pad(v, (0, (-n) % mult))


def kernel(x, c, positions, mod_w, mod_b, norm_mix_g, norm_ffn_g, conv_pw1_w, conv_pw1_b, conv_dw_w, conv_dw_b, conv_ln_g, conv_ln_b, conv_pw2_w, conv_pw2_b, kv_mod_w, kv_mod_b, kv_norm_g, w_kv, k_norm_g, w_q, q_norm_g, w_o, ffn_up_w, ffn_dw_w, ffn_dw_b, ffn_down_w, loss_target, m_mod_w, m_mod_b, m_norm_mix_g, m_norm_ffn_g, m_conv_pw1_w, m_conv_pw1_b, m_conv_dw_w, m_conv_dw_b, m_conv_ln_g, m_conv_ln_b, m_conv_pw2_w, m_conv_pw2_b, m_kv_mod_w, m_kv_mod_b, m_kv_norm_g, m_w_kv, m_k_norm_g, m_w_q, m_q_norm_g, m_w_o, m_ffn_up_w, m_ffn_dw_w, m_ffn_dw_b, m_ffn_down_w, v_mod_w, v_mod_b, v_norm_mix_g, v_norm_ffn_g, v_conv_pw1_w, v_conv_pw1_b, v_conv_dw_w, v_conv_dw_b, v_conv_ln_g, v_conv_ln_b, v_conv_pw2_w, v_conv_pw2_b, v_kv_mod_w, v_kv_mod_b, v_kv_norm_g, v_w_kv, v_k_norm_g, v_w_q, v_q_norm_g, v_w_o, v_ffn_up_w, v_ffn_dw_w, v_ffn_dw_b, v_ffn_down_w):
    _, s, d = x.shape
    f = ffn_dw_b.shape[1]
    qw = w_q.shape[2] * 4
    ax, ay, ac = lax.axis_index("x"), lax.axis_index("y"), lax.axis_index("c")
    slot = 2 * ax + ay
    me8 = 4 * ax + 2 * ay + ac
    slot_arr = jnp.reshape(slot, (1,)).astype(jnp.int32)
    c_arr = jnp.reshape(ac, (1,)).astype(jnp.int32)
    x2 = x.reshape(s, d)
    target = loss_target.reshape(s, d)
    row = lambda v: v.reshape(1, -1)

    c_all = _allgather8(c.reshape(8, d // 8), "gather_c").reshape(N_DEV, d)
    sc_all = jax.nn.silu(c_all)
    mod_part = _mod_proj(sc_all, mod_w, "mod_proj")
    kvm_part = _mod_proj(sc_all, kv_mod_w[None], "kvmod_proj")
    nm, nk = mod_part.shape[2], kvm_part.shape[2]
    small_sharded = [conv_pw1_b, conv_dw_w, conv_dw_b, conv_ln_g, conv_ln_b, conv_pw2_b, ffn_dw_w]
    pack = jnp.concatenate([mod_part.reshape(-1), kvm_part.reshape(-1)] + [t.reshape(-1) for t in small_sharded])
    plen = pack.shape[0]
    pack = _pad_to(pack, 1024)
    gathered = _allgather8(pack.reshape(8, -1), "gather_mod").reshape(4, 2, -1)[:, 0, :plen]
    off = 0

    def take(n_el):
        nonlocal off
        out = lax.slice_in_dim(gathered, off, off + n_el, axis=1)
        off += n_el
        return out

    mod_g = take(2 * N_DEV * nm).reshape(4, 2, N_DEV, nm)
    kvm_g = take(N_DEV * nk).reshape(4, N_DEV, nk)
    mine = lambda t, axis: lax.dynamic_index_in_dim(t, me8, axis=axis, keepdims=False)
    mod_vec = jnp.transpose(mine(mod_g, 2), (1, 0, 2)).reshape(2, 4 * nm) + mod_b
    kvm_vec = mine(kvm_g, 1).reshape(4 * nk) + kv_mod_b
    pw1_b_full = take(conv_pw1_b.shape[1]).reshape(1, -1)
    dw_w_full = jnp.transpose(take(CONV_K * (d // 4)).reshape(4, CONV_K, d // 4), (1, 0, 2)).reshape(CONV_K, d)
    dw_b_full, ln_g_full, ln_b_full, pw2_b_full = (take(d // 4).reshape(1, d) for _ in range(4))
    fdw_full = jnp.transpose(take(2 * FFN_K * (f // 4)).reshape(4, 2, FFN_K, f // 4), (1, 2, 0, 3)).reshape(2, FFN_K, f)
    mods = [[row(mod_vec[l, k * d:(k + 1) * d]) for k in range(6)] for l in range(2)]
    kv_sh, kv_sc = row(kvm_vec[:d]), row(kvm_vec[d:])
    zero_d = jnp.zeros((1, d), F32)

    big = [("pw1", conv_pw1_w[0], "col"), ("pw2", conv_pw2_w[0], "row"), ("wkv", w_kv, "col"), ("wq", w_q[0], "col"),
           ("wo", w_o[0], "col"), ("up0", ffn_up_w[0], "col"), ("up1", ffn_up_w[1], "col"), ("dn0", ffn_down_w[0], "row"),
           ("dn1", ffn_down_w[1], "row")]
    names = [b[0] for b in big]
    kinds = [b[2] for b in big]
    shard_shapes = [b[1].shape for b in big]
    own = {b[0]: _cast_into_full(b[1], b[2], slot_arr, "cast_" + b[0]) for b in big}
    meta = dict(zip(names, zip(kinds, shard_shapes)))
    full = {}

    def stage(make, group, src, **kw):
        return make([src[k] for k in group], [meta[k][0] for k in group], [meta[k][1] for k in group], **kw)

    grp_a, grp_b, grp_c, grp_d = ["pw1", "pw2"], ["up0", "dn0"], ["wkv", "wq", "wo"], ["up1", "dn1"]
    landed_a = dict(zip(grp_a, _run_rider(stage(_gather_chips, grp_a, own), "gather_a_chips")))
    full.update(zip(grp_a, _run_rider(stage(_gather_pair, grp_a, landed_a), "gather_a_pair")))
    chips_c = stage(_gather_chips, grp_c, own)

    def pair_stage(group, chips_rider):
        return stage(_gather_pair, group, dict(zip(group, chips_rider.results)))

    def ffn_fwd(l, x_in, h, tag, rider_up=None, rider_down=None):
        u = _matmul(h, full["up%d" % l], "nn", F32, "mm_up" + tag, rider=rider_up)
        z = _ffn_act_fwd(u, fdw_full[l], row(ffn_dw_b[l]), "ffn_act" + tag)
        y = _matmul(z, full["dn%d" % l], "nn", F32, "mm_down" + tag, rider=rider_down)
        return u, z, y

    sh_m0, sc_m0, g_m0, sh_f0, sc_f0, g_f0 = mods[0]
    sh_m1, sc_m1, g_m1, sh_f1, sc_f1, g_f1 = mods[1]
    gmix0, gmix1, gffn0, gffn1 = row(norm_mix_g[0]), row(norm_mix_g[1]), row(norm_ffn_g[0]), row(norm_ffn_g[1])
    (h0,) = _resid_mod(x2, None, [(gmix0, sc_m0, sh_m0)], None, "mod_in")
    chips_b0 = stage(_gather_chips, grp_b, own, piece=(0, 2))
    u0 = _matmul(h0, full["pw1"], "nn", F32, "mm_pw1", rider=chips_b0)
    glu = _glu_fwd(u0, pw1_b_full, "glu")
    chips_b = stage(_gather_chips, grp_b, dict(zip(grp_b, chips_b0.results)), piece=(1, 2))
    dwc = _dwconv_fwd(glu, dw_w_full, dw_b_full, "dwconv", rider=chips_b)
    act = _ln_silu_fwd(dwc, ln_g_full, ln_b_full, "ln_silu")
    pair_b = pair_stage(grp_b, chips_b)
    y0 = _matmul(act, full["pw2"], "nn", F32, "mm_pw2", rider=pair_b)
    full.update(zip(grp_b, pair_b.results))
    x1, hf0 = _resid_mod(x2, (y0, pw2_b_full, g_m0), [(gffn0, sc_f0, sh_f0)], None, "resid_conv")
    uf0 = _matmul(hf0, full["up0"], "nn", F32, "mm_up0", rider=chips_c)
    zf0 = _ffn_act_fwd(uf0, fdw_full[0], row(ffn_dw_b[0]), "ffn_act0")
    pair_c = pair_stage(grp_c, chips_c)
    yf0 = _matmul(zf0, full["dn0"], "nn", F32, "mm_down0", rider=pair_c)
    full.update(zip(grp_c, pair_c.results))
    gkv = row(kv_norm_g)
    xa, hk, hq = _resid_mod(x1, (yf0, zero_d, g_f0), [(gkv, kv_sc, kv_sh), (gmix1, sc_m1, sh_m1)], None, "resid_ffn0")
    chips_d0 = stage(_gather_chips, grp_d, own, piece=(0, 2))
    kvp = _matmul(hk, full["wkv"], "nn", F32, "mm_kv", rider=chips_d0)
    chips_d = stage(_gather_chips, grp_d, dict(zip(grp_d, chips_d0.results)), piece=(1, 2))
    qp = _matmul(hq, full["wq"], "nn", F32, "mm_q")
    inv_freq = ROPE_THETA ** (-jnp.arange(0, ROT_DIM, 2, dtype=F32) / ROT_DIM)
    half = ROT_DIM // 2
    freq_l = jnp.concatenate([inv_freq, inv_freq, jnp.zeros((HEAD_DIM - ROT_DIM,), F32)]).reshape(1, HEAD_DIM)
    sign_l = jnp.concatenate([-jnp.ones((half,), F32), jnp.ones((half,), F32), jnp.zeros((HEAD_DIM - ROT_DIM,), F32)]).reshape(1, HEAD_DIM)
    ctab, stab = _rope_tables(positions.reshape(s, 1), freq_l, sign_l, "rope_tables")
    gq, gk = row(q_norm_g[0]), row(k_norm_g)
    qn, kn = _qk_norm_rope_fwd(qp, kvp, gq, gk, ctab, stab, "qk_norm_rope", rider=chips_d)
    og, lg = zip(*[_attn_fwd(qn, kn, kvp, g, "attn_fwd%d" % g) for g in range(N_GROUPS)])
    o_mix, lse = _attn_combine(og, lg, "attn_combine")
    pair_d = pair_stage(grp_d, chips_d)
    ya = _matmul(o_mix, full["wo"], "nn", F32, "mm_o", rider=pair_d)
    full.update(zip(grp_d, pair_d.results))
    xb, hf1 = _resid_mod(xa, (ya, zero_d, g_m1), [(gffn1, sc_f1, sh_f1)], None, "resid_attn")
    uf1, zf1, yf1 = ffn_fwd(1, xb, hf1, "1")
    _, dxo, loss_cols = _resid_mod(xb, (yf1, zero_d, g_f1), [], target, "resid_loss")
    loss = lax.psum(jnp.sum(loss_cols), ("x", "y", "c"))

    gbig = {}

    parts, recv2 = {}, {}

    def rs_pair(group):
        return _rs_pair([gbig[k] for k in group], [meta[k][0] for k in group], [meta[k][1] for k in group])

    def rs_chips(group, pair_results, **kw):
        for k, r_ in zip(group, pair_results):
            parts[k] = _rs_pair_add(gbig[k], r_, meta[k][0], meta[k][1], c_arr, "rs_pair_add_" + k)
        return _rs_chips([parts[k] for k in group], **kw)

    rs_a, rs_b, rs_c, rs_d = ["dn1", "up1"], ["wo", "wq", "wkv"], ["dn0", "up0"], ["pw2", "pw1"]

    def ffn_bwd(l, dy, u, z, h, tag, rider_dx=None, act_rider=lambda: None):
        dz = _matmul(dy, full["dn%d" % l], "nt", F32, "mm_down_dx" + tag, rider=rider_dx)
        gbig["dn%d" % l] = _matmul(z, dy, "tn", F32, "mm_down_dw" + tag)
        du, dfw, dfb = _ffn_act_bwd(dz, u, fdw_full[l], row(ffn_dw_b[l]), "ffn_act_bwd" + tag, rider=act_rider())
        dh = _matmul(du, full["up%d" % l], "nt", F32, "mm_up_dx" + tag)
        gbig["up%d" % l] = _matmul(h, du, "tn", F32, "mm_up_dw" + tag)
        return dh, dfw, dfb

    dxb0, dyf1, dg_f1, _ = _bwd_step(dxo, xb, [], (yf1, zero_d, g_f1), "bwd_loss")
    dhf1, dfw1, dfb1 = ffn_bwd(1, dyf1, uf1, zf1, hf1, "1")
    dxb, dsh_f1, dsc_f1, dgffn1, dya, dg_m1, _ = _bwd_step(dxb0, xb, [(dhf1, gffn1, sc_f1)], (ya, zero_d, g_m1), "bwd_attn_out")
    do = _matmul(dya, full["wo"], "nt", F32, "mm_o_dx")
    gbig["wo"] = _matmul(o_mix, dya, "tn", F32, "mm_o_dw")
    delta = _attn_delta(do, o_mix, "attn_delta")
    pair_a = rs_pair(rs_a)
    dqs = [_attn_bwd_dq(qn, kn, kvp, do, lse, delta, g, "attn_dq%d" % g, rider=pair_a if g == 0 else None) for g in range(N_GROUPS)]
    dks, dvs = zip(*[_attn_bwd_dkv(qn, kn, kvp, do, lse, delta, g, "attn_dkv%d" % g) for g in range(N_GROUPS)])
    ride_a0 = rs_chips(rs_a, pair_a.results, piece=(0, 2))
    dqp, dkvp, dgq, dgk = _qk_norm_rope_bwd(dqs, dks, dvs, qp, kvp, gq, gk, ctab, stab, "qk_norm_rope_bwd", rider=ride_a0)
    dhq = _matmul(dqp, full["wq"], "nt", F32, "mm_q_dx")
    gbig["wq"] = _matmul(hq, dqp, "tn", F32, "mm_q_dw")
    ride_a = _rs_chips([parts[k] for k in rs_a], piece=(1, 2), dest=ride_a0.results)
    dhk = _matmul(dkvp, full["wkv"], "nt", F32, "mm_kv_dx", rider=ride_a)
    recv2.update(zip(rs_a, ride_a.results))
    gbig["wkv"] = _matmul(hk, dkvp, "tn", F32, "mm_kv_dw")
    (dxa, dsh_kv, dsc_kv, dgkv, dsh_m1, dsc_m1, dgmix1, dyf0, dg_f0, _) = _bwd_step(
        dxb, xa, [(dhk, gkv, kv_sc), (dhq, gmix1, sc_m1)], (yf0, zero_d, g_f0), "bwd_kvq")
    pair_b = rs_pair(rs_b)
    ride_b = []
    dhf0, dfw0, dfb0 = ffn_bwd(0, dyf0, uf0, zf0, hf0, "0", rider_dx=pair_b,
                               act_rider=lambda: ride_b.append(rs_chips(rs_b, pair_b.results)) or ride_b[0])
    recv2.update(zip(rs_b, ride_b[0].results))
    pair_c = rs_pair(rs_c)
    dx1, dsh_f0, dsc_f0, dgffn0, dy0, dg_m0, dpw2_b = _bwd_step(dxa, x1, [(dhf0, gffn0, sc_f0)], (y0, pw2_b_full, g_m0), "bwd_conv_out",
                                                             rider=pair_c)
    dact = _matmul(dy0, full["pw2"], "nt", F32, "mm_pw2_dx")
    gbig["pw2"] = _matmul(act, dy0, "tn", F32, "mm_pw2_dw")
    ddwc, dln_g, dln_b = _ln_silu_bwd(dact, dwc, ln_g_full, ln_b_full, "ln_silu_bwd")
    ride_c = rs_chips(rs_c, pair_c.results)
    dglu, ddw_w, ddw_b = _dwconv_bwd(ddwc, glu, dw_w_full, "dwconv_bwd", rider=ride_c)
    recv2.update(zip(rs_c, ride_c.results))
    du0, dpw1_b = _glu_bwd(dglu, u0, pw1_b_full, "glu_bwd")
    dh0 = _matmul(du0, full["pw1"], "nt", F32, "mm_pw1_dx")
    gbig["pw1"] = _matmul(h0, du0, "tn", F32, "mm_pw1_dw")
    grad_x, dsh_m0, dsc_m0, dgmix0 = _bwd_step(dx1, x2, [(dh0, gmix0, sc_m0)], None, "bwd_in")

    dmod = [jnp.concatenate([dsh_m0, dsc_m0, dg_m0, dsh_f0, dsc_f0, dg_f0], axis=1),
            jnp.concatenate([dsh_m1, dsc_m1, dg_m1, dsh_f1, dsc_f1, dg_f1], axis=1)]
    dkvm = jnp.concatenate([dsh_kv, dsc_kv], axis=1)
    per_ex = [dmod[0], dmod[1], dkvm]
    summed = [dgmix0, dgmix1, dgffn0, dgffn1, dpw1_b, ddw_w, ddw_b, dln_g, dln_b, dpw2_b, dgkv, dgk, dgq, dfw0, dfw1, dfb0, dfb1]
    vec = jnp.concatenate([t.reshape(-1) for t in per_ex + summed])
    vlen = vec.shape[0]
    vec = _pad_to(vec, 1024)
    vall = _allgather8(vec.reshape(8, -1), "gather_small").reshape(N_DEV, -1)
    vsum = _sum_rows(vall, "sum_small")[0]
    n_pe = sum(t.size for t in per_ex)
    dm_all = vall[:, :n_pe]
    off2 = n_pe
    sums = []
    for t in summed:
        sums.append(vsum[off2:off2 + t.size].reshape(t.shape))
        off2 += t.size
    (s_gmix0, s_gmix1, s_gffn0, s_gffn1, s_pw1_b, s_dw_w, s_dw_b, s_ln_g, s_ln_b, s_pw2_b, s_gkv, s_gk, s_gq, s_fw0, s_fw1,
     s_fb0, s_fb1) = sums
    shard_cols = lambda t, width: lax.dynamic_slice_in_dim(t, slot * width, width, axis=t.ndim - 1)
    dm_mod = jnp.stack([shard_cols(dm_all[:, l * 6 * d:(l + 1) * 6 * d], nm) for l in range(2)])
    dm_kv = shard_cols(dm_all[:, 12 * d:14 * d], nk)[None]
    sct = jnp.transpose(sc_all)

    recv2.update(zip(rs_d, _run_rider(rs_chips(rs_d, _run_rider(rs_pair(rs_d), "rs_pair_d")), "rs_chips_d")))
    reduced = {}
    for nme, shp in zip(names, shard_shapes):
        p_, r_ = parts[nme], recv2[nme]
        if nme in ("up0", "up1", "dn0", "dn1"):
            key, layer = nme[:-1], int(nme[-1])
            reduced[key] = _rs_chip_add(p_, r_, slot_arr, c_arr, reduced.get(key), layer, (2,) + shp, "rs_chip_add_" + nme)
        else:
            reduced[nme] = _rs_chip_add(p_, r_, slot_arr, c_arr, None, None, shp, "rs_chip_add_" + nme)
    g_pw1, g_pw2, g_wkv, g_wq, g_wo, g_up, g_dn = _rs_pair_share(
        [reduced[k] for k in ("pw1", "pw2", "wkv", "wq", "wo", "up", "dn")], "rs_pair_share")

    grads, deltas, new_m, new_v = {}, {}, {}, {}

    def put(nme, g_, res):
        grads[nme] = g_
        deltas[nme], new_m[nme], new_v[nme] = res

    for nme, g_, w_, m_, v_ in (("conv_pw1_w", g_pw1[None], conv_pw1_w, m_conv_pw1_w, v_conv_pw1_w),
                                ("conv_pw2_w", g_pw2[None], conv_pw2_w, m_conv_pw2_w, v_conv_pw2_w),
                                ("w_kv", g_wkv, w_kv, m_w_kv, v_w_kv), ("w_q", g_wq[None], w_q, m_w_q, v_w_q),
                                ("w_o", g_wo[None], w_o, m_w_o, v_w_o), ("ffn_up_w", g_up, ffn_up_w, m_ffn_up_w, v_ffn_up_w),
                                ("ffn_down_w", g_dn, ffn_down_w, m_ffn_down_w, v_ffn_down_w)):
        put(nme, g_, _adamw_big(w_, g_, m_, v_, "adamw_" + nme))
    g_, *res = _adamw_mod(mod_w, sct, dm_mod, m_mod_w, v_mod_w, "adamw_mod_w")
    put("mod_w", g_, res)
    g_, *res = _adamw_mod(kv_mod_w[None], sct, dm_kv, m_kv_mod_w[None], v_kv_mod_w[None], "adamw_kv_mod_w")
    put("kv_mod_w", g_[0], [t[0] for t in res])

    dm_sum = vsum[:n_pe]
    small = [
        ("mod_b", dm_sum[:12 * d].reshape(2, 6 * d), mod_b, m_mod_b, v_mod_b),
        ("norm_mix_g", jnp.concatenate([s_gmix0, s_gmix1], axis=0), norm_mix_g, m_norm_mix_g, v_norm_mix_g),
        ("norm_ffn_g", jnp.concatenate([s_gffn0, s_gffn1], axis=0), norm_ffn_g, m_norm_ffn_g, v_norm_ffn_g),
        ("conv_pw1_b", shard_cols(s_pw1_b, conv_pw1_b.shape[1]), conv_pw1_b, m_conv_pw1_b, v_conv_pw1_b),
        ("conv_dw_w", shard_cols(s_dw_w, d // 4)[None], conv_dw_w, m_conv_dw_w, v_conv_dw_w),
        ("conv_dw_b", shard_cols(s_dw_b, d // 4), conv_dw_b, m_conv_dw_b, v_conv_dw_b),
        ("conv_ln_g", shard_cols(s_ln_g, d // 4), conv_ln_g, m_conv_ln_g, v_conv_ln_g),
        ("conv_ln_b", shard_cols(s_ln_b, d // 4), conv_ln_b, m_conv_ln_b, v_conv_ln_b),
        ("conv_pw2_b", shard_cols(s_pw2_b, d // 4), conv_pw2_b, m_conv_pw2_b, v_conv_pw2_b),
        ("kv_mod_b", dm_sum[12 * d:14 * d], kv_mod_b, m_kv_mod_b, v_kv_mod_b),
        ("kv_norm_g", s_gkv.reshape(-1), kv_norm_g, m_kv_norm_g, v_kv_norm_g),
        ("k_norm_g", s_gk.reshape(-1), k_norm_g, m_k_norm_g, v_k_norm_g),
        ("q_norm_g", s_gq, q_norm_g, m_q_norm_g, v_q_norm_g),
        ("ffn_dw_w", shard_cols(jnp.stack([s_fw0, s_fw1]), f // 4), ffn_dw_w, m_ffn_dw_w, v_ffn_dw_w),
        ("ffn_dw_b", jnp.concatenate([s_fb0, s_fb1], axis=0), ffn_dw_b, m_ffn_dw_b, v_ffn_dw_b),
    ]
    as2d = lambda t: t.reshape(-1, t.shape[-1])
    sd_, sm_, sv_ = _adamw_small([as2d(t[2]) for t in small], [as2d(t[1]) for t in small], [as2d(t[3]) for t in small],
                                 [as2d(t[4]) for t in small], "adamw_small")
    for (nme, g_, w_, _, _), d_, mo_, vo_ in zip(small, sd_, sm_, sv_):
        put(nme, g_.reshape(w_.shape), [d_.reshape(w_.shape), mo_.reshape(w_.shape), vo_.reshape(w_.shape)])

    order = ["mod_w", "mod_b", "norm_mix_g", "norm_ffn_g", "conv_pw1_w", "conv_pw1_b", "conv_dw_w", "conv_dw_b", "conv_ln_g",
             "conv_ln_b", "conv_pw2_w", "conv_pw2_b", "kv_mod_w", "kv_mod_b", "kv_norm_g", "w_kv", "k_norm_g", "w_q", "q_norm_g",
             "w_o", "ffn_up_w", "ffn_dw_w", "ffn_dw_b", "ffn_down_w"]
    return (loss, grad_x.reshape(x.shape), *[grads[k] for k in order], *[deltas[k] for k in order], *[new_m[k] for k in order],
            *[new_v[k] for k in order])
```

```python
import functools
import math

import jax
import jax.numpy as jnp
from jax import lax
from jax.experimental import pallas as pl
from jax.experimental.pallas import tpu as pltpu

F32 = jnp.float32
BF16 = jnp.bfloat16
EPS = 1e-6
NEG = -1e30
HEAD_DIM = 128
ROT_DIM = 32
ROPE_THETA = 500000.0
BLK = 128
DILATIONS = (1, 4, 16)
N_GROUPS = 3
CONV_K = 31
FFN_K = 3
ADAM_LR, ADAM_B1, ADAM_B2, ADAM_EPS, ADAM_WD, ADAM_STEP = 0.001, 0.9, 0.999, 1e-08, 0.01, 10
N_DEV = 8
MESH = pl.DeviceIdType.MESH
VMEM_LIMIT_MB = 56
ROW_TILE = 256
ATTN_CHUNK = 2048


def _pick(n, pref, mult=128):
    best = None
    d = mult
    while d <= min(n, pref):
        if n % d == 0:
            best = d
        d += mult
    return best if best is not None else n


def _wide_tile(rows, cols):
    tc = _pick(cols, 4096)
    return _pick(rows, max(16, (1 << 19) // tc), 16), tc


class _Rider:
    def __init__(self, ins, out_shapes, aliases, n_sem, copies):
        self.ins, self.out_shapes, self.aliases, self.n_sem, self.copies = list(ins), list(out_shapes), dict(aliases), n_sem, copies
        self.results = None


def _call(body, *, name, grid, in_specs, out_specs, out_shape, scratch=(), nsp=0, rider=None):
    params = pltpu.CompilerParams(dimension_semantics=("arbitrary",) * len(grid), vmem_limit_bytes=VMEM_LIMIT_MB << 20)
    if rider is None:
        return pl.pallas_call(
            body, name=name,
            grid_spec=pltpu.PrefetchScalarGridSpec(num_scalar_prefetch=nsp, grid=grid, in_specs=in_specs, out_specs=out_specs,
                                                   scratch_shapes=list(scratch)),
            out_shape=out_shape, compiler_params=params, interpret=False,
        )
    single = not isinstance(out_shape, (list, tuple))
    out_shapes = [out_shape] if single else list(out_shape)
    out_specs_l = [out_specs] if single else list(out_specs)
    n_in, n_out, n_scr = len(in_specs), len(out_shapes), len(scratch)
    r_in, r_out = len(rider.ins), len(rider.out_shapes)
    hbm = pl.BlockSpec(memory_space=pltpu.HBM)
    last = tuple(g - 1 for g in grid)

    def wrapped(*refs):
        pre, ins, rin = refs[:nsp], refs[nsp:nsp + n_in], refs[nsp + n_in:nsp + n_in + r_in]
        o0 = nsp + n_in + r_in
        outs, rout = refs[o0:o0 + n_out], refs[o0 + n_out:o0 + n_out + r_out]
        s0 = o0 + n_out + r_out
        scr, (send_sems, recv_sems) = refs[s0:s0 + n_scr], refs[s0 + n_scr:]
        ids = [pl.program_id(a) for a in range(len(grid))]
        is_first = functools.reduce(jnp.logical_and, [i == 0 for i in ids])
        is_last = functools.reduce(jnp.logical_and, [i == l for i, l in zip(ids, last)])

        @pl.when(is_first)
        def _():
            for cp in rider.copies(rin, rout, send_sems, recv_sems)[0]:
                cp.start()

        body(*pre, *ins, *outs, *scr)

        @pl.when(is_last)
        def _():
            sends, recvs = rider.copies(rin, rout, send_sems, recv_sems)
            for cp in recvs:
                cp.wait_recv()
            for cp in sends:
                cp.wait_send()

    call = pl.pallas_call(
        wrapped, name=name,
        grid_spec=pltpu.PrefetchScalarGridSpec(
            num_scalar_prefetch=nsp, grid=grid, in_specs=list(in_specs) + [hbm] * r_in, out_specs=out_specs_l + [hbm] * r_out,
            scratch_shapes=list(scratch) + [pltpu.SemaphoreType.DMA((rider.n_sem,)), pltpu.SemaphoreType.DMA((rider.n_sem,))]),
        out_shape=out_shapes + rider.out_shapes,
        input_output_aliases={nsp + n_in + i: n_out + o for i, o in rider.aliases.items()},
        compiler_params=params, interpret=False,
    )

    def run(*args):
        res = call(*args, *rider.ins)
        rider.results = list(res[n_out:])
        return res[0] if single else list(res[:n_out])

    return run


def _run_rider(rider, name):
    r_in, r_out = len(rider.ins), len(rider.out_shapes)
    hbm = pl.BlockSpec(memory_space=pltpu.HBM)

    def body(*refs):
        sends, recvs = rider.copies(refs[:r_in], refs[r_in:r_in + r_out], *refs[r_in + r_out:])
        for cp in sends:
            cp.start()
        for cp in recvs:
            cp.wait_recv()
        for cp in sends:
            cp.wait_send()

    return pl.pallas_call(
        body, name=name, out_shape=rider.out_shapes, in_specs=[hbm] * r_in, out_specs=[hbm] * r_out,
        input_output_aliases=rider.aliases,
        scratch_shapes=[pltpu.SemaphoreType.DMA((rider.n_sem,)), pltpu.SemaphoreType.DMA((rider.n_sem,))], interpret=False,
    )(*rider.ins)


def _sds(shape, dtype):
    return jax.ShapeDtypeStruct(shape, dtype)


def _acc(ref, val, i):
    @pl.when(i == 0)
    def _():
        ref[...] = val

    @pl.when(i > 0)
    def _():
        ref[...] += val


def _colsum(v):
    return jnp.sum(v, axis=0, keepdims=True)


def _silu(v):
    return v * jax.nn.sigmoid(v)


def _dsilu(v):
    s = jax.nn.sigmoid(v)
    return s * (1.0 + v * (1.0 - s))


_DIMS = {"nn": (((1,), (0,)), ((), ())), "nt": (((1,), (1,)), ((), ())), "tn": (((0,), (0,)), ((), ()))}


def _matmul(a, b, mode, out_dtype, name, rider=None):
    if mode == "tn":
        assert out_dtype == F32 and rider is None
        return _matmul_tn(a, b, name)
    a_halves = a.shape[0] if a.ndim == 3 else 0
    if mode == "nn":
        (m, c), (_, n) = a.shape, b.shape
    else:
        m, c = (a.shape[1], a.shape[0] * a.shape[2]) if a_halves else a.shape
        n = b.shape[0]
    tm, tn = _pick(m, 1024), _pick(n, 1024)
    tc = _pick(c // a_halves, 2816) if a_halves else _pick(c, 2816)
    nk = c // tc
    if a_halves:
        per_a = c // a_halves // tc
        a_spec = pl.BlockSpec((None, tm, tc), lambda i, j, k: (k // per_a, i, k % per_a))
    else:
        a_spec = pl.BlockSpec((tm, tc), lambda i, j, k: (i, k))
    b_spec = pl.BlockSpec((tc, tn), lambda i, j, k: (k, j)) if mode == "nn" else pl.BlockSpec((tn, tc), lambda i, j, k: (j, k))
    dims = _DIMS[mode]

    def body(a_ref, b_ref, o_ref, acc_ref):
        k = pl.program_id(2)
        p = lax.dot_general(a_ref[...], b_ref[...], dims, preferred_element_type=F32)
        if nk == 1:
            o_ref[...] = p.astype(out_dtype)
        else:
            @pl.when(k == 0)
            def _():
                acc_ref[...] = p

            @pl.when(k > 0)
            def _():
                acc_ref[...] += p

            @pl.when(k == nk - 1)
            def _():
                o_ref[...] = acc_ref[...].astype(out_dtype)

    return _call(
        body, name=name, grid=(m // tm, n // tn, nk), in_specs=[a_spec, b_spec],
        out_specs=pl.BlockSpec((tm, tn), lambda i, j, k: (i, j)), out_shape=_sds((m, n), out_dtype),
        scratch=[pltpu.VMEM((tm, tn), F32)], rider=rider,
    )(a, b)


def _matmul_tn(a, b, name):
    b_halves = b.shape[0] if b.ndim == 3 else 0
    c, m = a.shape
    n = b.shape[0] * b.shape[2] if b_halves else b.shape[1]
    tm = _pick(m, 512)
    tn = _pick(n // b_halves, 1024) if b_halves else _pick(n, 1024)
    tc = _pick(c, 2048)
    nk, nj = c // tc, n // tn
    if b_halves:
        per_b = nj // b_halves
        b_spec = pl.BlockSpec((None, tc, tn), lambda i, k, j: (j // per_b, k, j % per_b))
    else:
        b_spec = pl.BlockSpec((tc, tn), lambda i, k, j: (k, j))

    def body(a_ref, b_ref, o_ref, at_ref, acc_ref):
        k, j = pl.program_id(1), pl.program_id(2)

        @pl.when(j == 0)
        def _():
            at_ref[...] = a_ref[...].T

        p = jnp.dot(at_ref[...], b_ref[...], preferred_element_type=F32)

        @pl.when(k == 0)
        def _():
            acc_ref[j] = p

        @pl.when(k > 0)
        def _():
            acc_ref[j] += p

        @pl.when(k == nk - 1)
        def _():
            o_ref[...] = acc_ref[j]

    return _call(
        body, name=name, grid=(m // tm, nk, nj), in_specs=[pl.BlockSpec((tc, tm), lambda i, k, j: (k, i)), b_spec],
        out_specs=pl.BlockSpec((tm, tn), lambda i, k, j: (i, jnp.where(k == nk - 1, j, 0))), out_shape=_sds((m, n), F32),
        scratch=[pltpu.VMEM((tm, tc), BF16), pltpu.VMEM((nj, tm, tn), F32)],
    )(a, b)


def _row_spec(tr, w):
    return pl.BlockSpec((tr, w), lambda i: (i, 0))


def _vec_spec(w):
    return pl.BlockSpec((1, w), lambda i: (0, 0))


def _resid_mod(x, prev, mods, target, name):
    s, d = x.shape
    tr = _pick(s, ROW_TILE, 8)
    n_mod = len(mods)

    def body(*refs):
        it = iter(refs)
        x_ref = next(it)
        if prev is not None:
            y_ref, yb_ref, gate_ref = next(it), next(it), next(it)
        mod_refs = [(next(it), next(it), next(it)) for _ in range(n_mod)]
        if target is not None:
            t_ref = next(it)
        if prev is not None:
            xo_ref = next(it)
        h_refs = [next(it) for _ in range(n_mod)]
        i = pl.program_id(0)
        xv = x_ref[...]
        if prev is not None:
            xv = xv + gate_ref[...] * (y_ref[...] + yb_ref[...])
            xo_ref[...] = xv
        if n_mod:
            nrm = xv * lax.rsqrt(jnp.mean(xv * xv, axis=-1, keepdims=True) + EPS)
            for (g_ref, sc_ref, sh_ref), h_ref in zip(mod_refs, h_refs):
                h_ref[...] = (nrm * g_ref[...] * (1.0 + sc_ref[...]) + sh_ref[...]).astype(BF16)
        if target is not None:
            dx_ref, loss_ref = next(it), next(it)
            err = xv - t_ref[...]
            dx_ref[...] = err * (1.0 / d)
            _acc(loss_ref, _colsum(err * err) * (0.5 / d), i)

    ins, in_specs = [x], [_row_spec(tr, d)]
    if prev is not None:
        ins += list(prev)
        in_specs += [_row_spec(tr, d), _vec_spec(d), _vec_spec(d)]
    for g, sc, sh in mods:
        ins += [g, sc, sh]
        in_specs += [_vec_spec(d)] * 3
    if target is not None:
        ins.append(target)
        in_specs.append(_row_spec(tr, d))
    out_shape, out_specs = [], []
    if prev is not None:
        out_shape.append(_sds((s, d), F32))
        out_specs.append(_row_spec(tr, d))
    for _ in mods:
        out_shape.append(_sds((s, d), BF16))
        out_specs.append(_row_spec(tr, d))
    if target is not None:
        out_shape += [_sds((s, d), F32), _sds((1, d), F32)]
        out_specs += [_row_spec(tr, d), _vec_spec(d)]
    return _call(body, name=name, grid=(s // tr,), in_specs=in_specs, out_specs=out_specs, out_shape=out_shape)(*ins)


def _bwd_step(dx_up, x, mods, prev, name, rider=None):
    s, d = x.shape
    tr = _pick(s, ROW_TILE, 8)
    n_mod = len(mods)

    def body(*refs):
        it = iter(refs)
        dxu_ref, x_ref = next(it), next(it)
        mod_refs = [(next(it), next(it), next(it)) for _ in range(n_mod)]
        if prev is not None:
            y_ref, yb_ref, gate_ref = next(it), next(it), next(it)
        dx_ref = next(it)
        acc_refs = [(next(it), next(it), next(it)) for _ in range(n_mod)]
        i = pl.program_id(0)
        dx = dxu_ref[...]
        if n_mod:
            xv = x_ref[...]
            rstd = lax.rsqrt(jnp.mean(xv * xv, axis=-1, keepdims=True) + EPS)
            nrm = xv * rstd
        for (dh_ref, g_ref, sc_ref), (dsh_ref, dsc_ref, dg_ref) in zip(mod_refs, acc_refs):
            dh = dh_ref[...]
            gv, one_sc = g_ref[...], 1.0 + sc_ref[...]
            _acc(dsh_ref, _colsum(dh), i)
            t = dh * nrm
            _acc(dsc_ref, _colsum(t) * gv, i)
            _acc(dg_ref, _colsum(t) * one_sc, i)
            dn = dh * (gv * one_sc)
            dx = dx + rstd * (dn - nrm * jnp.mean(dn * nrm, axis=-1, keepdims=True))
        dx_ref[...] = dx
        if prev is not None:
            dy_ref, dgate_ref, dyb_ref = next(it), next(it), next(it)
            dy = gate_ref[...] * dx
            dy_ref[...] = dy.astype(BF16)
            _acc(dgate_ref, _colsum(dx * (y_ref[...] + yb_ref[...])), i)
            _acc(dyb_ref, _colsum(dy), i)

    ins, in_specs = [dx_up, x], [_row_spec(tr, d)] * 2
    for dh, g, sc in mods:
        ins += [dh, g, sc]
        in_specs += [_row_spec(tr, d), _vec_spec(d), _vec_spec(d)]
    if prev is not None:
        ins += list(prev)
        in_specs += [_row_spec(tr, d), _vec_spec(d), _vec_spec(d)]
    out_shape, out_specs = [_sds((s, d), F32)], [_row_spec(tr, d)]
    for _ in mods:
        out_shape += [_sds((1, d), F32)] * 3
        out_specs += [_vec_spec(d)] * 3
    if prev is not None:
        out_shape += [_sds((s, d), BF16), _sds((1, d), F32), _sds((1, d), F32)]
        out_specs += [_row_spec(tr, d), _vec_spec(d), _vec_spec(d)]
    return _call(body, name=name, grid=(s // tr,), in_specs=in_specs, out_specs=out_specs, out_shape=out_shape, rider=rider)(*ins)


def _glu_fwd(u, bias, name):
    s, d2 = u.shape
    d = d2 // 2
    tr = _pick(s, ROW_TILE, 8)

    def body(u_ref, b_ref, o_ref):
        uv = u_ref[...] + b_ref[...]
        o_ref[...] = uv[:, :d] * jax.nn.sigmoid(uv[:, d:])

    return _call(body, name=name, grid=(s // tr,), in_specs=[_row_spec(tr, d2), _vec_spec(d2)],
                 out_specs=_row_spec(tr, d), out_shape=_sds((s, d), F32))(u, bias)


def _glu_bwd(dglu, u, bias, name):
    s, d2 = u.shape
    d = d2 // 2
    tr = _pick(s, ROW_TILE, 8)

    def body(dg_ref, u_ref, b_ref, du_ref, db_ref):
        i = pl.program_id(0)
        uv = u_ref[...] + b_ref[...]
        a, sg = uv[:, :d], jax.nn.sigmoid(uv[:, d:])
        dg = dg_ref[...]
        da = dg * sg
        dgt = dg * a * sg * (1.0 - sg)
        du_ref[:, :d] = da.astype(BF16)
        du_ref[:, d:] = dgt.astype(BF16)
        _acc(db_ref, jnp.concatenate([_colsum(da), _colsum(dgt)], axis=1), i)

    return _call(body, name=name, grid=(s // tr,), in_specs=[_row_spec(tr, d), _row_spec(tr, d2), _vec_spec(d2)],
                 out_specs=[_row_spec(tr, d2), _vec_spec(d2)], out_shape=[_sds((s, d2), BF16), _sds((1, d2), F32)])(dglu, u, bias)


def _halo_rows(k):
    return -(-(k - 1) // 8) * 8


CONV_CHUNK = 32


def _sublane_shifts(cat_ref, sh_ref, rows):
    for m in range(8):
        sh_ref[m] = cat_ref[pl.ds(m, rows), :]


def _dwconv_fwd(x, w, b, name, rider=None):
    s, c = x.shape
    kk = w.shape[0]
    hb = _halo_rows(kk)
    tr, tc = _pick(s, ROW_TILE, hb), _pick(c, 512)
    per = tr // hb
    ch = CONV_CHUNK

    def body(xp_ref, x_ref, w_ref, b_ref, o_ref, cat_ref, sh_ref):
        i = pl.program_id(1)
        cat_ref[0:hb, :] = jnp.where(i > 0, xp_ref[...], 0.0)
        cat_ref[hb:hb + tr, :] = x_ref[...]
        cat_ref[hb + tr:hb + tr + 8, :] = jnp.zeros((8, tc), F32)
        _sublane_shifts(cat_ref, sh_ref, tr + hb)

        def chunk(ci, carry):
            r0 = pl.multiple_of(ci * ch, ch)
            acc = jnp.zeros((ch, tc), F32) + b_ref[...]
            for k in range(kk):
                a, m = divmod(hb - (kk - 1) + k, 8)
                acc = acc + w_ref[k:k + 1, :] * sh_ref[m, pl.ds(r0 + 8 * a, ch), :]
            o_ref[pl.ds(r0, ch), :] = acc
            return carry

        lax.fori_loop(0, tr // ch, chunk, 0)

    return _call(
        body, name=name, grid=(c // tc, s // tr),
        in_specs=[pl.BlockSpec((hb, tc), lambda j, i: (jnp.maximum(i * per - 1, 0), j)), pl.BlockSpec((tr, tc), lambda j, i: (i, j)),
                  pl.BlockSpec((kk, tc), lambda j, i: (0, j)), pl.BlockSpec((1, tc), lambda j, i: (0, j))],
        out_specs=pl.BlockSpec((tr, tc), lambda j, i: (i, j)), out_shape=_sds((s, c), F32),
        scratch=[pltpu.VMEM((tr + hb + 8, tc), F32), pltpu.VMEM((8, tr + hb, tc), F32)], rider=rider,
    )(x, x, w, b)


def _dwconv_bwd(dy, x, w, name, rider=None):
    s, c = x.shape
    kk = w.shape[0]
    hb = _halo_rows(kk)
    tr, tc = _pick(s, ROW_TILE, hb), _pick(c, 512)
    per, nt = tr // hb, s // tr
    ch = CONV_CHUNK
    groups = ch // 8

    def body(xp_ref, x_ref, dy_ref, dyn_ref, w_ref, dx_ref, dw_ref, db_ref, cat_ref, shx_ref, shd_ref, acc_ref):
        i = pl.program_id(1)
        cat_ref[0:hb, :] = jnp.where(i > 0, xp_ref[...], 0.0)
        cat_ref[hb:hb + tr, :] = x_ref[...]
        cat_ref[hb + tr:hb + tr + 8, :] = jnp.zeros((8, tc), F32)
        _sublane_shifts(cat_ref, shx_ref, tr + hb)
        cat_ref[0:tr, :] = dy_ref[...]
        cat_ref[tr:tr + hb, :] = jnp.where(i < nt - 1, dyn_ref[...], 0.0)
        _sublane_shifts(cat_ref, shd_ref, tr + hb)

        @pl.when(i == 0)
        def _():
            acc_ref[...] = jnp.zeros_like(acc_ref)

        def fold(v):
            out = v[0:8, :]
            for g in range(1, groups):
                out = out + v[8 * g:8 * g + 8, :]
            return out

        def chunk(ci, carry):
            r0 = pl.multiple_of(ci * ch, ch)
            dyv = dy_ref[pl.ds(r0, ch), :]
            acc_ref[kk] += fold(dyv)
            dxv = jnp.zeros((ch, tc), F32)
            for k in range(kk):
                a, m = divmod(kk - 1 - k, 8)
                dxv = dxv + w_ref[k:k + 1, :] * shd_ref[m, pl.ds(r0 + 8 * a, ch), :]
                a, m = divmod(hb - (kk - 1) + k, 8)
                acc_ref[k] += fold(dyv * shx_ref[m, pl.ds(r0 + 8 * a, ch), :])
            dx_ref[pl.ds(r0, ch), :] = dxv
            return carry

        lax.fori_loop(0, tr // ch, chunk, 0)

        @pl.when(i == nt - 1)
        def _():
            for k in range(kk):
                dw_ref[k:k + 1, :] = _colsum(acc_ref[k])
            db_ref[...] = _colsum(acc_ref[kk])

    return _call(
        body, name=name, grid=(c // tc, nt),
        in_specs=[pl.BlockSpec((hb, tc), lambda j, i: (jnp.maximum(i * per - 1, 0), j)), pl.BlockSpec((tr, tc), lambda j, i: (i, j)),
                  pl.BlockSpec((tr, tc), lambda j, i: (i, j)),
                  pl.BlockSpec((hb, tc), lambda j, i: (jnp.minimum((i + 1) * per, s // hb - 1), j)),
                  pl.BlockSpec((kk, tc), lambda j, i: (0, j))],
        out_specs=[pl.BlockSpec((tr, tc), lambda j, i: (i, j)), pl.BlockSpec((kk, tc), lambda j, i: (0, j)),
                   pl.BlockSpec((1, tc), lambda j, i: (0, j))],
        out_shape=[_sds((s, c), F32), _sds((kk, c), F32), _sds((1, c), F32)],
        scratch=[pltpu.VMEM((tr + hb + 8, tc), F32), pltpu.VMEM((8, tr + hb, tc), F32), pltpu.VMEM((8, tr + hb, tc), F32),
                 pltpu.VMEM((kk + 1, 8, tc), F32)], rider=rider,
    )(x, x, dy, dy, w)


def _ln_silu_fwd(x, g, b, name):
    s, d = x.shape
    tr = _pick(s, ROW_TILE, 8)

    def body(x_ref, g_ref, b_ref, o_ref):
        xv = x_ref[...]
        mu = jnp.mean(xv, axis=-1, keepdims=True)
        xc = xv - mu
        ln = xc * lax.rsqrt(jnp.mean(xc * xc, axis=-1, keepdims=True) + EPS) * g_ref[...] + b_ref[...]
        o_ref[...] = _silu(ln).astype(BF16)

    return _call(body, name=name, grid=(s // tr,), in_specs=[_row_spec(tr, d), _vec_spec(d), _vec_spec(d)],
                 out_specs=_row_spec(tr, d), out_shape=_sds((s, d), BF16))(x, g, b)


def _ln_silu_bwd(dact, x, g, b, name):
    s, d = x.shape
    tr = _pick(s, ROW_TILE, 8)

    def body(da_ref, x_ref, g_ref, b_ref, dx_ref, dg_ref, db_ref):
        i = pl.program_id(0)
        xv = x_ref[...]
        mu = jnp.mean(xv, axis=-1, keepdims=True)
        xc = xv - mu
        rstd = lax.rsqrt(jnp.mean(xc * xc, axis=-1, keepdims=True) + EPS)
        xh = xc * rstd
        ln = xh * g_ref[...] + b_ref[...]
        dln = da_ref[...] * _dsilu(ln)
        _acc(dg_ref, _colsum(dln * xh), i)
        _acc(db_ref, _colsum(dln), i)
        dxh = dln * g_ref[...]
        dx_ref[...] = rstd * (dxh - jnp.mean(dxh, axis=-1, keepdims=True) - xh * jnp.mean(dxh * xh, axis=-1, keepdims=True))

    return _call(body, name=name, grid=(s // tr,), in_specs=[_row_spec(tr, d), _row_spec(tr, d), _vec_spec(d), _vec_spec(d)],
                 out_specs=[_row_spec(tr, d), _vec_spec(d), _vec_spec(d)],
                 out_shape=[_sds((s, d), F32), _sds((1, d), F32), _sds((1, d), F32)])(dact, x, g, b)


FFN_CHUNK = 16


def _fold8(v):
    out = v[0:8, :]
    for g in range(1, v.shape[0] // 8):
        out = out + v[8 * g:8 * g + 8, :]
    return out


def _ffn_act_fwd(u, w, b, name, rider=None):
    s, f2 = u.shape
    f = f2 // 2
    hb = 8
    tr, tc = _pick(s, ROW_TILE, 2 * hb), _pick(f, 1408)
    per, nf = tr // hb, f // tc
    ch = FFN_CHUNK

    def body(gp_ref, g_ref, v_ref, w_ref, b_ref, z_ref, cat_ref, sh_ref):
        i = pl.program_id(1)
        cat_ref[0:hb, :] = jnp.where(i > 0, gp_ref[...], 0.0)
        cat_ref[hb:hb + tr, :] = g_ref[...]
        for k in range(FFN_K - 1):
            sh_ref[k] = cat_ref[pl.ds(hb - (FFN_K - 1) + k, tr), :]

        def chunk(ci, carry):
            rows = pl.ds(pl.multiple_of(ci * ch, ch), ch)
            gc = b_ref[...] + w_ref[FFN_K - 1:FFN_K, :] * g_ref[rows, :]
            for k in range(FFN_K - 1):
                gc = gc + w_ref[k:k + 1, :] * sh_ref[k, rows, :]
            z_ref[rows, :] = (_silu(gc) * v_ref[rows, :]).astype(BF16)
            return carry

        lax.fori_loop(0, tr // ch, chunk, 0)

    return _call(
        body, name=name, grid=(nf, s // tr),
        in_specs=[pl.BlockSpec((hb, tc), lambda j, i: (jnp.maximum(i * per - 1, 0), j)), pl.BlockSpec((tr, tc), lambda j, i: (i, j)),
                  pl.BlockSpec((tr, tc), lambda j, i: (i, nf + j)), pl.BlockSpec((FFN_K, tc), lambda j, i: (0, j)),
                  pl.BlockSpec((1, tc), lambda j, i: (0, j))],
        out_specs=pl.BlockSpec((tr, tc), lambda j, i: (i, j)), out_shape=_sds((s, f), BF16),
        scratch=[pltpu.VMEM((tr + hb, tc), F32), pltpu.VMEM((FFN_K - 1, tr, tc), F32)], rider=rider,
    )(u, u, u, w, b)


def _ffn_act_bwd(dz, u, w, b, name, rider=None):
    s, f2 = u.shape
    f = f2 // 2
    hb = 8
    tr, tc = _pick(s, ROW_TILE, 2 * hb), _pick(f, 1408)
    per, nf, nt = tr // hb, f // tc, s // tr
    ext = tr + hb
    ch = FFN_CHUNK
    last_tap = FFN_K - 1

    def body(gp_ref, g_ref, gn_ref, v_ref, vn_ref, dz_ref, dzn_ref, w_ref, b_ref, du_ref, dw_ref, db_ref, cat_ref, sh_ref, dgc_ref,
             sd_ref, acc_ref):
        i = pl.program_id(1)
        last = i == nt - 1
        cat_ref[0:hb, :] = jnp.where(i > 0, gp_ref[...], 0.0)
        cat_ref[hb:hb + tr, :] = g_ref[...]
        cat_ref[hb + tr:hb + tr + hb, :] = gn_ref[...]
        for k in range(last_tap):
            sh_ref[k] = cat_ref[pl.ds(hb - last_tap + k, ext), :]

        def preact(rows, g_rows):
            gc = b_ref[...] + w_ref[last_tap:FFN_K, :] * g_rows
            for k in range(last_tap):
                gc = gc + w_ref[k:k + 1, :] * sh_ref[k, rows, :]
            return gc

        def chunk1(ci, carry):
            rows = pl.ds(pl.multiple_of(ci * ch, ch), ch)
            gc = preact(rows, g_ref[rows, :])
            sg = jax.nn.sigmoid(gc)
            dzv = dz_ref[rows, :]
            du_ref[1, rows, :] = (dzv * (gc * sg)).astype(BF16)
            dgc_ref[rows, :] = dzv * v_ref[rows, :] * (sg * (1.0 + gc * (1.0 - sg)))
            return carry

        lax.fori_loop(0, tr // ch, chunk1, 0)
        gcn = preact(pl.ds(tr, hb), gn_ref[...])
        dgc_ref[tr:ext, :] = jnp.where(last, 0.0, dzn_ref[...] * vn_ref[...] * _dsilu(gcn))
        for j in range(last_tap):
            sd_ref[j] = dgc_ref[pl.ds(j + 1, tr), :]

        @pl.when(i == 0)
        def _():
            acc_ref[...] = jnp.zeros_like(acc_ref)

        def chunk2(ci, carry):
            rows = pl.ds(pl.multiple_of(ci * ch, ch), ch)
            d0 = dgc_ref[rows, :]
            dgt = w_ref[last_tap:FFN_K, :] * d0
            for k in range(last_tap):
                dgt = dgt + w_ref[k:k + 1, :] * sd_ref[last_tap - 1 - k, rows, :]
            du_ref[0, rows, :] = dgt.astype(BF16)
            acc_ref[FFN_K] += _fold8(d0)
            acc_ref[last_tap] += _fold8(d0 * g_ref[rows, :])
            for k in range(last_tap):
                acc_ref[k] += _fold8(d0 * sh_ref[k, rows, :])
            return carry

        lax.fori_loop(0, tr // ch, chunk2, 0)

        @pl.when(last)
        def _():
            for k in range(FFN_K):
                dw_ref[k:k + 1, :] = _colsum(acc_ref[k])
            db_ref[...] = _colsum(acc_ref[FFN_K])

    prev_map = lambda j, i: (jnp.maximum(i * per - 1, 0), j)
    next_map = lambda j, i: (jnp.minimum((i + 1) * per, s // hb - 1), j)
    next_map_v = lambda j, i: (jnp.minimum((i + 1) * per, s // hb - 1), nf + j)
    return _call(
        body, name=name, grid=(nf, nt),
        in_specs=[pl.BlockSpec((hb, tc), prev_map), pl.BlockSpec((tr, tc), lambda j, i: (i, j)), pl.BlockSpec((hb, tc), next_map),
                  pl.BlockSpec((tr, tc), lambda j, i: (i, nf + j)), pl.BlockSpec((hb, tc), next_map_v),
                  pl.BlockSpec((tr, tc), lambda j, i: (i, j)), pl.BlockSpec((hb, tc), next_map),
                  pl.BlockSpec((FFN_K, tc), lambda j, i: (0, j)), pl.BlockSpec((1, tc), lambda j, i: (0, j))],
        out_specs=[pl.BlockSpec((2, tr, tc), lambda j, i: (0, i, j)), pl.BlockSpec((FFN_K, tc), lambda j, i: (0, j)),
                   pl.BlockSpec((1, tc), lambda j, i: (0, j))],
        out_shape=[_sds((2, s, f), BF16), _sds((FFN_K, f), F32), _sds((1, f), F32)],
        scratch=[pltpu.VMEM((tr + 2 * hb, tc), F32), pltpu.VMEM((last_tap, ext, tc), F32), pltpu.VMEM((ext, tc), F32),
                 pltpu.VMEM((last_tap, tr, tc), F32), pltpu.VMEM((FFN_K + 1, 8, tc), F32)], rider=rider,
    )(u, u, u, u, u, dz, dz, w, b)


def _rope_tables(pos, freq, sign, name):
    s = pos.shape[0]
    tr = _pick(s, 512, 8)

    def body(p_ref, f_ref, s_ref, c_ref, sn_ref):
        ang = p_ref[...].astype(F32) * f_ref[...]
        c_ref[...] = jnp.cos(ang)
        sn_ref[...] = jnp.sin(ang) * s_ref[...]

    return _call(body, name=name, grid=(s // tr,), in_specs=[pl.BlockSpec((tr, 1), lambda i: (i, 0)), _vec_spec(128), _vec_spec(128)],
                 out_specs=[_row_spec(tr, 128)] * 2, out_shape=[_sds((s, 128), F32)] * 2)(pos, freq, sign)


def _partner(v):
    lane = lax.broadcasted_iota(jnp.int32, v.shape, 1)
    lower = pltpu.roll(v, HEAD_DIM - ROT_DIM // 2, 1)
    upper = jnp.where(lane < ROT_DIM, pltpu.roll(v, ROT_DIM // 2, 1), 0.0)
    return jnp.where(lane < ROT_DIM // 2, lower, upper)


def _head_sum(v):
    hi = v.astype(BF16)
    lo = (v - hi.astype(F32)).astype(BF16)
    ones = jnp.ones((HEAD_DIM, HEAD_DIM), BF16)
    return lax.dot_general(hi, ones, _DIMS["nn"], preferred_element_type=F32) + lax.dot_general(
        lo, ones, _DIMS["nn"], preferred_element_type=F32)


def _qk_norm_rope_fwd(q, kv, gq, gk, ctab, stab, name, rider=None):
    s, w = q.shape
    tr = _pick(s, 128, 8)
    heads = w // HEAD_DIM

    def body(q_ref, k_ref, gq_ref, gk_ref, c_ref, s_ref, qn_ref, kn_ref):
        cv, sv = c_ref[...], s_ref[...]
        for src, g_ref, dst in ((q_ref, gq_ref, qn_ref), (k_ref, gk_ref, kn_ref)):
            gv = g_ref[...]
            for h in range(heads):
                cols = pl.ds(h * HEAD_DIM, HEAD_DIM)
                xv = src[:, cols]
                nv = xv * lax.rsqrt(_head_sum(xv * xv) * (1.0 / HEAD_DIM) + EPS) * gv
                dst[:, cols] = nv * cv + _partner(nv) * sv

    return _call(
        body, name=name, grid=(s // tr,),
        in_specs=[_row_spec(tr, w), _row_spec(tr, w), _vec_spec(128), _vec_spec(128), _row_spec(tr, 128), _row_spec(tr, 128)],
        out_specs=[_row_spec(tr, w)] * 2, out_shape=[_sds((s, w), F32)] * 2, rider=rider,
    )(q, kv, gq, gk, ctab, stab)


def _qk_norm_rope_bwd(dqs, dks, dvs, q, kv, gq, gk, ctab, stab, name, rider=None):
    s, w = q.shape
    gw = w // N_GROUPS
    tr = _pick(s, 128, 8)
    hpg = gw // HEAD_DIM

    def body(*refs):
        dq_refs, dk_refs, dv_refs = refs[0:3], refs[3:6], refs[6:9]
        q_ref, k_ref, gq_ref, gk_ref, c_ref, s_ref, dq_ref, dkv_ref, dgq_ref, dgk_ref = refs[9:]
        i = pl.program_id(0)
        cv, sv = c_ref[...], s_ref[...]
        for d_refs, src, g_ref, dst, dg_ref in ((dq_refs, q_ref, gq_ref, dq_ref, dgq_ref), (dk_refs, k_ref, gk_ref, dkv_ref, dgk_ref)):
            gv = g_ref[...]
            dg = jnp.zeros((1, HEAD_DIM), F32)
            for g in range(N_GROUPS):
                for h in range(hpg):
                    cols = pl.ds(g * gw + h * HEAD_DIM, HEAD_DIM)
                    dout = d_refs[g][:, pl.ds(h * HEAD_DIM, HEAD_DIM)]
                    dn = dout * cv + _partner(dout * sv)
                    xv = src[:, cols]
                    rstd = lax.rsqrt(_head_sum(xv * xv) * (1.0 / HEAD_DIM) + EPS)
                    xh = xv * rstd
                    dg = dg + _colsum(dn * xh)
                    dxh = dn * gv
                    dst[:, cols] = (rstd * (dxh - xh * (_head_sum(dxh * xh) * (1.0 / HEAD_DIM)))).astype(BF16)
            _acc(dg_ref, dg, i)
        for g in range(N_GROUPS):
            dkv_ref[:, pl.ds(w + g * gw, gw)] = dv_refs[g][...].astype(BF16)

    return _call(
        body, name=name, grid=(s // tr,),
        in_specs=[_row_spec(tr, gw)] * 9 + [_row_spec(tr, w), _row_spec(tr, w), _vec_spec(128), _vec_spec(128),
                                            _row_spec(tr, 128), _row_spec(tr, 128)],
        out_specs=[_row_spec(tr, w), _row_spec(tr, 2 * w), _vec_spec(128), _vec_spec(128)],
        out_shape=[_sds((s, w), BF16), _sds((s, 2 * w), BF16), _sds((1, 128), F32), _sds((1, 128), F32)], rider=rider,
    )(*dqs, *dks, *dvs, q, kv, gq, gk, ctab, stab)


def _rows(j, b, r):
    start = j + r * BLK * b
    return pl.ds(start, BLK, stride=r) if r > 1 else pl.ds(start, BLK)


def _dot_nt(a, b):
    return lax.dot_general(a, b, _DIMS["nt"], preferred_element_type=F32)


def _dot_nn(a, b):
    return lax.dot_general(a, b, _DIMS["nn"], preferred_element_type=F32)


def _band_masks():
    qi = lax.broadcasted_iota(jnp.int32, (BLK, BLK), 0)
    kj = lax.broadcasted_iota(jnp.int32, (BLK, BLK), 1)
    return kj <= qi, kj >= qi


def _attn_fwd(qn, kn, kv, g, name):
    s, w = qn.shape
    r = DILATIONS[g]
    gw = w // N_GROUPS
    cr = min(ATTN_CHUNK, s)
    nb = cr // (BLK * r)
    hp = 1
    cw = hp * HEAD_DIM
    gc = gw // cw
    scale = 1.0 / math.sqrt(HEAD_DIM)

    def body(q_ref, kc_ref, kp_ref, vc_ref, vp_ref, o_ref, l_ref):
        n = pl.program_id(1)
        same_m, prev_m = _band_masks()
        prev_first = jnp.logical_and(prev_m, n > 0)
        for h in range(hp):
            cols = pl.ds(h * HEAD_DIM, HEAD_DIM)
            for j in range(r):
                for b in range(nb):
                    rows = _rows(j, b, r)
                    qv = q_ref[rows, cols].astype(BF16)
                    kc, vc = kc_ref[rows, cols].astype(BF16), vc_ref[rows, cols].astype(BF16)
                    if b > 0:
                        rp = _rows(j, b - 1, r)
                        kp, vp, pm = kc_ref[rp, cols].astype(BF16), vc_ref[rp, cols].astype(BF16), prev_m
                    else:
                        rp = _rows(j, nb - 1, r)
                        kp, vp, pm = kp_ref[rp, cols].astype(BF16), vp_ref[rp, cols].astype(BF16), prev_first
                    sd = jnp.where(same_m, _dot_nt(qv, kc) * scale, NEG)
                    so = jnp.where(pm, _dot_nt(qv, kp) * scale, NEG)
                    m = jnp.maximum(jnp.max(sd, axis=-1, keepdims=True), jnp.max(so, axis=-1, keepdims=True))
                    pd, po = jnp.exp(sd - m), jnp.exp(so - m)
                    den = jnp.sum(pd, axis=-1, keepdims=True) + jnp.sum(po, axis=-1, keepdims=True)
                    ov = (_dot_nn(pd.astype(BF16), vc) + _dot_nn(po.astype(BF16), vp)) / den
                    o_ref[rows, cols] = ov
                    l_ref[rows, cols] = jnp.broadcast_to(m + jnp.log(den), (BLK, HEAD_DIM))

    cur = lambda base: (lambda c, n: (n, base + c))
    prv = lambda base: (lambda c, n: (jnp.maximum(n - 1, 0), base + c))
    qb, kb, vb = g * gc, g * gc, (N_GROUPS + g) * gc
    return _call(
        body, name=name, grid=(gc, s // cr),
        in_specs=[pl.BlockSpec((cr, cw), cur(qb)), pl.BlockSpec((cr, cw), cur(kb)), pl.BlockSpec((cr, cw), prv(kb)),
                  pl.BlockSpec((cr, cw), cur(vb)), pl.BlockSpec((cr, cw), prv(vb))],
        out_specs=[pl.BlockSpec((cr, cw), lambda c, n: (n, c))] * 2, out_shape=[_sds((s, gw), F32)] * 2,
    )(qn, kn, kn, kv, kv)


def _attn_combine(os_, lses, name):
    s, gw = os_[0].shape
    tr = _pick(s, ROW_TILE, 8)

    def body(o0, o1, o2, l0, l1, l2, o_ref, l_ref):
        a, b, c = l0[...], l1[...], l2[...]
        m = jnp.maximum(jnp.maximum(a, b), c)
        ea, eb, ec = jnp.exp(a - m), jnp.exp(b - m), jnp.exp(c - m)
        den = ea + eb + ec
        o_ref[...] = ((ea * o0[...] + eb * o1[...] + ec * o2[...]) / den).astype(BF16)
        l_ref[...] = m + jnp.log(den)

    return _call(body, name=name, grid=(s // tr,), in_specs=[_row_spec(tr, gw)] * 6, out_specs=[_row_spec(tr, gw)] * 2,
                 out_shape=[_sds((s, gw), BF16), _sds((s, gw), F32)])(*os_, *lses)


def _attn_delta(do, o, name):
    s, gw = do.shape
    tr = _pick(s, ROW_TILE, 8)

    def body(do_ref, o_ref, d_ref):
        for h in range(gw // HEAD_DIM):
            cols = pl.ds(h * HEAD_DIM, HEAD_DIM)
            t = jnp.sum(do_ref[:, cols] * o_ref[:, cols].astype(F32), axis=-1, keepdims=True)
            d_ref[:, cols] = jnp.broadcast_to(t, (tr, HEAD_DIM))

    return _call(body, name=name, grid=(s // tr,), in_specs=[_row_spec(tr, gw)] * 2, out_specs=_row_spec(tr, gw),
                 out_shape=_sds((s, gw), F32))(do, o)


def _pair_grads(qv, kv_, vv, dov, lse, delta, mask, scale):
    sc = jnp.where(mask, _dot_nt(qv, kv_) * scale, NEG)
    p = jnp.exp(sc - lse)
    ds = p * (_dot_nt(dov, vv) - delta) * scale
    return p, ds


def _attn_bwd_dq(qn, kn, kv, do, lse, delta, g, name, rider=None):
    s, w = qn.shape
    r = DILATIONS[g]
    gw = w // N_GROUPS
    cr = min(ATTN_CHUNK, s)
    nb = cr // (BLK * r)
    cw = HEAD_DIM
    gc = gw // cw
    scale = 1.0 / math.sqrt(HEAD_DIM)

    def body(q_ref, kc_ref, kp_ref, vc_ref, vp_ref, do_ref, l_ref, d_ref, dq_ref):
        n = pl.program_id(1)
        same_m, prev_m = _band_masks()
        prev_first = jnp.logical_and(prev_m, n > 0)
        for j in range(r):
            for b in range(nb):
                rows = _rows(j, b, r)
                qv, dov = q_ref[rows, :].astype(BF16), do_ref[rows, :].astype(BF16)
                lse, delta = l_ref[rows, :], d_ref[rows, :]
                kc, vc = kc_ref[rows, :].astype(BF16), vc_ref[rows, :].astype(BF16)
                if b > 0:
                    rp = _rows(j, b - 1, r)
                    kp, vp, pm = kc_ref[rp, :].astype(BF16), vc_ref[rp, :].astype(BF16), prev_m
                else:
                    rp = _rows(j, nb - 1, r)
                    kp, vp, pm = kp_ref[rp, :].astype(BF16), vp_ref[rp, :].astype(BF16), prev_first
                _, dsd = _pair_grads(qv, kc, vc, dov, lse, delta, same_m, scale)
                _, dso = _pair_grads(qv, kp, vp, dov, lse, delta, pm, scale)
                dq_ref[rows, :] = _dot_nn(dsd.astype(BF16), kc) + _dot_nn(dso.astype(BF16), kp)

    cur = lambda base: (lambda c, n: (n, base + c))
    prv = lambda base: (lambda c, n: (jnp.maximum(n - 1, 0), base + c))
    qb, vb = g * gc, (N_GROUPS + g) * gc
    own = pl.BlockSpec((cr, cw), lambda c, n: (n, c))
    return _call(
        body, name=name, grid=(gc, s // cr),
        in_specs=[pl.BlockSpec((cr, cw), cur(qb)), pl.BlockSpec((cr, cw), cur(qb)), pl.BlockSpec((cr, cw), prv(qb)),
                  pl.BlockSpec((cr, cw), cur(vb)), pl.BlockSpec((cr, cw), prv(vb)), own, own, own],
        out_specs=own, out_shape=_sds((s, gw), F32), rider=rider,
    )(qn, kn, kn, kv, kv, do, lse, delta)


def _attn_bwd_dkv(qn, kn, kv, do, lse, delta, g, name):
    s, w = qn.shape
    r = DILATIONS[g]
    gw = w // N_GROUPS
    cr = min(ATTN_CHUNK, s)
    nb = cr // (BLK * r)
    nchunk = s // cr
    cw = HEAD_DIM
    gc = gw // cw
    scale = 1.0 / math.sqrt(HEAD_DIM)

    def body(k_ref, v_ref, qc_ref, qx_ref, doc_ref, dox_ref, lc_ref, lx_ref, dc_ref, dx_ref, dk_ref, dv_ref):
        n = pl.program_id(1)
        same_m, prev_m = _band_masks()
        next_last = jnp.logical_and(prev_m, n < nchunk - 1)
        for j in range(r):
            for b in range(nb):
                rows = _rows(j, b, r)
                kv_, vv = k_ref[rows, :].astype(BF16), v_ref[rows, :].astype(BF16)
                qv, dov = qc_ref[rows, :].astype(BF16), doc_ref[rows, :].astype(BF16)
                pd, dsd = _pair_grads(qv, kv_, vv, dov, lc_ref[rows, :], dc_ref[rows, :], same_m, scale)
                if b < nb - 1:
                    rx = _rows(j, b + 1, r)
                    qx, dox, lx, dlx, xm = qc_ref[rx, :], doc_ref[rx, :], lc_ref[rx, :], dc_ref[rx, :], prev_m
                else:
                    rx = _rows(j, 0, r)
                    qx, dox, lx, dlx, xm = qx_ref[rx, :], dox_ref[rx, :], lx_ref[rx, :], dx_ref[rx, :], next_last
                qx, dox = qx.astype(BF16), dox.astype(BF16)
                po, dso = _pair_grads(qx, kv_, vv, dox, lx, dlx, xm, scale)
                dk_ref[rows, :] = _dot_nn(dsd.T.astype(BF16), qv) + _dot_nn(dso.T.astype(BF16), qx)
                dv_ref[rows, :] = _dot_nn(pd.T.astype(BF16), dov) + _dot_nn(po.T.astype(BF16), dox)

    cur = lambda base: (lambda c, n: (n, base + c))
    nxt = lambda base: (lambda c, n: (jnp.minimum(n + 1, nchunk - 1), base + c))
    qb, vb = g * gc, (N_GROUPS + g) * gc
    blk = lambda f: pl.BlockSpec((cr, cw), f)
    return _call(
        body, name=name, grid=(gc, nchunk),
        in_specs=[blk(cur(qb)), blk(cur(vb)), blk(cur(qb)), blk(nxt(qb)), blk(cur(0)), blk(nxt(0)), blk(cur(0)), blk(nxt(0)),
                  blk(cur(0)), blk(nxt(0))],
        out_specs=[blk(cur(0))] * 2, out_shape=[_sds((s, gw), F32)] * 2,
    )(kn, kv, qn, qn, do, do, lse, lse, delta, delta)


def _mod_proj(sc_all, w, name):
    l, d, ns = w.shape
    tn = _pick(ns, 512)

    def body(c_ref, w_ref, o_ref):
        o_ref[...] = jnp.dot(c_ref[...].astype(BF16), w_ref[...].astype(BF16), preferred_element_type=F32)

    return _call(
        body, name=name, grid=(l, ns // tn),
        in_specs=[pl.BlockSpec((N_DEV, d), lambda a, j: (0, 0)), pl.BlockSpec((None, d, tn), lambda a, j: (a, 0, j))],
        out_specs=pl.BlockSpec((None, N_DEV, tn), lambda a, j: (a, 0, j)), out_shape=_sds((l, N_DEV, ns), F32),
    )(sc_all, w)


def _adamw_math(w, g, m, v):
    m = ADAM_B1 * m + (1.0 - ADAM_B1) * g
    v = ADAM_B2 * v + (1.0 - ADAM_B2) * (g * g)
    m_hat = m / (1.0 - ADAM_B1 ** ADAM_STEP)
    v_hat = v / (1.0 - ADAM_B2 ** ADAM_STEP)
    delta = -ADAM_LR * (m_hat / (jnp.sqrt(v_hat) + ADAM_EPS) + ADAM_WD * w)
    return delta, m, v


def _adamw_big(w, g, m, v, name):
    shape = w.shape
    cols = shape[-1]
    rows = math.prod(shape[:-1])
    tr, tc = _wide_tile(rows, cols)
    w2, g2, m2, v2 = (t.reshape(rows, cols) for t in (w, g, m, v))

    def body(w_ref, g_ref, m_ref, v_ref, d_ref, mo_ref, vo_ref):
        d_ref[...], mo_ref[...], vo_ref[...] = _adamw_math(w_ref[...], g_ref[...], m_ref[...], v_ref[...])

    spec = pl.BlockSpec((tr, tc), lambda i, j: (i, j))
    outs = _call(body, name=name, grid=(rows // tr, cols // tc), in_specs=[spec] * 4, out_specs=[spec] * 3,
                 out_shape=[_sds((rows, cols), F32)] * 3)(w2, g2, m2, v2)
    return [t.reshape(shape) for t in outs]


def _adamw_mod(w, sct, dm, m, v, name):
    l, d, ns = w.shape
    tr, tc = _pick(d, 512, 8), _pick(ns, 1024)

    def body(w_ref, c_ref, dm_ref, m_ref, v_ref, g_ref, d_ref, mo_ref, vo_ref):
        cv, dv = c_ref[...].astype(BF16).astype(F32), dm_ref[...].astype(BF16).astype(F32)
        g = jnp.zeros((tr, tc), F32)
        for e in range(N_DEV):
            g = g + cv[:, e:e + 1] * dv[e:e + 1, :]
        g_ref[...] = g
        d_ref[...], mo_ref[...], vo_ref[...] = _adamw_math(w_ref[...], g, m_ref[...], v_ref[...])

    spec = pl.BlockSpec((None, tr, tc), lambda a, i, j: (a, i, j))
    return _call(
        body, name=name, grid=(l, d // tr, ns // tc),
        in_specs=[spec, pl.BlockSpec((tr, N_DEV), lambda a, i, j: (i, 0)), pl.BlockSpec((None, N_DEV, tc), lambda a, i, j: (a, 0, j)),
                  spec, spec],
        out_specs=[spec] * 4, out_shape=[_sds((l, d, ns), F32)] * 4,
    )(w, sct, dm, m, v)


def _adamw_small(ws, gs, ms, vs, name):
    n = len(ws)

    def body(*refs):
        w_r, g_r, m_r, v_r = refs[0:n], refs[n:2 * n], refs[2 * n:3 * n], refs[3 * n:4 * n]
        d_o, m_o, v_o = refs[4 * n:5 * n], refs[5 * n:6 * n], refs[6 * n:7 * n]
        for k in range(n):
            d_o[k][...], m_o[k][...], v_o[k][...] = _adamw_math(w_r[k][...], g_r[k][...], m_r[k][...], v_r[k][...])

    vm = pl.BlockSpec(memory_space=pltpu.VMEM)
    shapes = [_sds(w.shape, F32) for w in ws]
    outs = pl.pallas_call(body, name=name, in_specs=[vm] * (4 * n), out_specs=[vm] * (3 * n), out_shape=shapes * 3,
                          interpret=False)(*ws, *gs, *ms, *vs)
    return outs[0:n], outs[n:2 * n], outs[2 * n:3 * n]


def _sum_rows(a, name):
    n, v = a.shape
    tc = _pick(v, 8192)

    def body(a_ref, o_ref):
        o_ref[...] = jnp.sum(a_ref[...], axis=0, keepdims=True)

    return _call(body, name=name, grid=(v // tc,), in_specs=[pl.BlockSpec((n, tc), lambda j: (0, j))],
                 out_specs=pl.BlockSpec((1, tc), lambda j: (0, j)), out_shape=_sds((1, v), F32))(a)


def _cast_into_full(w3d, layer, kind, slot_arr, name):
    _, rows, cols = w3d.shape
    tr, tc = _wide_tile(rows, cols)
    nr, nc = rows // tr, cols // tc

    def body(s_ref, w_ref, o_ref):
        o_ref[...] = w_ref[...].astype(BF16)

    if kind == "col":
        o_map = lambda i, j, s_ref: (i, s_ref[0] * nc + j)
    else:
        o_map = lambda i, j, s_ref: (s_ref[0] * nr + i, j)
    return _call(body, name=name, grid=(nr, nc), in_specs=[pl.BlockSpec((None, tr, tc), lambda i, j, s_ref: (layer, i, j))],
                 out_specs=pl.BlockSpec((tr, tc), o_map), out_shape=_sds(_full_shape(kind, (rows, cols)), BF16), nsp=1)(slot_arr, w3d)


def _place():
    x, y, c = lax.axis_index("x"), lax.axis_index("y"), lax.axis_index("c")
    chips = [(1 - x, y), (x, 1 - y), (1 - x, 1 - y)]
    return x, y, c, chips


def _remote(src, dst, send_sem, recv_sem, to):
    return pltpu.make_async_remote_copy(src_ref=src, dst_ref=dst, send_sem=send_sem, recv_sem=recv_sem, device_id=to,
                                        device_id_type=MESH)


def _allgather8(a, name):
    m_per, n = a.shape

    def body(x_ref, out_ref, send_sems, recv_sems, local_sem):
        x, y, c, chips = _place()
        me, sibling = (x, y, c), (x, y, 1 - c)

        def rows(px, py, pc):
            return out_ref.at[pl.ds((4 * px + 2 * py + pc) * m_per, m_per), :]

        def copy(k, block, to, src=None):
            return _remote(rows(*block) if src is None else src, rows(*block), send_sems.at[k], recv_sems.at[k], to)

        mine = pltpu.make_async_copy(x_ref, rows(*me), local_sem)
        mine.start()
        first = [copy(0, me, sibling, src=x_ref)]
        first += [copy(1 + j, me, (*chip, c), src=x_ref) for j, chip in enumerate(chips)]
        for cp in first:
            cp.start()
        passed = [copy(4 + j, (*chip, c), sibling) for j, chip in enumerate(chips)]
        for j, chip in enumerate(chips):
            copy(1 + j, (*chip, c), me).wait_recv()
            passed[j].start()
        copy(0, sibling, me).wait_recv()
        for j, chip in enumerate(chips):
            copy(4 + j, (*chip, 1 - c), me).wait_recv()
        for cp in first + passed:
            cp.wait_send()
        mine.wait()

    return pl.pallas_call(
        body, name=name, out_shape=_sds((N_DEV * m_per, n), a.dtype),
        in_specs=[pl.BlockSpec(memory_space=pltpu.VMEM)], out_specs=pl.BlockSpec(memory_space=pltpu.VMEM),
        scratch_shapes=[pltpu.SemaphoreType.DMA((7,)), pltpu.SemaphoreType.DMA((7,)), pltpu.SemaphoreType.DMA],
        interpret=False,
    )(a)


def _region(ref, kind, shard_shape, slot, half, piece=(0, 1)):
    r, cs = shard_shape
    hr = r // 2
    n = hr // piece[1]
    start = half * hr + piece[0] * n
    if kind == "col":
        return ref.at[pl.ds(start, n), pl.ds(slot * cs, cs)]
    return ref.at[pl.ds(slot * r + start, n), :]


def _full_shape(kind, shard_shape):
    r, cs = shard_shape
    return (r, 4 * cs) if kind == "col" else (4 * r, cs)


_HBM = pl.BlockSpec(memory_space=pltpu.HBM)


def _in_place(arrays, n_sem, copies):
    return _Rider(arrays, [_sds(t.shape, t.dtype) for t in arrays], {a: a for a in range(len(arrays))}, n_sem, copies)


def _gather_chips(fulls, kinds, shapes, piece=(0, 1)):
    n = len(fulls)

    def copies(ins, outs, send_sems, recv_sems):
        x, y, c, chips = _place()
        me_slot = 2 * x + y
        sends, recvs = [], []
        for a in range(n):
            for j, (cx, cy) in enumerate(chips):
                k = 3 * a + j
                sends.append(_remote(_region(ins[a], kinds[a], shapes[a], me_slot, c, piece),
                                     _region(outs[a], kinds[a], shapes[a], me_slot, c, piece),
                                     send_sems.at[k], recv_sems.at[k], (cx, cy, c)))
                land = _region(outs[a], kinds[a], shapes[a], 2 * cx + cy, c, piece)
                recvs.append(_remote(land, land, send_sems.at[k], recv_sems.at[k], (cx, cy, c)))
        return sends, recvs

    return _in_place(fulls, 3 * n, copies)


def _gather_pair(fulls, kinds, shapes):
    n = len(fulls)

    def copies(ins, outs, send_sems, recv_sems):
        x, y, c, chips = _place()
        sibling = (x, y, 1 - c)
        sends, recvs = [], []
        for a in range(n):
            for j, (cx, cy) in enumerate(chips):
                k = 3 * a + j
                sends.append(_remote(_region(ins[a], kinds[a], shapes[a], 2 * cx + cy, c), _region(outs[a], kinds[a], shapes[a], 2 * cx + cy, c),
                                     send_sems.at[k], recv_sems.at[k], sibling))
                land = _region(outs[a], kinds[a], shapes[a], 2 * cx + cy, 1 - c)
                recvs.append(_remote(land, land, send_sems.at[k], recv_sems.at[k], sibling))
        return sends, recvs

    return _in_place(fulls, 3 * n, copies)


def _rs_pair(grads, kinds, shapes):
    n = len(grads)

    def copies(ins, outs, send_sems, recv_sems):
        x, y, c, _ = _place()
        sibling = (x, y, 1 - c)
        sends, recvs = [], []
        for a in range(n):
            for slot in range(4):
                k = 4 * a + slot
                sends.append(_remote(_region(ins[a], kinds[a], shapes[a], slot, 1 - c), outs[a].at[slot], send_sems.at[k],
                                     recv_sems.at[k], sibling))
                recvs.append(_remote(_region(ins[a], kinds[a], shapes[a], slot, c), outs[a].at[slot], send_sems.at[k],
                                     recv_sems.at[k], sibling))
        return sends, recvs

    return _Rider(grads, [_sds((4, s[0] // 2, s[1]), F32) for s in shapes], {}, 4 * n, copies)


def _rs_pair_add(grad, recv, kind, shape, c_arr, name):
    r, cs = shape
    hr = r // 2
    tr, tc = _wide_tile(hr, cs)
    nr, nc = hr // tr, cs // tc

    def body(c_ref, g_ref, r_ref, o_ref):
        o_ref[...] = (g_ref[...] + r_ref[...]).astype(BF16)

    if kind == "col":
        g_map = lambda s_, i, j, c_ref: (c_ref[0] * nr + i, s_ * nc + j)
    else:
        g_map = lambda s_, i, j, c_ref: (s_ * 2 * nr + c_ref[0] * nr + i, j)
    own = pl.BlockSpec((None, tr, tc), lambda s_, i, j, c_ref: (s_, i, j))
    return _call(body, name=name, grid=(4, nr, nc), in_specs=[pl.BlockSpec((tr, tc), g_map), own], out_specs=own,
                 out_shape=_sds((4, hr, cs), BF16), nsp=1)(c_arr, grad, recv)


def _rs_chips(parts, piece=(0, 1), dest=None):
    n = len(parts)

    def copies(ins, outs, send_sems, recv_sems):
        x, y, c, chips = _place()
        sends, recvs = [], []
        for a in range(n):
            rows = parts[a].shape[1] // piece[1]
            sel = pl.ds(piece[0] * rows, rows)
            for j, (cx, cy) in enumerate(chips):
                k = 3 * a + j
                cp = _remote(ins[a].at[2 * cx + cy, sel], outs[a].at[j, sel], send_sems.at[k], recv_sems.at[k], (cx, cy, c))
                sends.append(cp)
                recvs.append(cp)
        return sends, recvs

    out_shapes = [_sds((3,) + p.shape[1:], BF16) for p in parts]
    if dest is None:
        return _Rider(parts, out_shapes, {}, 3 * n, copies)
    return _Rider(list(parts) + list(dest), out_shapes, {n + a: a for a in range(n)}, 3 * n, copies)


def _rs_chip_add(part, recv, slot_arr, c_arr, dest, layer, out_shape, name):
    _, hr, cs = part.shape
    tr, tc = _wide_tile(hr, cs)
    nr = hr // tr

    def body(s_ref, c_ref, p_ref, r0_ref, r1_ref, r2_ref, *rest):
        o_ref = rest[-1]
        o_ref[...] = ((p_ref[...].astype(F32) + r0_ref[...].astype(F32)) + r1_ref[...].astype(F32)) + r2_ref[...].astype(F32)

    rk = lambda k: pl.BlockSpec((None, tr, tc), lambda i, j, s_ref, c_ref: (k, i, j))
    if layer is None:
        o_spec = pl.BlockSpec((tr, tc), lambda i, j, s_ref, c_ref: (c_ref[0] * nr + i, j))
    else:
        o_spec = pl.BlockSpec((None, tr, tc), lambda i, j, s_ref, c_ref: (layer, c_ref[0] * nr + i, j))
    in_specs = [pl.BlockSpec((None, tr, tc), lambda i, j, s_ref, c_ref: (s_ref[0], i, j)), rk(0), rk(1), rk(2)]
    args = [slot_arr, c_arr, part, recv, recv, recv]
    aliases = {}
    if dest is not None:
        in_specs.append(pl.BlockSpec(memory_space=pl.ANY))
        args.append(dest)
        aliases = {6: 0}
    return pl.pallas_call(
        body, name=name,
        grid_spec=pltpu.PrefetchScalarGridSpec(num_scalar_prefetch=2, grid=(nr, cs // tc), in_specs=in_specs, out_specs=o_spec),
        out_shape=_sds(out_shape, F32), input_output_aliases=aliases,
        compiler_params=pltpu.CompilerParams(dimension_semantics=("arbitrary",) * 2, vmem_limit_bytes=VMEM_LIMIT_MB << 20),
        interpret=False,
    )(*args)


def _rs_pair_share(shards, name):
    n = len(shards)
    views = []
    for a, t in enumerate(shards):
        views += [(a, None)] if t.ndim == 2 else [(a, l) for l in range(t.shape[0])]
    nv = len(views)

    def body(*refs):
        ins, outs = refs[0:n], refs[n:2 * n]
        send_sems, recv_sems = refs[2 * n:]
        x, y, c, _ = _place()
        sibling = (x, y, 1 - c)

        def rows(ref_list, k, half):
            a, layer = views[k]
            ref = ref_list[a] if layer is None else ref_list[a].at[layer]
            hr = ref.shape[0] // 2
            return ref.at[pl.ds(half * hr, hr), :]

        sent = []
        for k in range(nv):
            cp = _remote(rows(ins, k, c), rows(outs, k, c), send_sems.at[k], recv_sems.at[k], sibling)
            cp.start()
            sent.append(cp)
        for k in range(nv):
            _remote(rows(ins, k, 1 - c), rows(outs, k, 1 - c), send_sems.at[k], recv_sems.at[k], sibling).wait_recv()
        for cp in sent:
            cp.wait_send()

    return pl.pallas_call(
        body, name=name, out_shape=[_sds(t.shape, F32) for t in shards], in_specs=[_HBM] * n, out_specs=[_HBM] * n,
        input_output_aliases={a: a for a in range(n)},
        scratch_shapes=[pltpu.SemaphoreType.DMA((nv,)), pltpu.SemaphoreType.DMA((nv,))], interpret=False,
    )(*shards)


def _pad_to(v, mult):
    n = v.shape[0]
    return jnp.pad(v, (0, (-n) % mult))


def kernel(x, c, positions, mod_w, mod_b, norm_mix_g, norm_ffn_g, conv_pw1_w, conv_pw1_b, conv_dw_w, conv_dw_b, conv_ln_g, conv_ln_b, conv_pw2_w, conv_pw2_b, kv_mod_w, kv_mod_b, kv_norm_g, w_kv, k_norm_g, w_q, q_norm_g, w_o, ffn_up_w, ffn_dw_w, ffn_dw_b, ffn_down_w, loss_target, m_mod_w, m_mod_b, m_norm_mix_g, m_norm_ffn_g, m_conv_pw1_w, m_conv_pw1_b, m_conv_dw_w, m_conv_dw_b, m_conv_ln_g, m_conv_ln_b, m_conv_pw2_w, m_conv_pw2_b, m_kv_mod_w, m_kv_mod_b, m_kv_norm_g, m_w_kv, m_k_norm_g, m_w_q, m_q_norm_g, m_w_o, m_ffn_up_w, m_ffn_dw_w, m_ffn_dw_b, m_ffn_down_w, v_mod_w, v_mod_b, v_norm_mix_g, v_norm_ffn_g, v_conv_pw1_w, v_conv_pw1_b, v_conv_dw_w, v_conv_dw_b, v_conv_ln_g, v_conv_ln_b, v_conv_pw2_w, v_conv_pw2_b, v_kv_mod_w, v_kv_mod_b, v_kv_norm_g, v_w_kv, v_k_norm_g, v_w_q, v_q_norm_g, v_w_o, v_ffn_up_w, v_ffn_dw_w, v_ffn_dw_b, v_ffn_down_w):
    _, s, d = x.shape
    f = ffn_dw_b.shape[1]
    qw = w_q.shape[2] * 4
    ax, ay, ac = lax.axis_index("x"), lax.axis_index("y"), lax.axis_index("c")
    slot = 2 * ax + ay
    me8 = 4 * ax + 2 * ay + ac
    slot_arr = jnp.reshape(slot, (1,)).astype(jnp.int32)
    c_arr = jnp.reshape(ac, (1,)).astype(jnp.int32)
    x2 = x.reshape(s, d)
    target = loss_target.reshape(s, d)
    row = lambda v: v.reshape(1, -1)

    c_all = _allgather8(c.reshape(8, d // 8), "gather_c").reshape(N_DEV, d)
    sc_all = jax.nn.silu(c_all)
    mod_part = _mod_proj(sc_all, mod_w, "mod_proj")
    kvm_part = _mod_proj(sc_all, kv_mod_w[None], "kvmod_proj")
    nm, nk = mod_part.shape[2], kvm_part.shape[2]
    small_sharded = [conv_pw1_b, conv_dw_w, conv_dw_b, conv_ln_g, conv_ln_b, conv_pw2_b, ffn_dw_w]
    pack = jnp.concatenate([mod_part.reshape(-1), kvm_part.reshape(-1)] + [t.reshape(-1) for t in small_sharded])
    plen = pack.shape[0]
    pack = _pad_to(pack, 1024)
    gathered = _allgather8(pack.reshape(8, -1), "gather_mod").reshape(4, 2, -1)[:, 0, :plen]
    off = 0

    def take(n_el):
        nonlocal off
        out = lax.slice_in_dim(gathered, off, off + n_el, axis=1)
        off += n_el
        return out

    mod_g = take(2 * N_DEV * nm).reshape(4, 2, N_DEV, nm)
    kvm_g = take(N_DEV * nk).reshape(4, N_DEV, nk)
    mine = lambda t, axis: lax.dynamic_index_in_dim(t, me8, axis=axis, keepdims=False)
    mod_vec = jnp.transpose(mine(mod_g, 2), (1, 0, 2)).reshape(2, 4 * nm) + mod_b
    kvm_vec = mine(kvm_g, 1).reshape(4 * nk) + kv_mod_b
    pw1_b_full = take(conv_pw1_b.shape[1]).reshape(1, -1)
    dw_w_full = jnp.transpose(take(CONV_K * (d // 4)).reshape(4, CONV_K, d // 4), (1, 0, 2)).reshape(CONV_K, d)
    dw_b_full, ln_g_full, ln_b_full, pw2_b_full = (take(d // 4).reshape(1, d) for _ in range(4))
    fdw_full = jnp.transpose(take(2 * FFN_K * (f // 4)).reshape(4, 2, FFN_K, f // 4), (1, 2, 0, 3)).reshape(2, FFN_K, f)
    mods = [[row(mod_vec[l, k * d:(k + 1) * d]) for k in range(6)] for l in range(2)]
    kv_sh, kv_sc = row(kvm_vec[:d]), row(kvm_vec[d:])
    zero_d = jnp.zeros((1, d), F32)

    big = [("pw1", conv_pw1_w, 0, "col"), ("pw2", conv_pw2_w, 0, "row"), ("wkv", w_kv[None], 0, "col"), ("wq", w_q, 0, "col"),
           ("wo", w_o, 0, "col"), ("up0", ffn_up_w, 0, "col"), ("up1", ffn_up_w, 1, "col"), ("dn0", ffn_down_w, 0, "row"),
           ("dn1", ffn_down_w, 1, "row")]
    names = [b[0] for b in big]
    kinds = [b[3] for b in big]
    shard_shapes = [b[1].shape[1:] for b in big]
    own = {b[0]: _cast_into_full(b[1], b[2], b[3], slot_arr, "cast_" + b[0]) for b in big}
    meta = dict(zip(names, zip(kinds, shard_shapes)))
    full = {}

    def stage(make, group, src, **kw):
        return make([src[k] for k in group], [meta[k][0] for k in group], [meta[k][1] for k in group], **kw)

    grp_a, grp_b, grp_c, grp_d = ["pw1", "pw2"], ["up0", "dn0"], ["wkv", "wq", "wo"], ["up1", "dn1"]
    landed_a = dict(zip(grp_a, _run_rider(stage(_gather_chips, grp_a, own), "gather_a_chips")))
    full.update(zip(grp_a, _run_rider(stage(_gather_pair, grp_a, landed_a), "gather_a_pair")))
    chips_c = stage(_gather_chips, grp_c, own)

    def pair_stage(group, chips_rider):
        return stage(_gather_pair, group, dict(zip(group, chips_rider.results)))

    def ffn_fwd(l, x_in, h, tag, rider_up=None, rider_down=None):
        u = _matmul(h, full["up%d" % l], "nn", F32, "mm_up" + tag, rider=rider_up)
        z = _ffn_act_fwd(u, fdw_full[l], row(ffn_dw_b[l]), "ffn_act" + tag)
        y = _matmul(z, full["dn%d" % l], "nn", F32, "mm_down" + tag, rider=rider_down)
        return u, z, y

    sh_m0, sc_m0, g_m0, sh_f0, sc_f0, g_f0 = mods[0]
    sh_m1, sc_m1, g_m1, sh_f1, sc_f1, g_f1 = mods[1]
    gmix0, gmix1, gffn0, gffn1 = row(norm_mix_g[0]), row(norm_mix_g[1]), row(norm_ffn_g[0]), row(norm_ffn_g[1])
    (h0,) = _resid_mod(x2, None, [(gmix0, sc_m0, sh_m0)], None, "mod_in")
    chips_b0 = stage(_gather_chips, grp_b, own, piece=(0, 2))
    u0 = _matmul(h0, full["pw1"], "nn", F32, "mm_pw1", rider=chips_b0)
    glu = _glu_fwd(u0, pw1_b_full, "glu")
    chips_b = stage(_gather_chips, grp_b, dict(zip(grp_b, chips_b0.results)), piece=(1, 2))
    dwc = _dwconv_fwd(glu, dw_w_full, dw_b_full, "dwconv", rider=chips_b)
    act = _ln_silu_fwd(dwc, ln_g_full, ln_b_full, "ln_silu")
    pair_b = pair_stage(grp_b, chips_b)
    y0 = _matmul(act, full["pw2"], "nn", F32, "mm_pw2", rider=pair_b)
    full.update(zip(grp_b, pair_b.results))
    x1, hf0 = _resid_mod(x2, (y0, pw2_b_full, g_m0), [(gffn0, sc_f0, sh_f0)], None, "resid_conv")
    uf0 = _matmul(hf0, full["up0"], "nn", F32, "mm_up0", rider=chips_c)
    zf0 = _ffn_act_fwd(uf0, fdw_full[0], row(ffn_dw_b[0]), "ffn_act0")
    pair_c = pair_stage(grp_c, chips_c)
    yf0 = _matmul(zf0, full["dn0"], "nn", F32, "mm_down0", rider=pair_c)
    full.update(zip(grp_c, pair_c.results))
    gkv = row(kv_norm_g)
    xa, hk, hq = _resid_mod(x1, (yf0, zero_d, g_f0), [(gkv, kv_sc, kv_sh), (gmix1, sc_m1, sh_m1)], None, "resid_ffn0")
    chips_d0 = stage(_gather_chips, grp_d, own, piece=(0, 2))
    kvp = _matmul(hk, full["wkv"], "nn", F32, "mm_kv", rider=chips_d0)
    chips_d = stage(_gather_chips, grp_d, dict(zip(grp_d, chips_d0.results)), piece=(1, 2))
    qp = _matmul(hq, full["wq"], "nn", F32, "mm_q")
    inv_freq = ROPE_THETA ** (-jnp.arange(0, ROT_DIM, 2, dtype=F32) / ROT_DIM)
    half = ROT_DIM // 2
    freq_l = jnp.concatenate([inv_freq, inv_freq, jnp.zeros((HEAD_DIM - ROT_DIM,), F32)]).reshape(1, HEAD_DIM)
    sign_l = jnp.concatenate([-jnp.ones((half,), F32), jnp.ones((half,), F32), jnp.zeros((HEAD_DIM - ROT_DIM,), F32)]).reshape(1, HEAD_DIM)
    ctab, stab = _rope_tables(positions.reshape(s, 1), freq_l, sign_l, "rope_tables")
    gq, gk = row(q_norm_g[0]), row(k_norm_g)
    qn, kn = _qk_norm_rope_fwd(qp, kvp, gq, gk, ctab, stab, "qk_norm_rope", rider=chips_d)
    og, lg = zip(*[_attn_fwd(qn, kn, kvp, g, "attn_fwd%d" % g) for g in range(N_GROUPS)])
    o_mix, lse = _attn_combine(og, lg, "attn_combine")
    pair_d = pair_stage(grp_d, chips_d)
    ya = _matmul(o_mix, full["wo"], "nn", F32, "mm_o", rider=pair_d)
    full.update(zip(grp_d, pair_d.results))
    xb, hf1 = _resid_mod(xa, (ya, zero_d, g_m1), [(gffn1, sc_f1, sh_f1)], None, "resid_attn")
    uf1, zf1, yf1 = ffn_fwd(1, xb, hf1, "1")
    _, dxo, loss_cols = _resid_mod(xb, (yf1, zero_d, g_f1), [], target, "resid_loss")
    loss = lax.psum(jnp.sum(loss_cols), ("x", "y", "c"))

    gbig = {}

    parts, recv2 = {}, {}

    def rs_pair(group):
        return _rs_pair([gbig[k] for k in group], [meta[k][0] for k in group], [meta[k][1] for k in group])

    def rs_chips(group, pair_results, **kw):
        for k, r_ in zip(group, pair_results):
            parts[k] = _rs_pair_add(gbig[k], r_, meta[k][0], meta[k][1], c_arr, "rs_pair_add_" + k)
        return _rs_chips([parts[k] for k in group], **kw)

    rs_a, rs_b, rs_c, rs_d = ["dn1", "up1"], ["wo", "wq", "wkv"], ["dn0", "up0"], ["pw2", "pw1"]

    def ffn_bwd(l, dy, u, z, h, tag, rider_dx=None, act_rider=lambda: None):
        dz = _matmul(dy, full["dn%d" % l], "nt", F32, "mm_down_dx" + tag, rider=rider_dx)
        gbig["dn%d" % l] = _matmul(z, dy, "tn", F32, "mm_down_dw" + tag)
        du, dfw, dfb = _ffn_act_bwd(dz, u, fdw_full[l], row(ffn_dw_b[l]), "ffn_act_bwd" + tag, rider=act_rider())
        dh = _matmul(du, full["up%d" % l], "nt", F32, "mm_up_dx" + tag)
        gbig["up%d" % l] = _matmul(h, du, "tn", F32, "mm_up_dw" + tag)
        return dh, dfw, dfb

    dxb0, dyf1, dg_f1, _ = _bwd_step(dxo, xb, [], (yf1, zero_d, g_f1), "bwd_loss")
    dhf1, dfw1, dfb1 = ffn_bwd(1, dyf1, uf1, zf1, hf1, "1")
    dxb, dsh_f1, dsc_f1, dgffn1, dya, dg_m1, _ = _bwd_step(dxb0, xb, [(dhf1, gffn1, sc_f1)], (ya, zero_d, g_m1), "bwd_attn_out")
    do = _matmul(dya, full["wo"], "nt", F32, "mm_o_dx")
    gbig["wo"] = _matmul(o_mix, dya, "tn", F32, "mm_o_dw")
    delta = _attn_delta(do, o_mix, "attn_delta")
    pair_a = rs_pair(rs_a)
    dqs = [_attn_bwd_dq(qn, kn, kvp, do, lse, delta, g, "attn_dq%d" % g, rider=pair_a if g == 0 else None) for g in range(N_GROUPS)]
    dks, dvs = zip(*[_attn_bwd_dkv(qn, kn, kvp, do, lse, delta, g, "attn_dkv%d" % g) for g in range(N_GROUPS)])
    ride_a0 = rs_chips(rs_a, pair_a.results, piece=(0, 2))
    dqp, dkvp, dgq, dgk = _qk_norm_rope_bwd(dqs, dks, dvs, qp, kvp, gq, gk, ctab, stab, "qk_norm_rope_bwd", rider=ride_a0)
    dhq = _matmul(dqp, full["wq"], "nt", F32, "mm_q_dx")
    gbig["wq"] = _matmul(hq, dqp, "tn", F32, "mm_q_dw")
    ride_a = _rs_chips([parts[k] for k in rs_a], piece=(1, 2), dest=ride_a0.results)
    dhk = _matmul(dkvp, full["wkv"], "nt", F32, "mm_kv_dx", rider=ride_a)
    recv2.update(zip(rs_a, ride_a.results))
    gbig["wkv"] = _matmul(hk, dkvp, "tn", F32, "mm_kv_dw")
    (dxa, dsh_kv, dsc_kv, dgkv, dsh_m1, dsc_m1, dgmix1, dyf0, dg_f0, _) = _bwd_step(
        dxb, xa, [(dhk, gkv, kv_sc), (dhq, gmix1, sc_m1)], (yf0, zero_d, g_f0), "bwd_kvq")
    pair_b = rs_pair(rs_b)
    ride_b = []
    dhf0, dfw0, dfb0 = ffn_bwd(0, dyf0, uf0, zf0, hf0, "0", rider_dx=pair_b,
                               act_rider=lambda: ride_b.append(rs_chips(rs_b, pair_b.results)) or ride_b[0])
    recv2.update(zip(rs_b, ride_b[0].results))
    pair_c = rs_pair(rs_c)
    dx1, dsh_f0, dsc_f0, dgffn0, dy0, dg_m0, dpw2_b = _bwd_step(dxa, x1, [(dhf0, gffn0, sc_f0)], (y0, pw2_b_full, g_m0), "bwd_conv_out",
                                                             rider=pair_c)
    dact = _matmul(dy0, full["pw2"], "nt", F32, "mm_pw2_dx")
    gbig["pw2"] = _matmul(act, dy0, "tn", F32, "mm_pw2_dw")
    ddwc, dln_g, dln_b = _ln_silu_bwd(dact, dwc, ln_g_full, ln_b_full, "ln_silu_bwd")
    ride_c = rs_chips(rs_c, pair_c.results)
    dglu, ddw_w, ddw_b = _dwconv_bwd(ddwc, glu, dw_w_full, "dwconv_bwd", rider=ride_c)
    recv2.update(zip(rs_c, ride_c.results))
    du0, dpw1_b = _glu_bwd(dglu, u0, pw1_b_full, "glu_bwd")
    dh0 = _matmul(du0, full["pw1"], "nt", F32, "mm_pw1_dx")
    gbig["pw1"] = _matmul(h0, du0, "tn", F32, "mm_pw1_dw")
    grad_x, dsh_m0, dsc_m0, dgmix0 = _bwd_step(dx1, x2, [(dh0, gmix0, sc_m0)], None, "bwd_in")

    dmod = [jnp.concatenate([dsh_m0, dsc_m0, dg_m0, dsh_f0, dsc_f0, dg_f0], axis=1),
            jnp.concatenate([dsh_m1, dsc_m1, dg_m1, dsh_f1, dsc_f1, dg_f1], axis=1)]
    dkvm = jnp.concatenate([dsh_kv, dsc_kv], axis=1)
    per_ex = [dmod[0], dmod[1], dkvm]
    summed = [dgmix0, dgmix1, dgffn0, dgffn1, dpw1_b, ddw_w, ddw_b, dln_g, dln_b, dpw2_b, dgkv, dgk, dgq, dfw0, dfw1, dfb0, dfb1]
    vec = jnp.concatenate([t.reshape(-1) for t in per_ex + summed])
    vlen = vec.shape[0]
    vec = _pad_to(vec, 1024)
    vall = _allgather8(vec.reshape(8, -1), "gather_small").reshape(N_DEV, -1)
    vsum = _sum_rows(vall, "sum_small")[0]
    n_pe = sum(t.size for t in per_ex)
    dm_all = vall[:, :n_pe]
    off2 = n_pe
    sums = []
    for t in summed:
        sums.append(vsum[off2:off2 + t.size].reshape(t.shape))
        off2 += t.size
    (s_gmix0, s_gmix1, s_gffn0, s_gffn1, s_pw1_b, s_dw_w, s_dw_b, s_ln_g, s_ln_b, s_pw2_b, s_gkv, s_gk, s_gq, s_fw0, s_fw1,
     s_fb0, s_fb1) = sums
    shard_cols = lambda t, width: lax.dynamic_slice_in_dim(t, slot * width, width, axis=t.ndim - 1)
    dm_mod = jnp.stack([shard_cols(dm_all[:, l * 6 * d:(l + 1) * 6 * d], nm) for l in range(2)])
    dm_kv = shard_cols(dm_all[:, 12 * d:14 * d], nk)[None]
    sct = jnp.transpose(sc_all)

    recv2.update(zip(rs_d, _run_rider(rs_chips(rs_d, _run_rider(rs_pair(rs_d), "rs_pair_d")), "rs_chips_d")))
    reduced = {}
    for nme, shp in zip(names, shard_shapes):
        p_, r_ = parts[nme], recv2[nme]
        if nme in ("up0", "up1", "dn0", "dn1"):
            key, layer = nme[:-1], int(nme[-1])
            reduced[key] = _rs_chip_add(p_, r_, slot_arr, c_arr, reduced.get(key), layer, (2,) + shp, "rs_chip_add_" + nme)
        else:
            reduced[nme] = _rs_chip_add(p_, r_, slot_arr, c_arr, None, None, shp, "rs_chip_add_" + nme)
    g_pw1, g_pw2, g_wkv, g_wq, g_wo, g_up, g_dn = _rs_pair_share(
        [reduced[k] for k in ("pw1", "pw2", "wkv", "wq", "wo", "up", "dn")], "rs_pair_share")

    grads, deltas, new_m, new_v = {}, {}, {}, {}

    def put(nme, g_, res):
        grads[nme] = g_
        deltas[nme], new_m[nme], new_v[nme] = res

    for nme, g_, w_, m_, v_ in (("conv_pw1_w", g_pw1[None], conv_pw1_w, m_conv_pw1_w, v_conv_pw1_w),
                                ("conv_pw2_w", g_pw2[None], conv_pw2_w, m_conv_pw2_w, v_conv_pw2_w),
                                ("w_kv", g_wkv, w_kv, m_w_kv, v_w_kv), ("w_q", g_wq[None], w_q, m_w_q, v_w_q),
                                ("w_o", g_wo[None], w_o, m_w_o, v_w_o), ("ffn_up_w", g_up, ffn_up_w, m_ffn_up_w, v_ffn_up_w),
                                ("ffn_down_w", g_dn, ffn_down_w, m_ffn_down_w, v_ffn_down_w)):
        put(nme, g_, _adamw_big(w_, g_, m_, v_, "adamw_" + nme))
    g_, *res = _adamw_mod(mod_w, sct, dm_mod, m_mod_w, v_mod_w, "adamw_mod_w")
    put("mod_w", g_, res)
    g_, *res = _adamw_mod(kv_mod_w[None], sct, dm_kv, m_kv_mod_w[None], v_kv_mod_w[None], "adamw_kv_mod_w")
    put("kv_mod_w", g_[0], [t[0] for t in res])

    dm_sum = vsum[:n_pe]
    small = [
        ("mod_b", dm_sum[:12 * d].reshape(2, 6 * d), mod_b, m_mod_b, v_mod_b),
        ("norm_mix_g", jnp.concatenate([s_gmix0, s_gmix1], axis=0), norm_mix_g, m_norm_mix_g, v_norm_mix_g),
        ("norm_ffn_g", jnp.concatenate([s_gffn0, s_gffn1], axis=0), norm_ffn_g, m_norm_ffn_g, v_norm_ffn_g),
        ("conv_pw1_b", shard_cols(s_pw1_b, conv_pw1_b.shape[1]), conv_pw1_b, m_conv_pw1_b, v_conv_pw1_b),
        ("conv_dw_w", shard_cols(s_dw_w, d // 4)[None], conv_dw_w, m_conv_dw_w, v_conv_dw_w),
        ("conv_dw_b", shard_cols(s_dw_b, d // 4), conv_dw_b, m_conv_dw_b, v_conv_dw_b),
        ("conv_ln_g", shard_cols(s_ln_g, d // 4), conv_ln_g, m_conv_ln_g, v_conv_ln_g),
        ("conv_ln_b", shard_cols(s_ln_b, d // 4), conv_ln_b, m_conv_ln_b, v_conv_ln_b),
        ("conv_pw2_b", shard_cols(s_pw2_b, d // 4), conv_pw2_b, m_conv_pw2_b, v_conv_pw2_b),
        ("kv_mod_b", dm_sum[12 * d:14 * d], kv_mod_b, m_kv_mod_b, v_kv_mod_b),
        ("kv_norm_g", s_gkv.reshape(-1), kv_norm_g, m_kv_norm_g, v_kv_norm_g),
        ("k_norm_g", s_gk.reshape(-1), k_norm_g, m_k_norm_g, v_k_norm_g),
        ("q_norm_g", s_gq, q_norm_g, m_q_norm_g, v_q_norm_g),
        ("ffn_dw_w", shard_cols(jnp.stack([s_fw0, s_fw1]), f // 4), ffn_dw_w, m_ffn_dw_w, v_ffn_dw_w),
        ("ffn_dw_b", jnp.concatenate([s_fb0, s_fb1], axis=0), ffn_dw_b, m_ffn_dw_b, v_ffn_dw_b),
    ]
    as2d = lambda t: t.reshape(-1, t.shape[-1])
    sd_, sm_, sv_ = _adamw_small([as2d(t[2]) for t in small], [as2d(t[1]) for t in small], [as2d(t[3]) for t in small],
                                 [as2d(t[4]) for t in small], "adamw_small")
    for (nme, g_, w_, _, _), d_, mo_, vo_ in zip(small, sd_, sm_, sv_):
        put(nme, g_.reshape(w_.shape), [d_.reshape(w_.shape), mo_.reshape(w_.shape), vo_.reshape(w_.shape)])

    order = ["mod_w", "mod_b", "norm_mix_g", "norm_ffn_g", "conv_pw1_w", "conv_pw1_b", "conv_dw_w", "conv_dw_b", "conv_ln_g",
             "conv_ln_b", "conv_pw2_w", "conv_pw2_b", "kv_mod_w", "kv_mod_b", "kv_norm_g", "w_kv", "k_norm_g", "w_q", "q_norm_g",
             "w_o", "ffn_up_w", "ffn_dw_w", "ffn_dw_b", "ffn_down_w"]
    return (loss, grad_x.reshape(x.shape), *[grads[k] for k in order], *[deltas[k] for k in order], *[new_m[k] for k in order],
            *[new_v[k] for k in order])
```

```python
import functools
import math

import jax
import jax.numpy as jnp
from jax import lax
from jax.experimental import pallas as pl
from jax.experimental.pallas import tpu as pltpu

F32 = jnp.float32
BF16 = jnp.bfloat16
EPS = 1e-6
NEG = -1e30
HEAD_DIM = 128
ROT_DIM = 32
ROPE_THETA = 500000.0
BLK = 128
DILATIONS = (1, 4, 16)
N_GROUPS = 3
CONV_K = 31
FFN_K = 3
ADAM_LR, ADAM_B1, ADAM_B2, ADAM_EPS, ADAM_WD, ADAM_STEP = 0.001, 0.9, 0.999, 1e-08, 0.01, 10
N_DEV = 8
MESH = pl.DeviceIdType.MESH
VMEM_LIMIT_MB = 56
ROW_TILE = 256
ATTN_CHUNK = 2048


def _pick(n, pref, mult=128):
    best = None
    d = mult
    while d <= min(n, pref):
        if n % d == 0:
            best = d
        d += mult
    return best if best is not None else n


def _wide_tile(rows, cols):
    tc = _pick(cols, 4096)
    return _pick(rows, max(16, (1 << 19) // tc), 16), tc


class _Rider:
    def __init__(self, ins, out_shapes, aliases, n_sem, copies):
        self.ins, self.out_shapes, self.aliases, self.n_sem, self.copies = list(ins), list(out_shapes), dict(aliases), n_sem, copies
        self.results = None


def _call(body, *, name, grid, in_specs, out_specs, out_shape, scratch=(), nsp=0, rider=None):
    params = pltpu.CompilerParams(dimension_semantics=("arbitrary",) * len(grid), vmem_limit_bytes=VMEM_LIMIT_MB << 20)
    if rider is None:
        return pl.pallas_call(
            body, name=name,
            grid_spec=pltpu.PrefetchScalarGridSpec(num_scalar_prefetch=nsp, grid=grid, in_specs=in_specs, out_specs=out_specs,
                                                   scratch_shapes=list(scratch)),
            out_shape=out_shape, compiler_params=params, interpret=False,
        )
    single = not isinstance(out_shape, (list, tuple))
    out_shapes = [out_shape] if single else list(out_shape)
    out_specs_l = [out_specs] if single else list(out_specs)
    n_in, n_out, n_scr = len(in_specs), len(out_shapes), len(scratch)
    r_in, r_out = len(rider.ins), len(rider.out_shapes)
    hbm = pl.BlockSpec(memory_space=pltpu.HBM)
    last = tuple(g - 1 for g in grid)

    def wrapped(*refs):
        pre, ins, rin = refs[:nsp], refs[nsp:nsp + n_in], refs[nsp + n_in:nsp + n_in + r_in]
        o0 = nsp + n_in + r_in
        outs, rout = refs[o0:o0 + n_out], refs[o0 + n_out:o0 + n_out + r_out]
        s0 = o0 + n_out + r_out
        scr, (send_sems, recv_sems) = refs[s0:s0 + n_scr], refs[s0 + n_scr:]
        ids = [pl.program_id(a) for a in range(len(grid))]
        is_first = functools.reduce(jnp.logical_and, [i == 0 for i in ids])
        is_last = functools.reduce(jnp.logical_and, [i == l for i, l in zip(ids, last)])

        @pl.when(is_first)
        def _():
            for cp in rider.copies(rin, rout, send_sems, recv_sems)[0]:
                cp.start()

        body(*pre, *ins, *outs, *scr)

        @pl.when(is_last)
        def _():
            sends, recvs = rider.copies(rin, rout, send_sems, recv_sems)
            for cp in recvs:
                cp.wait_recv()
            for cp in sends:
                cp.wait_send()

    call = pl.pallas_call(
        wrapped, name=name,
        grid_spec=pltpu.PrefetchScalarGridSpec(
            num_scalar_prefetch=nsp, grid=grid, in_specs=list(in_specs) + [hbm] * r_in, out_specs=out_specs_l + [hbm] * r_out,
            scratch_shapes=list(scratch) + [pltpu.SemaphoreType.DMA((rider.n_sem,)), pltpu.SemaphoreType.DMA((rider.n_sem,))]),
        out_shape=out_shapes + rider.out_shapes,
        input_output_aliases={nsp + n_in + i: n_out + o for i, o in rider.aliases.items()},
        compiler_params=params, interpret=False,
    )

    def run(*args):
        res = call(*args, *rider.ins)
        rider.results = list(res[n_out:])
        return res[0] if single else list(res[:n_out])

    return run


def _run_rider(rider, name):
    r_in, r_out = len(rider.ins), len(rider.out_shapes)
    hbm = pl.BlockSpec(memory_space=pltpu.HBM)

    def body(*refs):
        sends, recvs = rider.copies(refs[:r_in], refs[r_in:r_in + r_out], *refs[r_in + r_out:])
        for cp in sends:
            cp.start()
        for cp in recvs:
            cp.wait_recv()
        for cp in sends:
            cp.wait_send()

    return pl.pallas_call(
        body, name=name, out_shape=rider.out_shapes, in_specs=[hbm] * r_in, out_specs=[hbm] * r_out,
        input_output_aliases=rider.aliases,
        scratch_shapes=[pltpu.SemaphoreType.DMA((rider.n_sem,)), pltpu.SemaphoreType.DMA((rider.n_sem,))], interpret=False,
    )(*rider.ins)


def _sds(shape, dtype):
    return jax.ShapeDtypeStruct(shape, dtype)


def _acc(ref, val, i):
    @pl.when(i == 0)
    def _():
        ref[...] = val

    @pl.when(i > 0)
    def _():
        ref[...] += val


def _colsum(v):
    return jnp.sum(v, axis=0, keepdims=True)


def _silu(v):
    return v * jax.nn.sigmoid(v)


def _dsilu(v):
    s = jax.nn.sigmoid(v)
    return s * (1.0 + v * (1.0 - s))


_DIMS = {"nn": (((1,), (0,)), ((), ())), "nt": (((1,), (1,)), ((), ())), "tn": (((0,), (0,)), ((), ()))}


def _matmul(a, b, mode, out_dtype, name, rider=None):
    a_halves = a.shape[0] if a.ndim == 3 else 0
    b_halves = b.shape[0] if b.ndim == 3 else 0
    if mode == "nn":
        (m, c), (_, n) = a.shape, b.shape
    elif mode == "nt":
        m, c = (a.shape[1], a.shape[0] * a.shape[2]) if a_halves else a.shape
        n = b.shape[0]
    else:
        c, m = a.shape
        n = b.shape[0] * b.shape[2] if b_halves else b.shape[1]
    tm = _pick(m, 1024)
    tn = _pick(n // b_halves, 1024) if b_halves else _pick(n, 1024)
    c_cap = 2048 if mode == "tn" else 2816
    tc = _pick(c // a_halves, c_cap) if a_halves else _pick(c, c_cap)
    nk = c // tc
    if a_halves:
        per_a = c // a_halves // tc
        a_spec = pl.BlockSpec((None, tm, tc), lambda i, j, k: (k // per_a, i, k % per_a))
    else:
        a_spec = {"nn": pl.BlockSpec((tm, tc), lambda i, j, k: (i, k)), "nt": pl.BlockSpec((tm, tc), lambda i, j, k: (i, k)),
                  "tn": pl.BlockSpec((tc, tm), lambda i, j, k: (k, i))}[mode]
    if b_halves:
        per_b = n // b_halves // tn
        b_spec = pl.BlockSpec((None, tc, tn), lambda i, j, k: (j // per_b, k, j % per_b))
    else:
        b_spec = {"nn": pl.BlockSpec((tc, tn), lambda i, j, k: (k, j)), "nt": pl.BlockSpec((tn, tc), lambda i, j, k: (j, k)),
                  "tn": pl.BlockSpec((tc, tn), lambda i, j, k: (k, j))}[mode]
    dims = _DIMS[mode]

    def body(a_ref, b_ref, o_ref, acc_ref):
        k = pl.program_id(2)
        p = lax.dot_general(a_ref[...], b_ref[...], dims, preferred_element_type=F32)
        if nk == 1:
            o_ref[...] = p.astype(out_dtype)
        else:
            @pl.when(k == 0)
            def _():
                acc_ref[...] = p

            @pl.when(k > 0)
            def _():
                acc_ref[...] += p

            @pl.when(k == nk - 1)
            def _():
                o_ref[...] = acc_ref[...].astype(out_dtype)

    return _call(
        body, name=name, grid=(m // tm, n // tn, nk), in_specs=[a_spec, b_spec],
        out_specs=pl.BlockSpec((tm, tn), lambda i, j, k: (i, j)), out_shape=_sds((m, n), out_dtype),
        scratch=[pltpu.VMEM((tm, tn), F32)], rider=rider,
    )(a, b)


def _row_spec(tr, w):
    return pl.BlockSpec((tr, w), lambda i: (i, 0))


def _vec_spec(w):
    return pl.BlockSpec((1, w), lambda i: (0, 0))


def _resid_mod(x, prev, mods, target, name):
    s, d = x.shape
    tr = _pick(s, ROW_TILE, 8)
    n_mod = len(mods)

    def body(*refs):
        it = iter(refs)
        x_ref = next(it)
        if prev is not None:
            y_ref, yb_ref, gate_ref = next(it), next(it), next(it)
        mod_refs = [(next(it), next(it), next(it)) for _ in range(n_mod)]
        if target is not None:
            t_ref = next(it)
        if prev is not None:
            xo_ref = next(it)
        h_refs = [next(it) for _ in range(n_mod)]
        i = pl.program_id(0)
        xv = x_ref[...]
        if prev is not None:
            xv = xv + gate_ref[...] * (y_ref[...] + yb_ref[...])
            xo_ref[...] = xv
        if n_mod:
            nrm = xv * lax.rsqrt(jnp.mean(xv * xv, axis=-1, keepdims=True) + EPS)
            for (g_ref, sc_ref, sh_ref), h_ref in zip(mod_refs, h_refs):
                h_ref[...] = (nrm * g_ref[...] * (1.0 + sc_ref[...]) + sh_ref[...]).astype(BF16)
        if target is not None:
            dx_ref, loss_ref = next(it), next(it)
            err = xv - t_ref[...]
            dx_ref[...] = err * (1.0 / d)
            _acc(loss_ref, _colsum(err * err) * (0.5 / d), i)

    ins, in_specs = [x], [_row_spec(tr, d)]
    if prev is not None:
        ins += list(prev)
        in_specs += [_row_spec(tr, d), _vec_spec(d), _vec_spec(d)]
    for g, sc, sh in mods:
        ins += [g, sc, sh]
        in_specs += [_vec_spec(d)] * 3
    if target is not None:
        ins.append(target)
        in_specs.append(_row_spec(tr, d))
    out_shape, out_specs = [], []
    if prev is not None:
        out_shape.append(_sds((s, d), F32))
        out_specs.append(_row_spec(tr, d))
    for _ in mods:
        out_shape.append(_sds((s, d), BF16))
        out_specs.append(_row_spec(tr, d))
    if target is not None:
        out_shape += [_sds((s, d), F32), _sds((1, d), F32)]
        out_specs += [_row_spec(tr, d), _vec_spec(d)]
    return _call(body, name=name, grid=(s // tr,), in_specs=in_specs, out_specs=out_specs, out_shape=out_shape)(*ins)


def _bwd_step(dx_up, x, mods, prev, name, rider=None):
    s, d = x.shape
    tr = _pick(s, ROW_TILE, 8)
    n_mod = len(mods)

    def body(*refs):
        it = iter(refs)
        dxu_ref, x_ref = next(it), next(it)
        mod_refs = [(next(it), next(it), next(it)) for _ in range(n_mod)]
        if prev is not None:
            y_ref, yb_ref, gate_ref = next(it), next(it), next(it)
        dx_ref = next(it)
        acc_refs = [(next(it), next(it), next(it)) for _ in range(n_mod)]
        i = pl.program_id(0)
        dx = dxu_ref[...]
        if n_mod:
            xv = x_ref[...]
            rstd = lax.rsqrt(jnp.mean(xv * xv, axis=-1, keepdims=True) + EPS)
            nrm = xv * rstd
        for (dh_ref, g_ref, sc_ref), (dsh_ref, dsc_ref, dg_ref) in zip(mod_refs, acc_refs):
            dh = dh_ref[...].astype(F32)
            gv, one_sc = g_ref[...], 1.0 + sc_ref[...]
            _acc(dsh_ref, _colsum(dh), i)
            t = dh * nrm
            _acc(dsc_ref, _colsum(t) * gv, i)
            _acc(dg_ref, _colsum(t) * one_sc, i)
            dn = dh * (gv * one_sc)
            dx = dx + rstd * (dn - nrm * jnp.mean(dn * nrm, axis=-1, keepdims=True))
        dx_ref[...] = dx
        if prev is not None:
            dy_ref, dgate_ref, dyb_ref = next(it), next(it), next(it)
            dy = gate_ref[...] * dx
            dy_ref[...] = dy.astype(BF16)
            _acc(dgate_ref, _colsum(dx * (y_ref[...] + yb_ref[...])), i)
            _acc(dyb_ref, _colsum(dy), i)

    ins, in_specs = [dx_up, x], [_row_spec(tr, d)] * 2
    for dh, g, sc in mods:
        ins += [dh, g, sc]
        in_specs += [_row_spec(tr, d), _vec_spec(d), _vec_spec(d)]
    if prev is not None:
        ins += list(prev)
        in_specs += [_row_spec(tr, d), _vec_spec(d), _vec_spec(d)]
    out_shape, out_specs = [_sds((s, d), F32)], [_row_spec(tr, d)]
    for _ in mods:
        out_shape += [_sds((1, d), F32)] * 3
        out_specs += [_vec_spec(d)] * 3
    if prev is not None:
        out_shape += [_sds((s, d), BF16), _sds((1, d), F32), _sds((1, d), F32)]
        out_specs += [_row_spec(tr, d), _vec_spec(d), _vec_spec(d)]
    return _call(body, name=name, grid=(s // tr,), in_specs=in_specs, out_specs=out_specs, out_shape=out_shape, rider=rider)(*ins)


def _glu_fwd(u, bias, name):
    s, d2 = u.shape
    d = d2 // 2
    tr = _pick(s, ROW_TILE, 8)

    def body(u_ref, b_ref, o_ref):
        uv = u_ref[...] + b_ref[...]
        o_ref[...] = uv[:, :d] * jax.nn.sigmoid(uv[:, d:])

    return _call(body, name=name, grid=(s // tr,), in_specs=[_row_spec(tr, d2), _vec_spec(d2)],
                 out_specs=_row_spec(tr, d), out_shape=_sds((s, d), F32))(u, bias)


def _glu_bwd(dglu, u, bias, name):
    s, d2 = u.shape
    d = d2 // 2
    tr = _pick(s, ROW_TILE, 8)

    def body(dg_ref, u_ref, b_ref, du_ref, db_ref):
        i = pl.program_id(0)
        uv = u_ref[...] + b_ref[...]
        a, sg = uv[:, :d], jax.nn.sigmoid(uv[:, d:])
        dg = dg_ref[...]
        da = dg * sg
        dgt = dg * a * sg * (1.0 - sg)
        du_ref[:, :d] = da.astype(BF16)
        du_ref[:, d:] = dgt.astype(BF16)
        _acc(db_ref, jnp.concatenate([_colsum(da), _colsum(dgt)], axis=1), i)

    return _call(body, name=name, grid=(s // tr,), in_specs=[_row_spec(tr, d), _row_spec(tr, d2), _vec_spec(d2)],
                 out_specs=[_row_spec(tr, d2), _vec_spec(d2)], out_shape=[_sds((s, d2), BF16), _sds((1, d2), F32)])(dglu, u, bias)


def _halo_rows(k):
    return -(-(k - 1) // 8) * 8


CONV_CHUNK = 32


def _sublane_shifts(cat_ref, sh_ref, rows):
    for m in range(8):
        sh_ref[m] = cat_ref[pl.ds(m, rows), :]


def _dwconv_fwd(x, w, b, name, rider=None):
    s, c = x.shape
    kk = w.shape[0]
    hb = _halo_rows(kk)
    tr, tc = _pick(s, ROW_TILE, hb), _pick(c, 512)
    per = tr // hb
    ch = CONV_CHUNK

    def body(xp_ref, x_ref, w_ref, b_ref, o_ref, cat_ref, sh_ref):
        i = pl.program_id(1)
        cat_ref[0:hb, :] = jnp.where(i > 0, xp_ref[...], 0.0)
        cat_ref[hb:hb + tr, :] = x_ref[...]
        cat_ref[hb + tr:hb + tr + 8, :] = jnp.zeros((8, tc), F32)
        _sublane_shifts(cat_ref, sh_ref, tr + hb)

        def chunk(ci, carry):
            r0 = pl.multiple_of(ci * ch, ch)
            acc = jnp.zeros((ch, tc), F32) + b_ref[...]
            for k in range(kk):
                a, m = divmod(hb - (kk - 1) + k, 8)
                acc = acc + w_ref[k:k + 1, :] * sh_ref[m, pl.ds(r0 + 8 * a, ch), :]
            o_ref[pl.ds(r0, ch), :] = acc
            return carry

        lax.fori_loop(0, tr // ch, chunk, 0)

    return _call(
        body, name=name, grid=(c // tc, s // tr),
        in_specs=[pl.BlockSpec((hb, tc), lambda j, i: (jnp.maximum(i * per - 1, 0), j)), pl.BlockSpec((tr, tc), lambda j, i: (i, j)),
                  pl.BlockSpec((kk, tc), lambda j, i: (0, j)), pl.BlockSpec((1, tc), lambda j, i: (0, j))],
        out_specs=pl.BlockSpec((tr, tc), lambda j, i: (i, j)), out_shape=_sds((s, c), F32),
        scratch=[pltpu.VMEM((tr + hb + 8, tc), F32), pltpu.VMEM((8, tr + hb, tc), F32)], rider=rider,
    )(x, x, w, b)


def _dwconv_bwd(dy, x, w, name, rider=None):
    s, c = x.shape
    kk = w.shape[0]
    hb = _halo_rows(kk)
    tr, tc = _pick(s, ROW_TILE, hb), _pick(c, 512)
    per, nt = tr // hb, s // tr
    ch = CONV_CHUNK
    groups = ch // 8

    def body(xp_ref, x_ref, dy_ref, dyn_ref, w_ref, dx_ref, dw_ref, db_ref, cat_ref, shx_ref, shd_ref, acc_ref):
        i = pl.program_id(1)
        cat_ref[0:hb, :] = jnp.where(i > 0, xp_ref[...], 0.0)
        cat_ref[hb:hb + tr, :] = x_ref[...]
        cat_ref[hb + tr:hb + tr + 8, :] = jnp.zeros((8, tc), F32)
        _sublane_shifts(cat_ref, shx_ref, tr + hb)
        cat_ref[0:tr, :] = dy_ref[...]
        cat_ref[tr:tr + hb, :] = jnp.where(i < nt - 1, dyn_ref[...], 0.0)
        _sublane_shifts(cat_ref, shd_ref, tr + hb)

        @pl.when(i == 0)
        def _():
            acc_ref[...] = jnp.zeros_like(acc_ref)

        def fold(v):
            out = v[0:8, :]
            for g in range(1, groups):
                out = out + v[8 * g:8 * g + 8, :]
            return out

        def chunk(ci, carry):
            r0 = pl.multiple_of(ci * ch, ch)
            dyv = dy_ref[pl.ds(r0, ch), :]
            acc_ref[kk] += fold(dyv)
            dxv = jnp.zeros((ch, tc), F32)
            for k in range(kk):
                a, m = divmod(kk - 1 - k, 8)
                dxv = dxv + w_ref[k:k + 1, :] * shd_ref[m, pl.ds(r0 + 8 * a, ch), :]
                a, m = divmod(hb - (kk - 1) + k, 8)
                acc_ref[k] += fold(dyv * shx_ref[m, pl.ds(r0 + 8 * a, ch), :])
            dx_ref[pl.ds(r0, ch), :] = dxv
            return carry

        lax.fori_loop(0, tr // ch, chunk, 0)

        @pl.when(i == nt - 1)
        def _():
            for k in range(kk):
                dw_ref[k:k + 1, :] = _colsum(acc_ref[k])
            db_ref[...] = _colsum(acc_ref[kk])

    return _call(
        body, name=name, grid=(c // tc, nt),
        in_specs=[pl.BlockSpec((hb, tc), lambda j, i: (jnp.maximum(i * per - 1, 0), j)), pl.BlockSpec((tr, tc), lambda j, i: (i, j)),
                  pl.BlockSpec((tr, tc), lambda j, i: (i, j)),
                  pl.BlockSpec((hb, tc), lambda j, i: (jnp.minimum((i + 1) * per, s // hb - 1), j)),
                  pl.BlockSpec((kk, tc), lambda j, i: (0, j))],
        out_specs=[pl.BlockSpec((tr, tc), lambda j, i: (i, j)), pl.BlockSpec((kk, tc), lambda j, i: (0, j)),
                   pl.BlockSpec((1, tc), lambda j, i: (0, j))],
        out_shape=[_sds((s, c), F32), _sds((kk, c), F32), _sds((1, c), F32)],
        scratch=[pltpu.VMEM((tr + hb + 8, tc), F32), pltpu.VMEM((8, tr + hb, tc), F32), pltpu.VMEM((8, tr + hb, tc), F32),
                 pltpu.VMEM((kk + 1, 8, tc), F32)], rider=rider,
    )(x, x, dy, dy, w)


def _ln_silu_fwd(x, g, b, name):
    s, d = x.shape
    tr = _pick(s, ROW_TILE, 8)

    def body(x_ref, g_ref, b_ref, o_ref):
        xv = x_ref[...]
        mu = jnp.mean(xv, axis=-1, keepdims=True)
        xc = xv - mu
        ln = xc * lax.rsqrt(jnp.mean(xc * xc, axis=-1, keepdims=True) + EPS) * g_ref[...] + b_ref[...]
        o_ref[...] = _silu(ln).astype(BF16)

    return _call(body, name=name, grid=(s // tr,), in_specs=[_row_spec(tr, d), _vec_spec(d), _vec_spec(d)],
                 out_specs=_row_spec(tr, d), out_shape=_sds((s, d), BF16))(x, g, b)


def _ln_silu_bwd(dact, x, g, b, name):
    s, d = x.shape
    tr = _pick(s, ROW_TILE, 8)

    def body(da_ref, x_ref, g_ref, b_ref, dx_ref, dg_ref, db_ref):
        i = pl.program_id(0)
        xv = x_ref[...]
        mu = jnp.mean(xv, axis=-1, keepdims=True)
        xc = xv - mu
        rstd = lax.rsqrt(jnp.mean(xc * xc, axis=-1, keepdims=True) + EPS)
        xh = xc * rstd
        ln = xh * g_ref[...] + b_ref[...]
        dln = da_ref[...].astype(F32) * _dsilu(ln)
        _acc(dg_ref, _colsum(dln * xh), i)
        _acc(db_ref, _colsum(dln), i)
        dxh = dln * g_ref[...]
        dx_ref[...] = rstd * (dxh - jnp.mean(dxh, axis=-1, keepdims=True) - xh * jnp.mean(dxh * xh, axis=-1, keepdims=True))

    return _call(body, name=name, grid=(s // tr,), in_specs=[_row_spec(tr, d), _row_spec(tr, d), _vec_spec(d), _vec_spec(d)],
                 out_specs=[_row_spec(tr, d), _vec_spec(d), _vec_spec(d)],
                 out_shape=[_sds((s, d), F32), _sds((1, d), F32), _sds((1, d), F32)])(dact, x, g, b)


FFN_CHUNK = 16


def _fold8(v):
    out = v[0:8, :]
    for g in range(1, v.shape[0] // 8):
        out = out + v[8 * g:8 * g + 8, :]
    return out


def _ffn_act_fwd(u, w, b, name, rider=None):
    s, f2 = u.shape
    f = f2 // 2
    hb = 8
    tr, tc = _pick(s, ROW_TILE, 2 * hb), _pick(f, 1408)
    per, nf = tr // hb, f // tc
    ch = FFN_CHUNK

    def body(gp_ref, g_ref, v_ref, w_ref, b_ref, z_ref, cat_ref, sh_ref):
        i = pl.program_id(1)
        cat_ref[0:hb, :] = jnp.where(i > 0, gp_ref[...], 0.0)
        cat_ref[hb:hb + tr, :] = g_ref[...]
        for k in range(FFN_K - 1):
            sh_ref[k] = cat_ref[pl.ds(hb - (FFN_K - 1) + k, tr), :]

        def chunk(ci, carry):
            rows = pl.ds(pl.multiple_of(ci * ch, ch), ch)
            gc = b_ref[...] + w_ref[FFN_K - 1:FFN_K, :] * g_ref[rows, :]
            for k in range(FFN_K - 1):
                gc = gc + w_ref[k:k + 1, :] * sh_ref[k, rows, :]
            z_ref[rows, :] = (_silu(gc) * v_ref[rows, :]).astype(BF16)
            return carry

        lax.fori_loop(0, tr // ch, chunk, 0)

    return _call(
        body, name=name, grid=(nf, s // tr),
        in_specs=[pl.BlockSpec((hb, tc), lambda j, i: (jnp.maximum(i * per - 1, 0), j)), pl.BlockSpec((tr, tc), lambda j, i: (i, j)),
                  pl.BlockSpec((tr, tc), lambda j, i: (i, nf + j)), pl.BlockSpec((FFN_K, tc), lambda j, i: (0, j)),
                  pl.BlockSpec((1, tc), lambda j, i: (0, j))],
        out_specs=pl.BlockSpec((tr, tc), lambda j, i: (i, j)), out_shape=_sds((s, f), BF16),
        scratch=[pltpu.VMEM((tr + hb, tc), F32), pltpu.VMEM((FFN_K - 1, tr, tc), F32)], rider=rider,
    )(u, u, u, w, b)


def _ffn_act_bwd(dz, u, w, b, name, rider=None):
    s, f2 = u.shape
    f = f2 // 2
    hb = 8
    tr, tc = _pick(s, ROW_TILE, 2 * hb), _pick(f, 1408)
    per, nf, nt = tr // hb, f // tc, s // tr
    ext = tr + hb
    ch = FFN_CHUNK
    last_tap = FFN_K - 1

    def body(gp_ref, g_ref, gn_ref, v_ref, vn_ref, dz_ref, dzn_ref, w_ref, b_ref, du_ref, dw_ref, db_ref, cat_ref, sh_ref, dgc_ref,
             sd_ref, acc_ref):
        i = pl.program_id(1)
        last = i == nt - 1
        cat_ref[0:hb, :] = jnp.where(i > 0, gp_ref[...], 0.0)
        cat_ref[hb:hb + tr, :] = g_ref[...]
        cat_ref[hb + tr:hb + tr + hb, :] = gn_ref[...]
        for k in range(last_tap):
            sh_ref[k] = cat_ref[pl.ds(hb - last_tap + k, ext), :]

        def preact(rows, g_rows):
            gc = b_ref[...] + w_ref[last_tap:FFN_K, :] * g_rows
            for k in range(last_tap):
                gc = gc + w_ref[k:k + 1, :] * sh_ref[k, rows, :]
            return gc

        def chunk1(ci, carry):
            rows = pl.ds(pl.multiple_of(ci * ch, ch), ch)
            gc = preact(rows, g_ref[rows, :])
            sg = jax.nn.sigmoid(gc)
            dzv = dz_ref[rows, :]
            du_ref[1, rows, :] = (dzv * (gc * sg)).astype(BF16)
            dgc_ref[rows, :] = dzv * v_ref[rows, :] * (sg * (1.0 + gc * (1.0 - sg)))
            return carry

        lax.fori_loop(0, tr // ch, chunk1, 0)
        gcn = preact(pl.ds(tr, hb), gn_ref[...])
        dgc_ref[tr:ext, :] = jnp.where(last, 0.0, dzn_ref[...] * vn_ref[...] * _dsilu(gcn))
        for j in range(last_tap):
            sd_ref[j] = dgc_ref[pl.ds(j + 1, tr), :]

        @pl.when(i == 0)
        def _():
            acc_ref[...] = jnp.zeros_like(acc_ref)

        def chunk2(ci, carry):
            rows = pl.ds(pl.multiple_of(ci * ch, ch), ch)
            d0 = dgc_ref[rows, :]
            dgt = w_ref[last_tap:FFN_K, :] * d0
            for k in range(last_tap):
                dgt = dgt + w_ref[k:k + 1, :] * sd_ref[last_tap - 1 - k, rows, :]
            du_ref[0, rows, :] = dgt.astype(BF16)
            acc_ref[FFN_K] += _fold8(d0)
            acc_ref[last_tap] += _fold8(d0 * g_ref[rows, :])
            for k in range(last_tap):
                acc_ref[k] += _fold8(d0 * sh_ref[k, rows, :])
            return carry

        lax.fori_loop(0, tr // ch, chunk2, 0)

        @pl.when(last)
        def _():
            for k in range(FFN_K):
                dw_ref[k:k + 1, :] = _colsum(acc_ref[k])
            db_ref[...] = _colsum(acc_ref[FFN_K])

    prev_map = lambda j, i: (jnp.maximum(i * per - 1, 0), j)
    next_map = lambda j, i: (jnp.minimum((i + 1) * per, s // hb - 1), j)
    next_map_v = lambda j, i: (jnp.minimum((i + 1) * per, s // hb - 1), nf + j)
    return _call(
        body, name=name, grid=(nf, nt),
        in_specs=[pl.BlockSpec((hb, tc), prev_map), pl.BlockSpec((tr, tc), lambda j, i: (i, j)), pl.BlockSpec((hb, tc), next_map),
                  pl.BlockSpec((tr, tc), lambda j, i: (i, nf + j)), pl.BlockSpec((hb, tc), next_map_v),
                  pl.BlockSpec((tr, tc), lambda j, i: (i, j)), pl.BlockSpec((hb, tc), next_map),
                  pl.BlockSpec((FFN_K, tc), lambda j, i: (0, j)), pl.BlockSpec((1, tc), lambda j, i: (0, j))],
        out_specs=[pl.BlockSpec((2, tr, tc), lambda j, i: (0, i, j)), pl.BlockSpec((FFN_K, tc), lambda j, i: (0, j)),
                   pl.BlockSpec((1, tc), lambda j, i: (0, j))],
        out_shape=[_sds((2, s, f), BF16), _sds((FFN_K, f), F32), _sds((1, f), F32)],
        scratch=[pltpu.VMEM((tr + 2 * hb, tc), F32), pltpu.VMEM((last_tap, ext, tc), F32), pltpu.VMEM((ext, tc), F32),
                 pltpu.VMEM((last_tap, tr, tc), F32), pltpu.VMEM((FFN_K + 1, 8, tc), F32)], rider=rider,
    )(u, u, u, u, u, dz, dz, w, b)


def _rope_tables(pos, freq, sign, name):
    s = pos.shape[0]
    tr = _pick(s, 512, 8)

    def body(p_ref, f_ref, s_ref, c_ref, sn_ref):
        ang = p_ref[...].astype(F32) * f_ref[...]
        c_ref[...] = jnp.cos(ang)
        sn_ref[...] = jnp.sin(ang) * s_ref[...]

    return _call(body, name=name, grid=(s // tr,), in_specs=[pl.BlockSpec((tr, 1), lambda i: (i, 0)), _vec_spec(128), _vec_spec(128)],
                 out_specs=[_row_spec(tr, 128)] * 2, out_shape=[_sds((s, 128), F32)] * 2)(pos, freq, sign)


def _partner(v):
    lane = lax.broadcasted_iota(jnp.int32, v.shape, 1)
    lower = pltpu.roll(v, HEAD_DIM - ROT_DIM // 2, 1)
    upper = jnp.where(lane < ROT_DIM, pltpu.roll(v, ROT_DIM // 2, 1), 0.0)
    return jnp.where(lane < ROT_DIM // 2, lower, upper)


def _head_sum(v):
    hi = v.astype(BF16)
    lo = (v - hi.astype(F32)).astype(BF16)
    ones = jnp.ones((HEAD_DIM, HEAD_DIM), BF16)
    return lax.dot_general(hi, ones, _DIMS["nn"], preferred_element_type=F32) + lax.dot_general(
        lo, ones, _DIMS["nn"], preferred_element_type=F32)


def _qk_norm_rope_fwd(q, kv, gq, gk, ctab, stab, name, rider=None):
    s, w = q.shape
    tr = _pick(s, 128, 8)
    heads = w // HEAD_DIM

    def body(q_ref, k_ref, gq_ref, gk_ref, c_ref, s_ref, qn_ref, kn_ref):
        cv, sv = c_ref[...], s_ref[...]
        for src, g_ref, dst in ((q_ref, gq_ref, qn_ref), (k_ref, gk_ref, kn_ref)):
            gv = g_ref[...]
            for h in range(heads):
                cols = pl.ds(h * HEAD_DIM, HEAD_DIM)
                xv = src[:, cols]
                nv = xv * lax.rsqrt(_head_sum(xv * xv) * (1.0 / HEAD_DIM) + EPS) * gv
                dst[:, cols] = nv * cv + _partner(nv) * sv

    return _call(
        body, name=name, grid=(s // tr,),
        in_specs=[_row_spec(tr, w), _row_spec(tr, w), _vec_spec(128), _vec_spec(128), _row_spec(tr, 128), _row_spec(tr, 128)],
        out_specs=[_row_spec(tr, w)] * 2, out_shape=[_sds((s, w), F32)] * 2, rider=rider,
    )(q, kv, gq, gk, ctab, stab)


def _qk_norm_rope_bwd(dqs, dks, dvs, q, kv, gq, gk, ctab, stab, name, rider=None):
    s, w = q.shape
    gw = w // N_GROUPS
    tr = _pick(s, 128, 8)
    hpg = gw // HEAD_DIM

    def body(*refs):
        dq_refs, dk_refs, dv_refs = refs[0:3], refs[3:6], refs[6:9]
        q_ref, k_ref, gq_ref, gk_ref, c_ref, s_ref, dq_ref, dkv_ref, dgq_ref, dgk_ref = refs[9:]
        i = pl.program_id(0)
        cv, sv = c_ref[...], s_ref[...]
        for d_refs, src, g_ref, dst, dg_ref in ((dq_refs, q_ref, gq_ref, dq_ref, dgq_ref), (dk_refs, k_ref, gk_ref, dkv_ref, dgk_ref)):
            gv = g_ref[...]
            dg = jnp.zeros((1, HEAD_DIM), F32)
            for g in range(N_GROUPS):
                for h in range(hpg):
                    cols = pl.ds(g * gw + h * HEAD_DIM, HEAD_DIM)
                    dout = d_refs[g][:, pl.ds(h * HEAD_DIM, HEAD_DIM)]
                    dn = dout * cv + _partner(dout * sv)
                    xv = src[:, cols]
                    rstd = lax.rsqrt(_head_sum(xv * xv) * (1.0 / HEAD_DIM) + EPS)
                    xh = xv * rstd
                    dg = dg + _colsum(dn * xh)
                    dxh = dn * gv
                    dst[:, cols] = (rstd * (dxh - xh * (_head_sum(dxh * xh) * (1.0 / HEAD_DIM)))).astype(BF16)
            _acc(dg_ref, dg, i)
        for g in range(N_GROUPS):
            dkv_ref[:, pl.ds(w + g * gw, gw)] = dv_refs[g][...].astype(BF16)

    return _call(
        body, name=name, grid=(s // tr,),
        in_specs=[_row_spec(tr, gw)] * 9 + [_row_spec(tr, w), _row_spec(tr, w), _vec_spec(128), _vec_spec(128),
                                            _row_spec(tr, 128), _row_spec(tr, 128)],
        out_specs=[_row_spec(tr, w), _row_spec(tr, 2 * w), _vec_spec(128), _vec_spec(128)],
        out_shape=[_sds((s, w), BF16), _sds((s, 2 * w), BF16), _sds((1, 128), F32), _sds((1, 128), F32)], rider=rider,
    )(*dqs, *dks, *dvs, q, kv, gq, gk, ctab, stab)


def _rows(j, b, r):
    start = j + r * BLK * b
    return pl.ds(start, BLK, stride=r) if r > 1 else pl.ds(start, BLK)


def _dot_nt(a, b):
    return lax.dot_general(a, b, _DIMS["nt"], preferred_element_type=F32)


def _dot_nn(a, b):
    return lax.dot_general(a, b, _DIMS["nn"], preferred_element_type=F32)


def _band_masks():
    qi = lax.broadcasted_iota(jnp.int32, (BLK, BLK), 0)
    kj = lax.broadcasted_iota(jnp.int32, (BLK, BLK), 1)
    return kj <= qi, kj >= qi


def _attn_fwd(qn, kn, kv, g, name):
    s, w = qn.shape
    r = DILATIONS[g]
    gw = w // N_GROUPS
    cr = min(ATTN_CHUNK, s)
    nb = cr // (BLK * r)
    hp = 1
    cw = hp * HEAD_DIM
    gc = gw // cw
    scale = 1.0 / math.sqrt(HEAD_DIM)

    def body(q_ref, kc_ref, kp_ref, vc_ref, vp_ref, o_ref, l_ref):
        n = pl.program_id(1)
        same_m, prev_m = _band_masks()
        prev_first = jnp.logical_and(prev_m, n > 0)
        for h in range(hp):
            cols = pl.ds(h * HEAD_DIM, HEAD_DIM)
            for j in range(r):
                for b in range(nb):
                    rows = _rows(j, b, r)
                    qv = q_ref[rows, cols].astype(BF16)
                    kc, vc = kc_ref[rows, cols].astype(BF16), vc_ref[rows, cols].astype(BF16)
                    if b > 0:
                        rp = _rows(j, b - 1, r)
                        kp, vp, pm = kc_ref[rp, cols].astype(BF16), vc_ref[rp, cols].astype(BF16), prev_m
                    else:
                        rp = _rows(j, nb - 1, r)
                        kp, vp, pm = kp_ref[rp, cols].astype(BF16), vp_ref[rp, cols].astype(BF16), prev_first
                    sd = jnp.where(same_m, _dot_nt(qv, kc) * scale, NEG)
                    so = jnp.where(pm, _dot_nt(qv, kp) * scale, NEG)
                    m = jnp.maximum(jnp.max(sd, axis=-1, keepdims=True), jnp.max(so, axis=-1, keepdims=True))
                    pd, po = jnp.exp(sd - m), jnp.exp(so - m)
                    den = jnp.sum(pd, axis=-1, keepdims=True) + jnp.sum(po, axis=-1, keepdims=True)
                    ov = (_dot_nn(pd.astype(BF16), vc) + _dot_nn(po.astype(BF16), vp)) / den
                    o_ref[rows, cols] = ov
                    l_ref[rows, cols] = jnp.broadcast_to(m + jnp.log(den), (BLK, HEAD_DIM))

    cur = lambda base: (lambda c, n: (n, base + c))
    prv = lambda base: (lambda c, n: (jnp.maximum(n - 1, 0), base + c))
    qb, kb, vb = g * gc, g * gc, (N_GROUPS + g) * gc
    return _call(
        body, name=name, grid=(gc, s // cr),
        in_specs=[pl.BlockSpec((cr, cw), cur(qb)), pl.BlockSpec((cr, cw), cur(kb)), pl.BlockSpec((cr, cw), prv(kb)),
                  pl.BlockSpec((cr, cw), cur(vb)), pl.BlockSpec((cr, cw), prv(vb))],
        out_specs=[pl.BlockSpec((cr, cw), lambda c, n: (n, c))] * 2, out_shape=[_sds((s, gw), F32)] * 2,
    )(qn, kn, kn, kv, kv)


def _attn_combine(os_, lses, name):
    s, gw = os_[0].shape
    tr = _pick(s, ROW_TILE, 8)

    def body(o0, o1, o2, l0, l1, l2, o_ref, l_ref):
        a, b, c = l0[...], l1[...], l2[...]
        m = jnp.maximum(jnp.maximum(a, b), c)
        ea, eb, ec = jnp.exp(a - m), jnp.exp(b - m), jnp.exp(c - m)
        den = ea + eb + ec
        o_ref[...] = ((ea * o0[...] + eb * o1[...] + ec * o2[...]) / den).astype(BF16)
        l_ref[...] = m + jnp.log(den)

    return _call(body, name=name, grid=(s // tr,), in_specs=[_row_spec(tr, gw)] * 6, out_specs=[_row_spec(tr, gw)] * 2,
                 out_shape=[_sds((s, gw), BF16), _sds((s, gw), F32)])(*os_, *lses)


def _attn_delta(do, o, name):
    s, gw = do.shape
    tr = _pick(s, ROW_TILE, 8)

    def body(do_ref, o_ref, d_ref):
        for h in range(gw // HEAD_DIM):
            cols = pl.ds(h * HEAD_DIM, HEAD_DIM)
            t = jnp.sum(do_ref[:, cols] * o_ref[:, cols].astype(F32), axis=-1, keepdims=True)
            d_ref[:, cols] = jnp.broadcast_to(t, (tr, HEAD_DIM))

    return _call(body, name=name, grid=(s // tr,), in_specs=[_row_spec(tr, gw)] * 2, out_specs=_row_spec(tr, gw),
                 out_shape=_sds((s, gw), F32))(do, o)


def _pair_grads(qv, kv_, vv, dov, lse, delta, mask, scale):
    sc = jnp.where(mask, _dot_nt(qv, kv_) * scale, NEG)
    p = jnp.exp(sc - lse)
    ds = p * (_dot_nt(dov, vv) - delta) * scale
    return p, ds


def _attn_bwd_dq(qn, kn, kv, do, lse, delta, g, name, rider=None):
    s, w = qn.shape
    r = DILATIONS[g]
    gw = w // N_GROUPS
    cr = min(ATTN_CHUNK, s)
    nb = cr // (BLK * r)
    cw = HEAD_DIM
    gc = gw // cw
    scale = 1.0 / math.sqrt(HEAD_DIM)

    def body(q_ref, kc_ref, kp_ref, vc_ref, vp_ref, do_ref, l_ref, d_ref, dq_ref):
        n = pl.program_id(1)
        same_m, prev_m = _band_masks()
        prev_first = jnp.logical_and(prev_m, n > 0)
        for j in range(r):
            for b in range(nb):
                rows = _rows(j, b, r)
                qv, dov = q_ref[rows, :].astype(BF16), do_ref[rows, :].astype(BF16)
                lse, delta = l_ref[rows, :], d_ref[rows, :]
                kc, vc = kc_ref[rows, :].astype(BF16), vc_ref[rows, :].astype(BF16)
                if b > 0:
                    rp = _rows(j, b - 1, r)
                    kp, vp, pm = kc_ref[rp, :].astype(BF16), vc_ref[rp, :].astype(BF16), prev_m
                else:
                    rp = _rows(j, nb - 1, r)
                    kp, vp, pm = kp_ref[rp, :].astype(BF16), vp_ref[rp, :].astype(BF16), prev_first
                _, dsd = _pair_grads(qv, kc, vc, dov, lse, delta, same_m, scale)
                _, dso = _pair_grads(qv, kp, vp, dov, lse, delta, pm, scale)
                dq_ref[rows, :] = _dot_nn(dsd.astype(BF16), kc) + _dot_nn(dso.astype(BF16), kp)

    cur = lambda base: (lambda c, n: (n, base + c))
    prv = lambda base: (lambda c, n: (jnp.maximum(n - 1, 0), base + c))
    qb, vb = g * gc, (N_GROUPS + g) * gc
    own = pl.BlockSpec((cr, cw), lambda c, n: (n, c))
    return _call(
        body, name=name, grid=(gc, s // cr),
        in_specs=[pl.BlockSpec((cr, cw), cur(qb)), pl.BlockSpec((cr, cw), cur(qb)), pl.BlockSpec((cr, cw), prv(qb)),
                  pl.BlockSpec((cr, cw), cur(vb)), pl.BlockSpec((cr, cw), prv(vb)), own, own, own],
        out_specs=own, out_shape=_sds((s, gw), F32), rider=rider,
    )(qn, kn, kn, kv, kv, do, lse, delta)


def _attn_bwd_dkv(qn, kn, kv, do, lse, delta, g, name):
    s, w = qn.shape
    r = DILATIONS[g]
    gw = w // N_GROUPS
    cr = min(ATTN_CHUNK, s)
    nb = cr // (BLK * r)
    nchunk = s // cr
    cw = HEAD_DIM
    gc = gw // cw
    scale = 1.0 / math.sqrt(HEAD_DIM)

    def body(k_ref, v_ref, qc_ref, qx_ref, doc_ref, dox_ref, lc_ref, lx_ref, dc_ref, dx_ref, dk_ref, dv_ref):
        n = pl.program_id(1)
        same_m, prev_m = _band_masks()
        next_last = jnp.logical_and(prev_m, n < nchunk - 1)
        for j in range(r):
            for b in range(nb):
                rows = _rows(j, b, r)
                kv_, vv = k_ref[rows, :].astype(BF16), v_ref[rows, :].astype(BF16)
                qv, dov = qc_ref[rows, :].astype(BF16), doc_ref[rows, :].astype(BF16)
                pd, dsd = _pair_grads(qv, kv_, vv, dov, lc_ref[rows, :], dc_ref[rows, :], same_m, scale)
                if b < nb - 1:
                    rx = _rows(j, b + 1, r)
                    qx, dox, lx, dlx, xm = qc_ref[rx, :], doc_ref[rx, :], lc_ref[rx, :], dc_ref[rx, :], prev_m
                else:
                    rx = _rows(j, 0, r)
                    qx, dox, lx, dlx, xm = qx_ref[rx, :], dox_ref[rx, :], lx_ref[rx, :], dx_ref[rx, :], next_last
                qx, dox = qx.astype(BF16), dox.astype(BF16)
                po, dso = _pair_grads(qx, kv_, vv, dox, lx, dlx, xm, scale)
                dk_ref[rows, :] = _dot_nn(dsd.T.astype(BF16), qv) + _dot_nn(dso.T.astype(BF16), qx)
                dv_ref[rows, :] = _dot_nn(pd.T.astype(BF16), dov) + _dot_nn(po.T.astype(BF16), dox)

    cur = lambda base: (lambda c, n: (n, base + c))
    nxt = lambda base: (lambda c, n: (jnp.minimum(n + 1, nchunk - 1), base + c))
    qb, vb = g * gc, (N_GROUPS + g) * gc
    blk = lambda f: pl.BlockSpec((cr, cw), f)
    return _call(
        body, name=name, grid=(gc, nchunk),
        in_specs=[blk(cur(qb)), blk(cur(vb)), blk(cur(qb)), blk(nxt(qb)), blk(cur(0)), blk(nxt(0)), blk(cur(0)), blk(nxt(0)),
                  blk(cur(0)), blk(nxt(0))],
        out_specs=[blk(cur(0))] * 2, out_shape=[_sds((s, gw), F32)] * 2,
    )(kn, kv, qn, qn, do, do, lse, lse, delta, delta)


def _mod_proj(sc_all, w, name):
    l, d, ns = w.shape
    tn = _pick(ns, 512)

    def body(c_ref, w_ref, o_ref):
        o_ref[...] = jnp.dot(c_ref[...].astype(BF16), w_ref[...].astype(BF16), preferred_element_type=F32)

    return _call(
        body, name=name, grid=(l, ns // tn),
        in_specs=[pl.BlockSpec((N_DEV, d), lambda a, j: (0, 0)), pl.BlockSpec((None, d, tn), lambda a, j: (a, 0, j))],
        out_specs=pl.BlockSpec((None, N_DEV, tn), lambda a, j: (a, 0, j)), out_shape=_sds((l, N_DEV, ns), F32),
    )(sc_all, w)


def _adamw_math(w, g, m, v):
    m = ADAM_B1 * m + (1.0 - ADAM_B1) * g
    v = ADAM_B2 * v + (1.0 - ADAM_B2) * (g * g)
    m_hat = m / (1.0 - ADAM_B1 ** ADAM_STEP)
    v_hat = v / (1.0 - ADAM_B2 ** ADAM_STEP)
    delta = -ADAM_LR * (m_hat / (jnp.sqrt(v_hat) + ADAM_EPS) + ADAM_WD * w)
    return delta, m, v


def _adamw_big(w, g, m, v, name):
    shape = w.shape
    cols = shape[-1]
    rows = math.prod(shape[:-1])
    tr, tc = _wide_tile(rows, cols)
    w2, g2, m2, v2 = (t.reshape(rows, cols) for t in (w, g, m, v))

    def body(w_ref, g_ref, m_ref, v_ref, d_ref, mo_ref, vo_ref):
        d_ref[...], mo_ref[...], vo_ref[...] = _adamw_math(w_ref[...], g_ref[...], m_ref[...], v_ref[...])

    spec = pl.BlockSpec((tr, tc), lambda i, j: (i, j))
    outs = _call(body, name=name, grid=(rows // tr, cols // tc), in_specs=[spec] * 4, out_specs=[spec] * 3,
                 out_shape=[_sds((rows, cols), F32)] * 3)(w2, g2, m2, v2)
    return [t.reshape(shape) for t in outs]


def _adamw_mod(w, sct, dm, m, v, name):
    l, d, ns = w.shape
    tr, tc = _pick(d, 512, 8), _pick(ns, 1024)

    def body(w_ref, c_ref, dm_ref, m_ref, v_ref, g_ref, d_ref, mo_ref, vo_ref):
        cv, dv = c_ref[...].astype(BF16).astype(F32), dm_ref[...].astype(BF16).astype(F32)
        g = jnp.zeros((tr, tc), F32)
        for e in range(N_DEV):
            g = g + cv[:, e:e + 1] * dv[e:e + 1, :]
        g_ref[...] = g
        d_ref[...], mo_ref[...], vo_ref[...] = _adamw_math(w_ref[...], g, m_ref[...], v_ref[...])

    spec = pl.BlockSpec((None, tr, tc), lambda a, i, j: (a, i, j))
    return _call(
        body, name=name, grid=(l, d // tr, ns // tc),
        in_specs=[spec, pl.BlockSpec((tr, N_DEV), lambda a, i, j: (i, 0)), pl.BlockSpec((None, N_DEV, tc), lambda a, i, j: (a, 0, j)),
                  spec, spec],
        out_specs=[spec] * 4, out_shape=[_sds((l, d, ns), F32)] * 4,
    )(w, sct, dm, m, v)


def _adamw_small(ws, gs, ms, vs, name):
    n = len(ws)

    def body(*refs):
        w_r, g_r, m_r, v_r = refs[0:n], refs[n:2 * n], refs[2 * n:3 * n], refs[3 * n:4 * n]
        d_o, m_o, v_o = refs[4 * n:5 * n], refs[5 * n:6 * n], refs[6 * n:7 * n]
        for k in range(n):
            d_o[k][...], m_o[k][...], v_o[k][...] = _adamw_math(w_r[k][...], g_r[k][...], m_r[k][...], v_r[k][...])

    vm = pl.BlockSpec(memory_space=pltpu.VMEM)
    shapes = [_sds(w.shape, F32) for w in ws]
    outs = pl.pallas_call(body, name=name, in_specs=[vm] * (4 * n), out_specs=[vm] * (3 * n), out_shape=shapes * 3,
                          interpret=False)(*ws, *gs, *ms, *vs)
    return outs[0:n], outs[n:2 * n], outs[2 * n:3 * n]


def _sum_rows(a, name):
    n, v = a.shape
    tc = _pick(v, 8192)

    def body(a_ref, o_ref):
        o_ref[...] = jnp.sum(a_ref[...], axis=0, keepdims=True)

    return _call(body, name=name, grid=(v // tc,), in_specs=[pl.BlockSpec((n, tc), lambda j: (0, j))],
                 out_specs=pl.BlockSpec((1, tc), lambda j: (0, j)), out_shape=_sds((1, v), F32))(a)


def _cast_into_full(w2d, kind, slot_arr, name):
    rows, cols = w2d.shape
    tr, tc = _wide_tile(rows, cols)
    nr, nc = rows // tr, cols // tc

    def body(s_ref, w_ref, o_ref):
        o_ref[...] = w_ref[...].astype(BF16)

    if kind == "col":
        o_map = lambda i, j, s_ref: (i, s_ref[0] * nc + j)
    else:
        o_map = lambda i, j, s_ref: (s_ref[0] * nr + i, j)
    return _call(body, name=name, grid=(nr, nc), in_specs=[pl.BlockSpec((tr, tc), lambda i, j, s_ref: (i, j))],
                 out_specs=pl.BlockSpec((tr, tc), o_map), out_shape=_sds(_full_shape(kind, (rows, cols)), BF16), nsp=1)(slot_arr, w2d)


def _place():
    x, y, c = lax.axis_index("x"), lax.axis_index("y"), lax.axis_index("c")
    chips = [(1 - x, y), (x, 1 - y), (1 - x, 1 - y)]
    return x, y, c, chips


def _remote(src, dst, send_sem, recv_sem, to):
    return pltpu.make_async_remote_copy(src_ref=src, dst_ref=dst, send_sem=send_sem, recv_sem=recv_sem, device_id=to,
                                        device_id_type=MESH)


def _allgather8(a, name):
    m_per, n = a.shape

    def body(x_ref, out_ref, send_sems, recv_sems, local_sem):
        x, y, c, chips = _place()
        me, sibling = (x, y, c), (x, y, 1 - c)

        def rows(px, py, pc):
            return out_ref.at[pl.ds((4 * px + 2 * py + pc) * m_per, m_per), :]

        def copy(k, block, to, src=None):
            return _remote(rows(*block) if src is None else src, rows(*block), send_sems.at[k], recv_sems.at[k], to)

        mine = pltpu.make_async_copy(x_ref, rows(*me), local_sem)
        mine.start()
        first = [copy(0, me, sibling, src=x_ref)]
        first += [copy(1 + j, me, (*chip, c), src=x_ref) for j, chip in enumerate(chips)]
        for cp in first:
            cp.start()
        passed = [copy(4 + j, (*chip, c), sibling) for j, chip in enumerate(chips)]
        for j, chip in enumerate(chips):
            copy(1 + j, (*chip, c), me).wait_recv()
            passed[j].start()
        copy(0, sibling, me).wait_recv()
        for j, chip in enumerate(chips):
            copy(4 + j, (*chip, 1 - c), me).wait_recv()
        for cp in first + passed:
            cp.wait_send()
        mine.wait()

    return pl.pallas_call(
        body, name=name, out_shape=_sds((N_DEV * m_per, n), a.dtype),
        in_specs=[pl.BlockSpec(memory_space=pltpu.VMEM)], out_specs=pl.BlockSpec(memory_space=pltpu.VMEM),
        scratch_shapes=[pltpu.SemaphoreType.DMA((7,)), pltpu.SemaphoreType.DMA((7,)), pltpu.SemaphoreType.DMA],
        interpret=False,
    )(a)


def _region(ref, kind, shard_shape, slot, half, piece=(0, 1)):
    r, cs = shard_shape
    hr = r // 2
    n = hr // piece[1]
    start = half * hr + piece[0] * n
    if kind == "col":
        return ref.at[pl.ds(start, n), pl.ds(slot * cs, cs)]
    return ref.at[pl.ds(slot * r + start, n), :]


def _full_shape(kind, shard_shape):
    r, cs = shard_shape
    return (r, 4 * cs) if kind == "col" else (4 * r, cs)


_HBM = pl.BlockSpec(memory_space=pltpu.HBM)


def _in_place(arrays, n_sem, copies):
    return _Rider(arrays, [_sds(t.shape, t.dtype) for t in arrays], {a: a for a in range(len(arrays))}, n_sem, copies)


def _gather_chips(fulls, kinds, shapes, piece=(0, 1)):
    n = len(fulls)

    def copies(ins, outs, send_sems, recv_sems):
        x, y, c, chips = _place()
        me_slot = 2 * x + y
        sends, recvs = [], []
        for a in range(n):
            for j, (cx, cy) in enumerate(chips):
                k = 3 * a + j
                sends.append(_remote(_region(ins[a], kinds[a], shapes[a], me_slot, c, piece),
                                     _region(outs[a], kinds[a], shapes[a], me_slot, c, piece),
                                     send_sems.at[k], recv_sems.at[k], (cx, cy, c)))
                land = _region(outs[a], kinds[a], shapes[a], 2 * cx + cy, c, piece)
                recvs.append(_remote(land, land, send_sems.at[k], recv_sems.at[k], (cx, cy, c)))
        return sends, recvs

    return _in_place(fulls, 3 * n, copies)


def _gather_pair(fulls, kinds, shapes):
    n = len(fulls)

    def copies(ins, outs, send_sems, recv_sems):
        x, y, c, chips = _place()
        sibling = (x, y, 1 - c)
        sends, recvs = [], []
        for a in range(n):
            for j, (cx, cy) in enumerate(chips):
                k = 3 * a + j
                sends.append(_remote(_region(ins[a], kinds[a], shapes[a], 2 * cx + cy, c), _region(outs[a], kinds[a], shapes[a], 2 * cx + cy, c),
                                     send_sems.at[k], recv_sems.at[k], sibling))
                land = _region(outs[a], kinds[a], shapes[a], 2 * cx + cy, 1 - c)
                recvs.append(_remote(land, land, send_sems.at[k], recv_sems.at[k], sibling))
        return sends, recvs

    return _in_place(fulls, 3 * n, copies)


def _rs_pair(grads, kinds, shapes):
    n = len(grads)

    def copies(ins, outs, send_sems, recv_sems):
        x, y, c, _ = _place()
        sibling = (x, y, 1 - c)
        sends, recvs = [], []
        for a in range(n):
            for slot in range(4):
                k = 4 * a + slot
                sends.append(_remote(_region(ins[a], kinds[a], shapes[a], slot, 1 - c), outs[a].at[slot], send_sems.at[k],
                                     recv_sems.at[k], sibling))
                recvs.append(_remote(_region(ins[a], kinds[a], shapes[a], slot, c), outs[a].at[slot], send_sems.at[k],
                                     recv_sems.at[k], sibling))
        return sends, recvs

    return _Rider(grads, [_sds((4, s[0] // 2, s[1]), F32) for s in shapes], {}, 4 * n, copies)


def _rs_pair_add(grad, recv, kind, shape, c_arr, name):
    r, cs = shape
    hr = r // 2
    tr, tc = _wide_tile(hr, cs)
    nr, nc = hr // tr, cs // tc

    def body(c_ref, g_ref, r_ref, o_ref):
        o_ref[...] = (g_ref[...] + r_ref[...]).astype(BF16)

    if kind == "col":
        g_map = lambda s_, i, j, c_ref: (c_ref[0] * nr + i, s_ * nc + j)
    else:
        g_map = lambda s_, i, j, c_ref: (s_ * 2 * nr + c_ref[0] * nr + i, j)
    own = pl.BlockSpec((None, tr, tc), lambda s_, i, j, c_ref: (s_, i, j))
    return _call(body, name=name, grid=(4, nr, nc), in_specs=[pl.BlockSpec((tr, tc), g_map), own], out_specs=own,
                 out_shape=_sds((4, hr, cs), BF16), nsp=1)(c_arr, grad, recv)


def _rs_chips(parts, piece=(0, 1), dest=None):
    n = len(parts)

    def copies(ins, outs, send_sems, recv_sems):
        x, y, c, chips = _place()
        sends, recvs = [], []
        for a in range(n):
            rows = parts[a].shape[1] // piece[1]
            sel = pl.ds(piece[0] * rows, rows)
            for j, (cx, cy) in enumerate(chips):
                k = 3 * a + j
                cp = _remote(ins[a].at[2 * cx + cy, sel], outs[a].at[j, sel], send_sems.at[k], recv_sems.at[k], (cx, cy, c))
                sends.append(cp)
                recvs.append(cp)
        return sends, recvs

    out_shapes = [_sds((3,) + p.shape[1:], BF16) for p in parts]
    if dest is None:
        return _Rider(parts, out_shapes, {}, 3 * n, copies)
    return _Rider(list(parts) + list(dest), out_shapes, {n + a: a for a in range(n)}, 3 * n, copies)


def _rs_chip_add(part, recv, slot_arr, c_arr, dest, layer, out_shape, name):
    _, hr, cs = part.shape
    tr, tc = _wide_tile(hr, cs)
    nr = hr // tr

    def body(s_ref, c_ref, p_ref, r0_ref, r1_ref, r2_ref, *rest):
        o_ref = rest[-1]
        o_ref[...] = ((p_ref[...].astype(F32) + r0_ref[...].astype(F32)) + r1_ref[...].astype(F32)) + r2_ref[...].astype(F32)

    rk = lambda k: pl.BlockSpec((None, tr, tc), lambda i, j, s_ref, c_ref: (k, i, j))
    if layer is None:
        o_spec = pl.BlockSpec((tr, tc), lambda i, j, s_ref, c_ref: (c_ref[0] * nr + i, j))
    else:
        o_spec = pl.BlockSpec((None, tr, tc), lambda i, j, s_ref, c_ref: (layer, c_ref[0] * nr + i, j))
    in_specs = [pl.BlockSpec((None, tr, tc), lambda i, j, s_ref, c_ref: (s_ref[0], i, j)), rk(0), rk(1), rk(2)]
    args = [slot_arr, c_arr, part, recv, recv, recv]
    aliases = {}
    if dest is not None:
        in_specs.append(pl.BlockSpec(memory_space=pl.ANY))
        args.append(dest)
        aliases = {6: 0}
    return pl.pallas_call(
        body, name=name,
        grid_spec=pltpu.PrefetchScalarGridSpec(num_scalar_prefetch=2, grid=(nr, cs // tc), in_specs=in_specs, out_specs=o_spec),
        out_shape=_sds(out_shape, F32), input_output_aliases=aliases,
        compiler_params=pltpu.CompilerParams(dimension_semantics=("arbitrary",) * 2, vmem_limit_bytes=VMEM_LIMIT_MB << 20),
        interpret=False,
    )(*args)


def _rs_pair_share(shards, name):
    n = len(shards)
    views = []
    for a, t in enumerate(shards):
        views += [(a, None)] if t.ndim == 2 else [(a, l) for l in range(t.shape[0])]
    nv = len(views)

    def body(*refs):
        ins, outs = refs[0:n], refs[n:2 * n]
        send_sems, recv_sems = refs[2 * n:]
        x, y, c, _ = _place()
        sibling = (x, y, 1 - c)

        def rows(ref_list, k, half):
            a, layer = views[k]
            ref = ref_list[a] if layer is None else ref_list[a].at[layer]
            hr = ref.shape[0] // 2
            return ref.at[pl.ds(half * hr, hr), :]

        sent = []
        for k in range(nv):
            cp = _remote(rows(ins, k, c), rows(outs, k, c), send_sems.at[k], recv_sems.at[k], sibling)
            cp.start()
            sent.append(cp)
        for k in range(nv):
            _remote(rows(ins, k, 1 - c), rows(outs, k, 1 - c), send_sems.at[k], recv_sems.at[k], sibling).wait_recv()
        for cp in sent:
            cp.wait_send()

    return pl.pallas_call(
        body, name=name, out_shape=[_sds(t.shape, F32) for t in shards], in_specs=[_HBM] * n, out_specs=[_HBM] * n,
        input_output_aliases={a: a for a in range(n)},
        scratch_shapes=[pltpu.SemaphoreType.DMA((nv,)), pltpu.SemaphoreType.DMA((nv,))], interpret=False,
    )(*shards)


def _pad_to(v, mult):
    n = v.shape[0]
    return jnp.pad(v, (0, (-n) % mult))


def kernel(x, c, positions, mod_w, mod_b, norm_mix_g, norm_ffn_g, conv_pw1_w, conv_pw1_b, conv_dw_w, conv_dw_b, conv_ln_g, conv_ln_b, conv_pw2_w, conv_pw2_b, kv_mod_w, kv_mod_b, kv_norm_g, w_kv, k_norm_g, w_q, q_norm_g, w_o, ffn_up_w, ffn_dw_w, ffn_dw_b, ffn_down_w, loss_target, m_mod_w, m_mod_b, m_norm_mix_g, m_norm_ffn_g, m_conv_pw1_w, m_conv_pw1_b, m_conv_dw_w, m_conv_dw_b, m_conv_ln_g, m_conv_ln_b, m_conv_pw2_w, m_conv_pw2_b, m_kv_mod_w, m_kv_mod_b, m_kv_norm_g, m_w_kv, m_k_norm_g, m_w_q, m_q_norm_g, m_w_o, m_ffn_up_w, m_ffn_dw_w, m_ffn_dw_b, m_ffn_down_w, v_mod_w, v_mod_b, v_norm_mix_g, v_norm_ffn_g, v_conv_pw1_w, v_conv_pw1_b, v_conv_dw_w, v_conv_dw_b, v_conv_ln_g, v_conv_ln_b, v_conv_pw2_w, v_conv_pw2_b, v_kv_mod_w, v_kv_mod_b, v_kv_norm_g, v_w_kv, v_k_norm_g, v_w_q, v_q_norm_g, v_w_o, v_ffn_up_w, v_ffn_dw_w, v_ffn_dw_b, v_ffn_down_w):
    _, s, d = x.shape
    f = ffn_dw_b.shape[1]
    qw = w_q.shape[2] * 4
    ax, ay, ac = lax.axis_index("x"), lax.axis_index("y"), lax.axis_index("c")
    slot = 2 * ax + ay
    me8 = 4 * ax + 2 * ay + ac
    slot_arr = jnp.reshape(slot, (1,)).astype(jnp.int32)
    c_arr = jnp.reshape(ac, (1,)).astype(jnp.int32)
    x2 = x.reshape(s, d)
    target = loss_target.reshape(s, d)
    row = lambda v: v.reshape(1, -1)

    c_all = _allgather8(c.reshape(8, d // 8), "gather_c").reshape(N_DEV, d)
    sc_all = jax.nn.silu(c_all)
    mod_part = _mod_proj(sc_all, mod_w, "mod_proj")
    kvm_part = _mod_proj(sc_all, kv_mod_w[None], "kvmod_proj")
    nm, nk = mod_part.shape[2], kvm_part.shape[2]
    small_sharded = [conv_pw1_b, conv_dw_w, conv_dw_b, conv_ln_g, conv_ln_b, conv_pw2_b, ffn_dw_w]
    pack = jnp.concatenate([mod_part.reshape(-1), kvm_part.reshape(-1)] + [t.reshape(-1) for t in small_sharded])
    plen = pack.shape[0]
    pack = _pad_to(pack, 1024)
    gathered = _allgather8(pack.reshape(8, -1), "gather_mod").reshape(4, 2, -1)[:, 0, :plen]
    off = 0

    def take(n_el):
        nonlocal off
        out = lax.slice_in_dim(gathered, off, off + n_el, axis=1)
        off += n_el
        return out

    mod_g = take(2 * N_DEV * nm).reshape(4, 2, N_DEV, nm)
    kvm_g = take(N_DEV * nk).reshape(4, N_DEV, nk)
    mine = lambda t, axis: lax.dynamic_index_in_dim(t, me8, axis=axis, keepdims=False)
    mod_vec = jnp.transpose(mine(mod_g, 2), (1, 0, 2)).reshape(2, 4 * nm) + mod_b
    kvm_vec = mine(kvm_g, 1).reshape(4 * nk) + kv_mod_b
    pw1_b_full = take(conv_pw1_b.shape[1]).reshape(1, -1)
    dw_w_full = jnp.transpose(take(CONV_K * (d // 4)).reshape(4, CONV_K, d // 4), (1, 0, 2)).reshape(CONV_K, d)
    dw_b_full, ln_g_full, ln_b_full, pw2_b_full = (take(d // 4).reshape(1, d) for _ in range(4))
    fdw_full = jnp.transpose(take(2 * FFN_K * (f // 4)).reshape(4, 2, FFN_K, f // 4), (1, 2, 0, 3)).reshape(2, FFN_K, f)
    mods = [[row(mod_vec[l, k * d:(k + 1) * d]) for k in range(6)] for l in range(2)]
    kv_sh, kv_sc = row(kvm_vec[:d]), row(kvm_vec[d:])
    zero_d = jnp.zeros((1, d), F32)

    big = [("pw1", conv_pw1_w[0], "col"), ("pw2", conv_pw2_w[0], "row"), ("wkv", w_kv, "col"), ("wq", w_q[0], "col"),
           ("wo", w_o[0], "col"), ("up0", ffn_up_w[0], "col"), ("up1", ffn_up_w[1], "col"), ("dn0", ffn_down_w[0], "row"),
           ("dn1", ffn_down_w[1], "row")]
    names = [b[0] for b in big]
    kinds = [b[2] for b in big]
    shard_shapes = [b[1].shape for b in big]
    own = {b[0]: _cast_into_full(b[1], b[2], slot_arr, "cast_" + b[0]) for b in big}
    meta = dict(zip(names, zip(kinds, shard_shapes)))
    full = {}

    def stage(make, group, src, **kw):
        return make([src[k] for k in group], [meta[k][0] for k in group], [meta[k][1] for k in group], **kw)

    grp_a, grp_b, grp_c, grp_d = ["pw1", "pw2"], ["up0", "dn0"], ["wkv", "wq", "wo"], ["up1", "dn1"]
    landed_a = dict(zip(grp_a, _run_rider(stage(_gather_chips, grp_a, own), "gather_a_chips")))
    full.update(zip(grp_a, _run_rider(stage(_gather_pair, grp_a, landed_a), "gather_a_pair")))
    chips_c = stage(_gather_chips, grp_c, own)

    def pair_stage(group, chips_rider):
        return stage(_gather_pair, group, dict(zip(group, chips_rider.results)))

    def ffn_fwd(l, x_in, h, tag, rider_up=None, rider_down=None):
        u = _matmul(h, full["up%d" % l], "nn", F32, "mm_up" + tag, rider=rider_up)
        z = _ffn_act_fwd(u, fdw_full[l], row(ffn_dw_b[l]), "ffn_act" + tag)
        y = _matmul(z, full["dn%d" % l], "nn", F32, "mm_down" + tag, rider=rider_down)
        return u, z, y

    sh_m0, sc_m0, g_m0, sh_f0, sc_f0, g_f0 = mods[0]
    sh_m1, sc_m1, g_m1, sh_f1, sc_f1, g_f1 = mods[1]
    gmix0, gmix1, gffn0, gffn1 = row(norm_mix_g[0]), row(norm_mix_g[1]), row(norm_ffn_g[0]), row(norm_ffn_g[1])
    (h0,) = _resid_mod(x2, None, [(gmix0, sc_m0, sh_m0)], None, "mod_in")
    chips_b0 = stage(_gather_chips, grp_b, own, piece=(0, 2))
    u0 = _matmul(h0, full["pw1"], "nn", F32, "mm_pw1", rider=chips_b0)
    glu = _glu_fwd(u0, pw1_b_full, "glu")
    chips_b = stage(_gather_chips, grp_b, dict(zip(grp_b, chips_b0.results)), piece=(1, 2))
    dwc = _dwconv_fwd(glu, dw_w_full, dw_b_full, "dwconv", rider=chips_b)
    act = _ln_silu_fwd(dwc, ln_g_full, ln_b_full, "ln_silu")
    pair_b = pair_stage(grp_b, chips_b)
    y0 = _matmul(act, full["pw2"], "nn", F32, "mm_pw2", rider=pair_b)
    full.update(zip(grp_b, pair_b.results))
    x1, hf0 = _resid_mod(x2, (y0, pw2_b_full, g_m0), [(gffn0, sc_f0, sh_f0)], None, "resid_conv")
    uf0 = _matmul(hf0, full["up0"], "nn", F32, "mm_up0", rider=chips_c)
    zf0 = _ffn_act_fwd(uf0, fdw_full[0], row(ffn_dw_b[0]), "ffn_act0")
    pair_c = pair_stage(grp_c, chips_c)
    yf0 = _matmul(zf0, full["dn0"], "nn", F32, "mm_down0", rider=pair_c)
    full.update(zip(grp_c, pair_c.results))
    gkv = row(kv_norm_g)
    xa, hk, hq = _resid_mod(x1, (yf0, zero_d, g_f0), [(gkv, kv_sc, kv_sh), (gmix1, sc_m1, sh_m1)], None, "resid_ffn0")
    chips_d0 = stage(_gather_chips, grp_d, own, piece=(0, 2))
    kvp = _matmul(hk, full["wkv"], "nn", F32, "mm_kv", rider=chips_d0)
    chips_d = stage(_gather_chips, grp_d, dict(zip(grp_d, chips_d0.results)), piece=(1, 2))
    qp = _matmul(hq, full["wq"], "nn", F32, "mm_q")
    inv_freq = ROPE_THETA ** (-jnp.arange(0, ROT_DIM, 2, dtype=F32) / ROT_DIM)
    half = ROT_DIM // 2
    freq_l = jnp.concatenate([inv_freq, inv_freq, jnp.zeros((HEAD_DIM - ROT_DIM,), F32)]).reshape(1, HEAD_DIM)
    sign_l = jnp.concatenate([-jnp.ones((half,), F32), jnp.ones((half,), F32), jnp.zeros((HEAD_DIM - ROT_DIM,), F32)]).reshape(1, HEAD_DIM)
    ctab, stab = _rope_tables(positions.reshape(s, 1), freq_l, sign_l, "rope_tables")
    gq, gk = row(q_norm_g[0]), row(k_norm_g)
    qn, kn = _qk_norm_rope_fwd(qp, kvp, gq, gk, ctab, stab, "qk_norm_rope", rider=chips_d)
    og, lg = zip(*[_attn_fwd(qn, kn, kvp, g, "attn_fwd%d" % g) for g in range(N_GROUPS)])
    o_mix, lse = _attn_combine(og, lg, "attn_combine")
    pair_d = pair_stage(grp_d, chips_d)
    ya = _matmul(o_mix, full["wo"], "nn", F32, "mm_o", rider=pair_d)
    full.update(zip(grp_d, pair_d.results))
    xb, hf1 = _resid_mod(xa, (ya, zero_d, g_m1), [(gffn1, sc_f1, sh_f1)], None, "resid_attn")
    uf1, zf1, yf1 = ffn_fwd(1, xb, hf1, "1")
    _, dxo, loss_cols = _resid_mod(xb, (yf1, zero_d, g_f1), [], target, "resid_loss")
    loss = lax.psum(jnp.sum(loss_cols), ("x", "y", "c"))

    gbig = {}

    parts, recv2 = {}, {}

    def rs_pair(group):
        return _rs_pair([gbig[k] for k in group], [meta[k][0] for k in group], [meta[k][1] for k in group])

    def rs_chips(group, pair_results, **kw):
        for k, r_ in zip(group, pair_results):
            parts[k] = _rs_pair_add(gbig[k], r_, meta[k][0], meta[k][1], c_arr, "rs_pair_add_" + k)
        return _rs_chips([parts[k] for k in group], **kw)

    rs_a, rs_b, rs_c, rs_d = ["dn1", "up1"], ["wo", "wq", "wkv"], ["dn0", "up0"], ["pw2", "pw1"]

    def ffn_bwd(l, dy, u, z, h, tag, rider_dx=None, act_rider=lambda: None):
        dz = _matmul(dy, full["dn%d" % l], "nt", F32, "mm_down_dx" + tag, rider=rider_dx)
        gbig["dn%d" % l] = _matmul(z, dy, "tn", F32, "mm_down_dw" + tag)
        du, dfw, dfb = _ffn_act_bwd(dz, u, fdw_full[l], row(ffn_dw_b[l]), "ffn_act_bwd" + tag, rider=act_rider())
        dh = _matmul(du, full["up%d" % l], "nt", BF16, "mm_up_dx" + tag)
        gbig["up%d" % l] = _matmul(h, du, "tn", F32, "mm_up_dw" + tag)
        return dh, dfw, dfb

    dxb0, dyf1, dg_f1, _ = _bwd_step(dxo, xb, [], (yf1, zero_d, g_f1), "bwd_loss")
    dhf1, dfw1, dfb1 = ffn_bwd(1, dyf1, uf1, zf1, hf1, "1")
    dxb, dsh_f1, dsc_f1, dgffn1, dya, dg_m1, _ = _bwd_step(dxb0, xb, [(dhf1, gffn1, sc_f1)], (ya, zero_d, g_m1), "bwd_attn_out")
    do = _matmul(dya, full["wo"], "nt", F32, "mm_o_dx")
    gbig["wo"] = _matmul(o_mix, dya, "tn", F32, "mm_o_dw")
    delta = _attn_delta(do, o_mix, "attn_delta")
    pair_a = rs_pair(rs_a)
    dqs = [_attn_bwd_dq(qn, kn, kvp, do, lse, delta, g, "attn_dq%d" % g, rider=pair_a if g == 0 else None) for g in range(N_GROUPS)]
    dks, dvs = zip(*[_attn_bwd_dkv(qn, kn, kvp, do, lse, delta, g, "attn_dkv%d" % g) for g in range(N_GROUPS)])
    ride_a0 = rs_chips(rs_a, pair_a.results, piece=(0, 2))
    dqp, dkvp, dgq, dgk = _qk_norm_rope_bwd(dqs, dks, dvs, qp, kvp, gq, gk, ctab, stab, "qk_norm_rope_bwd", rider=ride_a0)
    dhq = _matmul(dqp, full["wq"], "nt", BF16, "mm_q_dx")
    gbig["wq"] = _matmul(hq, dqp, "tn", F32, "mm_q_dw")
    ride_a = _rs_chips([parts[k] for k in rs_a], piece=(1, 2), dest=ride_a0.results)
    dhk = _matmul(dkvp, full["wkv"], "nt", BF16, "mm_kv_dx", rider=ride_a)
    recv2.update(zip(rs_a, ride_a.results))
    gbig["wkv"] = _matmul(hk, dkvp, "tn", F32, "mm_kv_dw")
    (dxa, dsh_kv, dsc_kv, dgkv, dsh_m1, dsc_m1, dgmix1, dyf0, dg_f0, _) = _bwd_step(
        dxb, xa, [(dhk, gkv, kv_sc), (dhq, gmix1, sc_m1)], (yf0, zero_d, g_f0), "bwd_kvq")
    pair_b = rs_pair(rs_b)
    ride_b = []
    dhf0, dfw0, dfb0 = ffn_bwd(0, dyf0, uf0, zf0, hf0, "0", rider_dx=pair_b,
                               act_rider=lambda: ride_b.append(rs_chips(rs_b, pair_b.results)) or ride_b[0])
    recv2.update(zip(rs_b, ride_b[0].results))
    pair_c = rs_pair(rs_c)
    dx1, dsh_f0, dsc_f0, dgffn0, dy0, dg_m0, dpw2_b = _bwd_step(dxa, x1, [(dhf0, gffn0, sc_f0)], (y0, pw2_b_full, g_m0), "bwd_conv_out",
                                                             rider=pair_c)
    dact = _matmul(dy0, full["pw2"], "nt", BF16, "mm_pw2_dx")
    gbig["pw2"] = _matmul(act, dy0, "tn", F32, "mm_pw2_dw")
    ddwc, dln_g, dln_b = _ln_silu_bwd(dact, dwc, ln_g_full, ln_b_full, "ln_silu_bwd")
    ride_c = rs_chips(rs_c, pair_c.results)
    dglu, ddw_w, ddw_b = _dwconv_bwd(ddwc, glu, dw_w_full, "dwconv_bwd", rider=ride_c)
    recv2.update(zip(rs_c, ride_c.results))
    du0, dpw1_b = _glu_bwd(dglu, u0, pw1_b_full, "glu_bwd")
    dh0 = _matmul(du0, full["pw1"], "nt", BF16, "mm_pw1_dx")
    gbig["pw1"] = _matmul(h0, du0, "tn", F32, "mm_pw1_dw")
    grad_x, dsh_m0, dsc_m0, dgmix0 = _bwd_step(dx1, x2, [(dh0, gmix0, sc_m0)], None, "bwd_in")

    dmod = [jnp.concatenate([dsh_m0, dsc_m0, dg_m0, dsh_f0, dsc_f0, dg_f0], axis=1),
            jnp.concatenate([dsh_m1, dsc_m1, dg_m1, dsh_f1, dsc_f1, dg_f1], axis=1)]
    dkvm = jnp.concatenate([dsh_kv, dsc_kv], axis=1)
    per_ex = [dmod[0], dmod[1], dkvm]
    summed = [dgmix0, dgmix1, dgffn0, dgffn1, dpw1_b, ddw_w, ddw_b, dln_g, dln_b, dpw2_b, dgkv, dgk, dgq, dfw0, dfw1, dfb0, dfb1]
    vec = jnp.concatenate([t.reshape(-1) for t in per_ex + summed])
    vlen = vec.shape[0]
    vec = _pad_to(vec, 1024)
    vall = _allgather8(vec.reshape(8, -1), "gather_small").reshape(N_DEV, -1)
    vsum = _sum_rows(vall, "sum_small")[0]
    n_pe = sum(t.size for t in per_ex)
    dm_all = vall[:, :n_pe]
    off2 = n_pe
    sums = []
    for t in summed:
        sums.append(vsum[off2:off2 + t.size].reshape(t.shape))
        off2 += t.size
    (s_gmix0, s_gmix1, s_gffn0, s_gffn1, s_pw1_b, s_dw_w, s_dw_b, s_ln_g, s_ln_b, s_pw2_b, s_gkv, s_gk, s_gq, s_fw0, s_fw1,
     s_fb0, s_fb1) = sums
    shard_cols = lambda t, width: lax.dynamic_slice_in_dim(t, slot * width, width, axis=t.ndim - 1)
    dm_mod = jnp.stack([shard_cols(dm_all[:, l * 6 * d:(l + 1) * 6 * d], nm) for l in range(2)])
    dm_kv = shard_cols(dm_all[:, 12 * d:14 * d], nk)[None]
    sct = jnp.transpose(sc_all)

    recv2.update(zip(rs_d, _run_rider(rs_chips(rs_d, _run_rider(rs_pair(rs_d), "rs_pair_d")), "rs_chips_d")))
    reduced = {}
    for nme, shp in zip(names, shard_shapes):
        p_, r_ = parts[nme], recv2[nme]
        if nme in ("up0", "up1", "dn0", "dn1"):
            key, layer = nme[:-1], int(nme[-1])
            reduced[key] = _rs_chip_add(p_, r_, slot_arr, c_arr, reduced.get(key), layer, (2,) + shp, "rs_chip_add_" + nme)
        else:
            reduced[nme] = _rs_chip_add(p_, r_, slot_arr, c_arr, None, None, shp, "rs_chip_add_" + nme)
    g_pw1, g_pw2, g_wkv, g_wq, g_wo, g_up, g_dn = _rs_pair_share(
        [reduced[k] for k in ("pw1", "pw2", "wkv", "wq", "wo", "up", "dn")], "rs_pair_share")

    grads, deltas, new_m, new_v = {}, {}, {}, {}

    def put(nme, g_, res):
        grads[nme] = g_
        deltas[nme], new_m[nme], new_v[nme] = res

    for nme, g_, w_, m_, v_ in (("conv_pw1_w", g_pw1[None], conv_pw1_w, m_conv_pw1_w, v_conv_pw1_w),
                                ("conv_pw2_w", g_pw2[None], conv_pw2_w, m_conv_pw2_w, v_conv_pw2_w),
                                ("w_kv", g_wkv, w_kv, m_w_kv, v_w_kv), ("w_q", g_wq[None], w_q, m_w_q, v_w_q),
                                ("w_o", g_wo[None], w_o, m_w_o, v_w_o), ("ffn_up_w", g_up, ffn_up_w, m_ffn_up_w, v_ffn_up_w),
                                ("ffn_down_w", g_dn, ffn_down_w, m_ffn_down_w, v_ffn_down_w)):
        put(nme, g_, _adamw_big(w_, g_, m_, v_, "adamw_" + nme))
    g_, *res = _adamw_mod(mod_w, sct, dm_mod, m_mod_w, v_mod_w, "adamw_mod_w")
    put("mod_w", g_, res)
    g_, *res = _adamw_mod(kv_mod_w[None], sct, dm_kv, m_kv_mod_w[None], v_kv_mod_w[None], "adamw_kv_mod_w")
    put("kv_mod_w", g_[0], [t[0] for t in res])

    dm_sum = vsum[:n_pe]
    small = [
        ("mod_b", dm_sum[:12 * d].reshape(2, 6 * d), mod_b, m_mod_b, v_mod_b),
        ("norm_mix_g", jnp.concatenate([s_gmix0, s_gmix1], axis=0), norm_mix_g, m_norm_mix_g, v_norm_mix_g),
        ("norm_ffn_g", jnp.concatenate([s_gffn0, s_gffn1], axis=0), norm_ffn_g, m_norm_ffn_g, v_norm_ffn_g),
        ("conv_pw1_b", shard_cols(s_pw1_b, conv_pw1_b.shape[1]), conv_pw1_b, m_conv_pw1_b, v_conv_pw1_b),
        ("conv_dw_w", shard_cols(s_dw_w, d // 4)[None], conv_dw_w, m_conv_dw_w, v_conv_dw_w),
        ("conv_dw_b", shard_cols(s_dw_b, d // 4), conv_dw_b, m_conv_dw_b, v_conv_dw_b),
        ("conv_ln_g", shard_cols(s_ln_g, d // 4), conv_ln_g, m_conv_ln_g, v_conv_ln_g),
        ("conv_ln_b", shard_cols(s_ln_b, d // 4), conv_ln_b, m_conv_ln_b, v_conv_ln_b),
        ("conv_pw2_b", shard_cols(s_pw2_b, d // 4), conv_pw2_b, m_conv_pw2_b, v_conv_pw2_b),
        ("kv_mod_b", dm_sum[12 * d:14 * d], kv_mod_b, m_kv_mod_b, v_kv_mod_b),
        ("kv_norm_g", s_gkv.reshape(-1), kv_norm_g, m_kv_norm_g, v_kv_norm_g),
        ("k_norm_g", s_gk.reshape(-1), k_norm_g, m_k_norm_g, v_k_norm_g),
        ("q_norm_g", s_gq, q_norm_g, m_q_norm_g, v_q_norm_g),
        ("ffn_dw_w", shard_cols(jnp.stack([s_fw0, s_fw1]), f // 4), ffn_dw_w, m_ffn_dw_w, v_ffn_dw_w),
        ("ffn_dw_b", jnp.concatenate([s_fb0, s_fb1], axis=0), ffn_dw_b, m_ffn_dw_b, v_ffn_dw_b),
    ]
    as2d = lambda t: t.reshape(-1, t.shape[-1])
    sd_, sm_, sv_ = _adamw_small([as2d(t[2]) for t in small], [as2d(t[1]) for t in small], [as2d(t[3]) for t in small],
                                 [as2d(t[4]) for t in small], "adamw_small")
    for (nme, g_, w_, _, _), d_, mo_, vo_ in zip(small, sd_, sm_, sv_):
        put(nme, g_.reshape(w_.shape), [d_.reshape(w_.shape), mo_.reshape(w_.shape), vo_.reshape(w_.shape)])

    order = ["mod_w", "mod_b", "norm_mix_g", "norm_ffn_g", "conv_pw1_w", "conv_pw1_b", "conv_dw_w", "conv_dw_b", "conv_ln_g",
             "conv_ln_b", "conv_pw2_w", "conv_pw2_b", "kv_mod_w", "kv_mod_b", "kv_norm_g", "w_kv", "k_norm_g", "w_q", "q_norm_g",
             "w_o", "ffn_up_w", "ffn_dw_w", "ffn_dw_b", "ffn_down_w"]
    return (loss, grad_x.reshape(x.shape), *[grads[k] for k in order], *[deltas[k] for k in order], *[new_m[k] for k in order],
            *[new_v[k] for k in order])
```
